```python
import math
import jax, jax.numpy as jnp
from jax import lax
import numpy as np

D_MODEL = 1024
BATCH = 4
SEQ = 8192
DEPTH = 4
DEC_BATCH = 8
DEC_SEQ = 64
PAST_LEN = 2048

CHUNK = 64
QUERY_BLOCK = 128
N_EVEN = (DEPTH + 1) // 2
N_ODD = DEPTH // 2
MLA_HEADS = 8
Q_RANK = 384
KV_RANK = 256
NOPE_DIM = 64
ROPE_DIM = 32
V_DIM = 64
ROPE_THETA = 10000.0
MLA_WIDTH = MLA_HEADS * V_DIM
SC_WIDTH = 512
SC_K = 3
MIX_E = MLA_WIDTH + SC_WIDTH
DN_HEADS = 8
DN_DK = 128
DN_DV = 128
DN_CONV = 4
DN_KEY = DN_HEADS * DN_DK
DN_VAL = DN_HEADS * DN_DV
DN_QKV = 2 * DN_KEY + DN_VAL
D_FF = 2816
FF_K = 3
EVEN_IN = Q_RANK + KV_RANK + ROPE_DIM + 3 * SC_WIDTH
E_SPLITS = (Q_RANK, Q_RANK + KV_RANK, Q_RANK + KV_RANK + ROPE_DIM,
            Q_RANK + KV_RANK + ROPE_DIM + SC_WIDTH, Q_RANK + KV_RANK + ROPE_DIM + 2 * SC_WIDTH)
ODD_IN = DN_QKV + DN_VAL + 2 * DN_HEADS
O_SPLITS = (DN_QKV, DN_QKV + DN_VAL, DN_QKV + DN_VAL + DN_HEADS)
ALPHA = (2 * DEPTH) ** 0.25
BETA_INIT = (8 * DEPTH) ** -0.25
NORM_EPS = 1e-6
NEG_INF = -1e30

kernel_name = "hybrid_mla_shortconv_gdn_convffn_stream_step"


def layer_norm(x, g, b):
    xf = x.astype(jnp.float32)
    mu = xf.mean(-1, keepdims=True)
    var = jnp.square(xf - mu).mean(-1, keepdims=True)
    return ((xf - mu) * lax.rsqrt(var + NORM_EPS) * g + b).astype(x.dtype)


def rms_norm(x, g):
    xf = x.astype(jnp.float32)
    return (xf * lax.rsqrt(jnp.mean(xf * xf, -1, keepdims=True) + NORM_EPS) * g).astype(x.dtype)


def l2_norm(x):
    xf = x.astype(jnp.float32)
    return xf * lax.rsqrt(jnp.sum(xf * xf, -1, keepdims=True) + NORM_EPS)


def causal_dwconv(x, w, buf):
    k = w.shape[0]
    s = x.shape[1]
    xp = jnp.concatenate([buf.astype(x.dtype), x], axis=1)
    y = xp[:, 0:s] * w[0]
    for i in range(1, k):
        y = y + xp[:, i:i + s] * w[i]
    return y, xp[:, xp.shape[1] - (k - 1):]


def rope(x, pos):
    half = ROPE_DIM // 2
    inv = ROPE_THETA ** (-jnp.arange(half, dtype=jnp.float32) / half)
    ang = pos.astype(jnp.float32)[:, None] * inv[None, :]
    cos = jnp.cos(ang)[None, :, None, :]
    sin = jnp.sin(ang)[None, :, None, :]
    xf = x.astype(jnp.float32)
    x1, x2 = xf[..., :half], xf[..., half:]
    return jnp.concatenate([x1 * cos - x2 * sin, x2 * cos + x1 * sin], -1).astype(x.dtype)


def chunk_attention(q, k, v, q_pos, k_pos):
    b, sq, h, dqk = q.shape
    dv = v.shape[-1]
    qb = QUERY_BLOCK if sq % QUERY_BLOCK == 0 else sq
    nb = sq // qb
    q_blocks = jnp.moveaxis(q.reshape(b, nb, qb, h, dqk), 1, 0)
    pos_blocks = q_pos.reshape(nb, qb)
    k_chunk = k_pos // CHUNK
    scale = (NOPE_DIM + ROPE_DIM) ** -0.5

    def one_block(args):
        qi, pi = args
        s = jnp.einsum('bqhd,bkhd->bhqk', qi, k).astype(jnp.float32) * scale
        visible = k_chunk[None, :] <= (pi // CHUNK)[:, None]
        s = jnp.where(visible[None, None], s, NEG_INF)
        p = jax.nn.softmax(s, axis=-1).astype(v.dtype)
        return jnp.einsum('bhqk,bkhd->bqhd', p, v)

    o = lax.map(one_block, (q_blocks, pos_blocks))
    return jnp.moveaxis(o, 0, 1).reshape(b, sq, h * dv)


def even_mixer(x, q_pos, k_pos, past_lat, past_kr, sc_buf, w_in, g_q, g_kv, w_uq, w_ukv, sc_w, w_o):
    b, s, _ = x.shape
    h = x @ w_in
    c_q, c_kv, k_r, gate_b, gate_c, sh = jnp.split(h, list(E_SPLITS), axis=-1)
    q = (rms_norm(c_q, g_q) @ w_uq).reshape(b, s, MLA_HEADS, NOPE_DIM + ROPE_DIM)
    q = jnp.concatenate([q[..., :NOPE_DIM], rope(q[..., NOPE_DIM:], q_pos)], -1)
    lat = rms_norm(c_kv, g_kv)
    kr = rope(k_r[:, :, None, :], q_pos)[:, :, 0]
    lat_all = jnp.concatenate([past_lat.astype(x.dtype), lat], 1)
    kr_all = jnp.concatenate([past_kr.astype(x.dtype), kr], 1)
    t = lat_all.shape[1]
    kv = (lat_all @ w_ukv).reshape(b, t, MLA_HEADS, NOPE_DIM + V_DIM)
    k = jnp.concatenate([kv[..., :NOPE_DIM],
                         jnp.broadcast_to(kr_all[:, :, None, :], (b, t, MLA_HEADS, ROPE_DIM))], -1)
    v = kv[..., NOPE_DIM:]
    o_att = chunk_attention(q, k, v, q_pos, k_pos)
    u, sc_new = causal_dwconv(gate_c * sh, sc_w, sc_buf)
    o_sc = gate_b * u
    y = jnp.concatenate([o_att, o_sc], -1) @ w_o
    return y, lat, kr, sc_new


def gated_delta_chunked(q, k, v, g, beta, s0):
    f32 = jnp.float32
    b, s, h, dk = q.shape
    dv = v.shape[-1]
    L = CHUNK if s % CHUNK == 0 else s
    n = s // L

    def blk(t):
        t = t.astype(f32).reshape((b, n, L, h) + t.shape[3:])
        return jnp.moveaxis(t, 3, 1)

    q, k, v, g, beta = blk(q), blk(k), blk(v), blk(g), blk(beta)
    G = jnp.cumsum(g, axis=-1)
    tri = jnp.tril(jnp.ones((L, L), bool))
    strict = jnp.tril(jnp.ones((L, L), bool), -1)
    decay = jnp.exp(jnp.where(tri, G[..., :, None] - G[..., None, :], -jnp.inf))
    kb = k * beta[..., None]
    m = jnp.where(strict, jnp.einsum('bhnid,bhnjd->bhnij', kb, k) * decay, 0.0)
    a = m + jnp.eye(L, dtype=f32)
    rhs = jnp.concatenate([v * beta[..., None], kb * jnp.exp(G)[..., None]], -1)
    sol = lax.linalg.triangular_solve(a, rhs, left_side=True, lower=True, unit_diagonal=True)
    w_val, k_cd = sol[..., :dv], sol[..., dv:]
    attn = jnp.einsum('bhnid,bhnjd->bhnij', q, k) * decay
    q_dec = q * jnp.exp(G)[..., None]
    g_last = G[..., -1:]
    k_tail = k * jnp.exp(g_last - G)[..., None]
    g_tot = jnp.exp(g_last[..., 0])

    def step(st, inp):
        qd, kcd, wv, at, kt, gt = inp
        v_new = wv - jnp.einsum('bhld,bhde->bhle', kcd, st)
        o = jnp.einsum('bhld,bhde->bhle', qd, st) + jnp.einsum('bhij,bhje->bhie', at, v_new)
        st = st * gt[..., None, None] + jnp.einsum('bhld,bhle->bhde', kt, v_new)
        return st, o

    xs = tuple(jnp.moveaxis(t, 2, 0) for t in (q_dec, k_cd, w_val, attn, k_tail, g_tot))
    s_fin, o = lax.scan(step, s0.astype(f32), xs)
    o = jnp.moveaxis(jnp.moveaxis(o, 0, 2), 1, 3).reshape(b, s, h, dv)
    return o, s_fin


def odd_mixer(x, dconv_buf, s0, w_in, dconv_w, a_log, dt_bias, g_o, w_o):
    b, s, _ = x.shape
    h = x @ w_in
    qkv, z, a, bt = jnp.split(h, list(O_SPLITS), axis=-1)
    qkv, dconv_new = causal_dwconv(qkv, dconv_w, dconv_buf)
    qkv = jax.nn.silu(qkv)
    q, k, v = jnp.split(qkv, [DN_KEY, 2 * DN_KEY], axis=-1)
    q = l2_norm(q.reshape(b, s, DN_HEADS, DN_DK)) * (DN_DK ** -0.5)
    k = l2_norm(k.reshape(b, s, DN_HEADS, DN_DK))
    v = v.reshape(b, s, DN_HEADS, DN_DV)
    beta = jax.nn.sigmoid(bt.astype(jnp.float32))
    g = -jnp.exp(a_log.astype(jnp.float32)) * jax.nn.softplus(a.astype(jnp.float32) + dt_bias)
    o, s_new = gated_delta_chunked(q, k, v, g, beta, s0)
    o = rms_norm(o, g_o) * jax.nn.silu(z.reshape(b, s, DN_HEADS, DN_DV).astype(jnp.float32))
    y = o.astype(x.dtype).reshape(b, s, DN_VAL) @ w_o
    return y, dconv_new, s_new.astype(s0.dtype)


def conv_ffn(x, buf, w_in, conv_w, w_out):
    h = x @ w_in
    gate, up = h[..., :D_FF], h[..., D_FF:]
    gate, buf_new = causal_dwconv(gate, conv_w, buf)
    return (jax.nn.silu(gate) * up) @ w_out, buf_new


def trunk(x, past_lat, past_kr, sc_buf, dconv_buf, delta_s, ff_buf, w):
    (w_in_e, g_qnorm, g_kvnorm, w_uq, w_ukv, sc_w, w_o_e, w_in_o, dconv_w, a_log, dt_bias,
     g_onorm, w_o_o, w_ff_in, ffconv_w, w_ff_out, ln_mix_g, ln_mix_b, ln_ff_g, ln_ff_b) = w
    s = x.shape[1]
    p = past_lat.shape[2]
    q_pos = p + jnp.arange(s, dtype=jnp.int32)
    k_pos = jnp.arange(p + s, dtype=jnp.int32)
    lats, krs, scs, dcs, dss, ffs = [], [], [], [], [], []
    for i in range(DEPTH):
        if i % 2 == 0:
            e = i // 2
            y, lat, kr, scb = even_mixer(x, q_pos, k_pos, past_lat[e], past_kr[e], sc_buf[e],
                                         w_in_e[e], g_qnorm[e], g_kvnorm[e], w_uq[e], w_ukv[e],
                                         sc_w[e], w_o_e[e])
            lats.append(lat); krs.append(kr); scs.append(scb)
        else:
            o = i // 2
            y, dcb, st = odd_mixer(x, dconv_buf[o], delta_s[o], w_in_o[o], dconv_w[o], a_log[o],
                                   dt_bias[o], g_onorm[o], w_o_o[o])
            dcs.append(dcb); dss.append(st)
        x = layer_norm(ALPHA * x + y, ln_mix_g[i], ln_mix_b[i])
        y, ffb = conv_ffn(x, ff_buf[i], w_ff_in[i], ffconv_w[i], w_ff_out[i])
        ffs.append(ffb)
        x = layer_norm(ALPHA * x + y, ln_ff_g[i], ln_ff_b[i])
    return (x, jnp.stack(lats), jnp.stack(krs), jnp.stack(scs), jnp.stack(dcs),
            jnp.stack(dss), jnp.stack(ffs))


def setup_inputs(seed: int = 0) -> dict:
    key = jax.random.key(seed)
    ks = jax.random.split(key, 32)
    f32 = jnp.float32

    def nrm(k, shape, scale):
        return jax.random.normal(k, shape, f32) * scale

    dt = jnp.exp(jax.random.uniform(ks[20], (N_ODD, DN_HEADS), f32, math.log(1e-3), math.log(1e-1)))
    return {
        "x_prompt": nrm(ks[0], (BATCH, SEQ, D_MODEL), 1.0),
        "x_sample": nrm(ks[1], (DEC_BATCH, DEC_SEQ, D_MODEL), 1.0),
        "cache_mla_latent": nrm(ks[2], (N_EVEN, DEC_BATCH, PAST_LEN, KV_RANK), 1.0),
        "cache_mla_krope": nrm(ks[3], (N_EVEN, DEC_BATCH, PAST_LEN, ROPE_DIM), 1.0),
        "state_sconv": nrm(ks[4], (N_EVEN, DEC_BATCH, SC_K - 1, SC_WIDTH), 1.0),
        "state_dconv": nrm(ks[5], (N_ODD, DEC_BATCH, DN_CONV - 1, DN_QKV), 1.0),
        "state_delta": nrm(ks[6], (N_ODD, DEC_BATCH, DN_HEADS, DN_DK, DN_DV), DN_DK ** -0.5),
        "state_ffconv": nrm(ks[7], (DEPTH, DEC_BATCH, FF_K - 1, D_FF), 1.0),
        "w_in_e": nrm(ks[8], (N_EVEN, D_MODEL, EVEN_IN), D_MODEL ** -0.5),
        "g_qnorm": 1.0 + nrm(ks[9], (N_EVEN, Q_RANK), 0.01),
        "g_kvnorm": 1.0 + nrm(ks[10], (N_EVEN, KV_RANK), 0.01),
        "w_uq": nrm(ks[11], (N_EVEN, Q_RANK, MLA_HEADS * (NOPE_DIM + ROPE_DIM)), Q_RANK ** -0.5),
        "w_ukv": nrm(ks[12], (N_EVEN, KV_RANK, MLA_HEADS * (NOPE_DIM + V_DIM)), KV_RANK ** -0.5),
        "sc_w": nrm(ks[13], (N_EVEN, SC_K, SC_WIDTH), SC_K ** -0.5),
        "w_o_e": nrm(ks[14], (N_EVEN, MIX_E, D_MODEL), MIX_E ** -0.5 * BETA_INIT),
        "w_in_o": nrm(ks[15], (N_ODD, D_MODEL, ODD_IN), D_MODEL ** -0.5),
        "dconv_w": nrm(ks[16], (N_ODD, DN_CONV, DN_QKV), DN_CONV ** -0.5),
        "a_log": jnp.log(jax.random.uniform(ks[17], (N_ODD, DN_HEADS), f32, 1.0, 16.0)),
        "dt_bias": dt + jnp.log(-jnp.expm1(-dt)),
        "g_onorm": 1.0 + nrm(ks[18], (N_ODD, DN_DV), 0.01),
        "w_o_o": nrm(ks[19], (N_ODD, DN_VAL, D_MODEL), DN_VAL ** -0.5 * BETA_INIT),
        "w_ff_in": nrm(ks[21], (DEPTH, D_MODEL, 2 * D_FF), D_MODEL ** -0.5),
        "ffconv_w": nrm(ks[22], (DEPTH, FF_K, D_FF), FF_K ** -0.5),
        "w_ff_out": nrm(ks[23], (DEPTH, D_FF, D_MODEL), D_FF ** -0.5 * BETA_INIT),
        "ln_mix_g": 1.0 + nrm(ks[24], (DEPTH, D_MODEL), 0.01),
        "ln_mix_b": nrm(ks[25], (DEPTH, D_MODEL), 0.01),
        "ln_ff_g": 1.0 + nrm(ks[26], (DEPTH, D_MODEL), 0.01),
        "ln_ff_b": nrm(ks[27], (DEPTH, D_MODEL), 0.01),
    }


def reference(x_prompt, x_sample, cache_mla_latent, cache_mla_krope, state_sconv, state_dconv,
              state_delta, state_ffconv, w_in_e, g_qnorm, g_kvnorm, w_uq, w_ukv, sc_w, w_o_e,
              w_in_o, dconv_w, a_log, dt_bias, g_onorm, w_o_o, w_ff_in, ffconv_w, w_ff_out,
              ln_mix_g, ln_mix_b, ln_ff_g, ln_ff_b):
    w = (w_in_e, g_qnorm, g_kvnorm, w_uq, w_ukv, sc_w, w_o_e, w_in_o, dconv_w, a_log, dt_bias,
         g_onorm, w_o_o, w_ff_in, ffconv_w, w_ff_out, ln_mix_g, ln_mix_b, ln_ff_g, ln_ff_b)
    bp = x_prompt.shape[0]
    dt = x_prompt.dtype
    y_p, p_lat, p_kr, p_sc, p_dc, p_ds, p_ff = trunk(
        x_prompt,
        jnp.zeros((N_EVEN, bp, 0, KV_RANK), dt),
        jnp.zeros((N_EVEN, bp, 0, ROPE_DIM), dt),
        jnp.zeros((N_EVEN, bp, SC_K - 1, SC_WIDTH), dt),
        jnp.zeros((N_ODD, bp, DN_CONV - 1, DN_QKV), dt),
        jnp.zeros((N_ODD, bp, DN_HEADS, DN_DK, DN_DV), dt),
        jnp.zeros((DEPTH, bp, FF_K - 1, D_FF), dt),
        w)
    y_s, s_lat, s_kr, s_sc, s_dc, s_ds, s_ff = trunk(
        x_sample, cache_mla_latent, cache_mla_krope, state_sconv, state_dconv, state_delta,
        state_ffconv, w)
    return (y_p, y_s, p_lat, p_kr, p_sc, p_dc, p_ds, p_ff, s_lat, s_kr, s_sc, s_dc, s_ds, s_ff)
```

```python
import functools
import math

import jax
import jax.numpy as jnp
from jax import lax
from jax.experimental import pallas as pl
from jax.experimental.pallas import tpu as pltpu

D_MODEL = 1024
DEPTH = 4
CHUNK = 64
N_EVEN = (DEPTH + 1) // 2
N_ODD = DEPTH // 2
MLA_HEADS = 8
Q_RANK = 384
KV_RANK = 256
NOPE_DIM = 64
ROPE_DIM = 32
V_DIM = 64
ROPE_THETA = 10000.0
MLA_WIDTH = MLA_HEADS * V_DIM
SC_WIDTH = 512
SC_K = 3
DN_HEADS = 8
DN_DK = 128
DN_DV = 128
DN_CONV = 4
DN_KEY = DN_HEADS * DN_DK
DN_VAL = DN_HEADS * DN_DV
DN_QKV = 2 * DN_KEY + DN_VAL
D_FF = 2816
FF_K = 3
ALPHA = (2 * DEPTH) ** 0.25
NORM_EPS = 1e-6
NEG_INF = -1e30

F32 = jnp.float32
BF16 = jnp.bfloat16
HIGHEST = lax.Precision.HIGHEST

LANES = 128
SUBLANES = 8
HEAD_PAD = 128
FF_CHUNK = 256
VMEM_LIMIT = 56 * 1024 * 1024

NT_DIMS = (((1,), (1,)), ((), ()))
TN_DIMS = (((0,), (0,)), ((), ()))


def _params(*sem):
    return pltpu.CompilerParams(dimension_semantics=sem, vmem_limit_bytes=VMEM_LIMIT)


def _dot(a, b):
    return jnp.dot(a.astype(BF16), b.astype(BF16), preferred_element_type=F32)


def _layer_norm(r, g, b):
    mu = jnp.mean(r, axis=-1, keepdims=True)
    d = r - mu
    var = jnp.mean(d * d, axis=-1, keepdims=True)
    return d * lax.rsqrt(var + NORM_EPS) * g + b


def _rms_norm(x, g):
    return x * lax.rsqrt(jnp.mean(x * x, axis=-1, keepdims=True) + NORM_EPS) * g


def _softplus(x):
    return jnp.maximum(x, 0.0) + jnp.log1p(jnp.exp(-jnp.abs(x)))


def _conv_taps(x, w):
    k_taps = w.shape[0]
    y = x * w[k_taps - 1:k_taps, :]
    for k in range(1, k_taps):
        y = y + pltpu.roll(x, k, axis=0) * w[k_taps - 1 - k:k_taps - k, :]
    return y


def _causal_conv(x, prev8, w):
    y = _conv_taps(x, w)
    head = _conv_taps(jnp.concatenate([prev8, x[:SUBLANES]], axis=0), w)
    return jnp.concatenate([head[SUBLANES:], y[SUBLANES:]], axis=0)


def _ffn_kernel(x_ref, wg_ref, wu_ref, wo_ref, cw_ref, buf_ref, g_ref, b_ref,
                out_ref, nbuf_ref, acc_ref, xb_ref, carry_ref, *, tiles_per_seq, n_chunks):
    i = pl.program_id(0)
    tm = x_ref.shape[0]
    xb_ref[...] = x_ref[...].astype(BF16)
    acc_ref[...] = jnp.zeros_like(acc_ref)

    @pl.when(i % tiles_per_seq == 0)
    def _():
        carry_ref[...] = buf_ref[0]

    def body(c, _):
        xb = xb_ref[...]
        gate = jnp.dot(xb, wg_ref[c], preferred_element_type=F32)
        up = jnp.dot(xb, wu_ref[c], preferred_element_type=F32)
        y = _causal_conv(gate, carry_ref[c], cw_ref[c])
        tail = gate[tm - SUBLANES:]
        carry_ref[c] = tail
        nbuf_ref[0, c] = tail
        act = (y * jax.nn.sigmoid(y) * up).astype(BF16)
        acc_ref[...] += jnp.dot(act, wo_ref[c], preferred_element_type=F32)
        return 0

    lax.fori_loop(0, n_chunks, body, 0)
    out_ref[...] = _layer_norm(ALPHA * x_ref[...] + acc_ref[...], g_ref[...], b_ref[...])


def _conv_ffn(x, buf8, w_gate, w_up, w_out, conv_w, ln_g, ln_b, *, seq, tm):
    m = x.shape[0]
    nb = m // seq
    tm = min(tm, seq)
    tps = seq // tm
    nc = w_gate.shape[0]
    const3 = lambda i: (0, 0, 0)
    return pl.pallas_call(
        functools.partial(_ffn_kernel, tiles_per_seq=tps, n_chunks=nc),
        grid=(m // tm,),
        in_specs=[
            pl.BlockSpec((tm, D_MODEL), lambda i: (i, 0)),
            pl.BlockSpec(w_gate.shape, const3),
            pl.BlockSpec(w_up.shape, const3),
            pl.BlockSpec(w_out.shape, const3),
            pl.BlockSpec(conv_w.shape, const3),
            pl.BlockSpec((1, nc, SUBLANES, FF_CHUNK), lambda i: (i // tps, 0, 0, 0)),
            pl.BlockSpec((1, D_MODEL), lambda i: (0, 0)),
            pl.BlockSpec((1, D_MODEL), lambda i: (0, 0)),
        ],
        out_specs=[
            pl.BlockSpec((tm, D_MODEL), lambda i: (i, 0)),
            pl.BlockSpec((1, nc, SUBLANES, FF_CHUNK), lambda i: (i // tps, 0, 0, 0)),
        ],
        out_shape=[
            jax.ShapeDtypeStruct((m, D_MODEL), F32),
            jax.ShapeDtypeStruct((nb, nc, SUBLANES, FF_CHUNK), F32),
        ],
        scratch_shapes=[
            pltpu.VMEM((tm, D_MODEL), F32),
            pltpu.VMEM((tm, D_MODEL), BF16),
            pltpu.VMEM((nc, SUBLANES, FF_CHUNK), F32),
        ],
        compiler_params=_params("arbitrary"),
        name="conv_ffn",
    )(x, w_gate, w_up, w_out, conv_w, buf8, ln_g, ln_b)


def _proj_ln_kernel(y_ref, w_ref, x_ref, g_ref, b_ref, out_ref):
    acc = jnp.dot(y_ref[...], w_ref[...], preferred_element_type=F32)
    out_ref[...] = _layer_norm(ALPHA * x_ref[...] + acc, g_ref[...], b_ref[...])


def _proj_ln(y, w, x, ln_g, ln_b, *, tm):
    m, k = y.shape
    tm = min(tm, m)
    return pl.pallas_call(
        _proj_ln_kernel,
        grid=(m // tm,),
        in_specs=[
            pl.BlockSpec((tm, k), lambda i: (i, 0)),
            pl.BlockSpec((k, D_MODEL), lambda i: (0, 0)),
            pl.BlockSpec((tm, D_MODEL), lambda i: (i, 0)),
            pl.BlockSpec((1, D_MODEL), lambda i: (0, 0)),
            pl.BlockSpec((1, D_MODEL), lambda i: (0, 0)),
        ],
        out_specs=pl.BlockSpec((tm, D_MODEL), lambda i: (i, 0)),
        out_shape=jax.ShapeDtypeStruct((m, D_MODEL), F32),
        compiler_params=_params("arbitrary"),
        name="proj_ln",
    )(y, w, x, ln_g, ln_b)


def _rope(x, tab):
    return (x * tab[:, 0:LANES]
            + pltpu.roll(x, ROPE_DIM // 2, axis=1) * tab[:, LANES:2 * LANES]
            + pltpu.roll(x, LANES - ROPE_DIM // 2, axis=1) * tab[:, 2 * LANES:3 * LANES])


def _even_in_kernel(x_ref, w1_ref, w2_ref, gq_ref, gkv_ref, wuq_ref, scw_ref, tab_ref, buf_ref,
                    q_ref, lat_ref, kr_ref, osc_ref, nbuf_ref, carry_ref, *, tiles_per_seq):
    i = pl.program_id(0)
    tm = x_ref.shape[0]

    @pl.when(i % tiles_per_seq == 0)
    def _():
        carry_ref[...] = buf_ref[0]

    xb = x_ref[...].astype(BF16)
    tab = tab_ref[...]
    h1 = jnp.dot(xb, w1_ref[...], preferred_element_type=F32)
    cq = h1[:, :Q_RANK]
    ckv = h1[:, Q_RANK:Q_RANK + KV_RANK]
    kr_raw = h1[:, Q_RANK + KV_RANK:]
    lat_ref[...] = _rms_norm(ckv, gkv_ref[...])
    kr_ref[...] = _rope(kr_raw, tab)
    q_raw = _dot(_rms_norm(cq, gq_ref[...]), wuq_ref[...])
    scale = (NOPE_DIM + ROPE_DIM) ** -0.5
    for h in range(MLA_HEADS):
        sl = slice(h * HEAD_PAD, (h + 1) * HEAD_PAD)
        q_ref[:, sl] = (_rope(q_raw[:, sl], tab) * scale).astype(BF16)

    h2 = jnp.dot(xb, w2_ref[...], preferred_element_type=F32)
    gate_b = h2[:, :SC_WIDTH]
    u_in = h2[:, SC_WIDTH:2 * SC_WIDTH] * h2[:, 2 * SC_WIDTH:]
    u = _causal_conv(u_in, carry_ref[...], scw_ref[...])
    tail = u_in[tm - SUBLANES:]
    carry_ref[...] = tail
    nbuf_ref[0] = tail
    osc_ref[...] = (gate_b * u).astype(BF16)


def _even_in(x, w1, w2, g_q, g_kv, w_uq, sc_w, tab, buf8, *, seq, tm):
    m = x.shape[0]
    nb = m // seq
    tm = min(tm, seq)
    tps = seq // tm
    const2 = lambda i: (0, 0)
    return pl.pallas_call(
        functools.partial(_even_in_kernel, tiles_per_seq=tps),
        grid=(m // tm,),
        in_specs=[
            pl.BlockSpec((tm, D_MODEL), lambda i: (i, 0)),
            pl.BlockSpec(w1.shape, const2),
            pl.BlockSpec(w2.shape, const2),
            pl.BlockSpec(g_q.shape, const2),
            pl.BlockSpec(g_kv.shape, const2),
            pl.BlockSpec(w_uq.shape, const2),
            pl.BlockSpec(sc_w.shape, const2),
            pl.BlockSpec((tm, 3 * LANES), lambda i: (i % tps, 0)),
            pl.BlockSpec((1, SUBLANES, SC_WIDTH), lambda i: (i // tps, 0, 0)),
        ],
        out_specs=[
            pl.BlockSpec((tm, MLA_HEADS * HEAD_PAD), lambda i: (i, 0)),
            pl.BlockSpec((tm, KV_RANK), lambda i: (i, 0)),
            pl.BlockSpec((tm, LANES), lambda i: (i, 0)),
            pl.BlockSpec((tm, SC_WIDTH), lambda i: (i, 0)),
            pl.BlockSpec((1, SUBLANES, SC_WIDTH), lambda i: (i // tps, 0, 0)),
        ],
        out_shape=[
            jax.ShapeDtypeStruct((m, MLA_HEADS * HEAD_PAD), BF16),
            jax.ShapeDtypeStruct((m, KV_RANK), F32),
            jax.ShapeDtypeStruct((m, LANES), F32),
            jax.ShapeDtypeStruct((m, SC_WIDTH), BF16),
            jax.ShapeDtypeStruct((nb, SUBLANES, SC_WIDTH), F32),
        ],
        scratch_shapes=[pltpu.VMEM((SUBLANES, SC_WIDTH), F32)],
        compiler_params=_params("arbitrary"),
        name="even_in",
    )(x, w1, w2, g_q, g_kv, w_uq, sc_w, tab, buf8)


def _kv_up_kernel(lat_ref, kr_ref, wuk_ref, wuv_ref, k_ref, v_ref):
    lb = lat_ref[...].astype(BF16)
    kn = jnp.dot(lb, wuk_ref[...], preferred_element_type=F32)
    kr = kr_ref[...]
    for h in range(MLA_HEADS):
        sl = slice(h * HEAD_PAD, (h + 1) * HEAD_PAD)
        k_ref[:, sl] = (kn[:, sl] + kr).astype(BF16)
    v_ref[...] = jnp.dot(lb, wuv_ref[...], preferred_element_type=F32).astype(BF16)


def _kv_up(lat, kr_pad, w_uk, w_uv, *, tm):
    m = lat.shape[0]
    tm = min(tm, m)
    const2 = lambda i: (0, 0)
    return pl.pallas_call(
        _kv_up_kernel,
        grid=(m // tm,),
        in_specs=[
            pl.BlockSpec((tm, KV_RANK), lambda i: (i, 0)),
            pl.BlockSpec((tm, LANES), lambda i: (i, 0)),
            pl.BlockSpec(w_uk.shape, const2),
            pl.BlockSpec(w_uv.shape, const2),
        ],
        out_specs=[
            pl.BlockSpec((tm, MLA_HEADS * HEAD_PAD), lambda i: (i, 0)),
            pl.BlockSpec((tm, MLA_WIDTH), lambda i: (i, 0)),
        ],
        out_shape=[
            jax.ShapeDtypeStruct((m, MLA_HEADS * HEAD_PAD), BF16),
            jax.ShapeDtypeStruct((m, MLA_WIDTH), BF16),
        ],
        compiler_params=_params("arbitrary"),
        name="kv_up",
    )(lat, kr_pad, w_uk, w_uv)


def _attn_kernel(q_ref, k_ref, v_ref, o_ref, m_ref, l_ref, acc_ref, *, tq, tk, past, t_valid, nk):
    qi = pl.program_id(1)
    kj = pl.program_id(2)
    q_lo = past + qi * tq
    chunk_lo = q_lo // CHUNK
    chunk_hi = (q_lo + tq - 1) // CHUNK
    last = jnp.minimum(((chunk_hi + 1) * CHUNK - 1) // tk, nk - 1)
    k_end = (kj + 1) * tk
    full = jnp.logical_and(k_end <= (chunk_lo + 1) * CHUNK, k_end <= t_valid)

    @pl.when(kj == 0)
    def _():
        m_ref[...] = jnp.full_like(m_ref, NEG_INF)
        l_ref[...] = jnp.zeros_like(l_ref)
        acc_ref[...] = jnp.zeros_like(acc_ref)

    def step(masked):
        if masked:
            qpos = q_lo + lax.broadcasted_iota(jnp.int32, (tq, tk), 0)
            kpos = kj * tk + lax.broadcasted_iota(jnp.int32, (tq, tk), 1)
            visible = jnp.logical_and(kpos // CHUNK <= qpos // CHUNK, kpos < t_valid)
        lane_lo = lax.broadcasted_iota(jnp.int32, (tq, 2 * V_DIM), 1) < V_DIM
        for hp in range(MLA_HEADS // 2):
            vp = v_ref[:, hp * 2 * V_DIM:(hp + 1) * 2 * V_DIM]
            pv = []
            alphas = []
            for h in (2 * hp, 2 * hp + 1):
                sl = slice(h * HEAD_PAD, (h + 1) * HEAD_PAD)
                s = lax.dot_general(q_ref[:, sl], k_ref[:, sl], NT_DIMS, preferred_element_type=F32)
                if masked:
                    s = jnp.where(visible, s, NEG_INF)
                m_prev = m_ref[h]
                m_new = jnp.maximum(m_prev, jnp.max(s, axis=-1, keepdims=True))
                alpha = jnp.exp(m_prev - m_new)
                p = jnp.exp(s - m_new)
                l_ref[h] = alpha * l_ref[h] + jnp.sum(p, axis=-1, keepdims=True)
                m_ref[h] = m_new
                pv.append(jnp.dot(p.astype(BF16), vp, preferred_element_type=F32))
                alphas.append(alpha)
            sl2 = slice(hp * 2 * V_DIM, (hp + 1) * 2 * V_DIM)
            acc_ref[:, sl2] = (jnp.where(lane_lo, alphas[0], alphas[1]) * acc_ref[:, sl2]
                               + jnp.where(lane_lo, pv[0], pv[1]))

    @pl.when(jnp.logical_and(kj <= last, full))
    def _():
        step(False)

    @pl.when(jnp.logical_and(kj <= last, jnp.logical_not(full)))
    def _():
        step(True)

    @pl.when(kj == last)
    def _():
        lane_lo = lax.broadcasted_iota(jnp.int32, (tq, 2 * V_DIM), 1) < V_DIM
        for hp in range(MLA_HEADS // 2):
            sl2 = slice(hp * 2 * V_DIM, (hp + 1) * 2 * V_DIM)
            inv = jnp.where(lane_lo, 1.0 / l_ref[2 * hp], 1.0 / l_ref[2 * hp + 1])
            o_ref[:, sl2] = (acc_ref[:, sl2] * inv).astype(BF16)


def _attention(q, k, v, *, nb, sq, t_pad, t_valid, past, tq, tk):
    nq = sq // tq
    nk = t_pad // tk

    def kv_index(b, qi, kj):
        chunk_hi = (past + qi * tq + tq - 1) // CHUNK
        last = jnp.minimum(((chunk_hi + 1) * CHUNK - 1) // tk, nk - 1)
        return (b * nk + jnp.minimum(kj, last), 0)

    return pl.pallas_call(
        functools.partial(_attn_kernel, tq=tq, tk=tk, past=past, t_valid=t_valid, nk=nk),
        grid=(nb, nq, nk),
        in_specs=[
            pl.BlockSpec((tq, MLA_HEADS * HEAD_PAD), lambda b, qi, kj: (b * nq + qi, 0)),
            pl.BlockSpec((tk, MLA_HEADS * HEAD_PAD), kv_index),
            pl.BlockSpec((tk, MLA_WIDTH), kv_index),
        ],
        out_specs=pl.BlockSpec((tq, MLA_WIDTH), lambda b, qi, kj: (b * nq + qi, 0)),
        out_shape=jax.ShapeDtypeStruct((nb * sq, MLA_WIDTH), BF16),
        scratch_shapes=[
            pltpu.VMEM((MLA_HEADS, tq, 1), F32),
            pltpu.VMEM((MLA_HEADS, tq, 1), F32),
            pltpu.VMEM((tq, MLA_WIDTH), F32),
        ],
        compiler_params=_params("arbitrary", "arbitrary", "arbitrary"),
        name="chunk_attention",
    )(q, k, v)


def _odd_in_kernel(x_ref, wqkv_ref, wz_ref, wab_ref, dcw_ref, ab_ref, buf_ref,
                   q_ref, k_ref, v_ref, z_ref, gb_ref, nbuf_ref, carry_ref, *, tiles_per_seq):
    i = pl.program_id(0)
    tm = x_ref.shape[0]

    @pl.when(i % tiles_per_seq == 0)
    def _():
        carry_ref[...] = buf_ref[0]

    xb = x_ref[...].astype(BF16)
    outs = (q_ref, k_ref, v_ref)
    for part in range(3):
        cols = slice(part * DN_KEY, (part + 1) * DN_KEY)
        pre = jnp.dot(xb, wqkv_ref[:, cols], preferred_element_type=F32)
        y = _causal_conv(pre, carry_ref[:, cols], dcw_ref[:, cols])
        tail = pre[tm - SUBLANES:]
        carry_ref[:, cols] = tail
        nbuf_ref[0, :, cols] = tail
        y = y * jax.nn.sigmoid(y)
        if part == 2:
            outs[part][...] = y
        else:
            post = DN_DK ** -0.5 if part == 0 else 1.0
            for h in range(DN_HEADS):
                sl = slice(h * DN_DK, (h + 1) * DN_DK)
                seg = y[:, sl]
                nrm = lax.rsqrt(jnp.sum(seg * seg, axis=-1, keepdims=True) + NORM_EPS)
                outs[part][:, sl] = seg * nrm * post if part == 0 else seg * nrm

    z_ref[...] = jnp.dot(xb, wz_ref[...], preferred_element_type=F32)
    ab = jnp.dot(xb, wab_ref[...], preferred_element_type=F32)
    lane = lax.broadcasted_iota(jnp.int32, ab.shape, 1)
    g = -jnp.exp(ab_ref[0:1, :]) * _softplus(ab + ab_ref[1:2, :])
    gb_ref[...] = jnp.where(lane < DN_HEADS, g, jax.nn.sigmoid(ab))


def _odd_in(x, w_qkv, w_z, w_ab, dconv_w, ab_const, buf8, *, seq, tm):
    m = x.shape[0]
    nb = m // seq
    tm = min(tm, seq)
    tps = seq // tm
    const2 = lambda i: (0, 0)
    row = lambda i: (i, 0)
    return pl.pallas_call(
        functools.partial(_odd_in_kernel, tiles_per_seq=tps),
        grid=(m // tm,),
        in_specs=[
            pl.BlockSpec((tm, D_MODEL), row),
            pl.BlockSpec(w_qkv.shape, const2),
            pl.BlockSpec(w_z.shape, const2),
            pl.BlockSpec(w_ab.shape, const2),
            pl.BlockSpec(dconv_w.shape, const2),
            pl.BlockSpec(ab_const.shape, const2),
            pl.BlockSpec((1, SUBLANES, DN_QKV), lambda i: (i // tps, 0, 0)),
        ],
        out_specs=[
            pl.BlockSpec((tm, DN_KEY), row),
            pl.BlockSpec((tm, DN_KEY), row),
            pl.BlockSpec((tm, DN_VAL), row),
            pl.BlockSpec((tm, DN_VAL), row),
            pl.BlockSpec((tm, LANES), row),
            pl.BlockSpec((1, SUBLANES, DN_QKV), lambda i: (i // tps, 0, 0)),
        ],
        out_shape=[
            jax.ShapeDtypeStruct((m, DN_KEY), F32),
            jax.ShapeDtypeStruct((m, DN_KEY), F32),
            jax.ShapeDtypeStruct((m, DN_VAL), F32),
            jax.ShapeDtypeStruct((m, DN_VAL), F32),
            jax.ShapeDtypeStruct((m, LANES), F32),
            jax.ShapeDtypeStruct((nb, SUBLANES, DN_QKV), F32),
        ],
        scratch_shapes=[pltpu.VMEM((SUBLANES, DN_QKV), F32)],
        compiler_params=_params("arbitrary"),
        name="odd_in",
    )(x, w_qkv, w_z, w_ab, dconv_w, ab_const, buf8)


def _delta_kernel(q_ref, k_ref, v_ref, z_ref, gb_ref, s0_ref, go_ref, o_ref, sout_ref, state_ref, *, n_chunks):
    n = pl.program_id(1)
    L = CHUNK

    @pl.when(n == 0)
    def _():
        state_ref[...] = s0_ref[0]

    gb = gb_ref[...]
    ri = lax.broadcasted_iota(jnp.int32, (L, L), 0)
    ci = lax.broadcasted_iota(jnp.int32, (L, L), 1)
    tril = ri >= ci
    strict = ri > ci
    eye = (ri == ci).astype(F32)
    g_col = jnp.dot(tril.astype(F32), gb, preferred_element_type=F32, precision=HIGHEST)
    sel = (lax.broadcasted_iota(jnp.int32, (SUBLANES, LANES), 0)
           == lax.broadcasted_iota(jnp.int32, (SUBLANES, LANES), 1)).astype(F32)
    g_row = lax.dot_general(sel, g_col, NT_DIMS, preferred_element_type=F32, precision=HIGHEST)
    go = go_ref[...]

    for h in range(DN_HEADS):
        sl = slice(h * DN_DK, (h + 1) * DN_DK)
        gc = g_col[:, h:h + 1]
        gr = g_row[h:h + 1, :]
        decay = jnp.exp(jnp.where(tril, gc - gr, -jnp.inf))
        e_g = jnp.exp(gc)
        g_last = g_col[L - 1:L, h:h + 1]
        e_tail = jnp.exp(g_last - gc)
        g_tot = jnp.exp(g_last)
        beta = gb[:, DN_HEADS + h:DN_HEADS + h + 1]
        qh = q_ref[:, sl]
        kh = k_ref[:, sl]
        vh = v_ref[:, sl]
        kb = kh * beta
        kk = lax.dot_general(kb, kh, NT_DIMS, preferred_element_type=F32, precision=HIGHEST)
        x = -jnp.where(strict, kk * decay, 0.0)
        t = eye + x
        p = x
        for _ in range(5):
            p = jnp.dot(p, p, preferred_element_type=F32, precision=HIGHEST)
            t = t + jnp.dot(t, p, preferred_element_type=F32, precision=HIGHEST)
        rhs = jnp.concatenate([vh * beta, kb * e_g], axis=1)
        sol = jnp.dot(t, rhs, preferred_element_type=F32, precision=HIGHEST)
        w_val = sol[:, :DN_DV]
        k_cd = sol[:, DN_DV:]
        attn = lax.dot_general(qh, kh, NT_DIMS, preferred_element_type=F32, precision=HIGHEST) * decay
        st = state_ref[h]
        v_new = w_val - jnp.dot(k_cd, st, preferred_element_type=F32, precision=HIGHEST)
        o = (jnp.dot(qh * e_g, st, preferred_element_type=F32, precision=HIGHEST)
             + jnp.dot(attn, v_new, preferred_element_type=F32, precision=HIGHEST))
        state_ref[h] = st * g_tot + lax.dot_general(kh * e_tail, v_new, TN_DIMS,
                                                    preferred_element_type=F32, precision=HIGHEST)
        zh = z_ref[:, sl]
        o_ref[:, sl] = (_rms_norm(o, go) * (zh * jax.nn.sigmoid(zh))).astype(BF16)

    @pl.when(n == n_chunks - 1)
    def _():
        sout_ref[0] = state_ref[...]


def _delta_rule(q, k, v, z, gb, s0, g_o, *, seq):
    m = q.shape[0]
    nb = m // seq
    nc = seq // CHUNK
    row = lambda b, n: (b * nc + n, 0)
    st_spec = pl.BlockSpec((1, DN_HEADS, DN_DK, DN_DV), lambda b, n: (b, 0, 0, 0))
    return pl.pallas_call(
        functools.partial(_delta_kernel, n_chunks=nc),
        grid=(nb, nc),
        in_specs=[
            pl.BlockSpec((CHUNK, DN_KEY), row),
            pl.BlockSpec((CHUNK, DN_KEY), row),
            pl.BlockSpec((CHUNK, DN_VAL), row),
            pl.BlockSpec((CHUNK, DN_VAL), row),
            pl.BlockSpec((CHUNK, LANES), row),
            st_spec,
            pl.BlockSpec((1, DN_DV), lambda b, n: (0, 0)),
        ],
        out_specs=[pl.BlockSpec((CHUNK, DN_VAL), row), st_spec],
        out_shape=[
            jax.ShapeDtypeStruct((m, DN_VAL), BF16),
            jax.ShapeDtypeStruct((nb, DN_HEADS, DN_DK, DN_DV), F32),
        ],
        scratch_shapes=[pltpu.VMEM((DN_HEADS, DN_DK, DN_DV), F32)],
        compiler_params=_params("arbitrary", "arbitrary"),
        name="gated_delta",
    )(q, k, v, z, gb, s0, g_o)


def _rope_table(pos):
    half = ROPE_DIM // 2
    inv = ROPE_THETA ** (-jnp.arange(half, dtype=F32) / half)
    ang = pos.astype(F32)[:, None] * inv[None, :]
    cos, sin = jnp.cos(ang), jnp.sin(ang)
    s = pos.shape[0]
    ones = jnp.ones((s, NOPE_DIM), F32)
    z16 = jnp.zeros((s, half), F32)
    z32 = jnp.zeros((s, HEAD_PAD - NOPE_DIM - ROPE_DIM), F32)
    z64 = jnp.zeros((s, NOPE_DIM), F32)
    c = jnp.concatenate([ones, cos, cos, z32], -1)
    sp = jnp.concatenate([z64, z16, sin, z32], -1)
    sm = jnp.concatenate([z64, -sin, z16, z32], -1)
    return jnp.concatenate([c, sp, sm], -1)


def _pad_heads(w, per_head, keep):
    kdim = w.shape[0]
    w = w.reshape(kdim, MLA_HEADS, per_head)[:, :, :keep]
    w = jnp.pad(w, ((0, 0), (0, 0), (0, HEAD_PAD - keep)))
    return w.reshape(kdim, MLA_HEADS * HEAD_PAD)


def _prep_even(w_in, w_uq, w_ukv):
    c1 = Q_RANK + KV_RANK
    w_kr = jnp.pad(w_in[:, c1:c1 + ROPE_DIM], ((0, 0), (NOPE_DIM, HEAD_PAD - NOPE_DIM - ROPE_DIM)))
    w1 = jnp.concatenate([w_in[:, :c1], w_kr], -1).astype(BF16)
    w2 = w_in[:, c1 + ROPE_DIM:].astype(BF16)
    wuq = _pad_heads(w_uq, NOPE_DIM + ROPE_DIM, NOPE_DIM + ROPE_DIM).astype(BF16)
    wuk = _pad_heads(w_ukv, NOPE_DIM + V_DIM, NOPE_DIM).astype(BF16)
    wuv = w_ukv.reshape(KV_RANK, MLA_HEADS, NOPE_DIM + V_DIM)[:, :, NOPE_DIM:].reshape(KV_RANK, MLA_WIDTH)
    return w1, w2, wuq, wuk, wuv.astype(BF16)


def _prep_odd(w_in, a_log, dt_bias):
    w_qkv = w_in[:, :DN_QKV].astype(BF16)
    w_z = w_in[:, DN_QKV:DN_QKV + DN_VAL].astype(BF16)
    w_ab = jnp.pad(w_in[:, DN_QKV + DN_VAL:], ((0, 0), (0, LANES - 2 * DN_HEADS))).astype(BF16)
    ab_const = jnp.zeros((SUBLANES, LANES), F32)
    ab_const = ab_const.at[0, :DN_HEADS].set(a_log.astype(F32)).at[1, :DN_HEADS].set(dt_bias.astype(F32))
    return w_qkv, w_z, w_ab, ab_const


def _prep_ffn(w_in, conv_w, w_out):
    nc = D_FF // FF_CHUNK
    wg = w_in[:, :D_FF].reshape(D_MODEL, nc, FF_CHUNK).transpose(1, 0, 2).astype(BF16)
    wu = w_in[:, D_FF:].reshape(D_MODEL, nc, FF_CHUNK).transpose(1, 0, 2).astype(BF16)
    wo = w_out.reshape(nc, FF_CHUNK, D_MODEL).astype(BF16)
    cw = conv_w.reshape(FF_K, nc, FF_CHUNK).transpose(1, 0, 2)
    return wg, wu, wo, cw


def _pad_rows8(buf):
    return jnp.pad(buf.astype(F32), ((0, 0), (SUBLANES - buf.shape[1], 0), (0, 0)))


def _trunk(x, past_lat, past_kr, sc_buf, dconv_buf, delta_s, ff_buf, w, *, tm, tq, tk):
    (even_w, odd_w, ffn_w, w_o_e, w_o_o, g_qnorm, g_kvnorm, sc_w, dconv_w, g_onorm,
     ln_mix_g, ln_mix_b, ln_ff_g, ln_ff_b) = w
    nb, seq, _ = x.shape
    past = 0 if past_lat is None else past_lat.shape[2]
    m = nb * seq
    x = x.reshape(m, D_MODEL)
    tab = _rope_table(past + jnp.arange(seq, dtype=jnp.int32))
    t_valid = past + seq
    tq = min(tq, seq)
    tk = min(tk, t_valid)
    t_pad = -(-t_valid // tk) * tk
    nc = D_FF // FF_CHUNK
    lats, krs, scs, dcs, dss, ffs = [], [], [], [], [], []
    for i in range(DEPTH):
        row = lambda a: a[i].reshape(1, -1)
        if i % 2 == 0:
            e = i // 2
            w1, w2, wuq, wuk, wuv = even_w[e]
            q, lat, kr_pad, o_sc, sc_new = _even_in(
                x, w1, w2, g_qnorm[e].reshape(1, -1), g_kvnorm[e].reshape(1, -1), wuq, sc_w[e], tab,
                _pad_rows8(sc_buf[e]), seq=seq, tm=tm)
            lat_all = lat.reshape(nb, seq, KV_RANK)
            kr_all = kr_pad.reshape(nb, seq, LANES)
            if past:
                kr_past = jnp.pad(past_kr[e].astype(F32),
                                  ((0, 0), (0, 0), (NOPE_DIM, HEAD_PAD - NOPE_DIM - ROPE_DIM)))
                lat_all = jnp.concatenate([past_lat[e].astype(F32), lat_all], 1)
                kr_all = jnp.concatenate([kr_past, kr_all], 1)
            if t_pad != t_valid:
                lat_all = jnp.pad(lat_all, ((0, 0), (0, t_pad - t_valid), (0, 0)))
                kr_all = jnp.pad(kr_all, ((0, 0), (0, t_pad - t_valid), (0, 0)))
            k, v = _kv_up(lat_all.reshape(nb * t_pad, KV_RANK), kr_all.reshape(nb * t_pad, LANES),
                          wuk, wuv, tm=tm)
            o_att = _attention(q, k, v, nb=nb, sq=seq, t_pad=t_pad, t_valid=t_valid, past=past, tq=tq, tk=tk)
            y_in = jnp.concatenate([o_att, o_sc], -1)
            x = _proj_ln(y_in, w_o_e[e], x, row(ln_mix_g), row(ln_mix_b), tm=tm)
            lats.append(lat.reshape(nb, seq, KV_RANK))
            krs.append(kr_pad.reshape(nb, seq, LANES)[:, :, NOPE_DIM:NOPE_DIM + ROPE_DIM])
            scs.append(sc_new[:, SUBLANES - (SC_K - 1):])
        else:
            o = i // 2
            w_qkv, w_z, w_ab, ab_const = odd_w[o]
            q, k, v, z, gb, dc_new = _odd_in(x, w_qkv, w_z, w_ab, dconv_w[o], ab_const,
                                             _pad_rows8(dconv_buf[o]), seq=seq, tm=min(tm, 256))
            og, s_new = _delta_rule(q, k, v, z, gb, delta_s[o].astype(F32), g_onorm[o].reshape(1, -1), seq=seq)
            x = _proj_ln(og, w_o_o[o], x, row(ln_mix_g), row(ln_mix_b), tm=tm)
            dcs.append(dc_new[:, SUBLANES - (DN_CONV - 1):])
            dss.append(s_new)
        wg, wu, wo, cw = ffn_w[i]
        buf8 = _pad_rows8(ff_buf[i]).reshape(nb, SUBLANES, nc, FF_CHUNK).transpose(0, 2, 1, 3)
        x, ff_new = _conv_ffn(x, buf8, wg, wu, wo, cw, row(ln_ff_g), row(ln_ff_b), seq=seq, tm=tm)
        ff_new = ff_new.transpose(0, 2, 1, 3).reshape(nb, SUBLANES, D_FF)
        ffs.append(ff_new[:, SUBLANES - (FF_K - 1):])
    return (x.reshape(nb, seq, D_MODEL), jnp.stack(lats), jnp.stack(krs), jnp.stack(scs), jnp.stack(dcs),
            jnp.stack(dss), jnp.stack(ffs))


def _prepare(w_in_e, w_uq, w_ukv, w_o_e, w_in_o, a_log, dt_bias, w_o_o, w_ff_in, ffconv_w, w_ff_out):
    even_w = [_prep_even(w_in_e[e], w_uq[e], w_ukv[e]) for e in range(N_EVEN)]
    odd_w = [_prep_odd(w_in_o[o], a_log[o], dt_bias[o]) for o in range(N_ODD)]
    ffn_w = [_prep_ffn(w_ff_in[i], ffconv_w[i], w_ff_out[i]) for i in range(DEPTH)]
    return even_w, odd_w, ffn_w, w_o_e.astype(BF16), w_o_o.astype(BF16)


def kernel(x_prompt, x_sample, cache_mla_latent, cache_mla_krope, state_sconv, state_dconv, state_delta, state_ffconv, w_in_e, g_qnorm, g_kvnorm, w_uq, w_ukv, sc_w, w_o_e, w_in_o, dconv_w, a_log, dt_bias, g_onorm, w_o_o, w_ff_in, ffconv_w, w_ff_out, ln_mix_g, ln_mix_b, ln_ff_g, ln_ff_b):
    even_w, odd_w, ffn_w, w_o_e_b, w_o_o_b = _prepare(
        w_in_e, w_uq, w_ukv, w_o_e, w_in_o, a_log, dt_bias, w_o_o, w_ff_in, ffconv_w, w_ff_out)
    w = (even_w, odd_w, ffn_w, w_o_e_b, w_o_o_b, g_qnorm, g_kvnorm, sc_w, dconv_w, g_onorm,
         ln_mix_g, ln_mix_b, ln_ff_g, ln_ff_b)
    bp = x_prompt.shape[0]
    f32 = x_prompt.dtype
    y_p, p_lat, p_kr, p_sc, p_dc, p_ds, p_ff = _trunk(
        x_prompt, None, None,
        jnp.zeros((N_EVEN, bp, SC_K - 1, SC_WIDTH), f32),
        jnp.zeros((N_ODD, bp, DN_CONV - 1, DN_QKV), f32),
        jnp.zeros((N_ODD, bp, DN_HEADS, DN_DK, DN_DV), f32),
        jnp.zeros((DEPTH, bp, FF_K - 1, D_FF), f32),
        w, tm=512, tq=512, tk=512)
    y_s, s_lat, s_kr, s_sc, s_dc, s_ds, s_ff = _trunk(
        x_sample, cache_mla_latent, cache_mla_krope, state_sconv, state_dconv, state_delta, state_ffconv,
        w, tm=512, tq=64, tk=256)
    return (y_p, y_s, p_lat, p_kr, p_sc, p_dc, p_ds, p_ff, s_lat, s_kr, s_sc, s_dc, s_ds, s_ff)
```

```python
import functools
import math

import jax
import jax.numpy as jnp
from jax import lax
from jax.experimental import pallas as pl
from jax.experimental.pallas import tpu as pltpu

D_MODEL = 1024
DEPTH = 4
CHUNK = 64
N_EVEN = (DEPTH + 1) // 2
N_ODD = DEPTH // 2
MLA_HEADS = 8
Q_RANK = 384
KV_RANK = 256
NOPE_DIM = 64
ROPE_DIM = 32
V_DIM = 64
ROPE_THETA = 10000.0
MLA_WIDTH = MLA_HEADS * V_DIM
SC_WIDTH = 512
SC_K = 3
DN_HEADS = 8
DN_DK = 128
DN_DV = 128
DN_CONV = 4
DN_KEY = DN_HEADS * DN_DK
DN_VAL = DN_HEADS * DN_DV
DN_QKV = 2 * DN_KEY + DN_VAL
D_FF = 2816
FF_K = 3
ALPHA = (2 * DEPTH) ** 0.25
NORM_EPS = 1e-6
NEG_INF = -1e30
LOG2_E = math.log2(math.e)

F32 = jnp.float32
BF16 = jnp.bfloat16
HIGHEST = lax.Precision.HIGHEST

LANES = 128
SUBLANES = 8
HEAD_PAD = 128
FF_CHUNK = 256
DELTA_BASE_BLOCK = 8
VMEM_LIMIT = 56 * 1024 * 1024

NT_DIMS = (((1,), (1,)), ((), ()))
TN_DIMS = (((0,), (0,)), ((), ()))


def _params(*sem):
    return pltpu.CompilerParams(dimension_semantics=sem, vmem_limit_bytes=VMEM_LIMIT)


def _dot(a, b):
    return jnp.dot(a.astype(BF16), b.astype(BF16), preferred_element_type=F32)


def _layer_norm(r, g, b):
    mu = jnp.mean(r, axis=-1, keepdims=True)
    d = r - mu
    var = jnp.mean(d * d, axis=-1, keepdims=True)
    return d * lax.rsqrt(var + NORM_EPS) * g + b


def _rms_norm(x, g):
    return x * lax.rsqrt(jnp.mean(x * x, axis=-1, keepdims=True) + NORM_EPS) * g


def _softplus(x):
    return jnp.maximum(x, 0.0) + jnp.log1p(jnp.exp(-jnp.abs(x)))


def _conv_taps(x, w):
    k_taps = w.shape[0]
    y = x * w[k_taps - 1:k_taps, :]
    for k in range(1, k_taps):
        y = y + pltpu.roll(x, k, axis=0) * w[k_taps - 1 - k:k_taps - k, :]
    return y


def _causal_conv(x, prev8, w):
    y = _conv_taps(x, w)
    head = _conv_taps(jnp.concatenate([prev8, x[:SUBLANES]], axis=0), w)
    return jnp.concatenate([head[SUBLANES:], y[SUBLANES:]], axis=0)


def _ffn_kernel(x_ref, wg_ref, wu_ref, wo_ref, cw_ref, buf_ref, g_ref, b_ref,
                out_ref, nbuf_ref, acc_ref, xb_ref, carry_ref, *, tiles_per_seq, n_chunks):
    i = pl.program_id(0)
    tm = x_ref.shape[0]
    xb_ref[...] = x_ref[...].astype(BF16)
    acc_ref[...] = jnp.zeros_like(acc_ref)

    @pl.when(i % tiles_per_seq == 0)
    def _():
        carry_ref[...] = buf_ref[0]

    def body(c, _):
        xb = xb_ref[...]
        gate = jnp.dot(xb, wg_ref[c], preferred_element_type=F32)
        up = jnp.dot(xb, wu_ref[c], preferred_element_type=F32)
        y = _causal_conv(gate, carry_ref[c], cw_ref[c])
        tail = gate[tm - SUBLANES:]
        carry_ref[c] = tail
        nbuf_ref[0, c] = tail
        act = (y * jax.nn.sigmoid(y) * up).astype(BF16)
        acc_ref[...] += jnp.dot(act, wo_ref[c], preferred_element_type=F32)
        return 0

    lax.fori_loop(0, n_chunks, body, 0)
    out_ref[...] = _layer_norm(ALPHA * x_ref[...] + acc_ref[...], g_ref[...], b_ref[...])


def _conv_ffn(x, buf8, w_gate, w_up, w_out, conv_w, ln_g, ln_b, *, seq, tm):
    m = x.shape[0]
    nb = m // seq
    tm = min(tm, seq)
    tps = seq // tm
    nc = w_gate.shape[0]
    const3 = lambda i: (0, 0, 0)
    return pl.pallas_call(
        functools.partial(_ffn_kernel, tiles_per_seq=tps, n_chunks=nc),
        grid=(m // tm,),
        in_specs=[
            pl.BlockSpec((tm, D_MODEL), lambda i: (i, 0)),
            pl.BlockSpec(w_gate.shape, const3),
            pl.BlockSpec(w_up.shape, const3),
            pl.BlockSpec(w_out.shape, const3),
            pl.BlockSpec(conv_w.shape, const3),
            pl.BlockSpec((1, nc, SUBLANES, FF_CHUNK), lambda i: (i // tps, 0, 0, 0)),
            pl.BlockSpec((1, D_MODEL), lambda i: (0, 0)),
            pl.BlockSpec((1, D_MODEL), lambda i: (0, 0)),
        ],
        out_specs=[
            pl.BlockSpec((tm, D_MODEL), lambda i: (i, 0)),
            pl.BlockSpec((1, nc, SUBLANES, FF_CHUNK), lambda i: (i // tps, 0, 0, 0)),
        ],
        out_shape=[
            jax.ShapeDtypeStruct((m, D_MODEL), F32),
            jax.ShapeDtypeStruct((nb, nc, SUBLANES, FF_CHUNK), F32),
        ],
        scratch_shapes=[
            pltpu.VMEM((tm, D_MODEL), F32),
            pltpu.VMEM((tm, D_MODEL), BF16),
            pltpu.VMEM((nc, SUBLANES, FF_CHUNK), F32),
        ],
        compiler_params=_params("arbitrary"),
        name="conv_ffn",
    )(x, w_gate, w_up, w_out, conv_w, buf8, ln_g, ln_b)


def _proj_ln_kernel(y_ref, w_ref, x_ref, g_ref, b_ref, out_ref):
    acc = jnp.dot(y_ref[...], w_ref[...], preferred_element_type=F32)
    out_ref[...] = _layer_norm(ALPHA * x_ref[...] + acc, g_ref[...], b_ref[...])


def _proj_ln(y, w, x, ln_g, ln_b, *, tm):
    m, k = y.shape
    tm = min(tm, m)
    return pl.pallas_call(
        _proj_ln_kernel,
        grid=(m // tm,),
        in_specs=[
            pl.BlockSpec((tm, k), lambda i: (i, 0)),
            pl.BlockSpec((k, D_MODEL), lambda i: (0, 0)),
            pl.BlockSpec((tm, D_MODEL), lambda i: (i, 0)),
            pl.BlockSpec((1, D_MODEL), lambda i: (0, 0)),
            pl.BlockSpec((1, D_MODEL), lambda i: (0, 0)),
        ],
        out_specs=pl.BlockSpec((tm, D_MODEL), lambda i: (i, 0)),
        out_shape=jax.ShapeDtypeStruct((m, D_MODEL), F32),
        compiler_params=_params("arbitrary"),
        name="proj_ln",
    )(y, w, x, ln_g, ln_b)


def _rope(x, tab):
    return (x * tab[:, 0:LANES]
            + pltpu.roll(x, ROPE_DIM // 2, axis=1) * tab[:, LANES:2 * LANES]
            + pltpu.roll(x, LANES - ROPE_DIM // 2, axis=1) * tab[:, 2 * LANES:3 * LANES])


def _even_in_kernel(x_ref, w1_ref, w2_ref, gq_ref, gkv_ref, wuq_ref, scw_ref, tab_ref, buf_ref,
                    q_ref, lat_ref, kr_ref, osc_ref, nbuf_ref, carry_ref, *, tiles_per_seq):
    i = pl.program_id(0)
    tm = x_ref.shape[0]

    @pl.when(i % tiles_per_seq == 0)
    def _():
        carry_ref[...] = buf_ref[0]

    xb = x_ref[...].astype(BF16)
    tab = tab_ref[...]
    h1 = jnp.dot(xb, w1_ref[...], preferred_element_type=F32)
    cq = h1[:, :Q_RANK]
    ckv = h1[:, Q_RANK:Q_RANK + KV_RANK]
    kr_raw = h1[:, Q_RANK + KV_RANK:]
    lat_ref[...] = _rms_norm(ckv, gkv_ref[...])
    kr_ref[...] = _rope(kr_raw, tab)
    q_raw = _dot(_rms_norm(cq, gq_ref[...]), wuq_ref[...])
    scale = (NOPE_DIM + ROPE_DIM) ** -0.5 * LOG2_E
    for h in range(MLA_HEADS):
        sl = slice(h * HEAD_PAD, (h + 1) * HEAD_PAD)
        q_ref[:, sl] = (_rope(q_raw[:, sl], tab) * scale).astype(BF16)

    h2 = jnp.dot(xb, w2_ref[...], preferred_element_type=F32)
    gate_b = h2[:, :SC_WIDTH]
    u_in = h2[:, SC_WIDTH:2 * SC_WIDTH] * h2[:, 2 * SC_WIDTH:]
    u = _causal_conv(u_in, carry_ref[...], scw_ref[...])
    tail = u_in[tm - SUBLANES:]
    carry_ref[...] = tail
    nbuf_ref[0] = tail
    osc_ref[...] = (gate_b * u).astype(BF16)


def _even_in(x, w1, w2, g_q, g_kv, w_uq, sc_w, tab, buf8, *, seq, tm):
    m = x.shape[0]
    nb = m // seq
    tm = min(tm, seq)
    tps = seq // tm
    const2 = lambda i: (0, 0)
    return pl.pallas_call(
        functools.partial(_even_in_kernel, tiles_per_seq=tps),
        grid=(m // tm,),
        in_specs=[
            pl.BlockSpec((tm, D_MODEL), lambda i: (i, 0)),
            pl.BlockSpec(w1.shape, const2),
            pl.BlockSpec(w2.shape, const2),
            pl.BlockSpec(g_q.shape, const2),
            pl.BlockSpec(g_kv.shape, const2),
            pl.BlockSpec(w_uq.shape, const2),
            pl.BlockSpec(sc_w.shape, const2),
            pl.BlockSpec((tm, 3 * LANES), lambda i: (i % tps, 0)),
            pl.BlockSpec((1, SUBLANES, SC_WIDTH), lambda i: (i // tps, 0, 0)),
        ],
        out_specs=[
            pl.BlockSpec((tm, MLA_HEADS * HEAD_PAD), lambda i: (i, 0)),
            pl.BlockSpec((tm, KV_RANK), lambda i: (i, 0)),
            pl.BlockSpec((tm, LANES), lambda i: (i, 0)),
            pl.BlockSpec((tm, SC_WIDTH), lambda i: (i, 0)),
            pl.BlockSpec((1, SUBLANES, SC_WIDTH), lambda i: (i // tps, 0, 0)),
        ],
        out_shape=[
            jax.ShapeDtypeStruct((m, MLA_HEADS * HEAD_PAD), BF16),
            jax.ShapeDtypeStruct((m, KV_RANK), F32),
            jax.ShapeDtypeStruct((m, LANES), F32),
            jax.ShapeDtypeStruct((m, SC_WIDTH), BF16),
            jax.ShapeDtypeStruct((nb, SUBLANES, SC_WIDTH), F32),
        ],
        scratch_shapes=[pltpu.VMEM((SUBLANES, SC_WIDTH), F32)],
        compiler_params=_params("arbitrary"),
        name="even_in",
    )(x, w1, w2, g_q, g_kv, w_uq, sc_w, tab, buf8)


def _kv_up_kernel(lat_ref, kr_ref, wuk_ref, wuv_ref, k_ref, v_ref):
    lb = lat_ref[...].astype(BF16)
    kn = jnp.dot(lb, wuk_ref[...], preferred_element_type=F32)
    kr = kr_ref[...]
    for h in range(MLA_HEADS):
        sl = slice(h * HEAD_PAD, (h + 1) * HEAD_PAD)
        k_ref[:, sl] = (kn[:, sl] + kr).astype(BF16)
    v_ref[...] = jnp.dot(lb, wuv_ref[...], preferred_element_type=F32).astype(BF16)


def _kv_up(lat, kr_pad, w_uk, w_uv, *, tm):
    m = lat.shape[0]
    tm = min(tm, m)
    const2 = lambda i: (0, 0)
    return pl.pallas_call(
        _kv_up_kernel,
        grid=(m // tm,),
        in_specs=[
            pl.BlockSpec((tm, KV_RANK), lambda i: (i, 0)),
            pl.BlockSpec((tm, LANES), lambda i: (i, 0)),
            pl.BlockSpec(w_uk.shape, const2),
            pl.BlockSpec(w_uv.shape, const2),
        ],
        out_specs=[
            pl.BlockSpec((tm, MLA_HEADS * HEAD_PAD), lambda i: (i, 0)),
            pl.BlockSpec((tm, MLA_WIDTH), lambda i: (i, 0)),
        ],
        out_shape=[
            jax.ShapeDtypeStruct((m, MLA_HEADS * HEAD_PAD), BF16),
            jax.ShapeDtypeStruct((m, MLA_WIDTH), BF16),
        ],
        compiler_params=_params("arbitrary"),
        name="kv_up",
    )(lat, kr_pad, w_uk, w_uv)


def _attn_kernel(q_ref, k_ref, v_ref, o_ref, m_ref, l_ref, acc_ref, *, tq, tk, past, t_valid, nk):
    qi = pl.program_id(1)
    kj = pl.program_id(2)
    q_lo = past + qi * tq
    chunk_lo = q_lo // CHUNK
    chunk_hi = (q_lo + tq - 1) // CHUNK
    last = jnp.minimum(((chunk_hi + 1) * CHUNK - 1) // tk, nk - 1)
    k_end = (kj + 1) * tk
    full = jnp.logical_and(k_end <= (chunk_lo + 1) * CHUNK, k_end <= t_valid)

    @pl.when(kj == 0)
    def _():
        m_ref[...] = jnp.full_like(m_ref, NEG_INF)
        l_ref[...] = jnp.zeros_like(l_ref)
        acc_ref[...] = jnp.zeros_like(acc_ref)

    def step(masked):
        if masked:
            qpos = q_lo + lax.broadcasted_iota(jnp.int32, (tq, tk), 0)
            kpos = kj * tk + lax.broadcasted_iota(jnp.int32, (tq, tk), 1)
            visible = jnp.logical_and(kpos // CHUNK <= qpos // CHUNK, kpos < t_valid)
        lane_lo = lax.broadcasted_iota(jnp.int32, (tq, 2 * V_DIM), 1) < V_DIM
        heads = range(MLA_HEADS)
        hsl = [slice(h * HEAD_PAD, (h + 1) * HEAD_PAD) for h in heads]
        psl = [slice(hp * 2 * V_DIM, (hp + 1) * 2 * V_DIM) for hp in range(MLA_HEADS // 2)]
        s = [lax.dot_general(q_ref[:, hsl[h]], k_ref[:, hsl[h]], NT_DIMS, preferred_element_type=F32)
             for h in heads]
        if masked:
            s = [jnp.where(visible, s[h], NEG_INF) for h in heads]
        m_prev = [m_ref[h] for h in heads]
        m_new = [jnp.maximum(m_prev[h], jnp.max(s[h], axis=-1, keepdims=True)) for h in heads]
        alpha = [jnp.exp2(m_prev[h] - m_new[h]) for h in heads]
        p = [jnp.exp2(s[h] - m_new[h]) for h in heads]
        for h in heads:
            l_ref[h] = alpha[h] * l_ref[h] + jnp.sum(p[h], axis=-1, keepdims=True)
            m_ref[h] = m_new[h]
        pv = [jnp.dot(p[h].astype(BF16), v_ref[:, psl[h // 2]], preferred_element_type=F32) for h in heads]
        for hp in range(MLA_HEADS // 2):
            acc_ref[:, psl[hp]] = (jnp.where(lane_lo, alpha[2 * hp], alpha[2 * hp + 1]) * acc_ref[:, psl[hp]]
                                   + jnp.where(lane_lo, pv[2 * hp], pv[2 * hp + 1]))

    @pl.when(jnp.logical_and(kj <= last, full))
    def _():
        step(False)

    @pl.when(jnp.logical_and(kj <= last, jnp.logical_not(full)))
    def _():
        step(True)

    @pl.when(kj == last)
    def _():
        lane_lo = lax.broadcasted_iota(jnp.int32, (tq, 2 * V_DIM), 1) < V_DIM
        for hp in range(MLA_HEADS // 2):
            sl2 = slice(hp * 2 * V_DIM, (hp + 1) * 2 * V_DIM)
            inv = jnp.where(lane_lo, 1.0 / l_ref[2 * hp], 1.0 / l_ref[2 * hp + 1])
            o_ref[:, sl2] = (acc_ref[:, sl2] * inv).astype(BF16)


def _attention(q, k, v, *, nb, sq, t_pad, t_valid, past, tq, tk):
    nq = sq // tq
    nk = t_pad // tk

    def kv_index(b, qi, kj):
        chunk_hi = (past + qi * tq + tq - 1) // CHUNK
        last = jnp.minimum(((chunk_hi + 1) * CHUNK - 1) // tk, nk - 1)
        return (b * nk + jnp.minimum(kj, last), 0)

    return pl.pallas_call(
        functools.partial(_attn_kernel, tq=tq, tk=tk, past=past, t_valid=t_valid, nk=nk),
        grid=(nb, nq, nk),
        in_specs=[
            pl.BlockSpec((tq, MLA_HEADS * HEAD_PAD), lambda b, qi, kj: (b * nq + qi, 0)),
            pl.BlockSpec((tk, MLA_HEADS * HEAD_PAD), kv_index),
            pl.BlockSpec((tk, MLA_WIDTH), kv_index),
        ],
        out_specs=pl.BlockSpec((tq, MLA_WIDTH), lambda b, qi, kj: (b * nq + qi, 0)),
        out_shape=jax.ShapeDtypeStruct((nb * sq, MLA_WIDTH), BF16),
        scratch_shapes=[
            pltpu.VMEM((MLA_HEADS, tq, 1), F32),
            pltpu.VMEM((MLA_HEADS, tq, 1), F32),
            pltpu.VMEM((tq, MLA_WIDTH), F32),
        ],
        compiler_params=_params("arbitrary", "arbitrary", "arbitrary"),
        name="chunk_attention",
    )(q, k, v)


def _odd_in_kernel(x_ref, wqkv_ref, wz_ref, wab_ref, dcw_ref, ab_ref, buf_ref,
                   q_ref, k_ref, v_ref, z_ref, gb_ref, nbuf_ref, carry_ref, *, tiles_per_seq):
    i = pl.program_id(0)
    tm = x_ref.shape[0]

    @pl.when(i % tiles_per_seq == 0)
    def _():
        carry_ref[...] = buf_ref[0]

    xb = x_ref[...].astype(BF16)
    outs = (q_ref, k_ref, v_ref)
    for part in range(3):
        cols = slice(part * DN_KEY, (part + 1) * DN_KEY)
        pre = jnp.dot(xb, wqkv_ref[:, cols], preferred_element_type=F32)
        y = _causal_conv(pre, carry_ref[:, cols], dcw_ref[:, cols])
        tail = pre[tm - SUBLANES:]
        carry_ref[:, cols] = tail
        nbuf_ref[0, :, cols] = tail
        y = y * jax.nn.sigmoid(y)
        if part == 2:
            outs[part][...] = y
        else:
            post = DN_DK ** -0.5 if part == 0 else 1.0
            for h in range(DN_HEADS):
                sl = slice(h * DN_DK, (h + 1) * DN_DK)
                seg = y[:, sl]
                nrm = lax.rsqrt(jnp.sum(seg * seg, axis=-1, keepdims=True) + NORM_EPS)
                outs[part][:, sl] = seg * nrm * post if part == 0 else seg * nrm

    z_ref[...] = jnp.dot(xb, wz_ref[...], preferred_element_type=F32)
    ab = jnp.dot(xb, wab_ref[...], preferred_element_type=F32)
    lane = lax.broadcasted_iota(jnp.int32, ab.shape, 1)
    g = -jnp.exp(ab_ref[0:1, :]) * _softplus(ab + ab_ref[1:2, :])
    gb_ref[...] = jnp.where(lane < DN_HEADS, g, jax.nn.sigmoid(ab))


def _odd_in(x, w_qkv, w_z, w_ab, dconv_w, ab_const, buf8, *, seq, tm):
    m = x.shape[0]
    nb = m // seq
    tm = min(tm, seq)
    tps = seq // tm
    const2 = lambda i: (0, 0)
    row = lambda i: (i, 0)
    return pl.pallas_call(
        functools.partial(_odd_in_kernel, tiles_per_seq=tps),
        grid=(m // tm,),
        in_specs=[
            pl.BlockSpec((tm, D_MODEL), row),
            pl.BlockSpec(w_qkv.shape, const2),
            pl.BlockSpec(w_z.shape, const2),
            pl.BlockSpec(w_ab.shape, const2),
            pl.BlockSpec(dconv_w.shape, const2),
            pl.BlockSpec(ab_const.shape, const2),
            pl.BlockSpec((1, SUBLANES, DN_QKV), lambda i: (i // tps, 0, 0)),
        ],
        out_specs=[
            pl.BlockSpec((tm, DN_KEY), row),
            pl.BlockSpec((tm, DN_KEY), row),
            pl.BlockSpec((tm, DN_VAL), row),
            pl.BlockSpec((tm, DN_VAL), row),
            pl.BlockSpec((tm, LANES), row),
            pl.BlockSpec((1, SUBLANES, DN_QKV), lambda i: (i // tps, 0, 0)),
        ],
        out_shape=[
            jax.ShapeDtypeStruct((m, DN_KEY), F32),
            jax.ShapeDtypeStruct((m, DN_KEY), F32),
            jax.ShapeDtypeStruct((m, DN_VAL), F32),
            jax.ShapeDtypeStruct((m, DN_VAL), F32),
            jax.ShapeDtypeStruct((m, LANES), F32),
            jax.ShapeDtypeStruct((nb, SUBLANES, DN_QKV), F32),
        ],
        scratch_shapes=[pltpu.VMEM((SUBLANES, DN_QKV), F32)],
        compiler_params=_params("arbitrary"),
        name="odd_in",
    )(x, w_qkv, w_z, w_ab, dconv_w, ab_const, buf8)


def _delta_kernel(q_ref, k_ref, v_ref, z_ref, gb_ref, s0_ref, go_ref, o_ref, sout_ref, state_ref, *, n_steps):
    n = pl.program_id(1)
    rows = q_ref.shape[0]
    n_sub = rows // CHUNK

    @pl.when(n == 0)
    def _():
        state_ref[...] = s0_ref[0]

    gb = gb_ref[...]
    ri = lax.broadcasted_iota(jnp.int32, (rows, rows), 0)
    ci = lax.broadcasted_iota(jnp.int32, (rows, rows), 1)

    def blk(size):
        return (ri // size) == (ci // size)

    same = blk(CHUNK)
    lower = ri > ci
    tril = jnp.logical_and(same, ri >= ci)
    strict = jnp.logical_and(same, lower)
    eye = (ri == ci).astype(F32)
    base = DELTA_BASE_BLOCK
    levels = []
    size = base
    while size < CHUNK:
        levels.append(jnp.logical_and(jnp.logical_and(blk(2 * size), jnp.logical_not(blk(size))), lower))
        size *= 2
    in_base = jnp.logical_and(blk(base), lower)
    g_col = jnp.dot(tril.astype(F32), gb, preferred_element_type=F32, precision=HIGHEST)
    g_end = jnp.dot(same.astype(F32), gb, preferred_element_type=F32, precision=HIGHEST)
    g_row = g_col.T
    go = go_ref[...]
    heads = range(DN_HEADS)
    cols = [slice(h * DN_DK, (h + 1) * DN_DK) for h in heads]

    gc = [g_col[:, h:h + 1] for h in heads]
    ge = [g_end[:, h:h + 1] for h in heads]
    decay = [jnp.exp(jnp.where(tril, gc[h] - g_row[h:h + 1, :], -jnp.inf)) for h in heads]
    beta = [gb[:, DN_HEADS + h:DN_HEADS + h + 1] for h in heads]
    kb = [k_ref[:, cols[h]] * beta[h] for h in heads]
    kbf = [k_ref[:, cols[h]].astype(BF16) for h in heads]
    m_full = [jnp.where(strict, lax.dot_general(kb[h].astype(BF16), kbf[h], NT_DIMS,
                                                preferred_element_type=F32) * decay[h], 0.0) for h in heads]
    p = [-jnp.where(in_base, m_full[h], 0.0) for h in heads]
    t = [eye + p[h] for h in heads]
    size = 2
    while size < base:
        p = [_dot(p[h], p[h]) for h in heads]
        t = [t[h] + _dot(t[h], p[h]) for h in heads]
        size *= 2
    for off in levels:
        u = [_dot(t[h], jnp.where(off, m_full[h], 0.0)) for h in heads]
        t = [t[h] - _dot(u[h], t[h]) for h in heads]
    e_g = [jnp.exp(gc[h]) for h in heads]
    sol = [_dot(t[h], jnp.concatenate([v_ref[:, cols[h]] * beta[h], kb[h] * e_g[h]], axis=1)) for h in heads]
    attn = [lax.dot_general(q_ref[:, cols[h]].astype(BF16), kbf[h], NT_DIMS,
                            preferred_element_type=F32) * decay[h] for h in heads]
    qd = [q_ref[:, cols[h]] * e_g[h] for h in heads]
    kt = [k_ref[:, cols[h]] * jnp.exp(ge[h] - gc[h]) for h in heads]
    st = [state_ref[h] for h in heads]
    v_new = [[] for _ in heads]
    o_inter = [[] for _ in heads]
    for c in range(n_sub):
        rs = slice(c * CHUNK, (c + 1) * CHUNK)
        for h in heads:
            r = _dot(jnp.concatenate([sol[h][rs, DN_DV:], qd[h][rs]], axis=0), st[h])
            v_c = sol[h][rs, :DN_DV] - r[:CHUNK]
            v_new[h].append(v_c)
            o_inter[h].append(r[CHUNK:])
            g_tot = jnp.exp(ge[h][c * CHUNK:c * CHUNK + 1, :])
            st[h] = st[h] * g_tot + lax.dot_general(kt[h][rs].astype(BF16), v_c.astype(BF16), TN_DIMS,
                                                    preferred_element_type=F32)
    for h in heads:
        state_ref[h] = st[h]
        o = jnp.concatenate(o_inter[h], axis=0) + _dot(attn[h], jnp.concatenate(v_new[h], axis=0))
        zh = z_ref[:, cols[h]]
        o_ref[:, cols[h]] = (_rms_norm(o, go) * (zh * jax.nn.sigmoid(zh))).astype(BF16)

    @pl.when(n == n_steps - 1)
    def _():
        sout_ref[0] = state_ref[...]


def _delta_rule(q, k, v, z, gb, s0, g_o, *, seq, rows):
    m = q.shape[0]
    nb = m // seq
    rows = min(rows, seq)
    ns = seq // rows
    row = lambda b, n: (b * ns + n, 0)
    st_spec = pl.BlockSpec((1, DN_HEADS, DN_DK, DN_DV), lambda b, n: (b, 0, 0, 0))
    return pl.pallas_call(
        functools.partial(_delta_kernel, n_steps=ns),
        grid=(nb, ns),
        in_specs=[
            pl.BlockSpec((rows, DN_KEY), row),
            pl.BlockSpec((rows, DN_KEY), row),
            pl.BlockSpec((rows, DN_VAL), row),
            pl.BlockSpec((rows, DN_VAL), row),
            pl.BlockSpec((rows, LANES), row),
            st_spec,
            pl.BlockSpec((1, DN_DV), lambda b, n: (0, 0)),
        ],
        out_specs=[pl.BlockSpec((rows, DN_VAL), row), st_spec],
        out_shape=[
            jax.ShapeDtypeStruct((m, DN_VAL), BF16),
            jax.ShapeDtypeStruct((nb, DN_HEADS, DN_DK, DN_DV), F32),
        ],
        scratch_shapes=[pltpu.VMEM((DN_HEADS, DN_DK, DN_DV), F32)],
        compiler_params=_params("arbitrary", "arbitrary"),
        name="gated_delta",
    )(q, k, v, z, gb, s0, g_o)


def _rope_table(pos):
    half = ROPE_DIM // 2
    inv = ROPE_THETA ** (-jnp.arange(half, dtype=F32) / half)
    ang = pos.astype(F32)[:, None] * inv[None, :]
    cos, sin = jnp.cos(ang), jnp.sin(ang)
    s = pos.shape[0]
    ones = jnp.ones((s, NOPE_DIM), F32)
    z16 = jnp.zeros((s, half), F32)
    z32 = jnp.zeros((s, HEAD_PAD - NOPE_DIM - ROPE_DIM), F32)
    z64 = jnp.zeros((s, NOPE_DIM), F32)
    c = jnp.concatenate([ones, cos, cos, z32], -1)
    sp = jnp.concatenate([z64, z16, sin, z32], -1)
    sm = jnp.concatenate([z64, -sin, z16, z32], -1)
    return jnp.concatenate([c, sp, sm], -1)


def _pad_heads(w, per_head, keep):
    kdim = w.shape[0]
    w = w.reshape(kdim, MLA_HEADS, per_head)[:, :, :keep]
    w = jnp.pad(w, ((0, 0), (0, 0), (0, HEAD_PAD - keep)))
    return w.reshape(kdim, MLA_HEADS * HEAD_PAD)


def _prep_even(w_in, w_uq, w_ukv):
    c1 = Q_RANK + KV_RANK
    w_kr = jnp.pad(w_in[:, c1:c1 + ROPE_DIM], ((0, 0), (NOPE_DIM, HEAD_PAD - NOPE_DIM - ROPE_DIM)))
    w1 = jnp.concatenate([w_in[:, :c1], w_kr], -1).astype(BF16)
    w2 = w_in[:, c1 + ROPE_DIM:].astype(BF16)
    wuq = _pad_heads(w_uq, NOPE_DIM + ROPE_DIM, NOPE_DIM + ROPE_DIM).astype(BF16)
    wuk = _pad_heads(w_ukv, NOPE_DIM + V_DIM, NOPE_DIM).astype(BF16)
    wuv = w_ukv.reshape(KV_RANK, MLA_HEADS, NOPE_DIM + V_DIM)[:, :, NOPE_DIM:].reshape(KV_RANK, MLA_WIDTH)
    return w1, w2, wuq, wuk, wuv.astype(BF16)


def _prep_odd(w_in, a_log, dt_bias):
    w_qkv = w_in[:, :DN_QKV].astype(BF16)
    w_z = w_in[:, DN_QKV:DN_QKV + DN_VAL].astype(BF16)
    w_ab = jnp.pad(w_in[:, DN_QKV + DN_VAL:], ((0, 0), (0, LANES - 2 * DN_HEADS))).astype(BF16)
    ab_const = jnp.zeros((SUBLANES, LANES), F32)
    ab_const = ab_const.at[0, :DN_HEADS].set(a_log.astype(F32)).at[1, :DN_HEADS].set(dt_bias.astype(F32))
    return w_qkv, w_z, w_ab, ab_const


def _prep_ffn(w_in, conv_w, w_out):
    nc = D_FF // FF_CHUNK
    wg = w_in[:, :D_FF].reshape(D_MODEL, nc, FF_CHUNK).transpose(1, 0, 2).astype(BF16)
    wu = w_in[:, D_FF:].reshape(D_MODEL, nc, FF_CHUNK).transpose(1, 0, 2).astype(BF16)
    wo = w_out.reshape(nc, FF_CHUNK, D_MODEL).astype(BF16)
    cw = conv_w.reshape(FF_K, nc, FF_CHUNK).transpose(1, 0, 2)
    return wg, wu, wo, cw


def _pad_rows8(buf):
    return jnp.pad(buf.astype(F32), ((0, 0), (SUBLANES - buf.shape[1], 0), (0, 0)))


def _trunk(x, past_lat, past_kr, sc_buf, dconv_buf, delta_s, ff_buf, w, *, tm, tq, tk, dr):
    (even_w, odd_w, ffn_w, w_o_e, w_o_o, g_qnorm, g_kvnorm, sc_w, dconv_w, g_onorm,
     ln_mix_g, ln_mix_b, ln_ff_g, ln_ff_b) = w
    nb, seq, _ = x.shape
    past = 0 if past_lat is None else past_lat.shape[2]
    m = nb * seq
    x = x.reshape(m, D_MODEL)
    tab = _rope_table(past + jnp.arange(seq, dtype=jnp.int32))
    t_valid = past + seq
    tq = min(tq, seq)
    tk = min(tk, t_valid)
    t_pad = -(-t_valid // tk) * tk
    nc = D_FF // FF_CHUNK
    lats, krs, scs, dcs, dss, ffs = [], [], [], [], [], []
    for i in range(DEPTH):
        row = lambda a: a[i].reshape(1, -1)
        if i % 2 == 0:
            e = i // 2
            w1, w2, wuq, wuk, wuv = even_w[e]
            q, lat, kr_pad, o_sc, sc_new = _even_in(
                x, w1, w2, g_qnorm[e].reshape(1, -1), g_kvnorm[e].reshape(1, -1), wuq, sc_w[e], tab,
                _pad_rows8(sc_buf[e]), seq=seq, tm=tm)
            lat_all = lat.reshape(nb, seq, KV_RANK)
            kr_all = kr_pad.reshape(nb, seq, LANES)
            if past:
                kr_past = jnp.pad(past_kr[e].astype(F32),
                                  ((0, 0), (0, 0), (NOPE_DIM, HEAD_PAD - NOPE_DIM - ROPE_DIM)))
                lat_all = jnp.concatenate([past_lat[e].astype(F32), lat_all], 1)
                kr_all = jnp.concatenate([kr_past, kr_all], 1)
            if t_pad != t_valid:
                lat_all = jnp.pad(lat_all, ((0, 0), (0, t_pad - t_valid), (0, 0)))
                kr_all = jnp.pad(kr_all, ((0, 0), (0, t_pad - t_valid), (0, 0)))
            k, v = _kv_up(lat_all.reshape(nb * t_pad, KV_RANK), kr_all.reshape(nb * t_pad, LANES),
                          wuk, wuv, tm=tm)
            o_att = _attention(q, k, v, nb=nb, sq=seq, t_pad=t_pad, t_valid=t_valid, past=past, tq=tq, tk=tk)
            y_in = jnp.concatenate([o_att, o_sc], -1)
            x = _proj_ln(y_in, w_o_e[e], x, row(ln_mix_g), row(ln_mix_b), tm=tm)
            lats.append(lat.reshape(nb, seq, KV_RANK))
            krs.append(kr_pad.reshape(nb, seq, LANES)[:, :, NOPE_DIM:NOPE_DIM + ROPE_DIM])
            scs.append(sc_new[:, SUBLANES - (SC_K - 1):])
        else:
            o = i // 2
            w_qkv, w_z, w_ab, ab_const = odd_w[o]
            q, k, v, z, gb, dc_new = _odd_in(x, w_qkv, w_z, w_ab, dconv_w[o], ab_const,
                                             _pad_rows8(dconv_buf[o]), seq=seq, tm=min(tm, 256))
            og, s_new = _delta_rule(q, k, v, z, gb, delta_s[o].astype(F32), g_onorm[o].reshape(1, -1), seq=seq,
                                    rows=dr)
            x = _proj_ln(og, w_o_o[o], x, row(ln_mix_g), row(ln_mix_b), tm=tm)
            dcs.append(dc_new[:, SUBLANES - (DN_CONV - 1):])
            dss.append(s_new)
        wg, wu, wo, cw = ffn_w[i]
        buf8 = _pad_rows8(ff_buf[i]).reshape(nb, SUBLANES, nc, FF_CHUNK).transpose(0, 2, 1, 3)
        x, ff_new = _conv_ffn(x, buf8, wg, wu, wo, cw, row(ln_ff_g), row(ln_ff_b), seq=seq, tm=tm)
        ff_new = ff_new.transpose(0, 2, 1, 3).reshape(nb, SUBLANES, D_FF)
        ffs.append(ff_new[:, SUBLANES - (FF_K - 1):])
    return (x.reshape(nb, seq, D_MODEL), jnp.stack(lats), jnp.stack(krs), jnp.stack(scs), jnp.stack(dcs),
            jnp.stack(dss), jnp.stack(ffs))


def _prepare(w_in_e, w_uq, w_ukv, w_o_e, w_in_o, a_log, dt_bias, w_o_o, w_ff_in, ffconv_w, w_ff_out):
    even_w = [_prep_even(w_in_e[e], w_uq[e], w_ukv[e]) for e in range(N_EVEN)]
    odd_w = [_prep_odd(w_in_o[o], a_log[o], dt_bias[o]) for o in range(N_ODD)]
    ffn_w = [_prep_ffn(w_ff_in[i], ffconv_w[i], w_ff_out[i]) for i in range(DEPTH)]
    return even_w, odd_w, ffn_w, w_o_e.astype(BF16), w_o_o.astype(BF16)


def kernel(x_prompt, x_sample, cache_mla_latent, cache_mla_krope, state_sconv, state_dconv, state_delta, state_ffconv, w_in_e, g_qnorm, g_kvnorm, w_uq, w_ukv, sc_w, w_o_e, w_in_o, dconv_w, a_log, dt_bias, g_onorm, w_o_o, w_ff_in, ffconv_w, w_ff_out, ln_mix_g, ln_mix_b, ln_ff_g, ln_ff_b):
    even_w, odd_w, ffn_w, w_o_e_b, w_o_o_b = _prepare(
        w_in_e, w_uq, w_ukv, w_o_e, w_in_o, a_log, dt_bias, w_o_o, w_ff_in, ffconv_w, w_ff_out)
    w = (even_w, odd_w, ffn_w, w_o_e_b, w_o_o_b, g_qnorm, g_kvnorm, sc_w, dconv_w, g_onorm,
         ln_mix_g, ln_mix_b, ln_ff_g, ln_ff_b)
    bp = x_prompt.shape[0]
    f32 = x_prompt.dtype
    y_p, p_lat, p_kr, p_sc, p_dc, p_ds, p_ff = _trunk(
        x_prompt, None, None,
        jnp.zeros((N_EVEN, bp, SC_K - 1, SC_WIDTH), f32),
        jnp.zeros((N_ODD, bp, DN_CONV - 1, DN_QKV), f32),
        jnp.zeros((N_ODD, bp, DN_HEADS, DN_DK, DN_DV), f32),
        jnp.zeros((DEPTH, bp, FF_K - 1, D_FF), f32),
        w, tm=512, tq=512, tk=512, dr=256)
    y_s, s_lat, s_kr, s_sc, s_dc, s_ds, s_ff = _trunk(
        x_sample, cache_mla_latent, cache_mla_krope, state_sconv, state_dconv, state_delta, state_ffconv,
        w, tm=512, tq=64, tk=256, dr=256)
    return (y_p, y_s, p_lat, p_kr, p_sc, p_dc, p_ds, p_ff, s_lat, s_kr, s_sc, s_dc, s_ds, s_ff)
```

```python
import functools
import math

import jax
import jax.numpy as jnp
from jax import lax
from jax.experimental import pallas as pl
from jax.experimental.pallas import tpu as pltpu

D_MODEL = 1024
DEPTH = 4
CHUNK = 64
N_EVEN = (DEPTH + 1) // 2
N_ODD = DEPTH // 2
MLA_HEADS = 8
Q_RANK = 384
KV_RANK = 256
NOPE_DIM = 64
ROPE_DIM = 32
V_DIM = 64
ROPE_THETA = 10000.0
MLA_WIDTH = MLA_HEADS * V_DIM
SC_WIDTH = 512
SC_K = 3
DN_HEADS = 8
DN_DK = 128
DN_DV = 128
DN_CONV = 4
DN_KEY = DN_HEADS * DN_DK
DN_VAL = DN_HEADS * DN_DV
DN_QKV = 2 * DN_KEY + DN_VAL
D_FF = 2816
FF_K = 3
ALPHA = (2 * DEPTH) ** 0.25
NORM_EPS = 1e-6
NEG_INF = -1e30
LOG2_E = math.log2(math.e)

F32 = jnp.float32
BF16 = jnp.bfloat16
HIGHEST = lax.Precision.HIGHEST

LANES = 128
SUBLANES = 8
HEAD_PAD = 128
FF_CHUNK = 256
DELTA_BASE_BLOCK = 8
VMEM_LIMIT = 56 * 1024 * 1024

NT_DIMS = (((1,), (1,)), ((), ()))
TN_DIMS = (((0,), (0,)), ((), ()))


def _params(*sem):
    return pltpu.CompilerParams(dimension_semantics=sem, vmem_limit_bytes=VMEM_LIMIT)


def _dot(a, b):
    return jnp.dot(a.astype(BF16), b.astype(BF16), preferred_element_type=F32)


def _layer_norm(r, g, b):
    mu = jnp.mean(r, axis=-1, keepdims=True)
    d = r - mu
    var = jnp.mean(d * d, axis=-1, keepdims=True)
    return d * lax.rsqrt(var + NORM_EPS) * g + b


def _rms_norm(x, g):
    return x * lax.rsqrt(jnp.mean(x * x, axis=-1, keepdims=True) + NORM_EPS) * g


def _softplus(x):
    return jnp.maximum(x, 0.0) + jnp.log1p(jnp.exp(-jnp.abs(x)))


def _causal_conv(x, win_ref, w):
    tm, c = x.shape
    k_taps = w.shape[0]
    groups = tm // SUBLANES
    x3 = x.reshape(groups, SUBLANES, c)
    prev = win_ref[...].reshape(1, SUBLANES, c)
    sub = lax.broadcasted_iota(jnp.int32, (groups, SUBLANES, c), 1)
    y = x3 * w[k_taps - 1:k_taps, :]
    for k in range(1, k_taps):
        rot = pltpu.roll(x3, k, axis=1)
        rot_prev = jnp.concatenate([pltpu.roll(prev, k, axis=1), rot[:groups - 1]], axis=0)
        y = y + jnp.where(sub < k, rot_prev, rot) * w[k_taps - 1 - k:k_taps - k, :]
    win_ref[...] = x[tm - SUBLANES:]
    return y.reshape(tm, c)


def _ffn_kernel(x_ref, wg_ref, wu_ref, wo_ref, cw_ref, buf_ref, g_ref, b_ref,
                out_ref, nbuf_ref, act_ref, xb_ref, win_ref, *, tiles_per_seq, n_chunks):
    i = pl.program_id(0)
    tm = x_ref.shape[0]
    xb_ref[...] = x_ref[...].astype(BF16)

    @pl.when(i % tiles_per_seq == 0)
    def _():
        win_ref[...] = buf_ref[0]

    for c in range(n_chunks):
        xb = xb_ref[...]
        gate = jnp.dot(xb, wg_ref[c], preferred_element_type=F32)
        up = jnp.dot(xb, wu_ref[c], preferred_element_type=F32)
        y = _causal_conv(gate, win_ref.at[c], cw_ref[c])
        nbuf_ref[0, c] = gate[tm - SUBLANES:]
        act_ref[:, c * FF_CHUNK:(c + 1) * FF_CHUNK] = (y * jax.nn.sigmoid(y) * up).astype(BF16)

    acc = jnp.dot(act_ref[...], wo_ref[...], preferred_element_type=F32)
    out_ref[...] = _layer_norm(ALPHA * x_ref[...] + acc, g_ref[...], b_ref[...])


def _conv_ffn(x, buf8, w_gate, w_up, w_out, conv_w, ln_g, ln_b, *, seq, tm):
    m = x.shape[0]
    nb = m // seq
    tm = min(tm, seq)
    tps = seq // tm
    nc = w_gate.shape[0]
    const3 = lambda i: (0, 0, 0)
    return pl.pallas_call(
        functools.partial(_ffn_kernel, tiles_per_seq=tps, n_chunks=nc),
        grid=(m // tm,),
        in_specs=[
            pl.BlockSpec((tm, D_MODEL), lambda i: (i, 0)),
            pl.BlockSpec(w_gate.shape, const3),
            pl.BlockSpec(w_up.shape, const3),
            pl.BlockSpec(w_out.shape, lambda i: (0, 0)),
            pl.BlockSpec(conv_w.shape, const3),
            pl.BlockSpec((1, nc, SUBLANES, FF_CHUNK), lambda i: (i // tps, 0, 0, 0)),
            pl.BlockSpec((1, D_MODEL), lambda i: (0, 0)),
            pl.BlockSpec((1, D_MODEL), lambda i: (0, 0)),
        ],
        out_specs=[
            pl.BlockSpec((tm, D_MODEL), lambda i: (i, 0)),
            pl.BlockSpec((1, nc, SUBLANES, FF_CHUNK), lambda i: (i // tps, 0, 0, 0)),
        ],
        out_shape=[
            jax.ShapeDtypeStruct((m, D_MODEL), F32),
            jax.ShapeDtypeStruct((nb, nc, SUBLANES, FF_CHUNK), F32),
        ],
        scratch_shapes=[
            pltpu.VMEM((tm, D_FF), BF16),
            pltpu.VMEM((tm, D_MODEL), BF16),
            pltpu.VMEM((nc, SUBLANES, FF_CHUNK), F32),
        ],
        compiler_params=_params("arbitrary"),
        name="conv_ffn",
    )(x, w_gate, w_up, w_out, conv_w, buf8, ln_g, ln_b)


def _proj_ln_kernel(y_ref, w_ref, x_ref, g_ref, b_ref, out_ref):
    acc = jnp.dot(y_ref[...], w_ref[...], preferred_element_type=F32)
    out_ref[...] = _layer_norm(ALPHA * x_ref[...] + acc, g_ref[...], b_ref[...])


def _proj_ln(y, w, x, ln_g, ln_b, *, tm):
    m, k = y.shape
    tm = min(tm, m)
    return pl.pallas_call(
        _proj_ln_kernel,
        grid=(m // tm,),
        in_specs=[
            pl.BlockSpec((tm, k), lambda i: (i, 0)),
            pl.BlockSpec((k, D_MODEL), lambda i: (0, 0)),
            pl.BlockSpec((tm, D_MODEL), lambda i: (i, 0)),
            pl.BlockSpec((1, D_MODEL), lambda i: (0, 0)),
            pl.BlockSpec((1, D_MODEL), lambda i: (0, 0)),
        ],
        out_specs=pl.BlockSpec((tm, D_MODEL), lambda i: (i, 0)),
        out_shape=jax.ShapeDtypeStruct((m, D_MODEL), F32),
        compiler_params=_params("arbitrary"),
        name="proj_ln",
    )(y, w, x, ln_g, ln_b)


def _rope(x, tab):
    return (x * tab[:, 0:LANES]
            + pltpu.roll(x, ROPE_DIM // 2, axis=1) * tab[:, LANES:2 * LANES]
            + pltpu.roll(x, LANES - ROPE_DIM // 2, axis=1) * tab[:, 2 * LANES:3 * LANES])


def _even_in_kernel(x_ref, w1_ref, w2_ref, gq_ref, gkv_ref, wuq_ref, scw_ref, tab_ref, buf_ref,
                    q_ref, lat_ref, kr_ref, osc_ref, nbuf_ref, win_ref, *, tiles_per_seq):
    i = pl.program_id(0)
    tm = x_ref.shape[0]

    @pl.when(i % tiles_per_seq == 0)
    def _():
        win_ref[...] = buf_ref[0]

    xb = x_ref[...].astype(BF16)
    tab = tab_ref[...]
    h1 = jnp.dot(xb, w1_ref[...], preferred_element_type=F32)
    cq = h1[:, :Q_RANK]
    ckv = h1[:, Q_RANK:Q_RANK + KV_RANK]
    kr_raw = h1[:, Q_RANK + KV_RANK:]
    lat_ref[...] = _rms_norm(ckv, gkv_ref[...])
    kr_ref[...] = _rope(kr_raw, tab)
    q_raw = _dot(_rms_norm(cq, gq_ref[...]), wuq_ref[...])
    scale = (NOPE_DIM + ROPE_DIM) ** -0.5 * LOG2_E
    for h in range(MLA_HEADS):
        sl = slice(h * HEAD_PAD, (h + 1) * HEAD_PAD)
        q_ref[:, sl] = (_rope(q_raw[:, sl], tab) * scale).astype(BF16)

    h2 = jnp.dot(xb, w2_ref[...], preferred_element_type=F32)
    gate_b = h2[:, :SC_WIDTH]
    u_in = h2[:, SC_WIDTH:2 * SC_WIDTH] * h2[:, 2 * SC_WIDTH:]
    u = _causal_conv(u_in, win_ref, scw_ref[...])
    nbuf_ref[0] = u_in[tm - SUBLANES:]
    osc_ref[...] = (gate_b * u).astype(BF16)


def _even_in(x, w1, w2, g_q, g_kv, w_uq, sc_w, tab, buf8, *, seq, tm):
    m = x.shape[0]
    nb = m // seq
    tm = min(tm, seq)
    tps = seq // tm
    const2 = lambda i: (0, 0)
    return pl.pallas_call(
        functools.partial(_even_in_kernel, tiles_per_seq=tps),
        grid=(m // tm,),
        in_specs=[
            pl.BlockSpec((tm, D_MODEL), lambda i: (i, 0)),
            pl.BlockSpec(w1.shape, const2),
            pl.BlockSpec(w2.shape, const2),
            pl.BlockSpec(g_q.shape, const2),
            pl.BlockSpec(g_kv.shape, const2),
            pl.BlockSpec(w_uq.shape, const2),
            pl.BlockSpec(sc_w.shape, const2),
            pl.BlockSpec((tm, 3 * LANES), lambda i: (i % tps, 0)),
            pl.BlockSpec((1, SUBLANES, SC_WIDTH), lambda i: (i // tps, 0, 0)),
        ],
        out_specs=[
            pl.BlockSpec((tm, MLA_HEADS * HEAD_PAD), lambda i: (i, 0)),
            pl.BlockSpec((tm, KV_RANK), lambda i: (i, 0)),
            pl.BlockSpec((tm, LANES), lambda i: (i, 0)),
            pl.BlockSpec((tm, SC_WIDTH), lambda i: (i, 0)),
            pl.BlockSpec((1, SUBLANES, SC_WIDTH), lambda i: (i // tps, 0, 0)),
        ],
        out_shape=[
            jax.ShapeDtypeStruct((m, MLA_HEADS * HEAD_PAD), BF16),
            jax.ShapeDtypeStruct((m, KV_RANK), F32),
            jax.ShapeDtypeStruct((m, LANES), F32),
            jax.ShapeDtypeStruct((m, SC_WIDTH), BF16),
            jax.ShapeDtypeStruct((nb, SUBLANES, SC_WIDTH), F32),
        ],
        scratch_shapes=[pltpu.VMEM((SUBLANES, SC_WIDTH), F32)],
        compiler_params=_params("arbitrary"),
        name="even_in",
    )(x, w1, w2, g_q, g_kv, w_uq, sc_w, tab, buf8)


def _kv_up_kernel(lat_ref, kr_ref, wuk_ref, wuvt_ref, k_ref, vt_ref):
    lb = lat_ref[...].astype(BF16)
    kn = jnp.dot(lb, wuk_ref[...], preferred_element_type=F32)
    kr = kr_ref[...]
    for h in range(MLA_HEADS):
        sl = slice(h * HEAD_PAD, (h + 1) * HEAD_PAD)
        k_ref[:, sl] = (kn[:, sl] + kr).astype(BF16)
    vt_ref[...] = lax.dot_general(wuvt_ref[...], lb, NT_DIMS, preferred_element_type=F32).astype(BF16)


def _kv_up(lat, kr_pad, w_uk, w_uv_t, *, nb, t_pad, tm):
    tpb = t_pad // tm
    const2 = lambda i: (0, 0)
    return pl.pallas_call(
        _kv_up_kernel,
        grid=(nb * tpb,),
        in_specs=[
            pl.BlockSpec((tm, KV_RANK), lambda i: (i, 0)),
            pl.BlockSpec((tm, LANES), lambda i: (i, 0)),
            pl.BlockSpec(w_uk.shape, const2),
            pl.BlockSpec(w_uv_t.shape, const2),
        ],
        out_specs=[
            pl.BlockSpec((tm, MLA_HEADS * HEAD_PAD), lambda i: (i, 0)),
            pl.BlockSpec((MLA_WIDTH, tm), lambda i: (i // tpb, i % tpb)),
        ],
        out_shape=[
            jax.ShapeDtypeStruct((nb * t_pad, MLA_HEADS * HEAD_PAD), BF16),
            jax.ShapeDtypeStruct((nb * MLA_WIDTH, t_pad), BF16),
        ],
        compiler_params=_params("arbitrary"),
        name="kv_up",
    )(lat, kr_pad, w_uk, w_uv_t)


def _attn_kernel(q_ref, k_ref, vt_ref, o_ref, m_ref, l_ref, acc_ref, *, tq, tk, past, t_valid, nk):
    qi = pl.program_id(1)
    kj = pl.program_id(2)
    q_lo = past + qi * tq
    chunk_lo = q_lo // CHUNK
    chunk_hi = (q_lo + tq - 1) // CHUNK
    last = jnp.minimum(((chunk_hi + 1) * CHUNK - 1) // tk, nk - 1)
    k_end = (kj + 1) * tk
    full = jnp.logical_and(k_end <= (chunk_lo + 1) * CHUNK, k_end <= t_valid)
    heads = range(MLA_HEADS)
    hsl = [slice(h * HEAD_PAD, (h + 1) * HEAD_PAD) for h in heads]
    vsl = [slice(h * V_DIM, (h + 1) * V_DIM) for h in heads]

    @pl.when(kj == 0)
    def _():
        m_ref[...] = jnp.full_like(m_ref, NEG_INF)
        l_ref[...] = jnp.zeros_like(l_ref)
        acc_ref[...] = jnp.zeros_like(acc_ref)

    def step(masked):
        if masked:
            kpos = kj * tk + lax.broadcasted_iota(jnp.int32, (tk, tq), 0)
            qpos = q_lo + lax.broadcasted_iota(jnp.int32, (tk, tq), 1)
            visible = jnp.logical_and(kpos // CHUNK <= qpos // CHUNK, kpos < t_valid)
        s = [lax.dot_general(k_ref[:, hsl[h]], q_ref[:, hsl[h]], NT_DIMS, preferred_element_type=F32)
             for h in heads]
        if masked:
            s = [jnp.where(visible, s[h], NEG_INF) for h in heads]
        m_prev = [m_ref[h:h + 1, :] for h in heads]
        m_new = [jnp.maximum(m_prev[h], jnp.max(s[h], axis=0, keepdims=True)) for h in heads]
        alpha = [jnp.exp2(m_prev[h] - m_new[h]) for h in heads]
        p = [jnp.exp2(s[h] - m_new[h]) for h in heads]
        for h in heads:
            l_ref[h:h + 1, :] = alpha[h] * l_ref[h:h + 1, :] + jnp.sum(p[h], axis=0, keepdims=True)
            m_ref[h:h + 1, :] = m_new[h]
        pv = [jnp.dot(vt_ref[vsl[h], :], p[h].astype(BF16), preferred_element_type=F32) for h in heads]
        for h in heads:
            acc_ref[vsl[h], :] = alpha[h] * acc_ref[vsl[h], :] + pv[h]

    @pl.when(jnp.logical_and(kj <= last, full))
    def _():
        step(False)

    @pl.when(jnp.logical_and(kj <= last, jnp.logical_not(full)))
    def _():
        step(True)

    @pl.when(kj == last)
    def _():
        for hp in range(MLA_HEADS // 2):
            pair = jnp.concatenate(
                [acc_ref[vsl[h], :] * (1.0 / l_ref[h:h + 1, :]) for h in (2 * hp, 2 * hp + 1)], axis=0)
            o_ref[:, hp * 2 * V_DIM:(hp + 1) * 2 * V_DIM] = pair.T.astype(BF16)


def _attention(q, k, vt, *, nb, sq, t_pad, t_valid, past, tq, tk):
    nq = sq // tq
    nk = t_pad // tk

    def last_tile(qi):
        chunk_hi = (past + qi * tq + tq - 1) // CHUNK
        return jnp.minimum(((chunk_hi + 1) * CHUNK - 1) // tk, nk - 1)

    return pl.pallas_call(
        functools.partial(_attn_kernel, tq=tq, tk=tk, past=past, t_valid=t_valid, nk=nk),
        grid=(nb, nq, nk),
        in_specs=[
            pl.BlockSpec((tq, MLA_HEADS * HEAD_PAD), lambda b, qi, kj: (b * nq + qi, 0)),
            pl.BlockSpec((tk, MLA_HEADS * HEAD_PAD),
                         lambda b, qi, kj: (b * nk + jnp.minimum(kj, last_tile(qi)), 0)),
            pl.BlockSpec((MLA_WIDTH, tk), lambda b, qi, kj: (b, jnp.minimum(kj, last_tile(qi)))),
        ],
        out_specs=pl.BlockSpec((tq, MLA_WIDTH), lambda b, qi, kj: (b * nq + qi, 0)),
        out_shape=jax.ShapeDtypeStruct((nb * sq, MLA_WIDTH), BF16),
        scratch_shapes=[
            pltpu.VMEM((MLA_HEADS, tq), F32),
            pltpu.VMEM((MLA_HEADS, tq), F32),
            pltpu.VMEM((MLA_WIDTH, tq), F32),
        ],
        compiler_params=_params("arbitrary", "arbitrary", "arbitrary"),
        name="chunk_attention",
    )(q, k, vt)


def _odd_in_kernel(x_ref, wqkv_ref, wz_ref, wab_ref, dcw_ref, ab_ref, buf_ref,
                   q_ref, k_ref, v_ref, z_ref, gb_ref, nbuf_ref, win_ref, *, tiles_per_seq):
    i = pl.program_id(0)
    tm = x_ref.shape[0]

    @pl.when(i % tiles_per_seq == 0)
    def _():
        for part in range(3):
            win_ref[part] = buf_ref[0, :, part * DN_KEY:(part + 1) * DN_KEY]

    xb = x_ref[...].astype(BF16)
    outs = (q_ref, k_ref, v_ref)
    for part in range(3):
        cols = slice(part * DN_KEY, (part + 1) * DN_KEY)
        pre = jnp.dot(xb, wqkv_ref[:, cols], preferred_element_type=F32)
        y = _causal_conv(pre, win_ref.at[part], dcw_ref[:, cols])
        nbuf_ref[0, :, cols] = pre[tm - SUBLANES:]
        y = y * jax.nn.sigmoid(y)
        if part == 2:
            outs[part][...] = y
        else:
            post = DN_DK ** -0.5 if part == 0 else 1.0
            for h in range(DN_HEADS):
                sl = slice(h * DN_DK, (h + 1) * DN_DK)
                seg = y[:, sl]
                nrm = lax.rsqrt(jnp.sum(seg * seg, axis=-1, keepdims=True) + NORM_EPS)
                outs[part][:, sl] = seg * nrm * post if part == 0 else seg * nrm

    z_ref[...] = jnp.dot(xb, wz_ref[...], preferred_element_type=F32)
    ab = jnp.dot(xb, wab_ref[...], preferred_element_type=F32)
    lane = lax.broadcasted_iota(jnp.int32, ab.shape, 1)
    g = -jnp.exp(ab_ref[0:1, :]) * _softplus(ab + ab_ref[1:2, :])
    gb_ref[...] = jnp.where(lane < DN_HEADS, g, jax.nn.sigmoid(ab))


def _odd_in(x, w_qkv, w_z, w_ab, dconv_w, ab_const, buf8, *, seq, tm):
    m = x.shape[0]
    nb = m // seq
    tm = min(tm, seq)
    tps = seq // tm
    const2 = lambda i: (0, 0)
    row = lambda i: (i, 0)
    return pl.pallas_call(
        functools.partial(_odd_in_kernel, tiles_per_seq=tps),
        grid=(m // tm,),
        in_specs=[
            pl.BlockSpec((tm, D_MODEL), row),
            pl.BlockSpec(w_qkv.shape, const2),
            pl.BlockSpec(w_z.shape, const2),
            pl.BlockSpec(w_ab.shape, const2),
            pl.BlockSpec(dconv_w.shape, const2),
            pl.BlockSpec(ab_const.shape, const2),
            pl.BlockSpec((1, SUBLANES, DN_QKV), lambda i: (i // tps, 0, 0)),
        ],
        out_specs=[
            pl.BlockSpec((tm, DN_KEY), row),
            pl.BlockSpec((tm, DN_KEY), row),
            pl.BlockSpec((tm, DN_VAL), row),
            pl.BlockSpec((tm, DN_VAL), row),
            pl.BlockSpec((tm, LANES), row),
            pl.BlockSpec((1, SUBLANES, DN_QKV), lambda i: (i // tps, 0, 0)),
        ],
        out_shape=[
            jax.ShapeDtypeStruct((m, DN_KEY), F32),
            jax.ShapeDtypeStruct((m, DN_KEY), F32),
            jax.ShapeDtypeStruct((m, DN_VAL), F32),
            jax.ShapeDtypeStruct((m, DN_VAL), F32),
            jax.ShapeDtypeStruct((m, LANES), F32),
            jax.ShapeDtypeStruct((nb, SUBLANES, DN_QKV), F32),
        ],
        scratch_shapes=[pltpu.VMEM((3, SUBLANES, DN_KEY), F32)],
        compiler_params=_params("arbitrary"),
        name="odd_in",
    )(x, w_qkv, w_z, w_ab, dconv_w, ab_const, buf8)


def _delta_kernel(q_ref, k_ref, v_ref, z_ref, gb_ref, s0_ref, go_ref, o_ref, sout_ref, state_ref, *, n_steps):
    n = pl.program_id(1)
    rows = q_ref.shape[0]
    n_sub = rows // CHUNK

    @pl.when(n == 0)
    def _():
        state_ref[...] = s0_ref[0]

    gb = gb_ref[...]
    ri = lax.broadcasted_iota(jnp.int32, (rows, rows), 0)
    ci = lax.broadcasted_iota(jnp.int32, (rows, rows), 1)

    def blk(size):
        return (ri // size) == (ci // size)

    same = blk(CHUNK)
    lower = ri > ci
    tril = jnp.logical_and(same, ri >= ci)
    strict = jnp.logical_and(same, lower)
    eye = (ri == ci).astype(F32)
    base = DELTA_BASE_BLOCK
    levels = []
    size = base
    while size < CHUNK:
        levels.append(jnp.logical_and(jnp.logical_and(blk(2 * size), jnp.logical_not(blk(size))), lower))
        size *= 2
    in_base = jnp.logical_and(blk(base), lower)
    g_col = jnp.dot(tril.astype(F32), gb, preferred_element_type=F32, precision=HIGHEST)
    g_end = jnp.dot(same.astype(F32), gb, preferred_element_type=F32, precision=HIGHEST)
    g_row = g_col.T
    go = go_ref[...]
    heads = range(DN_HEADS)
    cols = [slice(h * DN_DK, (h + 1) * DN_DK) for h in heads]

    gc = [g_col[:, h:h + 1] for h in heads]
    ge = [g_end[:, h:h + 1] for h in heads]
    decay = [jnp.exp(jnp.where(tril, gc[h] - g_row[h:h + 1, :], -jnp.inf)) for h in heads]
    beta = [gb[:, DN_HEADS + h:DN_HEADS + h + 1] for h in heads]
    kb = [k_ref[:, cols[h]] * beta[h] for h in heads]
    kbf = [k_ref[:, cols[h]].astype(BF16) for h in heads]
    m_full = [jnp.where(strict, lax.dot_general(kb[h].astype(BF16), kbf[h], NT_DIMS,
                                                preferred_element_type=F32) * decay[h], 0.0) for h in heads]
    p = [-jnp.where(in_base, m_full[h], 0.0) for h in heads]
    t = [eye + p[h] for h in heads]
    size = 2
    while size < base:
        p = [_dot(p[h], p[h]) for h in heads]
        t = [t[h] + _dot(t[h], p[h]) for h in heads]
        size *= 2
    for off in levels:
        u = [_dot(t[h], jnp.where(off, m_full[h], 0.0)) for h in heads]
        t = [t[h] - _dot(u[h], t[h]) for h in heads]
    e_g = [jnp.exp(gc[h]) for h in heads]
    sol = [_dot(t[h], jnp.concatenate([v_ref[:, cols[h]] * beta[h], kb[h] * e_g[h]], axis=1)) for h in heads]
    attn = [lax.dot_general(q_ref[:, cols[h]].astype(BF16), kbf[h], NT_DIMS,
                            preferred_element_type=F32) * decay[h] for h in heads]
    qd = [q_ref[:, cols[h]] * e_g[h] for h in heads]
    kt = [k_ref[:, cols[h]] * jnp.exp(ge[h] - gc[h]) for h in heads]
    st = [state_ref[h] for h in heads]
    v_new = [[] for _ in heads]
    o_inter = [[] for _ in heads]
    for c in range(n_sub):
        rs = slice(c * CHUNK, (c + 1) * CHUNK)
        for h in heads:
            r = _dot(jnp.concatenate([sol[h][rs, DN_DV:], qd[h][rs]], axis=0), st[h])
            v_c = sol[h][rs, :DN_DV] - r[:CHUNK]
            v_new[h].append(v_c)
            o_inter[h].append(r[CHUNK:])
            g_tot = jnp.exp(ge[h][c * CHUNK:c * CHUNK + 1, :])
            st[h] = st[h] * g_tot + lax.dot_general(kt[h][rs].astype(BF16), v_c.astype(BF16), TN_DIMS,
                                                    preferred_element_type=F32)
    for h in heads:
        state_ref[h] = st[h]
        o = jnp.concatenate(o_inter[h], axis=0) + _dot(attn[h], jnp.concatenate(v_new[h], axis=0))
        zh = z_ref[:, cols[h]]
        o_ref[:, cols[h]] = (_rms_norm(o, go) * (zh * jax.nn.sigmoid(zh))).astype(BF16)

    @pl.when(n == n_steps - 1)
    def _():
        sout_ref[0] = state_ref[...]


def _delta_rule(q, k, v, z, gb, s0, g_o, *, seq, rows):
    m = q.shape[0]
    nb = m // seq
    rows = min(rows, seq)
    ns = seq // rows
    row = lambda b, n: (b * ns + n, 0)
    st_spec = pl.BlockSpec((1, DN_HEADS, DN_DK, DN_DV), lambda b, n: (b, 0, 0, 0))
    return pl.pallas_call(
        functools.partial(_delta_kernel, n_steps=ns),
        grid=(nb, ns),
        in_specs=[
            pl.BlockSpec((rows, DN_KEY), row),
            pl.BlockSpec((rows, DN_KEY), row),
            pl.BlockSpec((rows, DN_VAL), row),
            pl.BlockSpec((rows, DN_VAL), row),
            pl.BlockSpec((rows, LANES), row),
            st_spec,
            pl.BlockSpec((1, DN_DV), lambda b, n: (0, 0)),
        ],
        out_specs=[pl.BlockSpec((rows, DN_VAL), row), st_spec],
        out_shape=[
            jax.ShapeDtypeStruct((m, DN_VAL), BF16),
            jax.ShapeDtypeStruct((nb, DN_HEADS, DN_DK, DN_DV), F32),
        ],
        scratch_shapes=[pltpu.VMEM((DN_HEADS, DN_DK, DN_DV), F32)],
        compiler_params=_params("arbitrary", "arbitrary"),
        name="gated_delta",
    )(q, k, v, z, gb, s0, g_o)


def _rope_table(pos):
    half = ROPE_DIM // 2
    inv = ROPE_THETA ** (-jnp.arange(half, dtype=F32) / half)
    ang = pos.astype(F32)[:, None] * inv[None, :]
    cos, sin = jnp.cos(ang), jnp.sin(ang)
    s = pos.shape[0]
    ones = jnp.ones((s, NOPE_DIM), F32)
    z16 = jnp.zeros((s, half), F32)
    z32 = jnp.zeros((s, HEAD_PAD - NOPE_DIM - ROPE_DIM), F32)
    z64 = jnp.zeros((s, NOPE_DIM), F32)
    c = jnp.concatenate([ones, cos, cos, z32], -1)
    sp = jnp.concatenate([z64, z16, sin, z32], -1)
    sm = jnp.concatenate([z64, -sin, z16, z32], -1)
    return jnp.concatenate([c, sp, sm], -1)


def _pad_heads(w, per_head, keep):
    kdim = w.shape[0]
    w = w.reshape(kdim, MLA_HEADS, per_head)[:, :, :keep]
    w = jnp.pad(w, ((0, 0), (0, 0), (0, HEAD_PAD - keep)))
    return w.reshape(kdim, MLA_HEADS * HEAD_PAD)


def _prep_even(w_in, w_uq, w_ukv):
    c1 = Q_RANK + KV_RANK
    w_kr = jnp.pad(w_in[:, c1:c1 + ROPE_DIM], ((0, 0), (NOPE_DIM, HEAD_PAD - NOPE_DIM - ROPE_DIM)))
    w1 = jnp.concatenate([w_in[:, :c1], w_kr], -1).astype(BF16)
    w2 = w_in[:, c1 + ROPE_DIM:].astype(BF16)
    wuq = _pad_heads(w_uq, NOPE_DIM + ROPE_DIM, NOPE_DIM + ROPE_DIM).astype(BF16)
    wuk = _pad_heads(w_ukv, NOPE_DIM + V_DIM, NOPE_DIM).astype(BF16)
    wuv = w_ukv.reshape(KV_RANK, MLA_HEADS, NOPE_DIM + V_DIM)[:, :, NOPE_DIM:].reshape(KV_RANK, MLA_WIDTH)
    return w1, w2, wuq, wuk, wuv.T.astype(BF16)


def _prep_odd(w_in, a_log, dt_bias):
    w_qkv = w_in[:, :DN_QKV].astype(BF16)
    w_z = w_in[:, DN_QKV:DN_QKV + DN_VAL].astype(BF16)
    w_ab = jnp.pad(w_in[:, DN_QKV + DN_VAL:], ((0, 0), (0, LANES - 2 * DN_HEADS))).astype(BF16)
    ab_const = jnp.zeros((SUBLANES, LANES), F32)
    ab_const = ab_const.at[0, :DN_HEADS].set(a_log.astype(F32)).at[1, :DN_HEADS].set(dt_bias.astype(F32))
    return w_qkv, w_z, w_ab, ab_const


def _prep_ffn(w_in, conv_w, w_out):
    nc = D_FF // FF_CHUNK
    wg = w_in[:, :D_FF].reshape(D_MODEL, nc, FF_CHUNK).transpose(1, 0, 2).astype(BF16)
    wu = w_in[:, D_FF:].reshape(D_MODEL, nc, FF_CHUNK).transpose(1, 0, 2).astype(BF16)
    wo = w_out.astype(BF16)
    cw = conv_w.reshape(FF_K, nc, FF_CHUNK).transpose(1, 0, 2)
    return wg, wu, wo, cw


def _pad_rows8(buf):
    return jnp.pad(buf.astype(F32), ((0, 0), (SUBLANES - buf.shape[1], 0), (0, 0)))


def _trunk(x, past_lat, past_kr, sc_buf, dconv_buf, delta_s, ff_buf, w, *, tm, tq, tk, dr):
    (even_w, odd_w, ffn_w, w_o_e, w_o_o, g_qnorm, g_kvnorm, sc_w, dconv_w, g_onorm,
     ln_mix_g, ln_mix_b, ln_ff_g, ln_ff_b) = w
    nb, seq, _ = x.shape
    past = 0 if past_lat is None else past_lat.shape[2]
    m = nb * seq
    x = x.reshape(m, D_MODEL)
    tab = _rope_table(past + jnp.arange(seq, dtype=jnp.int32))
    t_valid = past + seq
    tq = min(tq, seq)
    tk = min(tk, t_valid)
    t_pad = -(-t_valid // tk) * tk
    nc = D_FF // FF_CHUNK
    lats, krs, scs, dcs, dss, ffs = [], [], [], [], [], []
    for i in range(DEPTH):
        row = lambda a: a[i].reshape(1, -1)
        if i % 2 == 0:
            e = i // 2
            w1, w2, wuq, wuk, wuv = even_w[e]
            q, lat, kr_pad, o_sc, sc_new = _even_in(
                x, w1, w2, g_qnorm[e].reshape(1, -1), g_kvnorm[e].reshape(1, -1), wuq, sc_w[e], tab,
                _pad_rows8(sc_buf[e]), seq=seq, tm=tm)
            lat_all = lat.reshape(nb, seq, KV_RANK)
            kr_all = kr_pad.reshape(nb, seq, LANES)
            if past:
                kr_past = jnp.pad(past_kr[e].astype(F32),
                                  ((0, 0), (0, 0), (NOPE_DIM, HEAD_PAD - NOPE_DIM - ROPE_DIM)))
                lat_all = jnp.concatenate([past_lat[e].astype(F32), lat_all], 1)
                kr_all = jnp.concatenate([kr_past, kr_all], 1)
            if t_pad != t_valid:
                lat_all = jnp.pad(lat_all, ((0, 0), (0, t_pad - t_valid), (0, 0)))
                kr_all = jnp.pad(kr_all, ((0, 0), (0, t_pad - t_valid), (0, 0)))
            k, v = _kv_up(lat_all.reshape(nb * t_pad, KV_RANK), kr_all.reshape(nb * t_pad, LANES),
                          wuk, wuv, nb=nb, t_pad=t_pad, tm=tk)
            o_att = _attention(q, k, v, nb=nb, sq=seq, t_pad=t_pad, t_valid=t_valid, past=past, tq=tq, tk=tk)
            y_in = jnp.concatenate([o_att, o_sc], -1)
            x = _proj_ln(y_in, w_o_e[e], x, row(ln_mix_g), row(ln_mix_b), tm=tm)
            lats.append(lat.reshape(nb, seq, KV_RANK))
            krs.append(kr_pad.reshape(nb, seq, LANES)[:, :, NOPE_DIM:NOPE_DIM + ROPE_DIM])
            scs.append(sc_new[:, SUBLANES - (SC_K - 1):])
        else:
            o = i // 2
            w_qkv, w_z, w_ab, ab_const = odd_w[o]
            q, k, v, z, gb, dc_new = _odd_in(x, w_qkv, w_z, w_ab, dconv_w[o], ab_const,
                                             _pad_rows8(dconv_buf[o]), seq=seq, tm=tm)
            og, s_new = _delta_rule(q, k, v, z, gb, delta_s[o].astype(F32), g_onorm[o].reshape(1, -1), seq=seq,
                                    rows=dr)
            x = _proj_ln(og, w_o_o[o], x, row(ln_mix_g), row(ln_mix_b), tm=tm)
            dcs.append(dc_new[:, SUBLANES - (DN_CONV - 1):])
            dss.append(s_new)
        wg, wu, wo, cw = ffn_w[i]
        buf8 = _pad_rows8(ff_buf[i]).reshape(nb, SUBLANES, nc, FF_CHUNK).transpose(0, 2, 1, 3)
        x, ff_new = _conv_ffn(x, buf8, wg, wu, wo, cw, row(ln_ff_g), row(ln_ff_b), seq=seq, tm=tm)
        ff_new = ff_new.transpose(0, 2, 1, 3).reshape(nb, SUBLANES, D_FF)
        ffs.append(ff_new[:, SUBLANES - (FF_K - 1):])
    return (x.reshape(nb, seq, D_MODEL), jnp.stack(lats), jnp.stack(krs), jnp.stack(scs), jnp.stack(dcs),
            jnp.stack(dss), jnp.stack(ffs))


def _prepare(w_in_e, w_uq, w_ukv, w_o_e, w_in_o, a_log, dt_bias, w_o_o, w_ff_in, ffconv_w, w_ff_out):
    even_w = [_prep_even(w_in_e[e], w_uq[e], w_ukv[e]) for e in range(N_EVEN)]
    odd_w = [_prep_odd(w_in_o[o], a_log[o], dt_bias[o]) for o in range(N_ODD)]
    ffn_w = [_prep_ffn(w_ff_in[i], ffconv_w[i], w_ff_out[i]) for i in range(DEPTH)]
    return even_w, odd_w, ffn_w, w_o_e.astype(BF16), w_o_o.astype(BF16)


def kernel(x_prompt, x_sample, cache_mla_latent, cache_mla_krope, state_sconv, state_dconv, state_delta, state_ffconv, w_in_e, g_qnorm, g_kvnorm, w_uq, w_ukv, sc_w, w_o_e, w_in_o, dconv_w, a_log, dt_bias, g_onorm, w_o_o, w_ff_in, ffconv_w, w_ff_out, ln_mix_g, ln_mix_b, ln_ff_g, ln_ff_b):
    even_w, odd_w, ffn_w, w_o_e_b, w_o_o_b = _prepare(
        w_in_e, w_uq, w_ukv, w_o_e, w_in_o, a_log, dt_bias, w_o_o, w_ff_in, ffconv_w, w_ff_out)
    w = (even_w, odd_w, ffn_w, w_o_e_b, w_o_o_b, g_qnorm, g_kvnorm, sc_w, dconv_w, g_onorm,
         ln_mix_g, ln_mix_b, ln_ff_g, ln_ff_b)
    bp = x_prompt.shape[0]
    f32 = x_prompt.dtype
    y_p, p_lat, p_kr, p_sc, p_dc, p_ds, p_ff = _trunk(
        x_prompt, None, None,
        jnp.zeros((N_EVEN, bp, SC_K - 1, SC_WIDTH), f32),
        jnp.zeros((N_ODD, bp, DN_CONV - 1, DN_QKV), f32),
        jnp.zeros((N_ODD, bp, DN_HEADS, DN_DK, DN_DV), f32),
        jnp.zeros((DEPTH, bp, FF_K - 1, D_FF), f32),
        w, tm=512, tq=512, tk=512, dr=256)
    y_s, s_lat, s_kr, s_sc, s_dc, s_ds, s_ff = _trunk(
        x_sample, cache_mla_latent, cache_mla_krope, state_sconv, state_dconv, state_delta, state_ffconv,
        w, tm=512, tq=64, tk=256, dr=256)
    return (y_p, y_s, p_lat, p_kr, p_sc, p_dc, p_ds, p_ff, s_lat, s_kr, s_sc, s_dc, s_ds, s_ff)
```

```python
import functools
import math

import jax
import jax.numpy as jnp
from jax import lax
from jax.experimental import pallas as pl
from jax.experimental.pallas import tpu as pltpu

D_MODEL = 1024
DEPTH = 4
CHUNK = 64
N_EVEN = (DEPTH + 1) // 2
N_ODD = DEPTH // 2
MLA_HEADS = 8
Q_RANK = 384
KV_RANK = 256
NOPE_DIM = 64
ROPE_DIM = 32
V_DIM = 64
ROPE_THETA = 10000.0
MLA_WIDTH = MLA_HEADS * V_DIM
SC_WIDTH = 512
SC_K = 3
DN_HEADS = 8
DN_DK = 128
DN_DV = 128
DN_CONV = 4
DN_KEY = DN_HEADS * DN_DK
DN_VAL = DN_HEADS * DN_DV
DN_QKV = 2 * DN_KEY + DN_VAL
D_FF = 2816
FF_K = 3
ALPHA = (2 * DEPTH) ** 0.25
NORM_EPS = 1e-6
NEG_INF = -1e30
LOG2_E = math.log2(math.e)

F32 = jnp.float32
BF16 = jnp.bfloat16
HIGHEST = lax.Precision.HIGHEST

LANES = 128
SUBLANES = 8
BF16_SUBLANES = 16
HEAD_PAD = 128
FF_CHUNK = 256
DELTA_BASE_BLOCK = 8
ATTN_HEAD_GROUP = 8
VMEM_LIMIT = 56 * 1024 * 1024

NT_DIMS = (((1,), (1,)), ((), ()))
TN_DIMS = (((0,), (0,)), ((), ()))


def _params(*sem):
    return pltpu.CompilerParams(dimension_semantics=sem, vmem_limit_bytes=VMEM_LIMIT)


def _dot(a, b):
    return jnp.dot(a.astype(BF16), b.astype(BF16), preferred_element_type=F32)


def _layer_norm(r, g, b):
    mu = jnp.mean(r, axis=-1, keepdims=True)
    d = r - mu
    var = jnp.mean(d * d, axis=-1, keepdims=True)
    return d * lax.rsqrt(var + NORM_EPS) * g + b


def _rms_norm(x, g):
    return x * lax.rsqrt(jnp.mean(x * x, axis=-1, keepdims=True) + NORM_EPS) * g


def _softplus(x):
    return jnp.maximum(x, 0.0) + jnp.log1p(jnp.exp(-jnp.abs(x)))


def _causal_conv(x, prev, w):
    tm, c = x.shape
    nseq = prev.shape[0]
    k_taps = w.shape[0]
    groups = tm // nseq // SUBLANES
    x4 = x.reshape(nseq, groups, SUBLANES, c)
    prev4 = prev.reshape(nseq, 1, SUBLANES, c)
    sub = lax.broadcasted_iota(jnp.int32, x4.shape, 2)
    y = x4 * w[k_taps - 1:k_taps, :]
    for k in range(1, k_taps):
        rot = pltpu.roll(x4, k, axis=2)
        rot_prev = jnp.concatenate([pltpu.roll(prev4, k, axis=2), rot[:, :groups - 1]], axis=1)
        y = y + jnp.where(sub < k, rot_prev, rot) * w[k_taps - 1 - k:k_taps - k, :]
    return y.reshape(tm, c)


def _run_tails(x, nseq):
    tm, c = x.shape
    run = tm // nseq
    return x.reshape(nseq, run, c)[:, run - SUBLANES:, :]


def _row_tiling(m, seq, tm):
    tm = min(tm, m)
    if seq >= tm:
        return tm, seq // tm, 1
    return tm, 1, tm // seq


def _ffn_kernel(x_ref, win_w_ref, wo_ref, cw_ref, buf_ref, g_ref, b_ref,
                out_ref, nbuf_ref, act_ref, xb_ref, win_ref, *, tiles_per_seq):
    i = pl.program_id(0)
    nseq = win_ref.shape[0]
    xb_ref[...] = x_ref[...].astype(BF16)

    @pl.when(i % tiles_per_seq == 0)
    def _():
        win_ref[...] = buf_ref[...]

    for c in range(D_FF // FF_CHUNK):
        cols = slice(c * FF_CHUNK, (c + 1) * FF_CHUNK)
        up_cols = slice(D_FF + c * FF_CHUNK, D_FF + (c + 1) * FF_CHUNK)
        xb = xb_ref[...]
        gate = jnp.dot(xb, win_w_ref[:, cols], preferred_element_type=F32)
        up = jnp.dot(xb, win_w_ref[:, up_cols], preferred_element_type=F32)
        y = _causal_conv(gate, win_ref[:, :, cols], cw_ref[:, cols])
        tails = _run_tails(gate, nseq)
        win_ref[:, :, cols] = tails
        nbuf_ref[:, :, cols] = tails
        act_ref[:, cols] = (y * jax.nn.sigmoid(y) * up).astype(BF16)

    acc = jnp.dot(act_ref[...], wo_ref[...], preferred_element_type=F32)
    out_ref[...] = _layer_norm(ALPHA * x_ref[...] + acc, g_ref[...], b_ref[...])


def _conv_ffn(x, buf8, w_in, w_out, conv_w, ln_g, ln_b, *, seq, tm):
    m = x.shape[0]
    nb = m // seq
    tm, tps, nseq = _row_tiling(m, seq, tm)
    const2 = lambda i: (0, 0)
    return pl.pallas_call(
        functools.partial(_ffn_kernel, tiles_per_seq=tps),
        grid=(m // tm,),
        in_specs=[
            pl.BlockSpec((tm, D_MODEL), lambda i: (i, 0)),
            pl.BlockSpec(w_in.shape, const2),
            pl.BlockSpec(w_out.shape, const2),
            pl.BlockSpec(conv_w.shape, const2),
            pl.BlockSpec((nseq, SUBLANES, D_FF), lambda i: (i // tps, 0, 0)),
            pl.BlockSpec((1, D_MODEL), const2),
            pl.BlockSpec((1, D_MODEL), const2),
        ],
        out_specs=[
            pl.BlockSpec((tm, D_MODEL), lambda i: (i, 0)),
            pl.BlockSpec((nseq, SUBLANES, D_FF), lambda i: (i // tps, 0, 0)),
        ],
        out_shape=[
            jax.ShapeDtypeStruct((m, D_MODEL), F32),
            jax.ShapeDtypeStruct((nb, SUBLANES, D_FF), F32),
        ],
        scratch_shapes=[
            pltpu.VMEM((tm, D_FF), BF16),
            pltpu.VMEM((tm, D_MODEL), BF16),
            pltpu.VMEM((nseq, SUBLANES, D_FF), F32),
        ],
        compiler_params=_params("arbitrary"),
        name="conv_ffn",
    )(x, w_in, w_out, conv_w, buf8, ln_g, ln_b)


def _proj_ln_kernel(*refs):
    n = (len(refs) - 4) // 2
    x_ref, g_ref, b_ref, out_ref = refs[2 * n:]
    acc = jnp.dot(refs[0][...], refs[n][...], preferred_element_type=F32)
    for j in range(1, n):
        acc = acc + jnp.dot(refs[j][...], refs[n + j][...], preferred_element_type=F32)
    out_ref[...] = _layer_norm(ALPHA * x_ref[...] + acc, g_ref[...], b_ref[...])


def _proj_ln(ys, ws, x, ln_g, ln_b, *, tm):
    m = x.shape[0]
    tm = min(tm, m)
    const2 = lambda i: (0, 0)
    row = lambda i: (i, 0)
    return pl.pallas_call(
        _proj_ln_kernel,
        grid=(m // tm,),
        in_specs=([pl.BlockSpec((tm, y.shape[1]), row) for y in ys]
                  + [pl.BlockSpec(w.shape, const2) for w in ws]
                  + [pl.BlockSpec((tm, D_MODEL), row),
                     pl.BlockSpec((1, D_MODEL), const2),
                     pl.BlockSpec((1, D_MODEL), const2)]),
        out_specs=pl.BlockSpec((tm, D_MODEL), row),
        out_shape=jax.ShapeDtypeStruct((m, D_MODEL), F32),
        compiler_params=_params("arbitrary"),
        name="proj_ln",
    )(*ys, *ws, x, ln_g, ln_b)


def _rope(x, tab):
    return (x * tab[:, 0:LANES]
            + pltpu.roll(x, ROPE_DIM // 2, axis=1) * tab[:, LANES:2 * LANES]
            + pltpu.roll(x, LANES - ROPE_DIM // 2, axis=1) * tab[:, 2 * LANES:3 * LANES])


def _even_in_kernel(x_ref, w1_ref, w2_ref, gq_ref, gkv_ref, wuq_ref, scw_ref, tab_ref, buf_ref,
                    q_ref, lat_ref, kr_ref, osc_ref, nbuf_ref, win_ref, *, tiles_per_seq):
    i = pl.program_id(0)
    nseq = win_ref.shape[0]

    @pl.when(i % tiles_per_seq == 0)
    def _():
        win_ref[...] = buf_ref[...]

    xb = x_ref[...].astype(BF16)
    tab = tab_ref[...]
    h1 = jnp.dot(xb, w1_ref[...], preferred_element_type=F32)
    cq = h1[:, :Q_RANK]
    ckv = h1[:, Q_RANK:Q_RANK + KV_RANK]
    kr_raw = h1[:, Q_RANK + KV_RANK:]
    lat_ref[...] = _rms_norm(ckv, gkv_ref[...])
    kr_ref[...] = _rope(kr_raw, tab)
    q_raw = _dot(_rms_norm(cq, gq_ref[...]), wuq_ref[...])
    scale = (NOPE_DIM + ROPE_DIM) ** -0.5 * LOG2_E
    for h in range(MLA_HEADS):
        sl = slice(h * HEAD_PAD, (h + 1) * HEAD_PAD)
        q_ref[:, sl] = (_rope(q_raw[:, sl], tab) * scale).astype(BF16)

    h2 = jnp.dot(xb, w2_ref[...], preferred_element_type=F32)
    gate_b = h2[:, :SC_WIDTH]
    u_in = h2[:, SC_WIDTH:2 * SC_WIDTH] * h2[:, 2 * SC_WIDTH:]
    u = _causal_conv(u_in, win_ref[...], scw_ref[...])
    tails = _run_tails(u_in, nseq)
    win_ref[...] = tails
    nbuf_ref[...] = tails
    osc_ref[...] = (gate_b * u).astype(BF16)


def _even_in(x, w1, w2, g_q, g_kv, w_uq, sc_w, tab, buf8, *, seq, tm):
    m = x.shape[0]
    nb = m // seq
    tm, tps, nseq = _row_tiling(m, seq, tm)
    if nseq > 1:
        tab = jnp.tile(tab, (nseq, 1))
    const2 = lambda i: (0, 0)
    return pl.pallas_call(
        functools.partial(_even_in_kernel, tiles_per_seq=tps),
        grid=(m // tm,),
        in_specs=[
            pl.BlockSpec((tm, D_MODEL), lambda i: (i, 0)),
            pl.BlockSpec(w1.shape, const2),
            pl.BlockSpec(w2.shape, const2),
            pl.BlockSpec(g_q.shape, const2),
            pl.BlockSpec(g_kv.shape, const2),
            pl.BlockSpec(w_uq.shape, const2),
            pl.BlockSpec(sc_w.shape, const2),
            pl.BlockSpec((tm, 3 * LANES), lambda i: (i % tps, 0)),
            pl.BlockSpec((nseq, SUBLANES, SC_WIDTH), lambda i: (i // tps, 0, 0)),
        ],
        out_specs=[
            pl.BlockSpec((tm, MLA_HEADS * HEAD_PAD), lambda i: (i, 0)),
            pl.BlockSpec((tm, KV_RANK), lambda i: (i, 0)),
            pl.BlockSpec((tm, LANES), lambda i: (i, 0)),
            pl.BlockSpec((tm, SC_WIDTH), lambda i: (i, 0)),
            pl.BlockSpec((nseq, SUBLANES, SC_WIDTH), lambda i: (i // tps, 0, 0)),
        ],
        out_shape=[
            jax.ShapeDtypeStruct((m, MLA_HEADS * HEAD_PAD), BF16),
            jax.ShapeDtypeStruct((m, KV_RANK), F32),
            jax.ShapeDtypeStruct((m, LANES), F32),
            jax.ShapeDtypeStruct((m, SC_WIDTH), BF16),
            jax.ShapeDtypeStruct((nb, SUBLANES, SC_WIDTH), F32),
        ],
        scratch_shapes=[pltpu.VMEM((nseq, SUBLANES, SC_WIDTH), F32)],
        compiler_params=_params("arbitrary"),
        name="even_in",
    )(x, w1, w2, g_q, g_kv, w_uq, sc_w, tab, buf8)


def _kv_up_kernel(lat_ref, kr_ref, wuk_ref, wuvt_ref, k_ref, vt_ref):
    lb = lat_ref[...].astype(BF16)
    kn = jnp.dot(lb, wuk_ref[...], preferred_element_type=F32)
    kr = kr_ref[...]
    for h in range(MLA_HEADS):
        sl = slice(h * HEAD_PAD, (h + 1) * HEAD_PAD)
        k_ref[:, sl] = (kn[:, sl] + kr).astype(BF16)
    vt_ref[...] = lax.dot_general(wuvt_ref[...], lb, NT_DIMS, preferred_element_type=F32).astype(BF16)


def _kv_up(lat, kr_pad, w_uk, w_uv_t, *, nb, t_pad, tm):
    tpb = t_pad // tm
    const2 = lambda i: (0, 0)
    return pl.pallas_call(
        _kv_up_kernel,
        grid=(nb * tpb,),
        in_specs=[
            pl.BlockSpec((tm, KV_RANK), lambda i: (i, 0)),
            pl.BlockSpec((tm, LANES), lambda i: (i, 0)),
            pl.BlockSpec(w_uk.shape, const2),
            pl.BlockSpec(w_uv_t.shape, const2),
        ],
        out_specs=[
            pl.BlockSpec((tm, MLA_HEADS * HEAD_PAD), lambda i: (i, 0)),
            pl.BlockSpec((MLA_WIDTH, tm), lambda i: (i // tpb, i % tpb)),
        ],
        out_shape=[
            jax.ShapeDtypeStruct((nb * t_pad, MLA_HEADS * HEAD_PAD), BF16),
            jax.ShapeDtypeStruct((nb * MLA_WIDTH, t_pad), BF16),
        ],
        compiler_params=_params("arbitrary"),
        name="kv_up",
    )(lat, kr_pad, w_uk, w_uv_t)


def _attn_kernel(q_ref, k_ref, vt_ref, o_ref, m_ref, l_ref, acc_ref, *, tq, tk, past, t_valid, nk):
    qi = pl.program_id(1)
    kj = pl.program_id(2)
    q_lo = past + qi * tq
    chunk_lo = q_lo // CHUNK
    chunk_hi = (q_lo + tq - 1) // CHUNK
    last = jnp.minimum(((chunk_hi + 1) * CHUNK - 1) // tk, nk - 1)
    k_end = (kj + 1) * tk
    full = jnp.logical_and(k_end <= (chunk_lo + 1) * CHUNK, k_end <= t_valid)
    heads = range(MLA_HEADS)
    hsl = [slice(h * HEAD_PAD, (h + 1) * HEAD_PAD) for h in heads]
    vsl = [slice(h * V_DIM, (h + 1) * V_DIM) for h in heads]

    @pl.when(kj == 0)
    def _():
        m_ref[...] = jnp.full_like(m_ref, NEG_INF)
        l_ref[...] = jnp.zeros_like(l_ref)
        acc_ref[...] = jnp.zeros_like(acc_ref)

    def step(masked):
        if masked:
            kpos = kj * tk + lax.broadcasted_iota(jnp.int32, (tk, tq), 0)
            qpos = q_lo + lax.broadcasted_iota(jnp.int32, (tk, tq), 1)
            visible = jnp.logical_and(kpos // CHUNK <= qpos // CHUNK, kpos < t_valid)
        ones = jnp.ones((BF16_SUBLANES, tk), BF16)
        for g0 in range(0, MLA_HEADS, ATTN_HEAD_GROUP):
            grp = range(g0, g0 + ATTN_HEAD_GROUP)
            s = {h: lax.dot_general(k_ref[:, hsl[h]], q_ref[:, hsl[h]], NT_DIMS, preferred_element_type=F32)
                 for h in grp}
            if masked:
                s = {h: jnp.where(visible, s[h], NEG_INF) for h in grp}
            m_prev = {h: m_ref[h:h + 1, :] for h in grp}
            m_new = {h: jnp.maximum(m_prev[h], jnp.max(s[h], axis=0, keepdims=True)) for h in grp}
            alpha = {h: jnp.exp2(m_prev[h] - m_new[h]) for h in grp}
            p = {h: jnp.exp2(s[h] - m_new[h]).astype(BF16) for h in grp}
            pv = {h: jnp.dot(jnp.concatenate([vt_ref[vsl[h], :], ones], axis=0), p[h],
                             preferred_element_type=F32) for h in grp}
            for h in grp:
                l_ref[h:h + 1, :] = alpha[h] * l_ref[h:h + 1, :] + pv[h][V_DIM:V_DIM + 1, :]
                m_ref[h:h + 1, :] = m_new[h]
                acc_ref[vsl[h], :] = alpha[h] * acc_ref[vsl[h], :] + pv[h][:V_DIM, :]

    @pl.when(jnp.logical_and(kj <= last, full))
    def _():
        step(False)

    @pl.when(jnp.logical_and(kj <= last, jnp.logical_not(full)))
    def _():
        step(True)

    @pl.when(kj == last)
    def _():
        for hp in range(MLA_HEADS // 2):
            pair = jnp.concatenate(
                [acc_ref[vsl[h], :] * (1.0 / l_ref[h:h + 1, :]) for h in (2 * hp, 2 * hp + 1)], axis=0)
            o_ref[:, hp * 2 * V_DIM:(hp + 1) * 2 * V_DIM] = pair.T.astype(BF16)


def _attention(q, k, vt, *, nb, sq, t_pad, t_valid, past, tq, tk):
    nq = sq // tq
    nk = t_pad // tk

    def last_tile(qi):
        chunk_hi = (past + qi * tq + tq - 1) // CHUNK
        return jnp.minimum(((chunk_hi + 1) * CHUNK - 1) // tk, nk - 1)

    return pl.pallas_call(
        functools.partial(_attn_kernel, tq=tq, tk=tk, past=past, t_valid=t_valid, nk=nk),
        grid=(nb, nq, nk),
        in_specs=[
            pl.BlockSpec((tq, MLA_HEADS * HEAD_PAD), lambda b, qi, kj: (b * nq + qi, 0)),
            pl.BlockSpec((tk, MLA_HEADS * HEAD_PAD),
                         lambda b, qi, kj: (b * nk + jnp.minimum(kj, last_tile(qi)), 0)),
            pl.BlockSpec((MLA_WIDTH, tk), lambda b, qi, kj: (b, jnp.minimum(kj, last_tile(qi)))),
        ],
        out_specs=pl.BlockSpec((tq, MLA_WIDTH), lambda b, qi, kj: (b * nq + qi, 0)),
        out_shape=jax.ShapeDtypeStruct((nb * sq, MLA_WIDTH), BF16),
        scratch_shapes=[
            pltpu.VMEM((MLA_HEADS, tq), F32),
            pltpu.VMEM((MLA_HEADS, tq), F32),
            pltpu.VMEM((MLA_WIDTH, tq), F32),
        ],
        compiler_params=_params("arbitrary", "arbitrary", "arbitrary"),
        name="chunk_attention",
    )(q, k, vt)


def _odd_in_kernel(x_ref, wqkv_ref, wz_ref, wab_ref, dcw_ref, ab_ref, buf_ref,
                   q_ref, k_ref, v_ref, z_ref, gb_ref, nbuf_ref, win_ref, *, tiles_per_seq):
    i = pl.program_id(0)
    nseq = win_ref.shape[0]

    @pl.when(i % tiles_per_seq == 0)
    def _():
        win_ref[...] = buf_ref[...]

    xb = x_ref[...].astype(BF16)
    outs = (q_ref, k_ref, v_ref)
    for part in range(3):
        cols = slice(part * DN_KEY, (part + 1) * DN_KEY)
        pre = jnp.dot(xb, wqkv_ref[part], preferred_element_type=F32)
        y = _causal_conv(pre, win_ref[:, :, cols], dcw_ref[:, cols])
        tails = _run_tails(pre, nseq)
        win_ref[:, :, cols] = tails
        nbuf_ref[:, :, cols] = tails
        y = y * jax.nn.sigmoid(y)
        if part == 2:
            outs[part][...] = y
        else:
            post = DN_DK ** -0.5 if part == 0 else 1.0
            for h in range(DN_HEADS):
                sl = slice(h * DN_DK, (h + 1) * DN_DK)
                seg = y[:, sl]
                nrm = lax.rsqrt(jnp.sum(seg * seg, axis=-1, keepdims=True) + NORM_EPS)
                outs[part][:, sl] = seg * nrm * post if part == 0 else seg * nrm

    z_ref[...] = jnp.dot(xb, wz_ref[...], preferred_element_type=F32)
    ab = jnp.dot(xb, wab_ref[...], preferred_element_type=F32)
    lane = lax.broadcasted_iota(jnp.int32, ab.shape, 1)
    g = -jnp.exp(ab_ref[0:1, :]) * _softplus(ab + ab_ref[1:2, :])
    gb_ref[...] = jnp.where(lane < DN_HEADS, g, jax.nn.sigmoid(ab))


def _odd_in(x, w_qkv, w_z, w_ab, dconv_w, ab_const, buf8, *, seq, tm):
    m = x.shape[0]
    nb = m // seq
    tm, tps, nseq = _row_tiling(m, seq, tm)
    const2 = lambda i: (0, 0)
    row = lambda i: (i, 0)
    return pl.pallas_call(
        functools.partial(_odd_in_kernel, tiles_per_seq=tps),
        grid=(m // tm,),
        in_specs=[
            pl.BlockSpec((tm, D_MODEL), row),
            pl.BlockSpec(w_qkv.shape, lambda i: (0, 0, 0)),
            pl.BlockSpec(w_z.shape, const2),
            pl.BlockSpec(w_ab.shape, const2),
            pl.BlockSpec(dconv_w.shape, const2),
            pl.BlockSpec(ab_const.shape, const2),
            pl.BlockSpec((nseq, SUBLANES, DN_QKV), lambda i: (i // tps, 0, 0)),
        ],
        out_specs=[
            pl.BlockSpec((tm, DN_KEY), row),
            pl.BlockSpec((tm, DN_KEY), row),
            pl.BlockSpec((tm, DN_VAL), row),
            pl.BlockSpec((tm, DN_VAL), row),
            pl.BlockSpec((tm, LANES), row),
            pl.BlockSpec((nseq, SUBLANES, DN_QKV), lambda i: (i // tps, 0, 0)),
        ],
        out_shape=[
            jax.ShapeDtypeStruct((m, DN_KEY), F32),
            jax.ShapeDtypeStruct((m, DN_KEY), F32),
            jax.ShapeDtypeStruct((m, DN_VAL), F32),
            jax.ShapeDtypeStruct((m, DN_VAL), F32),
            jax.ShapeDtypeStruct((m, LANES), F32),
            jax.ShapeDtypeStruct((nb, SUBLANES, DN_QKV), F32),
        ],
        scratch_shapes=[pltpu.VMEM((nseq, SUBLANES, DN_QKV), F32)],
        compiler_params=_params("arbitrary"),
        name="odd_in",
    )(x, w_qkv, w_z, w_ab, dconv_w, ab_const, buf8)


def _delta_kernel(q_ref, k_ref, v_ref, z_ref, gb_ref, s0_ref, go_ref, o_ref, sout_ref, state_ref, *, n_steps):
    n = pl.program_id(1)
    rows = q_ref.shape[0]
    n_sub = rows // CHUNK

    @pl.when(n == 0)
    def _():
        state_ref[...] = s0_ref[0]

    gb = gb_ref[...]
    ri = lax.broadcasted_iota(jnp.int32, (rows, rows), 0)
    ci = lax.broadcasted_iota(jnp.int32, (rows, rows), 1)

    def blk(size):
        return (ri // size) == (ci // size)

    same = blk(CHUNK)
    lower = ri > ci
    tril = jnp.logical_and(same, ri >= ci)
    strict = jnp.logical_and(same, lower)
    eye = (ri == ci).astype(F32)
    base = DELTA_BASE_BLOCK
    levels = []
    size = base
    while size < CHUNK:
        levels.append(jnp.logical_and(jnp.logical_and(blk(2 * size), jnp.logical_not(blk(size))), lower))
        size *= 2
    in_base = jnp.logical_and(blk(base), lower)
    g_col = jnp.dot(tril.astype(F32), gb, preferred_element_type=F32, precision=HIGHEST)
    g_end = jnp.dot(same.astype(F32), gb, preferred_element_type=F32, precision=HIGHEST)
    g_row = g_col.T
    go = go_ref[...]
    heads = range(DN_HEADS)
    cols = [slice(h * DN_DK, (h + 1) * DN_DK) for h in heads]

    gc = [g_col[:, h:h + 1] for h in heads]
    ge = [g_end[:, h:h + 1] for h in heads]
    decay = [jnp.exp(jnp.where(tril, gc[h] - g_row[h:h + 1, :], -jnp.inf)) for h in heads]
    beta = [gb[:, DN_HEADS + h:DN_HEADS + h + 1] for h in heads]
    kb = [k_ref[:, cols[h]] * beta[h] for h in heads]
    kbf = [k_ref[:, cols[h]].astype(BF16) for h in heads]
    m_full = [jnp.where(strict, lax.dot_general(kb[h].astype(BF16), kbf[h], NT_DIMS,
                                                preferred_element_type=F32) * decay[h], 0.0) for h in heads]
    p = [-jnp.where(in_base, m_full[h], 0.0) for h in heads]
    t = [eye + p[h] for h in heads]
    size = 2
    while size < base:
        p = [_dot(p[h], p[h]) for h in heads]
        t = [t[h] + _dot(t[h], p[h]) for h in heads]
        size *= 2
    for off in levels:
        u = [_dot(t[h], jnp.where(off, m_full[h], 0.0)) for h in heads]
        t = [t[h] - _dot(u[h], t[h]) for h in heads]
    e_g = [jnp.exp(gc[h]) for h in heads]
    sol = [_dot(t[h], jnp.concatenate([v_ref[:, cols[h]] * beta[h], kb[h] * e_g[h]], axis=1)) for h in heads]
    attn = [lax.dot_general(q_ref[:, cols[h]].astype(BF16), kbf[h], NT_DIMS,
                            preferred_element_type=F32) * decay[h] for h in heads]
    qd = [q_ref[:, cols[h]] * e_g[h] for h in heads]
    kt = [k_ref[:, cols[h]] * jnp.exp(ge[h] - gc[h]) for h in heads]
    st = [state_ref[h] for h in heads]
    v_new = [[] for _ in heads]
    o_inter = [[] for _ in heads]
    for c in range(n_sub):
        rs = slice(c * CHUNK, (c + 1) * CHUNK)
        for h in heads:
            r = _dot(jnp.concatenate([sol[h][rs, DN_DV:], qd[h][rs]], axis=0), st[h])
            v_c = sol[h][rs, :DN_DV] - r[:CHUNK]
            v_new[h].append(v_c)
            o_inter[h].append(r[CHUNK:])
            g_tot = jnp.exp(ge[h][c * CHUNK:c * CHUNK + 1, :])
            st[h] = st[h] * g_tot + lax.dot_general(kt[h][rs].astype(BF16), v_c.astype(BF16), TN_DIMS,
                                                    preferred_element_type=F32)
    for h in heads:
        state_ref[h] = st[h]
        o = jnp.concatenate(o_inter[h], axis=0) + _dot(attn[h], jnp.concatenate(v_new[h], axis=0))
        zh = z_ref[:, cols[h]]
        o_ref[:, cols[h]] = (_rms_norm(o, go) * (zh * jax.nn.sigmoid(zh))).astype(BF16)

    @pl.when(n == n_steps - 1)
    def _():
        sout_ref[0] = state_ref[...]


def _delta_rule(q, k, v, z, gb, s0, g_o, *, seq, rows):
    m = q.shape[0]
    nb = m // seq
    rows = min(rows, seq)
    ns = seq // rows
    row = lambda b, n: (b * ns + n, 0)
    st_spec = pl.BlockSpec((1, DN_HEADS, DN_DK, DN_DV), lambda b, n: (b, 0, 0, 0))
    return pl.pallas_call(
        functools.partial(_delta_kernel, n_steps=ns),
        grid=(nb, ns),
        in_specs=[
            pl.BlockSpec((rows, DN_KEY), row),
            pl.BlockSpec((rows, DN_KEY), row),
            pl.BlockSpec((rows, DN_VAL), row),
            pl.BlockSpec((rows, DN_VAL), row),
            pl.BlockSpec((rows, LANES), row),
            st_spec,
            pl.BlockSpec((1, DN_DV), lambda b, n: (0, 0)),
        ],
        out_specs=[pl.BlockSpec((rows, DN_VAL), row), st_spec],
        out_shape=[
            jax.ShapeDtypeStruct((m, DN_VAL), BF16),
            jax.ShapeDtypeStruct((nb, DN_HEADS, DN_DK, DN_DV), F32),
        ],
        scratch_shapes=[pltpu.VMEM((DN_HEADS, DN_DK, DN_DV), F32)],
        compiler_params=_params("arbitrary", "arbitrary"),
        name="gated_delta",
    )(q, k, v, z, gb, s0, g_o)


def _rope_table(pos):
    half = ROPE_DIM // 2
    inv = ROPE_THETA ** (-jnp.arange(half, dtype=F32) / half)
    ang = pos.astype(F32)[:, None] * inv[None, :]
    cs = jnp.concatenate([jnp.cos(ang), jnp.sin(ang)], -1)
    place = jnp.zeros((2 * half, 3 * LANES), F32)
    j = jnp.arange(half)
    place = place.at[j, NOPE_DIM + j].set(1.0).at[j, NOPE_DIM + half + j].set(1.0)
    place = place.at[half + j, LANES + NOPE_DIM + half + j].set(1.0)
    place = place.at[half + j, 2 * LANES + NOPE_DIM + j].set(-1.0)
    ones = (jnp.arange(3 * LANES) < NOPE_DIM).astype(F32)
    return jnp.dot(cs, place, precision=HIGHEST) + ones[None, :]


def _pad_heads(w, per_head, keep):
    kdim = w.shape[0]
    w = w.reshape(kdim, MLA_HEADS, per_head)[:, :, :keep]
    w = jnp.pad(w, ((0, 0), (0, 0), (0, HEAD_PAD - keep)))
    return w.reshape(kdim, MLA_HEADS * HEAD_PAD)


def _prep_even(w_in, w_uq, w_ukv):
    c1 = Q_RANK + KV_RANK
    w_kr = jnp.pad(w_in[:, c1:c1 + ROPE_DIM], ((0, 0), (NOPE_DIM, HEAD_PAD - NOPE_DIM - ROPE_DIM)))
    w1 = jnp.concatenate([w_in[:, :c1], w_kr], -1).astype(BF16)
    w2 = w_in[:, c1 + ROPE_DIM:].astype(BF16)
    wuq = _pad_heads(w_uq, NOPE_DIM + ROPE_DIM, NOPE_DIM + ROPE_DIM).astype(BF16)
    wuk = _pad_heads(w_ukv, NOPE_DIM + V_DIM, NOPE_DIM).astype(BF16)
    wuv = w_ukv.reshape(KV_RANK, MLA_HEADS, NOPE_DIM + V_DIM)[:, :, NOPE_DIM:].reshape(KV_RANK, MLA_WIDTH)
    return w1, w2, wuq, wuk, wuv.T.astype(BF16)


def _prep_odd(w_in, a_log, dt_bias):
    w_qkv = w_in[:, :DN_QKV].reshape(D_MODEL, 3, DN_KEY).transpose(1, 0, 2).astype(BF16)
    w_z = w_in[:, DN_QKV:DN_QKV + DN_VAL].astype(BF16)
    w_ab = jnp.pad(w_in[:, DN_QKV + DN_VAL:], ((0, 0), (0, LANES - 2 * DN_HEADS))).astype(BF16)
    ab_const = jnp.zeros((SUBLANES, LANES), F32)
    ab_const = ab_const.at[0, :DN_HEADS].set(a_log.astype(F32)).at[1, :DN_HEADS].set(dt_bias.astype(F32))
    return w_qkv, w_z, w_ab, ab_const


def _prep_ffn(w_in, conv_w, w_out):
    return w_in.astype(BF16), w_out.astype(BF16), conv_w


def _pad_rows8(buf):
    return jnp.pad(buf.astype(F32), ((0, 0), (SUBLANES - buf.shape[1], 0), (0, 0)))


def _trunk(x, past_lat, past_kr, sc_buf, dconv_buf, delta_s, ff_buf, w, *, tm, tq, tk, dr):
    (even_w, odd_w, ffn_w, w_o_e, w_o_o, g_qnorm, g_kvnorm, sc_w, dconv_w, g_onorm,
     ln_mix_g, ln_mix_b, ln_ff_g, ln_ff_b) = w
    nb, seq, _ = x.shape
    past = 0 if past_lat is None else past_lat.shape[2]
    m = nb * seq
    x = x.reshape(m, D_MODEL)
    tab = _rope_table(past + jnp.arange(seq, dtype=jnp.int32))
    t_valid = past + seq
    tq = min(tq, seq)
    tk = min(tk, t_valid)
    t_pad = -(-t_valid // tk) * tk
    lats, krs, scs, dcs, dss, ffs = [], [], [], [], [], []
    for i in range(DEPTH):
        row = lambda a: a[i].reshape(1, -1)
        if i % 2 == 0:
            e = i // 2
            w1, w2, wuq, wuk, wuv = even_w[e]
            q, lat, kr_pad, o_sc, sc_new = _even_in(
                x, w1, w2, g_qnorm[e].reshape(1, -1), g_kvnorm[e].reshape(1, -1), wuq, sc_w[e], tab,
                _pad_rows8(sc_buf[e]), seq=seq, tm=tm)
            lat_all = lat.reshape(nb, seq, KV_RANK)
            kr_all = kr_pad.reshape(nb, seq, LANES)
            if past:
                kr_past = jnp.pad(past_kr[e].astype(F32),
                                  ((0, 0), (0, 0), (NOPE_DIM, HEAD_PAD - NOPE_DIM - ROPE_DIM)))
                lat_all = jnp.concatenate([past_lat[e].astype(F32), lat_all], 1)
                kr_all = jnp.concatenate([kr_past, kr_all], 1)
            if t_pad != t_valid:
                lat_all = jnp.pad(lat_all, ((0, 0), (0, t_pad - t_valid), (0, 0)))
                kr_all = jnp.pad(kr_all, ((0, 0), (0, t_pad - t_valid), (0, 0)))
            k, v = _kv_up(lat_all.reshape(nb * t_pad, KV_RANK), kr_all.reshape(nb * t_pad, LANES),
                          wuk, wuv, nb=nb, t_pad=t_pad, tm=tk)
            o_att = _attention(q, k, v, nb=nb, sq=seq, t_pad=t_pad, t_valid=t_valid, past=past, tq=tq, tk=tk)
            x = _proj_ln([o_att, o_sc], [w_o_e[e][:MLA_WIDTH], w_o_e[e][MLA_WIDTH:]], x,
                         row(ln_mix_g), row(ln_mix_b), tm=tm)
            lats.append(lat.reshape(nb, seq, KV_RANK))
            krs.append(kr_pad.reshape(nb, seq, LANES)[:, :, NOPE_DIM:NOPE_DIM + ROPE_DIM])
            scs.append(sc_new[:, SUBLANES - (SC_K - 1):])
        else:
            o = i // 2
            w_qkv, w_z, w_ab, ab_const = odd_w[o]
            q, k, v, z, gb, dc_new = _odd_in(x, w_qkv, w_z, w_ab, dconv_w[o], ab_const,
                                             _pad_rows8(dconv_buf[o]), seq=seq, tm=tm)
            og, s_new = _delta_rule(q, k, v, z, gb, delta_s[o].astype(F32), g_onorm[o].reshape(1, -1), seq=seq,
                                    rows=dr)
            x = _proj_ln([og], [w_o_o[o]], x, row(ln_mix_g), row(ln_mix_b), tm=tm)
            dcs.append(dc_new[:, SUBLANES - (DN_CONV - 1):])
            dss.append(s_new)
        w_ff_in, w_ff_out, cw = ffn_w[i]
        x, ff_new = _conv_ffn(x, _pad_rows8(ff_buf[i]), w_ff_in, w_ff_out, cw, row(ln_ff_g), row(ln_ff_b),
                              seq=seq, tm=tm)
        ffs.append(ff_new[:, SUBLANES - (FF_K - 1):])
    return (x.reshape(nb, seq, D_MODEL), jnp.stack(lats), jnp.stack(krs), jnp.stack(scs), jnp.stack(dcs),
            jnp.stack(dss), jnp.stack(ffs))


def _prepare(w_in_e, w_uq, w_ukv, w_o_e, w_in_o, a_log, dt_bias, w_o_o, w_ff_in, ffconv_w, w_ff_out):
    even_w = [_prep_even(w_in_e[e], w_uq[e], w_ukv[e]) for e in range(N_EVEN)]
    odd_w = [_prep_odd(w_in_o[o], a_log[o], dt_bias[o]) for o in range(N_ODD)]
    ffn_w = [_prep_ffn(w_ff_in[i], ffconv_w[i], w_ff_out[i]) for i in range(DEPTH)]
    return even_w, odd_w, ffn_w, w_o_e.astype(BF16), w_o_o.astype(BF16)


def kernel(x_prompt, x_sample, cache_mla_latent, cache_mla_krope, state_sconv, state_dconv, state_delta, state_ffconv, w_in_e, g_qnorm, g_kvnorm, w_uq, w_ukv, sc_w, w_o_e, w_in_o, dconv_w, a_log, dt_bias, g_onorm, w_o_o, w_ff_in, ffconv_w, w_ff_out, ln_mix_g, ln_mix_b, ln_ff_g, ln_ff_b):
    even_w, odd_w, ffn_w, w_o_e_b, w_o_o_b = _prepare(
        w_in_e, w_uq, w_ukv, w_o_e, w_in_o, a_log, dt_bias, w_o_o, w_ff_in, ffconv_w, w_ff_out)
    w = (even_w, odd_w, ffn_w, w_o_e_b, w_o_o_b, g_qnorm, g_kvnorm, sc_w, dconv_w, g_onorm,
         ln_mix_g, ln_mix_b, ln_ff_g, ln_ff_b)
    bp = x_prompt.shape[0]
    f32 = x_prompt.dtype
    y_p, p_lat, p_kr, p_sc, p_dc, p_ds, p_ff = _trunk(
        x_prompt, None, None,
        jnp.zeros((N_EVEN, bp, SC_K - 1, SC_WIDTH), f32),
        jnp.zeros((N_ODD, bp, DN_CONV - 1, DN_QKV), f32),
        jnp.zeros((N_ODD, bp, DN_HEADS, DN_DK, DN_DV), f32),
        jnp.zeros((DEPTH, bp, FF_K - 1, D_FF), f32),
        w, tm=512, tq=512, tk=512, dr=256)
    y_s, s_lat, s_kr, s_sc, s_dc, s_ds, s_ff = _trunk(
        x_sample, cache_mla_latent, cache_mla_krope, state_sconv, state_dconv, state_delta, state_ffconv,
        w, tm=512, tq=64, tk=768, dr=256)
    return (y_p, y_s, p_lat, p_kr, p_sc, p_dc, p_ds, p_ff, s_lat, s_kr, s_sc, s_dc, s_ds, s_ff)
```

```python
import functools
import math

import jax
import jax.numpy as jnp
from jax import lax
from jax.experimental import pallas as pl
from jax.experimental.pallas import tpu as pltpu

D_MODEL = 1024
DEPTH = 4
CHUNK = 64
N_EVEN = (DEPTH + 1) // 2
N_ODD = DEPTH // 2
MLA_HEADS = 8
Q_RANK = 384
KV_RANK = 256
NOPE_DIM = 64
ROPE_DIM = 32
V_DIM = 64
ROPE_THETA = 10000.0
MLA_WIDTH = MLA_HEADS * V_DIM
SC_WIDTH = 512
SC_K = 3
DN_HEADS = 8
DN_DK = 128
DN_DV = 128
DN_CONV = 4
DN_KEY = DN_HEADS * DN_DK
DN_VAL = DN_HEADS * DN_DV
DN_QKV = 2 * DN_KEY + DN_VAL
D_FF = 2816
FF_K = 3
ALPHA = (2 * DEPTH) ** 0.25
NORM_EPS = 1e-6
NEG_INF = -1e30
LOG2_E = math.log2(math.e)

F32 = jnp.float32
BF16 = jnp.bfloat16
HIGHEST = lax.Precision.HIGHEST

LANES = 128
SUBLANES = 8
BF16_SUBLANES = 16
HEAD_PAD = 128
FF_CHUNK = 256
DELTA_BASE_BLOCK = 8
ATTN_HEAD_GROUP = 8
FAST_HEAD_GROUP = 8
FAST_SOFTMAX_MAX_RISE = 60.0
VMEM_LIMIT = 56 * 1024 * 1024

NT_DIMS = (((1,), (1,)), ((), ()))
TN_DIMS = (((0,), (0,)), ((), ()))


def _params(*sem):
    return pltpu.CompilerParams(dimension_semantics=sem, vmem_limit_bytes=VMEM_LIMIT)


def _dot(a, b):
    return jnp.dot(a.astype(BF16), b.astype(BF16), preferred_element_type=F32)


def _layer_norm(r, g, b):
    mu = jnp.mean(r, axis=-1, keepdims=True)
    d = r - mu
    var = jnp.mean(d * d, axis=-1, keepdims=True)
    return d * lax.rsqrt(var + NORM_EPS) * g + b


def _rms_norm(x, g):
    return x * lax.rsqrt(jnp.mean(x * x, axis=-1, keepdims=True) + NORM_EPS) * g


def _softplus(x):
    return jnp.maximum(x, 0.0) + jnp.log1p(jnp.exp(-jnp.abs(x)))


def _causal_conv(x, prev, w):
    tm, c = x.shape
    nseq = prev.shape[0]
    k_taps = w.shape[0]
    groups = tm // nseq // SUBLANES
    x4 = x.reshape(nseq, groups, SUBLANES, c)
    prev4 = prev.reshape(nseq, 1, SUBLANES, c)
    sub = lax.broadcasted_iota(jnp.int32, x4.shape, 2)
    y = x4 * w[k_taps - 1:k_taps, :]
    for k in range(1, k_taps):
        rot = pltpu.roll(x4, k, axis=2)
        rot_prev = jnp.concatenate([pltpu.roll(prev4, k, axis=2), rot[:, :groups - 1]], axis=1)
        y = y + jnp.where(sub < k, rot_prev, rot) * w[k_taps - 1 - k:k_taps - k, :]
    return y.reshape(tm, c)


def _run_tails(x, nseq):
    tm, c = x.shape
    run = tm // nseq
    return x.reshape(nseq, run, c)[:, run - SUBLANES:, :]


def _row_tiling(m, seq, tm):
    tm = min(tm, m)
    if seq >= tm:
        return tm, seq // tm, 1
    return tm, 1, tm // seq


def _ffn_kernel(x_ref, win_w_ref, wo_ref, cw_ref, buf_ref, g_ref, b_ref,
                out_ref, nbuf_ref, act_ref, xb_ref, win_ref, *, tiles_per_seq):
    i = pl.program_id(0)
    nseq = win_ref.shape[0]
    xb_ref[...] = x_ref[...].astype(BF16)

    @pl.when(i % tiles_per_seq == 0)
    def _():
        win_ref[...] = buf_ref[...]

    for c in range(D_FF // FF_CHUNK):
        cols = slice(c * FF_CHUNK, (c + 1) * FF_CHUNK)
        up_cols = slice(D_FF + c * FF_CHUNK, D_FF + (c + 1) * FF_CHUNK)
        xb = xb_ref[...]
        gate = jnp.dot(xb, win_w_ref[:, cols], preferred_element_type=F32)
        up = jnp.dot(xb, win_w_ref[:, up_cols], preferred_element_type=F32)
        y = _causal_conv(gate, win_ref[:, :, cols], cw_ref[:, cols])
        tails = _run_tails(gate, nseq)
        win_ref[:, :, cols] = tails
        nbuf_ref[:, :, cols] = tails
        act_ref[:, cols] = (y * jax.nn.sigmoid(y) * up).astype(BF16)

    acc = jnp.dot(act_ref[...], wo_ref[...], preferred_element_type=F32)
    out_ref[...] = _layer_norm(ALPHA * x_ref[...] + acc, g_ref[...], b_ref[...])


def _conv_ffn(x, buf8, w_in, w_out, conv_w, ln_g, ln_b, *, seq, tm):
    m = x.shape[0]
    nb = m // seq
    tm, tps, nseq = _row_tiling(m, seq, tm)
    const2 = lambda i: (0, 0)
    return pl.pallas_call(
        functools.partial(_ffn_kernel, tiles_per_seq=tps),
        grid=(m // tm,),
        in_specs=[
            pl.BlockSpec((tm, D_MODEL), lambda i: (i, 0)),
            pl.BlockSpec(w_in.shape, const2),
            pl.BlockSpec(w_out.shape, const2),
            pl.BlockSpec(conv_w.shape, const2),
            pl.BlockSpec((nseq, SUBLANES, D_FF), lambda i: (i // tps, 0, 0)),
            pl.BlockSpec((1, D_MODEL), const2),
            pl.BlockSpec((1, D_MODEL), const2),
        ],
        out_specs=[
            pl.BlockSpec((tm, D_MODEL), lambda i: (i, 0)),
            pl.BlockSpec((nseq, SUBLANES, D_FF), lambda i: (i // tps, 0, 0)),
        ],
        out_shape=[
            jax.ShapeDtypeStruct((m, D_MODEL), F32),
            jax.ShapeDtypeStruct((nb, SUBLANES, D_FF), F32),
        ],
        scratch_shapes=[
            pltpu.VMEM((tm, D_FF), BF16),
            pltpu.VMEM((tm, D_MODEL), BF16),
            pltpu.VMEM((nseq, SUBLANES, D_FF), F32),
        ],
        compiler_params=_params("arbitrary"),
        name="conv_ffn",
    )(x, w_in, w_out, conv_w, buf8, ln_g, ln_b)


def _proj_ln_kernel(*refs):
    n = (len(refs) - 4) // 2
    x_ref, g_ref, b_ref, out_ref = refs[2 * n:]
    acc = jnp.dot(refs[0][...], refs[n][...], preferred_element_type=F32)
    for j in range(1, n):
        acc = acc + jnp.dot(refs[j][...], refs[n + j][...], preferred_element_type=F32)
    out_ref[...] = _layer_norm(ALPHA * x_ref[...] + acc, g_ref[...], b_ref[...])


def _proj_ln(ys, ws, x, ln_g, ln_b, *, tm):
    m = x.shape[0]
    tm = min(tm, m)
    const2 = lambda i: (0, 0)
    row = lambda i: (i, 0)
    return pl.pallas_call(
        _proj_ln_kernel,
        grid=(m // tm,),
        in_specs=([pl.BlockSpec((tm, y.shape[1]), row) for y in ys]
                  + [pl.BlockSpec(w.shape, const2) for w in ws]
                  + [pl.BlockSpec((tm, D_MODEL), row),
                     pl.BlockSpec((1, D_MODEL), const2),
                     pl.BlockSpec((1, D_MODEL), const2)]),
        out_specs=pl.BlockSpec((tm, D_MODEL), row),
        out_shape=jax.ShapeDtypeStruct((m, D_MODEL), F32),
        compiler_params=_params("arbitrary"),
        name="proj_ln",
    )(*ys, *ws, x, ln_g, ln_b)


def _rope(x, tab):
    return (x * tab[:, 0:LANES]
            + pltpu.roll(x, ROPE_DIM // 2, axis=1) * tab[:, LANES:2 * LANES]
            + pltpu.roll(x, LANES - ROPE_DIM // 2, axis=1) * tab[:, 2 * LANES:3 * LANES])


def _even_in_kernel(x_ref, w1_ref, w2_ref, gq_ref, gkv_ref, wuq_ref, scw_ref, tab_ref, buf_ref,
                    q_ref, lat_ref, kr_ref, osc_ref, nbuf_ref, win_ref, *, tiles_per_seq):
    i = pl.program_id(0)
    nseq = win_ref.shape[0]

    @pl.when(i % tiles_per_seq == 0)
    def _():
        win_ref[...] = buf_ref[...]

    xb = x_ref[...].astype(BF16)
    tab = tab_ref[...]
    h1 = jnp.dot(xb, w1_ref[...], preferred_element_type=F32)
    cq = h1[:, :Q_RANK]
    ckv = h1[:, Q_RANK:Q_RANK + KV_RANK]
    kr_raw = h1[:, Q_RANK + KV_RANK:]
    lat_ref[...] = _rms_norm(ckv, gkv_ref[...])
    kr_ref[...] = _rope(kr_raw, tab)
    q_raw = _dot(_rms_norm(cq, gq_ref[...]), wuq_ref[...])
    scale = (NOPE_DIM + ROPE_DIM) ** -0.5 * LOG2_E
    for h in range(MLA_HEADS):
        sl = slice(h * HEAD_PAD, (h + 1) * HEAD_PAD)
        q_ref[:, sl] = (_rope(q_raw[:, sl], tab) * scale).astype(BF16)

    h2 = jnp.dot(xb, w2_ref[...], preferred_element_type=F32)
    gate_b = h2[:, :SC_WIDTH]
    u_in = h2[:, SC_WIDTH:2 * SC_WIDTH] * h2[:, 2 * SC_WIDTH:]
    u = _causal_conv(u_in, win_ref[...], scw_ref[...])
    tails = _run_tails(u_in, nseq)
    win_ref[...] = tails
    nbuf_ref[...] = tails
    osc_ref[...] = (gate_b * u).astype(BF16)


def _even_in(x, w1, w2, g_q, g_kv, w_uq, sc_w, tab, buf8, *, seq, tm):
    m = x.shape[0]
    nb = m // seq
    tm, tps, nseq = _row_tiling(m, seq, tm)
    if nseq > 1:
        tab = jnp.tile(tab, (nseq, 1))
    const2 = lambda i: (0, 0)
    return pl.pallas_call(
        functools.partial(_even_in_kernel, tiles_per_seq=tps),
        grid=(m // tm,),
        in_specs=[
            pl.BlockSpec((tm, D_MODEL), lambda i: (i, 0)),
            pl.BlockSpec(w1.shape, const2),
            pl.BlockSpec(w2.shape, const2),
            pl.BlockSpec(g_q.shape, const2),
            pl.BlockSpec(g_kv.shape, const2),
            pl.BlockSpec(w_uq.shape, const2),
            pl.BlockSpec(sc_w.shape, const2),
            pl.BlockSpec((tm, 3 * LANES), lambda i: (i % tps, 0)),
            pl.BlockSpec((nseq, SUBLANES, SC_WIDTH), lambda i: (i // tps, 0, 0)),
        ],
        out_specs=[
            pl.BlockSpec((tm, MLA_HEADS * HEAD_PAD), lambda i: (i, 0)),
            pl.BlockSpec((tm, KV_RANK), lambda i: (i, 0)),
            pl.BlockSpec((tm, LANES), lambda i: (i, 0)),
            pl.BlockSpec((tm, SC_WIDTH), lambda i: (i, 0)),
            pl.BlockSpec((nseq, SUBLANES, SC_WIDTH), lambda i: (i // tps, 0, 0)),
        ],
        out_shape=[
            jax.ShapeDtypeStruct((m, MLA_HEADS * HEAD_PAD), BF16),
            jax.ShapeDtypeStruct((m, KV_RANK), F32),
            jax.ShapeDtypeStruct((m, LANES), F32),
            jax.ShapeDtypeStruct((m, SC_WIDTH), BF16),
            jax.ShapeDtypeStruct((nb, SUBLANES, SC_WIDTH), F32),
        ],
        scratch_shapes=[pltpu.VMEM((nseq, SUBLANES, SC_WIDTH), F32)],
        compiler_params=_params("arbitrary"),
        name="even_in",
    )(x, w1, w2, g_q, g_kv, w_uq, sc_w, tab, buf8)


def _kv_up_kernel(lat_ref, kr_ref, wuk_ref, wuvt_ref, k_ref, vt_ref):
    lb = lat_ref[...].astype(BF16)
    kn = jnp.dot(lb, wuk_ref[...], preferred_element_type=F32)
    kr = kr_ref[...]
    for h in range(MLA_HEADS):
        sl = slice(h * HEAD_PAD, (h + 1) * HEAD_PAD)
        k_ref[:, sl] = (kn[:, sl] + kr).astype(BF16)
    vt_ref[...] = lax.dot_general(wuvt_ref[...], lb, NT_DIMS, preferred_element_type=F32).astype(BF16)


def _kv_up(lat, kr_pad, w_uk, w_uv_t, *, nb, t_pad, tm):
    tpb = t_pad // tm
    const2 = lambda i: (0, 0)
    return pl.pallas_call(
        _kv_up_kernel,
        grid=(nb * tpb,),
        in_specs=[
            pl.BlockSpec((tm, KV_RANK), lambda i: (i, 0)),
            pl.BlockSpec((tm, LANES), lambda i: (i, 0)),
            pl.BlockSpec(w_uk.shape, const2),
            pl.BlockSpec(w_uv_t.shape, const2),
        ],
        out_specs=[
            pl.BlockSpec((tm, MLA_HEADS * HEAD_PAD), lambda i: (i, 0)),
            pl.BlockSpec((MLA_WIDTH, tm), lambda i: (i // tpb, i % tpb)),
        ],
        out_shape=[
            jax.ShapeDtypeStruct((nb * t_pad, MLA_HEADS * HEAD_PAD), BF16),
            jax.ShapeDtypeStruct((nb * MLA_WIDTH, t_pad), BF16),
        ],
        compiler_params=_params("arbitrary"),
        name="kv_up",
    )(lat, kr_pad, w_uk, w_uv_t)


def _attn_kernel(q_ref, k_ref, vt_ref, o_ref, m_ref, l_ref, acc_ref, *, tq, tk, past, t_valid, nk):
    qi = pl.program_id(1)
    kj = pl.program_id(2)
    q_lo = past + qi * tq
    chunk_lo = q_lo // CHUNK
    chunk_hi = (q_lo + tq - 1) // CHUNK
    last = jnp.minimum(((chunk_hi + 1) * CHUNK - 1) // tk, nk - 1)
    k_end = (kj + 1) * tk
    full = jnp.logical_and(k_end <= (chunk_lo + 1) * CHUNK, k_end <= t_valid)
    heads = list(range(MLA_HEADS))
    hsl =[slice(h * HEAD_PAD, (h + 1) * HEAD_PAD) for h in heads]
    vsl = [slice(h * V_DIM, (h + 1) * V_DIM) for h in heads]

    @pl.when(kj == 0)
    def _():
        m_ref[...] = jnp.full_like(m_ref, NEG_INF)
        l_ref[...] = jnp.zeros_like(l_ref)
        acc_ref[...] = jnp.zeros_like(acc_ref)

    def visibility():
        kpos = kj * tk + lax.broadcasted_iota(jnp.int32, (tk, tq), 0)
        qpos = q_lo + lax.broadcasted_iota(jnp.int32, (tk, tq), 1)
        return jnp.logical_and(kpos // CHUNK <= qpos // CHUNK, kpos < t_valid)

    def scores(h, visible):
        s = lax.dot_general(k_ref[:, hsl[h]], q_ref[:, hsl[h]], NT_DIMS, preferred_element_type=F32)
        return s if visible is None else jnp.where(visible, s, NEG_INF)

    def step(masked):
        visible = visibility() if masked else None
        ones = jnp.ones((BF16_SUBLANES, tk), BF16)
        for g0 in range(0, MLA_HEADS, ATTN_HEAD_GROUP):
            grp = range(g0, g0 + ATTN_HEAD_GROUP)
            s = {h: scores(h, visible) for h in grp}
            m_prev = {h: m_ref[h:h + 1, :] for h in grp}
            m_new = {h: jnp.maximum(m_prev[h], jnp.max(s[h], axis=0, keepdims=True)) for h in grp}
            alpha = {h: jnp.exp2(m_prev[h] - m_new[h]) for h in grp}
            p = {h: jnp.exp2(s[h] - m_new[h]).astype(BF16) for h in grp}
            pv = {h: jnp.dot(jnp.concatenate([vt_ref[vsl[h], :], ones], axis=0), p[h],
                             preferred_element_type=F32) for h in grp}
            for h in grp:
                l_ref[h:h + 1, :] = alpha[h] * l_ref[h:h + 1, :] + pv[h][V_DIM:V_DIM + 1, :]
                m_ref[h:h + 1, :] = m_new[h]
                acc_ref[vsl[h], :] = alpha[h] * acc_ref[vsl[h], :] + pv[h][:V_DIM, :]

    def fast_step(masked):
        visible = visibility() if masked else None
        ones = jnp.ones((BF16_SUBLANES, tk), BF16)
        pv, t_max = {}, {}
        for g0 in range(0, MLA_HEADS, FAST_HEAD_GROUP):
            grp = range(g0, g0 + FAST_HEAD_GROUP)
            s = {h: scores(h, visible) for h in grp}
            p = {h: jnp.exp2(s[h] - m_ref[h:h + 1, :]).astype(BF16) for h in grp}
            for h in grp:
                t_max[h] = jnp.max(s[h], axis=0, keepdims=True)
            for h in grp:
                pv[h] = jnp.dot(jnp.concatenate([vt_ref[vsl[h], :], ones], axis=0), p[h],
                                preferred_element_type=F32)
        rise = t_max[0] - m_ref[0:1, :]
        for h in heads[1:]:
            rise = jnp.maximum(rise, t_max[h] - m_ref[h:h + 1, :])
        in_range = jnp.max(rise) <= FAST_SOFTMAX_MAX_RISE

        @pl.when(in_range)
        def _():
            for h in heads:
                m_prev = m_ref[h:h + 1, :]
                m_new = jnp.maximum(m_prev, t_max[h])
                alpha = jnp.exp2(m_prev - m_new)
                l_ref[h:h + 1, :] = (l_ref[h:h + 1, :] + pv[h][V_DIM:V_DIM + 1, :]) * alpha
                acc_ref[vsl[h], :] = (acc_ref[vsl[h], :] + pv[h][:V_DIM, :]) * alpha
                m_ref[h:h + 1, :] = m_new

        @pl.when(jnp.logical_not(in_range))
        def _():
            step(masked)

    first = kj == 0
    live = kj <= last

    @pl.when(jnp.logical_and(first, full))
    def _():
        step(False)

    @pl.when(jnp.logical_and(first, jnp.logical_not(full)))
    def _():
        step(True)

    @pl.when(jnp.logical_and(jnp.logical_and(live, jnp.logical_not(first)), full))
    def _():
        fast_step(False)

    @pl.when(jnp.logical_and(jnp.logical_and(live, jnp.logical_not(first)), jnp.logical_not(full)))
    def _():
        fast_step(True)

    @pl.when(kj == last)
    def _():
        for hp in range(MLA_HEADS // 2):
            pair = jnp.concatenate(
                [acc_ref[vsl[h], :] * (1.0 / l_ref[h:h + 1, :]) for h in (2 * hp, 2 * hp + 1)], axis=0)
            o_ref[:, hp * 2 * V_DIM:(hp + 1) * 2 * V_DIM] = pair.T.astype(BF16)


def _attention(q, k, vt, *, nb, sq, t_pad, t_valid, past, tq, tk):
    nq = sq // tq
    nk = t_pad // tk

    def last_tile(qi):
        chunk_hi = (past + qi * tq + tq - 1) // CHUNK
        return jnp.minimum(((chunk_hi + 1) * CHUNK - 1) // tk, nk - 1)

    return pl.pallas_call(
        functools.partial(_attn_kernel, tq=tq, tk=tk, past=past, t_valid=t_valid, nk=nk),
        grid=(nb, nq, nk),
        in_specs=[
            pl.BlockSpec((tq, MLA_HEADS * HEAD_PAD), lambda b, qi, kj: (b * nq + qi, 0)),
            pl.BlockSpec((tk, MLA_HEADS * HEAD_PAD),
                         lambda b, qi, kj: (b * nk + jnp.minimum(kj, last_tile(qi)), 0)),
            pl.BlockSpec((MLA_WIDTH, tk), lambda b, qi, kj: (b, jnp.minimum(kj, last_tile(qi)))),
        ],
        out_specs=pl.BlockSpec((tq, MLA_WIDTH), lambda b, qi, kj: (b * nq + qi, 0)),
        out_shape=jax.ShapeDtypeStruct((nb * sq, MLA_WIDTH), BF16),
        scratch_shapes=[
            pltpu.VMEM((MLA_HEADS, tq), F32),
            pltpu.VMEM((MLA_HEADS, tq), F32),
            pltpu.VMEM((MLA_WIDTH, tq), F32),
        ],
        compiler_params=_params("arbitrary", "arbitrary", "arbitrary"),
        name="chunk_attention",
    )(q, k, vt)


def _odd_in_kernel(x_ref, wqkv_ref, wz_ref, wab_ref, dcw_ref, ab_ref, buf_ref,
                   q_ref, k_ref, v_ref, z_ref, gb_ref, nbuf_ref, win_ref, *, tiles_per_seq):
    i = pl.program_id(0)
    nseq = win_ref.shape[0]

    @pl.when(i % tiles_per_seq == 0)
    def _():
        win_ref[...] = buf_ref[...]

    xb = x_ref[...].astype(BF16)
    outs = (q_ref, k_ref, v_ref)
    for part in range(3):
        cols = slice(part * DN_KEY, (part + 1) * DN_KEY)
        pre = jnp.dot(xb, wqkv_ref[part], preferred_element_type=F32)
        y = _causal_conv(pre, win_ref[:, :, cols], dcw_ref[:, cols])
        tails = _run_tails(pre, nseq)
        win_ref[:, :, cols] = tails
        nbuf_ref[:, :, cols] = tails
        y = y * jax.nn.sigmoid(y)
        if part == 2:
            outs[part][...] = y
        else:
            post = DN_DK ** -0.5 if part == 0 else 1.0
            for h in range(DN_HEADS):
                sl = slice(h * DN_DK, (h + 1) * DN_DK)
                seg = y[:, sl]
                nrm = lax.rsqrt(jnp.sum(seg * seg, axis=-1, keepdims=True) + NORM_EPS)
                outs[part][:, sl] = seg * nrm * post if part == 0 else seg * nrm

    z_ref[...] = jnp.dot(xb, wz_ref[...], preferred_element_type=F32)
    ab = jnp.dot(xb, wab_ref[...], preferred_element_type=F32)
    lane = lax.broadcasted_iota(jnp.int32, ab.shape, 1)
    g = -jnp.exp(ab_ref[0:1, :]) * _softplus(ab + ab_ref[1:2, :])
    gb_ref[...] = jnp.where(lane < DN_HEADS, g, jax.nn.sigmoid(ab))


def _odd_in(x, w_qkv, w_z, w_ab, dconv_w, ab_const, buf8, *, seq, tm):
    m = x.shape[0]
    nb = m // seq
    tm, tps, nseq = _row_tiling(m, seq, tm)
    const2 = lambda i: (0, 0)
    row = lambda i: (i, 0)
    return pl.pallas_call(
        functools.partial(_odd_in_kernel, tiles_per_seq=tps),
        grid=(m // tm,),
        in_specs=[
            pl.BlockSpec((tm, D_MODEL), row),
            pl.BlockSpec(w_qkv.shape, lambda i: (0, 0, 0)),
            pl.BlockSpec(w_z.shape, const2),
            pl.BlockSpec(w_ab.shape, const2),
            pl.BlockSpec(dconv_w.shape, const2),
            pl.BlockSpec(ab_const.shape, const2),
            pl.BlockSpec((nseq, SUBLANES, DN_QKV), lambda i: (i // tps, 0, 0)),
        ],
        out_specs=[
            pl.BlockSpec((tm, DN_KEY), row),
            pl.BlockSpec((tm, DN_KEY), row),
            pl.BlockSpec((tm, DN_VAL), row),
            pl.BlockSpec((tm, DN_VAL), row),
            pl.BlockSpec((tm, LANES), row),
            pl.BlockSpec((nseq, SUBLANES, DN_QKV), lambda i: (i // tps, 0, 0)),
        ],
        out_shape=[
            jax.ShapeDtypeStruct((m, DN_KEY), F32),
            jax.ShapeDtypeStruct((m, DN_KEY), F32),
            jax.ShapeDtypeStruct((m, DN_VAL), F32),
            jax.ShapeDtypeStruct((m, DN_VAL), F32),
            jax.ShapeDtypeStruct((m, LANES), F32),
            jax.ShapeDtypeStruct((nb, SUBLANES, DN_QKV), F32),
        ],
        scratch_shapes=[pltpu.VMEM((nseq, SUBLANES, DN_QKV), F32)],
        compiler_params=_params("arbitrary"),
        name="odd_in",
    )(x, w_qkv, w_z, w_ab, dconv_w, ab_const, buf8)


def _delta_kernel(q_ref, k_ref, v_ref, z_ref, gb_ref, s0_ref, go_ref, o_ref, sout_ref, state_ref, *, n_steps):
    n = pl.program_id(1)
    rows = q_ref.shape[0]
    n_sub = rows // CHUNK

    @pl.when(n == 0)
    def _():
        state_ref[...] = s0_ref[0]

    gb = gb_ref[...]
    ri = lax.broadcasted_iota(jnp.int32, (rows, rows), 0)
    ci = lax.broadcasted_iota(jnp.int32, (rows, rows), 1)

    def blk(size):
        return (ri // size) == (ci // size)

    same = blk(CHUNK)
    lower = ri > ci
    tril = jnp.logical_and(same, ri >= ci)
    strict = jnp.logical_and(same, lower)
    eye = (ri == ci).astype(F32)
    base = DELTA_BASE_BLOCK
    levels = []
    size = base
    while size < CHUNK:
        levels.append(jnp.logical_and(jnp.logical_and(blk(2 * size), jnp.logical_not(blk(size))), lower))
        size *= 2
    in_base = jnp.logical_and(blk(base), lower)
    g_col = jnp.dot(tril.astype(F32), gb, preferred_element_type=F32, precision=HIGHEST)
    g_end = jnp.dot(same.astype(F32), gb, preferred_element_type=F32, precision=HIGHEST)
    g_row = g_col.T
    go = go_ref[...]
    heads = range(DN_HEADS)
    cols = [slice(h * DN_DK, (h + 1) * DN_DK) for h in heads]

    gc = [g_col[:, h:h + 1] for h in heads]
    ge = [g_end[:, h:h + 1] for h in heads]
    decay = [jnp.exp(jnp.where(tril, gc[h] - g_row[h:h + 1, :], -jnp.inf)) for h in heads]
    beta = [gb[:, DN_HEADS + h:DN_HEADS + h + 1] for h in heads]
    kb = [k_ref[:, cols[h]] * beta[h] for h in heads]
    kbf = [k_ref[:, cols[h]].astype(BF16) for h in heads]
    m_full = [jnp.where(strict, lax.dot_general(kb[h].astype(BF16), kbf[h], NT_DIMS,
                                                preferred_element_type=F32) * decay[h], 0.0) for h in heads]
    p = [-jnp.where(in_base, m_full[h], 0.0) for h in heads]
    t = [eye + p[h] for h in heads]
    size = 2
    while size < base:
        p = [_dot(p[h], p[h]) for h in heads]
        t = [t[h] + _dot(t[h], p[h]) for h in heads]
        size *= 2
    for off in levels:
        u = [_dot(t[h], jnp.where(off, m_full[h], 0.0)) for h in heads]
        t = [t[h] - _dot(u[h], t[h]) for h in heads]
    e_g = [jnp.exp(gc[h]) for h in heads]
    sol = [_dot(t[h], jnp.concatenate([v_ref[:, cols[h]] * beta[h], kb[h] * e_g[h]], axis=1)) for h in heads]
    attn = [lax.dot_general(q_ref[:, cols[h]].astype(BF16), kbf[h], NT_DIMS,
                            preferred_element_type=F32) * decay[h] for h in heads]
    qd = [q_ref[:, cols[h]] * e_g[h] for h in heads]
    kt = [k_ref[:, cols[h]] * jnp.exp(ge[h] - gc[h]) for h in heads]
    st = [state_ref[h] for h in heads]
    v_new = [[] for _ in heads]
    o_inter = [[] for _ in heads]
    for c in range(n_sub):
        rs = slice(c * CHUNK, (c + 1) * CHUNK)
        for h in heads:
            r = _dot(jnp.concatenate([sol[h][rs, DN_DV:], qd[h][rs]], axis=0), st[h])
            v_c = sol[h][rs, :DN_DV] - r[:CHUNK]
            v_new[h].append(v_c)
            o_inter[h].append(r[CHUNK:])
            g_tot = jnp.exp(ge[h][c * CHUNK:c * CHUNK + 1, :])
            st[h] = st[h] * g_tot + lax.dot_general(kt[h][rs].astype(BF16), v_c.astype(BF16), TN_DIMS,
                                                    preferred_element_type=F32)
    for h in heads:
        state_ref[h] = st[h]
        o = jnp.concatenate(o_inter[h], axis=0) + _dot(attn[h], jnp.concatenate(v_new[h], axis=0))
        zh = z_ref[:, cols[h]]
        o_ref[:, cols[h]] = (_rms_norm(o, go) * (zh * jax.nn.sigmoid(zh))).astype(BF16)

    @pl.when(n == n_steps - 1)
    def _():
        sout_ref[0] = state_ref[...]


def _delta_rule(q, k, v, z, gb, s0, g_o, *, seq, rows):
    m = q.shape[0]
    nb = m // seq
    rows = min(rows, seq)
    ns = seq // rows
    row = lambda b, n: (b * ns + n, 0)
    st_spec = pl.BlockSpec((1, DN_HEADS, DN_DK, DN_DV), lambda b, n: (b, 0, 0, 0))
    return pl.pallas_call(
        functools.partial(_delta_kernel, n_steps=ns),
        grid=(nb, ns),
        in_specs=[
            pl.BlockSpec((rows, DN_KEY), row),
            pl.BlockSpec((rows, DN_KEY), row),
            pl.BlockSpec((rows, DN_VAL), row),
            pl.BlockSpec((rows, DN_VAL), row),
            pl.BlockSpec((rows, LANES), row),
            st_spec,
            pl.BlockSpec((1, DN_DV), lambda b, n: (0, 0)),
        ],
        out_specs=[pl.BlockSpec((rows, DN_VAL), row), st_spec],
        out_shape=[
            jax.ShapeDtypeStruct((m, DN_VAL), BF16),
            jax.ShapeDtypeStruct((nb, DN_HEADS, DN_DK, DN_DV), F32),
        ],
        scratch_shapes=[pltpu.VMEM((DN_HEADS, DN_DK, DN_DV), F32)],
        compiler_params=_params("arbitrary", "arbitrary"),
        name="gated_delta",
    )(q, k, v, z, gb, s0, g_o)


def _rope_table(pos):
    half = ROPE_DIM // 2
    inv = ROPE_THETA ** (-jnp.arange(half, dtype=F32) / half)
    ang = pos.astype(F32)[:, None] * inv[None, :]
    cs = jnp.concatenate([jnp.cos(ang), jnp.sin(ang)], -1)
    place = jnp.zeros((2 * half, 3 * LANES), F32)
    j = jnp.arange(half)
    place = place.at[j, NOPE_DIM + j].set(1.0).at[j, NOPE_DIM + half + j].set(1.0)
    place = place.at[half + j, LANES + NOPE_DIM + half + j].set(1.0)
    place = place.at[half + j, 2 * LANES + NOPE_DIM + j].set(-1.0)
    ones = (jnp.arange(3 * LANES) < NOPE_DIM).astype(F32)
    return jnp.dot(cs, place, precision=HIGHEST) + ones[None, :]


def _pad_heads(w, per_head, keep):
    kdim = w.shape[0]
    w = w.reshape(kdim, MLA_HEADS, per_head)[:, :, :keep]
    w = jnp.pad(w, ((0, 0), (0, 0), (0, HEAD_PAD - keep)))
    return w.reshape(kdim, MLA_HEADS * HEAD_PAD)


def _prep_even(w_in, w_uq, w_ukv):
    c1 = Q_RANK + KV_RANK
    w_kr = jnp.pad(w_in[:, c1:c1 + ROPE_DIM], ((0, 0), (NOPE_DIM, HEAD_PAD - NOPE_DIM - ROPE_DIM)))
    w1 = jnp.concatenate([w_in[:, :c1], w_kr], -1).astype(BF16)
    w2 = w_in[:, c1 + ROPE_DIM:].astype(BF16)
    wuq = _pad_heads(w_uq, NOPE_DIM + ROPE_DIM, NOPE_DIM + ROPE_DIM).astype(BF16)
    wuk = _pad_heads(w_ukv, NOPE_DIM + V_DIM, NOPE_DIM).astype(BF16)
    wuv = w_ukv.reshape(KV_RANK, MLA_HEADS, NOPE_DIM + V_DIM)[:, :, NOPE_DIM:].reshape(KV_RANK, MLA_WIDTH)
    return w1, w2, wuq, wuk, wuv.T.astype(BF16)


def _prep_odd(w_in, a_log, dt_bias):
    w_qkv = w_in[:, :DN_QKV].reshape(D_MODEL, 3, DN_KEY).transpose(1, 0, 2).astype(BF16)
    w_z = w_in[:, DN_QKV:DN_QKV + DN_VAL].astype(BF16)
    w_ab = jnp.pad(w_in[:, DN_QKV + DN_VAL:], ((0, 0), (0, LANES - 2 * DN_HEADS))).astype(BF16)
    ab_const = jnp.zeros((SUBLANES, LANES), F32)
    ab_const = ab_const.at[0, :DN_HEADS].set(a_log.astype(F32)).at[1, :DN_HEADS].set(dt_bias.astype(F32))
    return w_qkv, w_z, w_ab, ab_const


def _prep_ffn(w_in, conv_w, w_out):
    return w_in.astype(BF16), w_out.astype(BF16), conv_w


def _pad_rows8(buf):
    return jnp.pad(buf.astype(F32), ((0, 0), (SUBLANES - buf.shape[1], 0), (0, 0)))


def _trunk(x, past_lat, past_kr, sc_buf, dconv_buf, delta_s, ff_buf, w, *, tm, tq, tk, dr):
    (even_w, odd_w, ffn_w, w_o_e, w_o_o, g_qnorm, g_kvnorm, sc_w, dconv_w, g_onorm,
     ln_mix_g, ln_mix_b, ln_ff_g, ln_ff_b) = w
    nb, seq, _ = x.shape
    past = 0 if past_lat is None else past_lat.shape[2]
    m = nb * seq
    x = x.reshape(m, D_MODEL)
    tab = _rope_table(past + jnp.arange(seq, dtype=jnp.int32))
    t_valid = past + seq
    tq = min(tq, seq)
    tk = min(tk, t_valid)
    t_pad = -(-t_valid // tk) * tk
    lats, krs, scs, dcs, dss, ffs = [], [], [], [], [], []
    for i in range(DEPTH):
        row = lambda a: a[i].reshape(1, -1)
        if i % 2 == 0:
            e = i // 2
            w1, w2, wuq, wuk, wuv = even_w[e]
            q, lat, kr_pad, o_sc, sc_new = _even_in(
                x, w1, w2, g_qnorm[e].reshape(1, -1), g_kvnorm[e].reshape(1, -1), wuq, sc_w[e], tab,
                _pad_rows8(sc_buf[e]), seq=seq, tm=tm)
            lat_all = lat.reshape(nb, seq, KV_RANK)
            kr_all = kr_pad.reshape(nb, seq, LANES)
            if past:
                kr_past = jnp.pad(past_kr[e].astype(F32),
                                  ((0, 0), (0, 0), (NOPE_DIM, HEAD_PAD - NOPE_DIM - ROPE_DIM)))
                lat_all = jnp.concatenate([past_lat[e].astype(F32), lat_all], 1)
                kr_all = jnp.concatenate([kr_past, kr_all], 1)
            if t_pad != t_valid:
                lat_all = jnp.pad(lat_all, ((0, 0), (0, t_pad - t_valid), (0, 0)))
                kr_all = jnp.pad(kr_all, ((0, 0), (0, t_pad - t_valid), (0, 0)))
            k, v = _kv_up(lat_all.reshape(nb * t_pad, KV_RANK), kr_all.reshape(nb * t_pad, LANES),
                          wuk, wuv, nb=nb, t_pad=t_pad, tm=tk)
            o_att = _attention(q, k, v, nb=nb, sq=seq, t_pad=t_pad, t_valid=t_valid, past=past, tq=tq, tk=tk)
            x = _proj_ln([o_att, o_sc], [w_o_e[e][:MLA_WIDTH], w_o_e[e][MLA_WIDTH:]], x,
                         row(ln_mix_g), row(ln_mix_b), tm=tm)
            lats.append(lat.reshape(nb, seq, KV_RANK))
            krs.append(kr_pad.reshape(nb, seq, LANES)[:, :, NOPE_DIM:NOPE_DIM + ROPE_DIM])
            scs.append(sc_new[:, SUBLANES - (SC_K - 1):])
        else:
            o = i // 2
            w_qkv, w_z, w_ab, ab_const = odd_w[o]
            q, k, v, z, gb, dc_new = _odd_in(x, w_qkv, w_z, w_ab, dconv_w[o], ab_const,
                                             _pad_rows8(dconv_buf[o]), seq=seq, tm=tm)
            og, s_new = _delta_rule(q, k, v, z, gb, delta_s[o].astype(F32), g_onorm[o].reshape(1, -1), seq=seq,
                                    rows=dr)
            x = _proj_ln([og], [w_o_o[o]], x, row(ln_mix_g), row(ln_mix_b), tm=tm)
            dcs.append(dc_new[:, SUBLANES - (DN_CONV - 1):])
            dss.append(s_new)
        w_ff_in, w_ff_out, cw = ffn_w[i]
        x, ff_new = _conv_ffn(x, _pad_rows8(ff_buf[i]), w_ff_in, w_ff_out, cw, row(ln_ff_g), row(ln_ff_b),
                              seq=seq, tm=tm)
        ffs.append(ff_new[:, SUBLANES - (FF_K - 1):])
    return (x.reshape(nb, seq, D_MODEL), jnp.stack(lats), jnp.stack(krs), jnp.stack(scs), jnp.stack(dcs),
            jnp.stack(dss), jnp.stack(ffs))


def _prepare(w_in_e, w_uq, w_ukv, w_o_e, w_in_o, a_log, dt_bias, w_o_o, w_ff_in, ffconv_w, w_ff_out):
    even_w = [_prep_even(w_in_e[e], w_uq[e], w_ukv[e]) for e in range(N_EVEN)]
    odd_w = [_prep_odd(w_in_o[o], a_log[o], dt_bias[o]) for o in range(N_ODD)]
    ffn_w = [_prep_ffn(w_ff_in[i], ffconv_w[i], w_ff_out[i]) for i in range(DEPTH)]
    return even_w, odd_w, ffn_w, w_o_e.astype(BF16), w_o_o.astype(BF16)


def kernel(x_prompt, x_sample, cache_mla_latent, cache_mla_krope, state_sconv, state_dconv, state_delta, state_ffconv, w_in_e, g_qnorm, g_kvnorm, w_uq, w_ukv, sc_w, w_o_e, w_in_o, dconv_w, a_log, dt_bias, g_onorm, w_o_o, w_ff_in, ffconv_w, w_ff_out, ln_mix_g, ln_mix_b, ln_ff_g, ln_ff_b):
    even_w, odd_w, ffn_w, w_o_e_b, w_o_o_b = _prepare(
        w_in_e, w_uq, w_ukv, w_o_e, w_in_o, a_log, dt_bias, w_o_o, w_ff_in, ffconv_w, w_ff_out)
    w = (even_w, odd_w, ffn_w, w_o_e_b, w_o_o_b, g_qnorm, g_kvnorm, sc_w, dconv_w, g_onorm,
         ln_mix_g, ln_mix_b, ln_ff_g, ln_ff_b)
    bp = x_prompt.shape[0]
    f32 = x_prompt.dtype
    y_p, p_lat, p_kr, p_sc, p_dc, p_ds, p_ff = _trunk(
        x_prompt, None, None,
        jnp.zeros((N_EVEN, bp, SC_K - 1, SC_WIDTH), f32),
        jnp.zeros((N_ODD, bp, DN_CONV - 1, DN_QKV), f32),
        jnp.zeros((N_ODD, bp, DN_HEADS, DN_DK, DN_DV), f32),
        jnp.zeros((DEPTH, bp, FF_K - 1, D_FF), f32),
        w, tm=512, tq=512, tk=512, dr=256)
    y_s, s_lat, s_kr, s_sc, s_dc, s_ds, s_ff = _trunk(
        x_sample, cache_mla_latent, cache_mla_krope, state_sconv, state_dconv, state_delta, state_ffconv,
        w, tm=512, tq=64, tk=768, dr=256)
    return (y_p, y_s, p_lat, p_kr, p_sc, p_dc, p_ds, p_ff, s_lat, s_kr, s_sc, s_dc, s_ds, s_ff)
```

```python
import functools
import math

import jax
import jax.numpy as jnp
from jax import lax
from jax.experimental import pallas as pl
from jax.experimental.pallas import tpu as pltpu

D_MODEL = 1024
DEPTH = 4
CHUNK = 64
N_EVEN = (DEPTH + 1) // 2
N_ODD = DEPTH // 2
MLA_HEADS = 8
Q_RANK = 384
KV_RANK = 256
NOPE_DIM = 64
ROPE_DIM = 32
V_DIM = 64
ROPE_THETA = 10000.0
MLA_WIDTH = MLA_HEADS * V_DIM
SC_WIDTH = 512
SC_K = 3
DN_HEADS = 8
DN_DK = 128
DN_DV = 128
DN_CONV = 4
DN_KEY = DN_HEADS * DN_DK
DN_VAL = DN_HEADS * DN_DV
DN_QKV = 2 * DN_KEY + DN_VAL
D_FF = 2816
FF_K = 3
ALPHA = (2 * DEPTH) ** 0.25
NORM_EPS = 1e-6
NEG_INF = -1e30
LOG2_E = math.log2(math.e)

F32 = jnp.float32
BF16 = jnp.bfloat16
HIGHEST = lax.Precision.HIGHEST

LANES = 128
SUBLANES = 8
BF16_SUBLANES = 16
HEAD_PAD = 128
FF_CHUNK = 256
DELTA_BASE_BLOCK = 8
ATTN_HEAD_GROUP = 8
FAST_HEAD_GROUP = 8
FAST_SOFTMAX_MAX_RISE = 60.0
VMEM_LIMIT = 56 * 1024 * 1024

NT_DIMS = (((1,), (1,)), ((), ()))
TN_DIMS = (((0,), (0,)), ((), ()))


def _params(*sem):
    return pltpu.CompilerParams(dimension_semantics=sem, vmem_limit_bytes=VMEM_LIMIT)


def _dot(a, b):
    return jnp.dot(a.astype(BF16), b.astype(BF16), preferred_element_type=F32)


def _layer_norm(r, g, b):
    mu = jnp.mean(r, axis=-1, keepdims=True)
    d = r - mu
    var = jnp.mean(d * d, axis=-1, keepdims=True)
    return d * lax.rsqrt(var + NORM_EPS) * g + b


def _rms_norm(x, g):
    return x * lax.rsqrt(jnp.mean(x * x, axis=-1, keepdims=True) + NORM_EPS) * g


def _softplus(x):
    return jnp.maximum(x, 0.0) + jnp.log1p(jnp.exp(-jnp.abs(x)))


def _causal_conv(x, prev, w):
    tm, c = x.shape
    nseq = prev.shape[0]
    k_taps = w.shape[0]
    groups = tm // nseq // SUBLANES
    x4 = x.reshape(nseq, groups, SUBLANES, c)
    prev4 = prev.reshape(nseq, 1, SUBLANES, c)
    sub = lax.broadcasted_iota(jnp.int32, x4.shape, 2)
    y = x4 * w[k_taps - 1:k_taps, :]
    for k in range(1, k_taps):
        rot = pltpu.roll(x4, k, axis=2)
        rot_prev = jnp.concatenate([pltpu.roll(prev4, k, axis=2), rot[:, :groups - 1]], axis=1)
        y = y + jnp.where(sub < k, rot_prev, rot) * w[k_taps - 1 - k:k_taps - k, :]
    return y.reshape(tm, c)


def _run_tails(x, nseq):
    tm, c = x.shape
    run = tm // nseq
    return x.reshape(nseq, run, c)[:, run - SUBLANES:, :]


def _row_tiling(m, seq, tm):
    tm = min(tm, m)
    if seq >= tm:
        return tm, seq // tm, 1
    return tm, 1, tm // seq


def _mix_ffn_kernel(*refs, n_mix, tiles_per_seq):
    ys, wos = refs[:n_mix], refs[n_mix:2 * n_mix]
    (x_ref, g1_ref, b1_ref, w_in_ref, w_out_ref, cw_ref, buf_ref, g2_ref, b2_ref,
     out_ref, nbuf_ref, act_ref, x1_ref, xb_ref, win_ref) = refs[2 * n_mix:]
    i = pl.program_id(0)
    nseq = win_ref.shape[0]

    mix = jnp.dot(ys[0][...], wos[0][...], preferred_element_type=F32)
    for j in range(1, n_mix):
        mix = mix + jnp.dot(ys[j][...], wos[j][...], preferred_element_type=F32)
    x1 = _layer_norm(ALPHA * x_ref[...] + mix, g1_ref[...], b1_ref[...])
    x1_ref[...] = x1
    xb_ref[...] = x1.astype(BF16)

    @pl.when(i % tiles_per_seq == 0)
    def _():
        win_ref[...] = buf_ref[...]

    for c in range(D_FF // FF_CHUNK):
        cols = slice(c * FF_CHUNK, (c + 1) * FF_CHUNK)
        up_cols = slice(D_FF + c * FF_CHUNK, D_FF + (c + 1) * FF_CHUNK)
        xb = xb_ref[...]
        gate = jnp.dot(xb, w_in_ref[:, cols], preferred_element_type=F32)
        up = jnp.dot(xb, w_in_ref[:, up_cols], preferred_element_type=F32)
        y = _causal_conv(gate, win_ref[:, :, cols], cw_ref[:, cols])
        tails = _run_tails(gate, nseq)
        win_ref[:, :, cols] = tails
        nbuf_ref[:, :, cols] = tails
        act_ref[:, cols] = (y * jax.nn.sigmoid(y) * up).astype(BF16)

    acc = jnp.dot(act_ref[...], w_out_ref[...], preferred_element_type=F32)
    out_ref[...] = _layer_norm(ALPHA * x1_ref[...] + acc, g2_ref[...], b2_ref[...])


def _mix_ffn(ys, wos, x, mix_g, mix_b, buf8, w_in, w_out, conv_w, ff_g, ff_b, *, seq, tm):
    m = x.shape[0]
    nb = m // seq
    tm, tps, nseq = _row_tiling(m, seq, tm)
    const2 = lambda i: (0, 0)
    row = lambda i: (i, 0)
    resident = lambda a: pl.BlockSpec(a.shape, const2, pipeline_mode=pl.Buffered(1))
    vec = pl.BlockSpec((1, D_MODEL), const2)
    state = pl.BlockSpec((nseq, SUBLANES, D_FF), lambda i: (i // tps, 0, 0))
    return pl.pallas_call(
        functools.partial(_mix_ffn_kernel, n_mix=len(ys), tiles_per_seq=tps),
        grid=(m // tm,),
        in_specs=([pl.BlockSpec((tm, y.shape[1]), row) for y in ys]
                  + [resident(w) for w in wos]
                  + [pl.BlockSpec((tm, D_MODEL), row), vec, vec,
                     resident(w_in), resident(w_out), pl.BlockSpec(conv_w.shape, const2), state, vec, vec]),
        out_specs=[pl.BlockSpec((tm, D_MODEL), row), state],
        out_shape=[
            jax.ShapeDtypeStruct((m, D_MODEL), F32),
            jax.ShapeDtypeStruct((nb, SUBLANES, D_FF), F32),
        ],
        scratch_shapes=[
            pltpu.VMEM((tm, D_FF), BF16),
            pltpu.VMEM((tm, D_MODEL), F32),
            pltpu.VMEM((tm, D_MODEL), BF16),
            pltpu.VMEM((nseq, SUBLANES, D_FF), F32),
        ],
        compiler_params=_params("arbitrary"),
        name="mix_ffn",
    )(*ys, *wos, x, mix_g, mix_b, w_in, w_out, conv_w, buf8, ff_g, ff_b)


def _rope(x, tab):
    return (x * tab[:, 0:LANES]
            + pltpu.roll(x, ROPE_DIM // 2, axis=1) * tab[:, LANES:2 * LANES]
            + pltpu.roll(x, LANES - ROPE_DIM // 2, axis=1) * tab[:, 2 * LANES:3 * LANES])


def _even_in_kernel(x_ref, w1_ref, w2_ref, gq_ref, gkv_ref, wuq_ref, scw_ref, tab_ref, buf_ref,
                    q_ref, lat_ref, kr_ref, osc_ref, nbuf_ref, win_ref, *, tiles_per_seq):
    i = pl.program_id(0)
    nseq = win_ref.shape[0]

    @pl.when(i % tiles_per_seq == 0)
    def _():
        win_ref[...] = buf_ref[...]

    xb = x_ref[...].astype(BF16)
    tab = tab_ref[...]
    h1 = jnp.dot(xb, w1_ref[...], preferred_element_type=F32)
    cq = h1[:, :Q_RANK]
    ckv = h1[:, Q_RANK:Q_RANK + KV_RANK]
    kr_raw = h1[:, Q_RANK + KV_RANK:]
    lat_ref[...] = _rms_norm(ckv, gkv_ref[...])
    kr_ref[...] = _rope(kr_raw, tab)
    q_raw = _dot(_rms_norm(cq, gq_ref[...]), wuq_ref[...])
    scale = (NOPE_DIM + ROPE_DIM) ** -0.5 * LOG2_E
    for h in range(MLA_HEADS):
        sl = slice(h * HEAD_PAD, (h + 1) * HEAD_PAD)
        q_ref[:, sl] = (_rope(q_raw[:, sl], tab) * scale).astype(BF16)

    h2 = jnp.dot(xb, w2_ref[...], preferred_element_type=F32)
    gate_b = h2[:, :SC_WIDTH]
    u_in = h2[:, SC_WIDTH:2 * SC_WIDTH] * h2[:, 2 * SC_WIDTH:]
    u = _causal_conv(u_in, win_ref[...], scw_ref[...])
    tails = _run_tails(u_in, nseq)
    win_ref[...] = tails
    nbuf_ref[...] = tails
    osc_ref[...] = (gate_b * u).astype(BF16)


def _even_in(x, w1, w2, g_q, g_kv, w_uq, sc_w, tab, buf8, *, seq, tm):
    m = x.shape[0]
    nb = m // seq
    tm, tps, nseq = _row_tiling(m, seq, tm)
    if nseq > 1:
        tab = jnp.tile(tab, (nseq, 1))
    const2 = lambda i: (0, 0)
    return pl.pallas_call(
        functools.partial(_even_in_kernel, tiles_per_seq=tps),
        grid=(m // tm,),
        in_specs=[
            pl.BlockSpec((tm, D_MODEL), lambda i: (i, 0)),
            pl.BlockSpec(w1.shape, const2),
            pl.BlockSpec(w2.shape, const2),
            pl.BlockSpec(g_q.shape, const2),
            pl.BlockSpec(g_kv.shape, const2),
            pl.BlockSpec(w_uq.shape, const2),
            pl.BlockSpec(sc_w.shape, const2),
            pl.BlockSpec((tm, 3 * LANES), lambda i: (i % tps, 0)),
            pl.BlockSpec((nseq, SUBLANES, SC_WIDTH), lambda i: (i // tps, 0, 0)),
        ],
        out_specs=[
            pl.BlockSpec((tm, MLA_HEADS * HEAD_PAD), lambda i: (i, 0)),
            pl.BlockSpec((tm, KV_RANK), lambda i: (i, 0)),
            pl.BlockSpec((tm, LANES), lambda i: (i, 0)),
            pl.BlockSpec((tm, SC_WIDTH), lambda i: (i, 0)),
            pl.BlockSpec((nseq, SUBLANES, SC_WIDTH), lambda i: (i // tps, 0, 0)),
        ],
        out_shape=[
            jax.ShapeDtypeStruct((m, MLA_HEADS * HEAD_PAD), BF16),
            jax.ShapeDtypeStruct((m, KV_RANK), F32),
            jax.ShapeDtypeStruct((m, LANES), F32),
            jax.ShapeDtypeStruct((m, SC_WIDTH), BF16),
            jax.ShapeDtypeStruct((nb, SUBLANES, SC_WIDTH), F32),
        ],
        scratch_shapes=[pltpu.VMEM((nseq, SUBLANES, SC_WIDTH), F32)],
        compiler_params=_params("arbitrary"),
        name="even_in",
    )(x, w1, w2, g_q, g_kv, w_uq, sc_w, tab, buf8)


def _kv_up_kernel(lat_ref, kr_ref, wuk_ref, wuvt_ref, k_ref, vt_ref):
    lb = lat_ref[...].astype(BF16)
    kn = jnp.dot(lb, wuk_ref[...], preferred_element_type=F32)
    kr = kr_ref[...]
    for h in range(MLA_HEADS):
        sl = slice(h * HEAD_PAD, (h + 1) * HEAD_PAD)
        k_ref[:, sl] = (kn[:, sl] + kr).astype(BF16)
    vt_ref[...] = lax.dot_general(wuvt_ref[...], lb, NT_DIMS, preferred_element_type=F32).astype(BF16)


def _kv_up(lat, kr_pad, w_uk, w_uv_t, *, nb, t_pad, tm):
    tpb = t_pad // tm
    const2 = lambda i: (0, 0)
    return pl.pallas_call(
        _kv_up_kernel,
        grid=(nb * tpb,),
        in_specs=[
            pl.BlockSpec((tm, KV_RANK), lambda i: (i, 0)),
            pl.BlockSpec((tm, LANES), lambda i: (i, 0)),
            pl.BlockSpec(w_uk.shape, const2),
            pl.BlockSpec(w_uv_t.shape, const2),
        ],
        out_specs=[
            pl.BlockSpec((tm, MLA_HEADS * HEAD_PAD), lambda i: (i, 0)),
            pl.BlockSpec((MLA_WIDTH, tm), lambda i: (i // tpb, i % tpb)),
        ],
        out_shape=[
            jax.ShapeDtypeStruct((nb * t_pad, MLA_HEADS * HEAD_PAD), BF16),
            jax.ShapeDtypeStruct((nb * MLA_WIDTH, t_pad), BF16),
        ],
        compiler_params=_params("arbitrary"),
        name="kv_up",
    )(lat, kr_pad, w_uk, w_uv_t)


def _attn_kernel(q_ref, k_ref, vt_ref, o_ref, m_ref, l_ref, acc_ref, *, tq, tk, past, t_valid, nk):
    qi = pl.program_id(1)
    kj = pl.program_id(2)
    q_lo = past + qi * tq
    chunk_lo = q_lo // CHUNK
    chunk_hi = (q_lo + tq - 1) // CHUNK
    last = jnp.minimum(((chunk_hi + 1) * CHUNK - 1) // tk, nk - 1)
    k_end = (kj + 1) * tk
    full = jnp.logical_and(k_end <= (chunk_lo + 1) * CHUNK, k_end <= t_valid)
    heads = list(range(MLA_HEADS))
    hsl =[slice(h * HEAD_PAD, (h + 1) * HEAD_PAD) for h in heads]
    vsl = [slice(h * V_DIM, (h + 1) * V_DIM) for h in heads]

    @pl.when(kj == 0)
    def _():
        m_ref[...] = jnp.full_like(m_ref, NEG_INF)
        l_ref[...] = jnp.zeros_like(l_ref)
        acc_ref[...] = jnp.zeros_like(acc_ref)

    def visibility():
        kpos = kj * tk + lax.broadcasted_iota(jnp.int32, (tk, tq), 0)
        qpos = q_lo + lax.broadcasted_iota(jnp.int32, (tk, tq), 1)
        return jnp.logical_and(kpos // CHUNK <= qpos // CHUNK, kpos < t_valid)

    def scores(h, visible):
        s = lax.dot_general(k_ref[:, hsl[h]], q_ref[:, hsl[h]], NT_DIMS, preferred_element_type=F32)
        return s if visible is None else jnp.where(visible, s, NEG_INF)

    def step(masked):
        visible = visibility() if masked else None
        ones = jnp.ones((BF16_SUBLANES, tk), BF16)
        for g0 in range(0, MLA_HEADS, ATTN_HEAD_GROUP):
            grp = range(g0, g0 + ATTN_HEAD_GROUP)
            s = {h: scores(h, visible) for h in grp}
            m_prev = {h: m_ref[h:h + 1, :] for h in grp}
            m_new = {h: jnp.maximum(m_prev[h], jnp.max(s[h], axis=0, keepdims=True)) for h in grp}
            alpha = {h: jnp.exp2(m_prev[h] - m_new[h]) for h in grp}
            p = {h: jnp.exp2(s[h] - m_new[h]).astype(BF16) for h in grp}
            pv = {h: jnp.dot(jnp.concatenate([vt_ref[vsl[h], :], ones], axis=0), p[h],
                             preferred_element_type=F32) for h in grp}
            for h in grp:
                l_ref[h:h + 1, :] = alpha[h] * l_ref[h:h + 1, :] + pv[h][V_DIM:V_DIM + 1, :]
                m_ref[h:h + 1, :] = m_new[h]
                acc_ref[vsl[h], :] = alpha[h] * acc_ref[vsl[h], :] + pv[h][:V_DIM, :]

    def fast_step(masked):
        visible = visibility() if masked else None
        ones = jnp.ones((BF16_SUBLANES, tk), BF16)
        pv, t_max = {}, {}
        for g0 in range(0, MLA_HEADS, FAST_HEAD_GROUP):
            grp = range(g0, g0 + FAST_HEAD_GROUP)
            s = {h: scores(h, visible) for h in grp}
            p = {h: jnp.exp2(s[h] - m_ref[h:h + 1, :]).astype(BF16) for h in grp}
            for h in grp:
                t_max[h] = jnp.max(s[h], axis=0, keepdims=True)
            for h in grp:
                pv[h] = jnp.dot(jnp.concatenate([vt_ref[vsl[h], :], ones], axis=0), p[h],
                                preferred_element_type=F32)
        rise = t_max[0] - m_ref[0:1, :]
        for h in heads[1:]:
            rise = jnp.maximum(rise, t_max[h] - m_ref[h:h + 1, :])
        in_range = jnp.max(rise) <= FAST_SOFTMAX_MAX_RISE

        @pl.when(in_range)
        def _():
            for h in heads:
                m_prev = m_ref[h:h + 1, :]
                m_new = jnp.maximum(m_prev, t_max[h])
                alpha = jnp.exp2(m_prev - m_new)
                l_ref[h:h + 1, :] = (l_ref[h:h + 1, :] + pv[h][V_DIM:V_DIM + 1, :]) * alpha
                acc_ref[vsl[h], :] = (acc_ref[vsl[h], :] + pv[h][:V_DIM, :]) * alpha
                m_ref[h:h + 1, :] = m_new

        @pl.when(jnp.logical_not(in_range))
        def _():
            step(masked)

    first = kj == 0
    live = kj <= last

    @pl.when(jnp.logical_and(first, full))
    def _():
        step(False)

    @pl.when(jnp.logical_and(first, jnp.logical_not(full)))
    def _():
        step(True)

    @pl.when(jnp.logical_and(jnp.logical_and(live, jnp.logical_not(first)), full))
    def _():
        fast_step(False)

    @pl.when(jnp.logical_and(jnp.logical_and(live, jnp.logical_not(first)), jnp.logical_not(full)))
    def _():
        fast_step(True)

    @pl.when(kj == last)
    def _():
        for hp in range(MLA_HEADS // 2):
            pair = jnp.concatenate(
                [acc_ref[vsl[h], :] * (1.0 / l_ref[h:h + 1, :]) for h in (2 * hp, 2 * hp + 1)], axis=0)
            o_ref[:, hp * 2 * V_DIM:(hp + 1) * 2 * V_DIM] = pair.T.astype(BF16)


def _attention(q, k, vt, *, nb, sq, t_pad, t_valid, past, tq, tk):
    nq = sq // tq
    nk = t_pad // tk

    def last_tile(qi):
        chunk_hi = (past + qi * tq + tq - 1) // CHUNK
        return jnp.minimum(((chunk_hi + 1) * CHUNK - 1) // tk, nk - 1)

    return pl.pallas_call(
        functools.partial(_attn_kernel, tq=tq, tk=tk, past=past, t_valid=t_valid, nk=nk),
        grid=(nb, nq, nk),
        in_specs=[
            pl.BlockSpec((tq, MLA_HEADS * HEAD_PAD), lambda b, qi, kj: (b * nq + qi, 0)),
            pl.BlockSpec((tk, MLA_HEADS * HEAD_PAD),
                         lambda b, qi, kj: (b * nk + jnp.minimum(kj, last_tile(qi)), 0)),
            pl.BlockSpec((MLA_WIDTH, tk), lambda b, qi, kj: (b, jnp.minimum(kj, last_tile(qi)))),
        ],
        out_specs=pl.BlockSpec((tq, MLA_WIDTH), lambda b, qi, kj: (b * nq + qi, 0)),
        out_shape=jax.ShapeDtypeStruct((nb * sq, MLA_WIDTH), BF16),
        scratch_shapes=[
            pltpu.VMEM((MLA_HEADS, tq), F32),
            pltpu.VMEM((MLA_HEADS, tq), F32),
            pltpu.VMEM((MLA_WIDTH, tq), F32),
        ],
        compiler_params=_params("arbitrary", "arbitrary", "arbitrary"),
        name="chunk_attention",
    )(q, k, vt)


def _odd_in_kernel(x_ref, wqkv_ref, wz_ref, wab_ref, dcw_ref, ab_ref, buf_ref,
                   q_ref, k_ref, v_ref, z_ref, gb_ref, nbuf_ref, win_ref, *, tiles_per_seq):
    i = pl.program_id(0)
    nseq = win_ref.shape[0]

    @pl.when(i % tiles_per_seq == 0)
    def _():
        win_ref[...] = buf_ref[...]

    xb = x_ref[...].astype(BF16)
    outs = (q_ref, k_ref, v_ref)
    for part in range(3):
        cols = slice(part * DN_KEY, (part + 1) * DN_KEY)
        pre = jnp.dot(xb, wqkv_ref[part], preferred_element_type=F32)
        y = _causal_conv(pre, win_ref[:, :, cols], dcw_ref[:, cols])
        tails = _run_tails(pre, nseq)
        win_ref[:, :, cols] = tails
        nbuf_ref[:, :, cols] = tails
        y = y * jax.nn.sigmoid(y)
        if part == 2:
            outs[part][...] = y
        else:
            post = DN_DK ** -0.5 if part == 0 else 1.0
            for h in range(DN_HEADS):
                sl = slice(h * DN_DK, (h + 1) * DN_DK)
                seg = y[:, sl]
                nrm = lax.rsqrt(jnp.sum(seg * seg, axis=-1, keepdims=True) + NORM_EPS)
                outs[part][:, sl] = seg * nrm * post if part == 0 else seg * nrm

    z_ref[...] = jnp.dot(xb, wz_ref[...], preferred_element_type=F32)
    ab = jnp.dot(xb, wab_ref[...], preferred_element_type=F32)
    lane = lax.broadcasted_iota(jnp.int32, ab.shape, 1)
    g = -jnp.exp(ab_ref[0:1, :]) * _softplus(ab + ab_ref[1:2, :])
    gb_ref[...] = jnp.where(lane < DN_HEADS, g, jax.nn.sigmoid(ab))


def _odd_in(x, w_qkv, w_z, w_ab, dconv_w, ab_const, buf8, *, seq, tm):
    m = x.shape[0]
    nb = m // seq
    tm, tps, nseq = _row_tiling(m, seq, tm)
    const2 = lambda i: (0, 0)
    row = lambda i: (i, 0)
    return pl.pallas_call(
        functools.partial(_odd_in_kernel, tiles_per_seq=tps),
        grid=(m // tm,),
        in_specs=[
            pl.BlockSpec((tm, D_MODEL), row),
            pl.BlockSpec(w_qkv.shape, lambda i: (0, 0, 0)),
            pl.BlockSpec(w_z.shape, const2),
            pl.BlockSpec(w_ab.shape, const2),
            pl.BlockSpec(dconv_w.shape, const2),
            pl.BlockSpec(ab_const.shape, const2),
            pl.BlockSpec((nseq, SUBLANES, DN_QKV), lambda i: (i // tps, 0, 0)),
        ],
        out_specs=[
            pl.BlockSpec((tm, DN_KEY), row),
            pl.BlockSpec((tm, DN_KEY), row),
            pl.BlockSpec((tm, DN_VAL), row),
            pl.BlockSpec((tm, DN_VAL), row),
            pl.BlockSpec((tm, LANES), row),
            pl.BlockSpec((nseq, SUBLANES, DN_QKV), lambda i: (i // tps, 0, 0)),
        ],
        out_shape=[
            jax.ShapeDtypeStruct((m, DN_KEY), F32),
            jax.ShapeDtypeStruct((m, DN_KEY), F32),
            jax.ShapeDtypeStruct((m, DN_VAL), F32),
            jax.ShapeDtypeStruct((m, DN_VAL), F32),
            jax.ShapeDtypeStruct((m, LANES), F32),
            jax.ShapeDtypeStruct((nb, SUBLANES, DN_QKV), F32),
        ],
        scratch_shapes=[pltpu.VMEM((nseq, SUBLANES, DN_QKV), F32)],
        compiler_params=_params("arbitrary"),
        name="odd_in",
    )(x, w_qkv, w_z, w_ab, dconv_w, ab_const, buf8)


def _delta_kernel(q_ref, k_ref, v_ref, z_ref, gb_ref, s0_ref, go_ref, o_ref, sout_ref, state_ref, *, n_steps):
    n = pl.program_id(1)
    rows = q_ref.shape[0]
    n_sub = rows // CHUNK

    @pl.when(n == 0)
    def _():
        state_ref[...] = s0_ref[0]

    gb = gb_ref[...]
    ri = lax.broadcasted_iota(jnp.int32, (rows, rows), 0)
    ci = lax.broadcasted_iota(jnp.int32, (rows, rows), 1)

    def blk(size):
        return (ri // size) == (ci // size)

    same = blk(CHUNK)
    lower = ri > ci
    tril = jnp.logical_and(same, ri >= ci)
    strict = jnp.logical_and(same, lower)
    eye = (ri == ci).astype(F32)
    base = DELTA_BASE_BLOCK
    levels = []
    size = base
    while size < CHUNK:
        levels.append(jnp.logical_and(jnp.logical_and(blk(2 * size), jnp.logical_not(blk(size))), lower))
        size *= 2
    in_base = jnp.logical_and(blk(base), lower)
    g_col = jnp.dot(tril.astype(F32), gb, preferred_element_type=F32, precision=HIGHEST)
    g_end = jnp.dot(same.astype(F32), gb, preferred_element_type=F32, precision=HIGHEST)
    g_row = g_col.T
    go = go_ref[...]
    heads = range(DN_HEADS)
    cols = [slice(h * DN_DK, (h + 1) * DN_DK) for h in heads]

    gc = [g_col[:, h:h + 1] for h in heads]
    ge = [g_end[:, h:h + 1] for h in heads]
    decay = [jnp.exp(jnp.where(tril, gc[h] - g_row[h:h + 1, :], -jnp.inf)) for h in heads]
    beta = [gb[:, DN_HEADS + h:DN_HEADS + h + 1] for h in heads]
    kb = [k_ref[:, cols[h]] * beta[h] for h in heads]
    kbf = [k_ref[:, cols[h]].astype(BF16) for h in heads]
    m_full = [jnp.where(strict, lax.dot_general(kb[h].astype(BF16), kbf[h], NT_DIMS,
                                                preferred_element_type=F32) * decay[h], 0.0) for h in heads]
    p = [-jnp.where(in_base, m_full[h], 0.0) for h in heads]
    t = [eye + p[h] for h in heads]
    size = 2
    while size < base:
        p = [_dot(p[h], p[h]) for h in heads]
        t = [t[h] + _dot(t[h], p[h]) for h in heads]
        size *= 2
    size = base
    for off in levels:
        nblk = rows // (2 * size)
        t4 = [t[h].reshape(nblk, 2, size, rows) for h in heads]
        t_low = [t4[h][:, 1].reshape(rows // 2, rows) for h in heads]
        u = [_dot(t_low[h], jnp.where(off, m_full[h], 0.0)) for h in heads]
        t_low = [(t_low[h] - _dot(u[h], t[h])).reshape(nblk, 1, size, rows) for h in heads]
        t = [jnp.concatenate([t4[h][:, 0:1], t_low[h]], axis=1).reshape(rows, rows) for h in heads]
        size *= 2
    e_g = [jnp.exp(gc[h]) for h in heads]
    sol = [_dot(t[h], jnp.concatenate([v_ref[:, cols[h]] * beta[h], kb[h] * e_g[h]], axis=1)) for h in heads]
    attn = [lax.dot_general(q_ref[:, cols[h]].astype(BF16), kbf[h], NT_DIMS,
                            preferred_element_type=F32) * decay[h] for h in heads]
    qd = [q_ref[:, cols[h]] * e_g[h] for h in heads]
    kt = [k_ref[:, cols[h]] * jnp.exp(ge[h] - gc[h]) for h in heads]
    st = [state_ref[h] for h in heads]
    v_new = [[] for _ in heads]
    o_inter = [[] for _ in heads]
    for c in range(n_sub):
        rs = slice(c * CHUNK, (c + 1) * CHUNK)
        for h in heads:
            r = _dot(jnp.concatenate([sol[h][rs, DN_DV:], qd[h][rs]], axis=0), st[h])
            v_c = sol[h][rs, :DN_DV] - r[:CHUNK]
            v_new[h].append(v_c)
            o_inter[h].append(r[CHUNK:])
            g_tot = jnp.exp(ge[h][c * CHUNK:c * CHUNK + 1, :])
            st[h] = st[h] * g_tot + lax.dot_general(kt[h][rs].astype(BF16), v_c.astype(BF16), TN_DIMS,
                                                    preferred_element_type=F32)
    for h in heads:
        state_ref[h] = st[h]
        o = jnp.concatenate(o_inter[h], axis=0) + _dot(attn[h], jnp.concatenate(v_new[h], axis=0))
        zh = z_ref[:, cols[h]]
        o_ref[:, cols[h]] = (_rms_norm(o, go) * (zh * jax.nn.sigmoid(zh))).astype(BF16)

    @pl.when(n == n_steps - 1)
    def _():
        sout_ref[0] = state_ref[...]


def _delta_rule(q, k, v, z, gb, s0, g_o, *, seq, rows):
    m = q.shape[0]
    nb = m // seq
    rows = min(rows, seq)
    ns = seq // rows
    row = lambda b, n: (b * ns + n, 0)
    st_spec = pl.BlockSpec((1, DN_HEADS, DN_DK, DN_DV), lambda b, n: (b, 0, 0, 0))
    return pl.pallas_call(
        functools.partial(_delta_kernel, n_steps=ns),
        grid=(nb, ns),
        in_specs=[
            pl.BlockSpec((rows, DN_KEY), row),
            pl.BlockSpec((rows, DN_KEY), row),
            pl.BlockSpec((rows, DN_VAL), row),
            pl.BlockSpec((rows, DN_VAL), row),
            pl.BlockSpec((rows, LANES), row),
            st_spec,
            pl.BlockSpec((1, DN_DV), lambda b, n: (0, 0)),
        ],
        out_specs=[pl.BlockSpec((rows, DN_VAL), row), st_spec],
        out_shape=[
            jax.ShapeDtypeStruct((m, DN_VAL), BF16),
            jax.ShapeDtypeStruct((nb, DN_HEADS, DN_DK, DN_DV), F32),
        ],
        scratch_shapes=[pltpu.VMEM((DN_HEADS, DN_DK, DN_DV), F32)],
        compiler_params=_params("arbitrary", "arbitrary"),
        name="gated_delta",
    )(q, k, v, z, gb, s0, g_o)


def _rope_table(pos):
    half = ROPE_DIM // 2
    inv = ROPE_THETA ** (-jnp.arange(half, dtype=F32) / half)
    ang = pos.astype(F32)[:, None] * inv[None, :]
    cs = jnp.concatenate([jnp.cos(ang), jnp.sin(ang)], -1)
    place = jnp.zeros((2 * half, 3 * LANES), F32)
    j = jnp.arange(half)
    place = place.at[j, NOPE_DIM + j].set(1.0).at[j, NOPE_DIM + half + j].set(1.0)
    place = place.at[half + j, LANES + NOPE_DIM + half + j].set(1.0)
    place = place.at[half + j, 2 * LANES + NOPE_DIM + j].set(-1.0)
    ones = (jnp.arange(3 * LANES) < NOPE_DIM).astype(F32)
    return jnp.dot(cs, place, precision=HIGHEST) + ones[None, :]


def _pad_heads(w, per_head, keep):
    kdim = w.shape[0]
    w = w.reshape(kdim, MLA_HEADS, per_head)[:, :, :keep]
    w = jnp.pad(w, ((0, 0), (0, 0), (0, HEAD_PAD - keep)))
    return w.reshape(kdim, MLA_HEADS * HEAD_PAD)


def _prep_even(w_in, w_uq, w_ukv):
    c1 = Q_RANK + KV_RANK
    w_kr = jnp.pad(w_in[:, c1:c1 + ROPE_DIM], ((0, 0), (NOPE_DIM, HEAD_PAD - NOPE_DIM - ROPE_DIM)))
    w1 = jnp.concatenate([w_in[:, :c1], w_kr], -1).astype(BF16)
    w2 = w_in[:, c1 + ROPE_DIM:].astype(BF16)
    wuq = _pad_heads(w_uq, NOPE_DIM + ROPE_DIM, NOPE_DIM + ROPE_DIM).astype(BF16)
    wuk = _pad_heads(w_ukv, NOPE_DIM + V_DIM, NOPE_DIM).astype(BF16)
    wuv = w_ukv.reshape(KV_RANK, MLA_HEADS, NOPE_DIM + V_DIM)[:, :, NOPE_DIM:].reshape(KV_RANK, MLA_WIDTH)
    return w1, w2, wuq, wuk, wuv.T.astype(BF16)


def _prep_odd(w_in, a_log, dt_bias):
    w_qkv = w_in[:, :DN_QKV].reshape(D_MODEL, 3, DN_KEY).transpose(1, 0, 2).astype(BF16)
    w_z = w_in[:, DN_QKV:DN_QKV + DN_VAL].astype(BF16)
    w_ab = jnp.pad(w_in[:, DN_QKV + DN_VAL:], ((0, 0), (0, LANES - 2 * DN_HEADS))).astype(BF16)
    ab_const = jnp.zeros((SUBLANES, LANES), F32)
    ab_const = ab_const.at[0, :DN_HEADS].set(a_log.astype(F32)).at[1, :DN_HEADS].set(dt_bias.astype(F32))
    return w_qkv, w_z, w_ab, ab_const


def _prep_ffn(w_in, conv_w, w_out):
    return w_in.astype(BF16), w_out.astype(BF16), conv_w


def _pad_rows8(buf):
    return jnp.pad(buf.astype(F32), ((0, 0), (SUBLANES - buf.shape[1], 0), (0, 0)))


def _trunk(x, past_lat, past_kr, sc_buf, dconv_buf, delta_s, ff_buf, w, *, tm, tq, tk, dr):
    (even_w, odd_w, ffn_w, w_o_e, w_o_o, g_qnorm, g_kvnorm, sc_w, dconv_w, g_onorm,
     ln_mix_g, ln_mix_b, ln_ff_g, ln_ff_b) = w
    nb, seq, _ = x.shape
    past = 0 if past_lat is None else past_lat.shape[2]
    m = nb * seq
    x = x.reshape(m, D_MODEL)
    tab = _rope_table(past + jnp.arange(seq, dtype=jnp.int32))
    t_valid = past + seq
    tq = min(tq, seq)
    tk = min(tk, t_valid)
    t_pad = -(-t_valid // tk) * tk
    lats, krs, scs, dcs, dss, ffs = [], [], [], [], [], []
    for i in range(DEPTH):
        row = lambda a: a[i].reshape(1, -1)
        if i % 2 == 0:
            e = i // 2
            w1, w2, wuq, wuk, wuv = even_w[e]
            q, lat, kr_pad, o_sc, sc_new = _even_in(
                x, w1, w2, g_qnorm[e].reshape(1, -1), g_kvnorm[e].reshape(1, -1), wuq, sc_w[e], tab,
                _pad_rows8(sc_buf[e]), seq=seq, tm=tm)
            lat_all = lat.reshape(nb, seq, KV_RANK)
            kr_all = kr_pad.reshape(nb, seq, LANES)
            if past:
                kr_past = jnp.pad(past_kr[e].astype(F32),
                                  ((0, 0), (0, 0), (NOPE_DIM, HEAD_PAD - NOPE_DIM - ROPE_DIM)))
                lat_all = jnp.concatenate([past_lat[e].astype(F32), lat_all], 1)
                kr_all = jnp.concatenate([kr_past, kr_all], 1)
            if t_pad != t_valid:
                lat_all = jnp.pad(lat_all, ((0, 0), (0, t_pad - t_valid), (0, 0)))
                kr_all = jnp.pad(kr_all, ((0, 0), (0, t_pad - t_valid), (0, 0)))
            k, v = _kv_up(lat_all.reshape(nb * t_pad, KV_RANK), kr_all.reshape(nb * t_pad, LANES),
                          wuk, wuv, nb=nb, t_pad=t_pad, tm=tk)
            o_att = _attention(q, k, v, nb=nb, sq=seq, t_pad=t_pad, t_valid=t_valid, past=past, tq=tq, tk=tk)
            mix, mix_w = [o_att, o_sc], [w_o_e[e][:MLA_WIDTH], w_o_e[e][MLA_WIDTH:]]
            lats.append(lat.reshape(nb, seq, KV_RANK))
            krs.append(kr_pad.reshape(nb, seq, LANES)[:, :, NOPE_DIM:NOPE_DIM + ROPE_DIM])
            scs.append(sc_new[:, SUBLANES - (SC_K - 1):])
        else:
            o = i // 2
            w_qkv, w_z, w_ab, ab_const = odd_w[o]
            q, k, v, z, gb, dc_new = _odd_in(x, w_qkv, w_z, w_ab, dconv_w[o], ab_const,
                                             _pad_rows8(dconv_buf[o]), seq=seq, tm=tm)
            og, s_new = _delta_rule(q, k, v, z, gb, delta_s[o].astype(F32), g_onorm[o].reshape(1, -1), seq=seq,
                                    rows=dr)
            mix, mix_w = [og], [w_o_o[o]]
            dcs.append(dc_new[:, SUBLANES - (DN_CONV - 1):])
            dss.append(s_new)
        w_ff_in, w_ff_out, cw = ffn_w[i]
        x, ff_new = _mix_ffn(mix, mix_w, x, row(ln_mix_g), row(ln_mix_b), _pad_rows8(ff_buf[i]),
                             w_ff_in, w_ff_out, cw, row(ln_ff_g), row(ln_ff_b), seq=seq, tm=tm)
        ffs.append(ff_new[:, SUBLANES - (FF_K - 1):])
    return (x.reshape(nb, seq, D_MODEL), jnp.stack(lats), jnp.stack(krs), jnp.stack(scs), jnp.stack(dcs),
            jnp.stack(dss), jnp.stack(ffs))


def _prepare(w_in_e, w_uq, w_ukv, w_o_e, w_in_o, a_log, dt_bias, w_o_o, w_ff_in, ffconv_w, w_ff_out):
    even_w = [_prep_even(w_in_e[e], w_uq[e], w_ukv[e]) for e in range(N_EVEN)]
    odd_w = [_prep_odd(w_in_o[o], a_log[o], dt_bias[o]) for o in range(N_ODD)]
    ffn_w = [_prep_ffn(w_ff_in[i], ffconv_w[i], w_ff_out[i]) for i in range(DEPTH)]
    return even_w, odd_w, ffn_w, w_o_e.astype(BF16), w_o_o.astype(BF16)


def kernel(x_prompt, x_sample, cache_mla_latent, cache_mla_krope, state_sconv, state_dconv, state_delta, state_ffconv, w_in_e, g_qnorm, g_kvnorm, w_uq, w_ukv, sc_w, w_o_e, w_in_o, dconv_w, a_log, dt_bias, g_onorm, w_o_o, w_ff_in, ffconv_w, w_ff_out, ln_mix_g, ln_mix_b, ln_ff_g, ln_ff_b):
    even_w, odd_w, ffn_w, w_o_e_b, w_o_o_b = _prepare(
        w_in_e, w_uq, w_ukv, w_o_e, w_in_o, a_log, dt_bias, w_o_o, w_ff_in, ffconv_w, w_ff_out)
    w = (even_w, odd_w, ffn_w, w_o_e_b, w_o_o_b, g_qnorm, g_kvnorm, sc_w, dconv_w, g_onorm,
         ln_mix_g, ln_mix_b, ln_ff_g, ln_ff_b)
    bp = x_prompt.shape[0]
    f32 = x_prompt.dtype
    y_p, p_lat, p_kr, p_sc, p_dc, p_ds, p_ff = _trunk(
        x_prompt, None, None,
        jnp.zeros((N_EVEN, bp, SC_K - 1, SC_WIDTH), f32),
        jnp.zeros((N_ODD, bp, DN_CONV - 1, DN_QKV), f32),
        jnp.zeros((N_ODD, bp, DN_HEADS, DN_DK, DN_DV), f32),
        jnp.zeros((DEPTH, bp, FF_K - 1, D_FF), f32),
        w, tm=512, tq=512, tk=512, dr=256)
    y_s, s_lat, s_kr, s_sc, s_dc, s_ds, s_ff = _trunk(
        x_sample, cache_mla_latent, cache_mla_krope, state_sconv, state_dconv, state_delta, state_ffconv,
        w, tm=512, tq=64, tk=768, dr=256)
    return (y_p, y_s, p_lat, p_kr, p_sc, p_dc, p_ds, p_ff, s_lat, s_kr, s_sc, s_dc, s_ds, s_ff)
```

```python
import functools
import math

import jax
import jax.numpy as jnp
from jax import lax
from jax.experimental import pallas as pl
from jax.experimental.pallas import tpu as pltpu

D_MODEL = 1024
DEPTH = 4
CHUNK = 64
N_EVEN = (DEPTH + 1) // 2
N_ODD = DEPTH // 2
MLA_HEADS = 8
Q_RANK = 384
KV_RANK = 256
NOPE_DIM = 64
ROPE_DIM = 32
V_DIM = 64
ROPE_THETA = 10000.0
MLA_WIDTH = MLA_HEADS * V_DIM
SC_WIDTH = 512
SC_K = 3
DN_HEADS = 8
DN_DK = 128
DN_DV = 128
DN_CONV = 4
DN_KEY = DN_HEADS * DN_DK
DN_VAL = DN_HEADS * DN_DV
DN_QKV = 2 * DN_KEY + DN_VAL
D_FF = 2816
FF_K = 3
ALPHA = (2 * DEPTH) ** 0.25
NORM_EPS = 1e-6
NEG_INF = -1e30
LOG2_E = math.log2(math.e)

F32 = jnp.float32
BF16 = jnp.bfloat16
HIGHEST = lax.Precision.HIGHEST

LANES = 128
SUBLANES = 8
BF16_SUBLANES = 16
HEAD_PAD = 128
FF_CHUNK = 256
DELTA_BASE_BLOCK = 8
DELTA_SEQS_PER_STEP = 2
ATTN_HEAD_GROUP = 8
FAST_HEAD_GROUP = 4
FAST_SOFTMAX_MAX_RISE = 60.0
VMEM_LIMIT = 56 * 1024 * 1024

NT_DIMS = (((1,), (1,)), ((), ()))
TN_DIMS = (((0,), (0,)), ((), ()))


def _params(*sem):
    return pltpu.CompilerParams(dimension_semantics=sem, vmem_limit_bytes=VMEM_LIMIT)


def _dot(a, b):
    return jnp.dot(a.astype(BF16), b.astype(BF16), preferred_element_type=F32)


def _layer_norm(r, g, b):
    mu = jnp.mean(r, axis=-1, keepdims=True)
    d = r - mu
    var = jnp.mean(d * d, axis=-1, keepdims=True)
    return d * lax.rsqrt(var + NORM_EPS) * g + b


def _rms_norm(x, g):
    return x * lax.rsqrt(jnp.mean(x * x, axis=-1, keepdims=True) + NORM_EPS) * g


def _softplus(x):
    return jnp.maximum(x, 0.0) + jnp.log1p(jnp.exp(-jnp.abs(x)))


def _causal_conv(x, prev, w):
    tm, c = x.shape
    nseq = prev.shape[0]
    k_taps = w.shape[0]
    groups = tm // nseq // SUBLANES
    x4 = x.reshape(nseq, groups, SUBLANES, c)
    prev4 = prev.reshape(nseq, 1, SUBLANES, c)
    sub = lax.broadcasted_iota(jnp.int32, x4.shape, 2)
    y = x4 * w[k_taps - 1:k_taps, :]
    for k in range(1, k_taps):
        rot = pltpu.roll(x4, k, axis=2)
        rot_prev = jnp.concatenate([pltpu.roll(prev4, k, axis=2), rot[:, :groups - 1]], axis=1)
        y = y + jnp.where(sub < k, rot_prev, rot) * w[k_taps - 1 - k:k_taps - k, :]
    return y.reshape(tm, c)


def _run_tails(x, nseq):
    tm, c = x.shape
    run = tm // nseq
    return x.reshape(nseq, run, c)[:, run - SUBLANES:, :]


def _row_tiling(m, seq, tm):
    tm = min(tm, m)
    if seq >= tm:
        return tm, seq // tm, 1
    return tm, 1, tm // seq


def _mix_ffn_kernel(*refs, n_mix, tiles_per_seq):
    ys, wos = refs[:n_mix], refs[n_mix:2 * n_mix]
    (x_ref, g1_ref, b1_ref, w_in_ref, w_out_ref, cw_ref, buf_ref, g2_ref, b2_ref,
     out_ref, nbuf_ref, act_ref, x1_ref, xb_ref, win_ref) = refs[2 * n_mix:]
    i = pl.program_id(0)
    nseq = win_ref.shape[0]

    tm = x_ref.shape[0]
    halves = [slice(0, tm // 2), slice(tm // 2, tm)]
    seq_half = [slice(0, nseq // 2), slice(nseq // 2, nseq)]

    for rows in halves:
        mix = jnp.dot(ys[0][rows, :], wos[0][...], preferred_element_type=F32)
        for j in range(1, n_mix):
            mix = mix + jnp.dot(ys[j][rows, :], wos[j][...], preferred_element_type=F32)
        x1 = _layer_norm(ALPHA * x_ref[rows, :] + mix, g1_ref[...], b1_ref[...])
        x1_ref[rows, :] = x1
        xb_ref[rows, :] = x1.astype(BF16)

    @pl.when(i % tiles_per_seq == 0)
    def _():
        win_ref[...] = buf_ref[...]

    for c in range(D_FF // FF_CHUNK):
        cols = slice(c * FF_CHUNK, (c + 1) * FF_CHUNK)
        up_cols = slice(D_FF + c * FF_CHUNK, D_FF + (c + 1) * FF_CHUNK)
        prev = win_ref[:, :, cols]
        tails = []
        for hf, rows in enumerate(halves):
            xb = xb_ref[rows, :]
            gate = jnp.dot(xb, w_in_ref[:, cols], preferred_element_type=F32)
            up = jnp.dot(xb, w_in_ref[:, up_cols], preferred_element_type=F32)
            if nseq == 1:
                y = _causal_conv(gate, prev if hf == 0 else tails[0], cw_ref[:, cols])
                tails.append(_run_tails(gate, 1))
            else:
                y = _causal_conv(gate, prev[seq_half[hf]], cw_ref[:, cols])
                tails.append(_run_tails(gate, nseq // 2))
            act_ref[rows, cols] = (y * jax.nn.sigmoid(y) * up).astype(BF16)
        tails = tails[1] if nseq == 1 else jnp.concatenate(tails, axis=0)
        win_ref[:, :, cols] = tails
        nbuf_ref[:, :, cols] = tails

    for rows in halves:
        acc = jnp.dot(act_ref[rows, :], w_out_ref[...], preferred_element_type=F32)
        out_ref[rows, :] = _layer_norm(ALPHA * x1_ref[rows, :] + acc, g2_ref[...], b2_ref[...])


def _mix_ffn(ys, wos, x, mix_g, mix_b, buf8, w_in, w_out, conv_w, ff_g, ff_b, *, seq, tm):
    m = x.shape[0]
    nb = m // seq
    tm, tps, nseq = _row_tiling(m, seq, tm)
    const2 = lambda i: (0, 0)
    row = lambda i: (i, 0)
    resident = lambda a: pl.BlockSpec(a.shape, const2, pipeline_mode=pl.Buffered(1))
    vec = pl.BlockSpec((1, D_MODEL), const2)
    state = pl.BlockSpec((nseq, SUBLANES, D_FF), lambda i: (i // tps, 0, 0))
    return pl.pallas_call(
        functools.partial(_mix_ffn_kernel, n_mix=len(ys), tiles_per_seq=tps),
        grid=(m // tm,),
        in_specs=([pl.BlockSpec((tm, y.shape[1]), row) for y in ys]
                  + [resident(w) for w in wos]
                  + [pl.BlockSpec((tm, D_MODEL), row), vec, vec,
                     resident(w_in), resident(w_out), pl.BlockSpec(conv_w.shape, const2), state, vec, vec]),
        out_specs=[pl.BlockSpec((tm, D_MODEL), row), state],
        out_shape=[
            jax.ShapeDtypeStruct((m, D_MODEL), F32),
            jax.ShapeDtypeStruct((nb, SUBLANES, D_FF), F32),
        ],
        scratch_shapes=[
            pltpu.VMEM((tm, D_FF), BF16),
            pltpu.VMEM((tm, D_MODEL), F32),
            pltpu.VMEM((tm, D_MODEL), BF16),
            pltpu.VMEM((nseq, SUBLANES, D_FF), F32),
        ],
        compiler_params=_params("arbitrary"),
        name="mix_ffn",
    )(*ys, *wos, x, mix_g, mix_b, w_in, w_out, conv_w, buf8, ff_g, ff_b)


def _rope(x, tab):
    return (x * tab[:, 0:LANES]
            + pltpu.roll(x, ROPE_DIM // 2, axis=1) * tab[:, LANES:2 * LANES]
            + pltpu.roll(x, LANES - ROPE_DIM // 2, axis=1) * tab[:, 2 * LANES:3 * LANES])


def _even_in_kernel(x_ref, w1_ref, w2_ref, gq_ref, gkv_ref, wuq_ref, scw_ref, tab_ref, buf_ref,
                    q_ref, lat_ref, kr_ref, osc_ref, nbuf_ref, win_ref, *, tiles_per_seq):
    i = pl.program_id(0)
    nseq = win_ref.shape[0]

    @pl.when(i % tiles_per_seq == 0)
    def _():
        win_ref[...] = buf_ref[...]

    xb = x_ref[...].astype(BF16)
    tab = tab_ref[...]
    h1 = jnp.dot(xb, w1_ref[...], preferred_element_type=F32)
    cq = h1[:, :Q_RANK]
    ckv = h1[:, Q_RANK:Q_RANK + KV_RANK]
    kr_raw = h1[:, Q_RANK + KV_RANK:]
    lat_ref[...] = _rms_norm(ckv, gkv_ref[...])
    kr_ref[...] = _rope(kr_raw, tab)
    q_raw = _dot(_rms_norm(cq, gq_ref[...]), wuq_ref[...])
    scale = (NOPE_DIM + ROPE_DIM) ** -0.5 * LOG2_E
    for h in range(MLA_HEADS):
        sl = slice(h * HEAD_PAD, (h + 1) * HEAD_PAD)
        q_ref[:, sl] = (_rope(q_raw[:, sl], tab) * scale).astype(BF16)

    h2 = jnp.dot(xb, w2_ref[...], preferred_element_type=F32)
    gate_b = h2[:, :SC_WIDTH]
    u_in = h2[:, SC_WIDTH:2 * SC_WIDTH] * h2[:, 2 * SC_WIDTH:]
    u = _causal_conv(u_in, win_ref[...], scw_ref[...])
    tails = _run_tails(u_in, nseq)
    win_ref[...] = tails
    nbuf_ref[...] = tails
    osc_ref[...] = (gate_b * u).astype(BF16)


def _even_in(x, w1, w2, g_q, g_kv, w_uq, sc_w, tab, buf8, *, seq, tm):
    m = x.shape[0]
    nb = m // seq
    tm, tps, nseq = _row_tiling(m, seq, tm)
    if nseq > 1:
        tab = jnp.tile(tab, (nseq, 1))
    const2 = lambda i: (0, 0)
    return pl.pallas_call(
        functools.partial(_even_in_kernel, tiles_per_seq=tps),
        grid=(m // tm,),
        in_specs=[
            pl.BlockSpec((tm, D_MODEL), lambda i: (i, 0)),
            pl.BlockSpec(w1.shape, const2),
            pl.BlockSpec(w2.shape, const2),
            pl.BlockSpec(g_q.shape, const2),
            pl.BlockSpec(g_kv.shape, const2),
            pl.BlockSpec(w_uq.shape, const2),
            pl.BlockSpec(sc_w.shape, const2),
            pl.BlockSpec((tm, 3 * LANES), lambda i: (i % tps, 0)),
            pl.BlockSpec((nseq, SUBLANES, SC_WIDTH), lambda i: (i // tps, 0, 0)),
        ],
        out_specs=[
            pl.BlockSpec((tm, MLA_HEADS * HEAD_PAD), lambda i: (i, 0)),
            pl.BlockSpec((tm, KV_RANK), lambda i: (i, 0)),
            pl.BlockSpec((tm, LANES), lambda i: (i, 0)),
            pl.BlockSpec((tm, SC_WIDTH), lambda i: (i, 0)),
            pl.BlockSpec((nseq, SUBLANES, SC_WIDTH), lambda i: (i // tps, 0, 0)),
        ],
        out_shape=[
            jax.ShapeDtypeStruct((m, MLA_HEADS * HEAD_PAD), BF16),
            jax.ShapeDtypeStruct((m, KV_RANK), F32),
            jax.ShapeDtypeStruct((m, LANES), F32),
            jax.ShapeDtypeStruct((m, SC_WIDTH), BF16),
            jax.ShapeDtypeStruct((nb, SUBLANES, SC_WIDTH), F32),
        ],
        scratch_shapes=[pltpu.VMEM((nseq, SUBLANES, SC_WIDTH), F32)],
        compiler_params=_params("arbitrary"),
        name="even_in",
    )(x, w1, w2, g_q, g_kv, w_uq, sc_w, tab, buf8)


def _kv_up_kernel(lat_ref, kr_ref, wuk_ref, wuvt_ref, k_ref, vt_ref):
    lb = lat_ref[...].astype(BF16)
    kn = jnp.dot(lb, wuk_ref[...], preferred_element_type=F32)
    kr = kr_ref[...]
    for h in range(MLA_HEADS):
        sl = slice(h * HEAD_PAD, (h + 1) * HEAD_PAD)
        k_ref[:, sl] = (kn[:, sl] + kr).astype(BF16)
    vt_ref[...] = lax.dot_general(wuvt_ref[...], lb, NT_DIMS, preferred_element_type=F32).astype(BF16)


def _kv_up(lat, kr_pad, w_uk, w_uv_t, *, nb, t_pad, tm):
    tpb = t_pad // tm
    const2 = lambda i: (0, 0)
    return pl.pallas_call(
        _kv_up_kernel,
        grid=(nb * tpb,),
        in_specs=[
            pl.BlockSpec((tm, KV_RANK), lambda i: (i, 0)),
            pl.BlockSpec((tm, LANES), lambda i: (i, 0)),
            pl.BlockSpec(w_uk.shape, const2),
            pl.BlockSpec(w_uv_t.shape, const2),
        ],
        out_specs=[
            pl.BlockSpec((tm, MLA_HEADS * HEAD_PAD), lambda i: (i, 0)),
            pl.BlockSpec((MLA_WIDTH, tm), lambda i: (i // tpb, i % tpb)),
        ],
        out_shape=[
            jax.ShapeDtypeStruct((nb * t_pad, MLA_HEADS * HEAD_PAD), BF16),
            jax.ShapeDtypeStruct((nb * MLA_WIDTH, t_pad), BF16),
        ],
        compiler_params=_params("arbitrary"),
        name="kv_up",
    )(lat, kr_pad, w_uk, w_uv_t)


def _attn_kernel(q_ref, k_ref, vt_ref, o_ref, m_ref, l_ref, acc_ref, *, tq, tk, past, t_valid, nk):
    qi = pl.program_id(1)
    kj = pl.program_id(2)
    q_lo = past + qi * tq
    chunk_lo = q_lo // CHUNK
    chunk_hi = (q_lo + tq - 1) // CHUNK
    last = jnp.minimum(((chunk_hi + 1) * CHUNK - 1) // tk, nk - 1)
    k_end = (kj + 1) * tk
    full = jnp.logical_and(k_end <= (chunk_lo + 1) * CHUNK, k_end <= t_valid)
    heads = list(range(MLA_HEADS))
    hsl =[slice(h * HEAD_PAD, (h + 1) * HEAD_PAD) for h in heads]
    vsl = [slice(h * V_DIM, (h + 1) * V_DIM) for h in heads]

    @pl.when(kj == 0)
    def _():
        m_ref[...] = jnp.full_like(m_ref, NEG_INF)
        l_ref[...] = jnp.zeros_like(l_ref)
        acc_ref[...] = jnp.zeros_like(acc_ref)

    def visibility():
        kpos = kj * tk + lax.broadcasted_iota(jnp.int32, (tk, tq), 0)
        qpos = q_lo + lax.broadcasted_iota(jnp.int32, (tk, tq), 1)
        return jnp.logical_and(kpos // CHUNK <= qpos // CHUNK, kpos < t_valid)

    def scores(h, visible):
        s = lax.dot_general(k_ref[:, hsl[h]], q_ref[:, hsl[h]], NT_DIMS, preferred_element_type=F32)
        return s if visible is None else jnp.where(visible, s, NEG_INF)

    def step(masked):
        visible = visibility() if masked else None
        ones = jnp.ones((BF16_SUBLANES, tk), BF16)
        for g0 in range(0, MLA_HEADS, ATTN_HEAD_GROUP):
            grp = range(g0, g0 + ATTN_HEAD_GROUP)
            s = {h: scores(h, visible) for h in grp}
            m_prev = {h: m_ref[h:h + 1, :] for h in grp}
            m_new = {h: jnp.maximum(m_prev[h], jnp.max(s[h], axis=0, keepdims=True)) for h in grp}
            alpha = {h: jnp.exp2(m_prev[h] - m_new[h]) for h in grp}
            p = {h: jnp.exp2(s[h] - m_new[h]).astype(BF16) for h in grp}
            pv = {h: jnp.dot(jnp.concatenate([vt_ref[vsl[h], :], ones], axis=0), p[h],
                             preferred_element_type=F32) for h in grp}
            for h in grp:
                l_ref[h:h + 1, :] = alpha[h] * l_ref[h:h + 1, :] + pv[h][V_DIM:V_DIM + 1, :]
                m_ref[h:h + 1, :] = m_new[h]
                acc_ref[vsl[h], :] = alpha[h] * acc_ref[vsl[h], :] + pv[h][:V_DIM, :]

    def fast_step(masked):
        visible = visibility() if masked else None
        ones = jnp.ones((BF16_SUBLANES, tk), BF16)
        pv, t_max = {}, {}
        for g0 in range(0, MLA_HEADS, FAST_HEAD_GROUP):
            grp = range(g0, g0 + FAST_HEAD_GROUP)
            s = {h: scores(h, visible) for h in grp}
            p = {h: jnp.exp2(s[h] - m_ref[h:h + 1, :]).astype(BF16) for h in grp}
            for h in grp:
                t_max[h] = jnp.max(s[h], axis=0, keepdims=True)
            for h in grp:
                pv[h] = jnp.dot(jnp.concatenate([vt_ref[vsl[h], :], ones], axis=0), p[h],
                                preferred_element_type=F32)
        rise = t_max[0] - m_ref[0:1, :]
        for h in heads[1:]:
            rise = jnp.maximum(rise, t_max[h] - m_ref[h:h + 1, :])
        in_range = jnp.max(rise) <= FAST_SOFTMAX_MAX_RISE

        @pl.when(in_range)
        def _():
            for h in heads:
                m_prev = m_ref[h:h + 1, :]
                m_new = jnp.maximum(m_prev, t_max[h])
                alpha = jnp.exp2(m_prev - m_new)
                l_ref[h:h + 1, :] = (l_ref[h:h + 1, :] + pv[h][V_DIM:V_DIM + 1, :]) * alpha
                acc_ref[vsl[h], :] = (acc_ref[vsl[h], :] + pv[h][:V_DIM, :]) * alpha
                m_ref[h:h + 1, :] = m_new

        @pl.when(jnp.logical_not(in_range))
        def _():
            step(masked)

    first = kj == 0
    live = kj <= last

    @pl.when(jnp.logical_and(first, full))
    def _():
        step(False)

    @pl.when(jnp.logical_and(first, jnp.logical_not(full)))
    def _():
        step(True)

    @pl.when(jnp.logical_and(jnp.logical_and(live, jnp.logical_not(first)), full))
    def _():
        fast_step(False)

    @pl.when(jnp.logical_and(jnp.logical_and(live, jnp.logical_not(first)), jnp.logical_not(full)))
    def _():
        fast_step(True)

    @pl.when(kj == last)
    def _():
        for hp in range(MLA_HEADS // 2):
            pair = jnp.concatenate(
                [acc_ref[vsl[h], :] * (1.0 / l_ref[h:h + 1, :]) for h in (2 * hp, 2 * hp + 1)], axis=0)
            o_ref[:, hp * 2 * V_DIM:(hp + 1) * 2 * V_DIM] = pair.T.astype(BF16)


def _attention(q, k, vt, *, nb, sq, t_pad, t_valid, past, tq, tk):
    nq = sq // tq
    nk = t_pad // tk

    def last_tile(qi):
        chunk_hi = (past + qi * tq + tq - 1) // CHUNK
        return jnp.minimum(((chunk_hi + 1) * CHUNK - 1) // tk, nk - 1)

    return pl.pallas_call(
        functools.partial(_attn_kernel, tq=tq, tk=tk, past=past, t_valid=t_valid, nk=nk),
        grid=(nb, nq, nk),
        in_specs=[
            pl.BlockSpec((tq, MLA_HEADS * HEAD_PAD), lambda b, qi, kj: (b * nq + qi, 0)),
            pl.BlockSpec((tk, MLA_HEADS * HEAD_PAD),
                         lambda b, qi, kj: (b * nk + jnp.minimum(kj, last_tile(qi)), 0)),
            pl.BlockSpec((MLA_WIDTH, tk), lambda b, qi, kj: (b, jnp.minimum(kj, last_tile(qi)))),
        ],
        out_specs=pl.BlockSpec((tq, MLA_WIDTH), lambda b, qi, kj: (b * nq + qi, 0)),
        out_shape=jax.ShapeDtypeStruct((nb * sq, MLA_WIDTH), BF16),
        scratch_shapes=[
            pltpu.VMEM((MLA_HEADS, tq), F32),
            pltpu.VMEM((MLA_HEADS, tq), F32),
            pltpu.VMEM((MLA_WIDTH, tq), F32),
        ],
        compiler_params=_params("arbitrary", "arbitrary", "arbitrary"),
        name="chunk_attention",
    )(q, k, vt)


def _odd_in_kernel(x_ref, wqkv_ref, wz_ref, wab_ref, dcw_ref, ab_ref, buf_ref,
                   q_ref, k_ref, v_ref, z_ref, gb_ref, nbuf_ref, win_ref, *, tiles_per_seq):
    i = pl.program_id(0)
    nseq = win_ref.shape[0]

    @pl.when(i % tiles_per_seq == 0)
    def _():
        win_ref[...] = buf_ref[...]

    xb = x_ref[...].astype(BF16)
    outs = (q_ref, k_ref, v_ref)
    for part in range(3):
        cols = slice(part * DN_KEY, (part + 1) * DN_KEY)
        pre = jnp.dot(xb, wqkv_ref[part], preferred_element_type=F32)
        y = _causal_conv(pre, win_ref[:, :, cols], dcw_ref[:, cols])
        tails = _run_tails(pre, nseq)
        win_ref[:, :, cols] = tails
        nbuf_ref[:, :, cols] = tails
        y = y * jax.nn.sigmoid(y)
        if part == 2:
            outs[part][...] = y
        else:
            post = DN_DK ** -0.5 if part == 0 else 1.0
            for h in range(DN_HEADS):
                sl = slice(h * DN_DK, (h + 1) * DN_DK)
                seg = y[:, sl]
                nrm = lax.rsqrt(jnp.sum(seg * seg, axis=-1, keepdims=True) + NORM_EPS)
                outs[part][:, sl] = seg * nrm * post if part == 0 else seg * nrm

    z_ref[...] = jnp.dot(xb, wz_ref[...], preferred_element_type=F32)
    ab = jnp.dot(xb, wab_ref[...], preferred_element_type=F32)
    lane = lax.broadcasted_iota(jnp.int32, ab.shape, 1)
    g = -jnp.exp(ab_ref[0:1, :]) * _softplus(ab + ab_ref[1:2, :])
    gb_ref[...] = jnp.where(lane < DN_HEADS, g, jax.nn.sigmoid(ab))


def _odd_in(x, w_qkv, w_z, w_ab, dconv_w, ab_const, buf8, *, seq, tm):
    m = x.shape[0]
    nb = m // seq
    tm, tps, nseq = _row_tiling(m, seq, tm)
    const2 = lambda i: (0, 0)
    row = lambda i: (i, 0)
    return pl.pallas_call(
        functools.partial(_odd_in_kernel, tiles_per_seq=tps),
        grid=(m // tm,),
        in_specs=[
            pl.BlockSpec((tm, D_MODEL), row),
            pl.BlockSpec(w_qkv.shape, lambda i: (0, 0, 0)),
            pl.BlockSpec(w_z.shape, const2),
            pl.BlockSpec(w_ab.shape, const2),
            pl.BlockSpec(dconv_w.shape, const2),
            pl.BlockSpec(ab_const.shape, const2),
            pl.BlockSpec((nseq, SUBLANES, DN_QKV), lambda i: (i // tps, 0, 0)),
        ],
        out_specs=[
            pl.BlockSpec((tm, DN_KEY), row),
            pl.BlockSpec((tm, DN_KEY), row),
            pl.BlockSpec((tm, DN_VAL), row),
            pl.BlockSpec((tm, DN_VAL), row),
            pl.BlockSpec((tm, LANES), row),
            pl.BlockSpec((nseq, SUBLANES, DN_QKV), lambda i: (i // tps, 0, 0)),
        ],
        out_shape=[
            jax.ShapeDtypeStruct((m, DN_KEY), F32),
            jax.ShapeDtypeStruct((m, DN_KEY), F32),
            jax.ShapeDtypeStruct((m, DN_VAL), F32),
            jax.ShapeDtypeStruct((m, DN_VAL), F32),
            jax.ShapeDtypeStruct((m, LANES), F32),
            jax.ShapeDtypeStruct((nb, SUBLANES, DN_QKV), F32),
        ],
        scratch_shapes=[pltpu.VMEM((nseq, SUBLANES, DN_QKV), F32)],
        compiler_params=_params("arbitrary"),
        name="odd_in",
    )(x, w_qkv, w_z, w_ab, dconv_w, ab_const, buf8)


def _delta_kernel(q_ref, k_ref, v_ref, z_ref, gb_ref, s0_ref, go_ref, o_ref, sout_ref, state_ref, *, n_steps):
    n = pl.program_id(1)
    n_par, rows = q_ref.shape[0], q_ref.shape[1]
    n_sub = rows // CHUNK

    @pl.when(n == 0)
    def _():
        state_ref[...] = s0_ref[...]

    ri = lax.broadcasted_iota(jnp.int32, (rows, rows), 0)
    ci = lax.broadcasted_iota(jnp.int32, (rows, rows), 1)

    def blk(size):
        return (ri // size) == (ci // size)

    same = blk(CHUNK)
    lower = ri > ci
    tril = jnp.logical_and(same, ri >= ci)
    strict = jnp.logical_and(same, lower)
    eye = (ri == ci).astype(F32)
    base = DELTA_BASE_BLOCK
    levels = []
    size = base
    while size < CHUNK:
        levels.append(jnp.logical_and(jnp.logical_and(blk(2 * size), jnp.logical_not(blk(size))), lower))
        size *= 2
    in_base = jnp.logical_and(blk(base), lower)
    gb = [gb_ref[b] for b in range(n_par)]
    g_col = [jnp.dot(tril.astype(F32), gb[b], preferred_element_type=F32, precision=HIGHEST)
             for b in range(n_par)]
    g_end = [jnp.dot(same.astype(F32), gb[b], preferred_element_type=F32, precision=HIGHEST)
             for b in range(n_par)]
    g_row = [g_col[b].T for b in range(n_par)]
    go = go_ref[...]
    chain = [(b, hd) for b in range(n_par) for hd in range(DN_HEADS)]
    heads = range(len(chain))
    col = [slice(hd * DN_DK, (hd + 1) * DN_DK) for _, hd in chain]
    seq_of = [b for b, _ in chain]
    head_of = [hd for _, hd in chain]

    def tile(ref, h):
        return ref[seq_of[h], :, col[h]]

    gc = [g_col[seq_of[h]][:, head_of[h]:head_of[h] + 1] for h in heads]
    ge = [g_end[seq_of[h]][:, head_of[h]:head_of[h] + 1] for h in heads]
    decay = [jnp.exp(jnp.where(tril, gc[h] - g_row[seq_of[h]][head_of[h]:head_of[h] + 1, :], -jnp.inf))
             for h in heads]
    beta = [gb[seq_of[h]][:, DN_HEADS + head_of[h]:DN_HEADS + head_of[h] + 1] for h in heads]
    kb = [tile(k_ref, h) * beta[h] for h in heads]
    kbf = [tile(k_ref, h).astype(BF16) for h in heads]
    m_full = [jnp.where(strict, lax.dot_general(kb[h].astype(BF16), kbf[h], NT_DIMS,
                                                preferred_element_type=F32) * decay[h], 0.0) for h in heads]
    p = [-jnp.where(in_base, m_full[h], 0.0) for h in heads]
    t = [eye + p[h] for h in heads]
    size = 2
    while size < base:
        p = [_dot(p[h], p[h]) for h in heads]
        t = [t[h] + _dot(t[h], p[h]) for h in heads]
        size *= 2
    size = base
    for off in levels:
        nblk = rows // (2 * size)
        t4 = [t[h].reshape(nblk, 2, size, rows) for h in heads]
        t_low = [t4[h][:, 1].reshape(rows // 2, rows) for h in heads]
        u = [_dot(t_low[h], jnp.where(off, m_full[h], 0.0)) for h in heads]
        t_low = [(t_low[h] - _dot(u[h], t[h])).reshape(nblk, 1, size, rows) for h in heads]
        t = [jnp.concatenate([t4[h][:, 0:1], t_low[h]], axis=1).reshape(rows, rows) for h in heads]
        size *= 2
    e_g = [jnp.exp(gc[h]) for h in heads]
    sol = [_dot(t[h], jnp.concatenate([tile(v_ref, h) * beta[h], kb[h] * e_g[h]], axis=1)) for h in heads]
    attn = [lax.dot_general(tile(q_ref, h).astype(BF16), kbf[h], NT_DIMS,
                            preferred_element_type=F32) * decay[h] for h in heads]
    qd = [tile(q_ref, h) * e_g[h] for h in heads]
    kt = [tile(k_ref, h) * jnp.exp(ge[h] - gc[h]) for h in heads]
    st = [state_ref[seq_of[h], head_of[h]] for h in heads]
    v_new = [[] for _ in heads]
    o_inter = [[] for _ in heads]
    for c in range(n_sub):
        rs = slice(c * CHUNK, (c + 1) * CHUNK)
        for h in heads:
            r = _dot(jnp.concatenate([sol[h][rs, DN_DV:], qd[h][rs]], axis=0), st[h])
            v_c = sol[h][rs, :DN_DV] - r[:CHUNK]
            v_new[h].append(v_c)
            o_inter[h].append(r[CHUNK:])
            g_tot = jnp.exp(ge[h][c * CHUNK:c * CHUNK + 1, :])
            st[h] = st[h] * g_tot + lax.dot_general(kt[h][rs].astype(BF16), v_c.astype(BF16), TN_DIMS,
                                                    preferred_element_type=F32)
    for h in heads:
        state_ref[seq_of[h], head_of[h]] = st[h]
        o = jnp.concatenate(o_inter[h], axis=0) + _dot(attn[h], jnp.concatenate(v_new[h], axis=0))
        zh = tile(z_ref, h)
        o_ref[seq_of[h], :, col[h]] = (_rms_norm(o, go) * (zh * jax.nn.sigmoid(zh))).astype(BF16)

    @pl.when(n == n_steps - 1)
    def _():
        sout_ref[...] = state_ref[...]


def _delta_rule(q, k, v, z, gb, s0, g_o, *, seq, rows, n_par):
    m = q.shape[0]
    nb = m // seq
    rows = min(rows, seq)
    ns = seq // rows
    q, k, v, z, gb = (a.reshape(nb, seq, a.shape[-1]) for a in (q, k, v, z, gb))
    row = lambda b, n: (b, n, 0)
    st_spec = pl.BlockSpec((n_par, DN_HEADS, DN_DK, DN_DV), lambda b, n: (b, 0, 0, 0))
    og, s_new = pl.pallas_call(
        functools.partial(_delta_kernel, n_steps=ns),
        grid=(nb // n_par, ns),
        in_specs=[
            pl.BlockSpec((n_par, rows, DN_KEY), row),
            pl.BlockSpec((n_par, rows, DN_KEY), row),
            pl.BlockSpec((n_par, rows, DN_VAL), row),
            pl.BlockSpec((n_par, rows, DN_VAL), row),
            pl.BlockSpec((n_par, rows, LANES), row),
            st_spec,
            pl.BlockSpec((1, DN_DV), lambda b, n: (0, 0)),
        ],
        out_specs=[pl.BlockSpec((n_par, rows, DN_VAL), row), st_spec],
        out_shape=[
            jax.ShapeDtypeStruct((nb, seq, DN_VAL), BF16),
            jax.ShapeDtypeStruct((nb, DN_HEADS, DN_DK, DN_DV), F32),
        ],
        scratch_shapes=[pltpu.VMEM((n_par, DN_HEADS, DN_DK, DN_DV), F32)],
        compiler_params=_params("arbitrary", "arbitrary"),
        name="gated_delta",
    )(q, k, v, z, gb, s0, g_o)
    return og.reshape(m, DN_VAL), s_new


def _rope_table(pos):
    half = ROPE_DIM // 2
    inv = ROPE_THETA ** (-jnp.arange(half, dtype=F32) / half)
    ang = pos.astype(F32)[:, None] * inv[None, :]
    cs = jnp.concatenate([jnp.cos(ang), jnp.sin(ang)], -1)
    place = jnp.zeros((2 * half, 3 * LANES), F32)
    j = jnp.arange(half)
    place = place.at[j, NOPE_DIM + j].set(1.0).at[j, NOPE_DIM + half + j].set(1.0)
    place = place.at[half + j, LANES + NOPE_DIM + half + j].set(1.0)
    place = place.at[half + j, 2 * LANES + NOPE_DIM + j].set(-1.0)
    ones = (jnp.arange(3 * LANES) < NOPE_DIM).astype(F32)
    return jnp.dot(cs, place, precision=HIGHEST) + ones[None, :]


def _pad_heads(w, per_head, keep):
    kdim = w.shape[0]
    w = w.reshape(kdim, MLA_HEADS, per_head)[:, :, :keep]
    w = jnp.pad(w, ((0, 0), (0, 0), (0, HEAD_PAD - keep)))
    return w.reshape(kdim, MLA_HEADS * HEAD_PAD)


def _prep_even(w_in, w_uq, w_ukv):
    c1 = Q_RANK + KV_RANK
    w_kr = jnp.pad(w_in[:, c1:c1 + ROPE_DIM], ((0, 0), (NOPE_DIM, HEAD_PAD - NOPE_DIM - ROPE_DIM)))
    w1 = jnp.concatenate([w_in[:, :c1], w_kr], -1).astype(BF16)
    w2 = w_in[:, c1 + ROPE_DIM:].astype(BF16)
    wuq = _pad_heads(w_uq, NOPE_DIM + ROPE_DIM, NOPE_DIM + ROPE_DIM).astype(BF16)
    wuk = _pad_heads(w_ukv, NOPE_DIM + V_DIM, NOPE_DIM).astype(BF16)
    wuv = w_ukv.reshape(KV_RANK, MLA_HEADS, NOPE_DIM + V_DIM)[:, :, NOPE_DIM:].reshape(KV_RANK, MLA_WIDTH)
    return w1, w2, wuq, wuk, wuv.T.astype(BF16)


def _prep_odd(w_in, a_log, dt_bias):
    w_qkv = w_in[:, :DN_QKV].reshape(D_MODEL, 3, DN_KEY).transpose(1, 0, 2).astype(BF16)
    w_z = w_in[:, DN_QKV:DN_QKV + DN_VAL].astype(BF16)
    w_ab = jnp.pad(w_in[:, DN_QKV + DN_VAL:], ((0, 0), (0, LANES - 2 * DN_HEADS))).astype(BF16)
    ab_const = jnp.zeros((SUBLANES, LANES), F32)
    ab_const = ab_const.at[0, :DN_HEADS].set(a_log.astype(F32)).at[1, :DN_HEADS].set(dt_bias.astype(F32))
    return w_qkv, w_z, w_ab, ab_const


def _prep_ffn(w_in, conv_w, w_out):
    return w_in.astype(BF16), w_out.astype(BF16), conv_w


def _pad_rows8(buf):
    return jnp.pad(buf.astype(F32), ((0, 0), (SUBLANES - buf.shape[1], 0), (0, 0)))


def _trunk(x, past_lat, past_kr, sc_buf, dconv_buf, delta_s, ff_buf, w, *, tm, tq, tk, dr):
    (even_w, odd_w, ffn_w, w_o_e, w_o_o, g_qnorm, g_kvnorm, sc_w, dconv_w, g_onorm,
     ln_mix_g, ln_mix_b, ln_ff_g, ln_ff_b) = w
    nb, seq, _ = x.shape
    past = 0 if past_lat is None else past_lat.shape[2]
    m = nb * seq
    x = x.reshape(m, D_MODEL)
    tab = _rope_table(past + jnp.arange(seq, dtype=jnp.int32))
    t_valid = past + seq
    tq = min(tq, seq)
    tk = min(tk, t_valid)
    t_pad = -(-t_valid // tk) * tk
    lats, krs, scs, dcs, dss, ffs = [], [], [], [], [], []
    for i in range(DEPTH):
        row = lambda a: a[i].reshape(1, -1)
        if i % 2 == 0:
            e = i // 2
            w1, w2, wuq, wuk, wuv = even_w[e]
            q, lat, kr_pad, o_sc, sc_new = _even_in(
                x, w1, w2, g_qnorm[e].reshape(1, -1), g_kvnorm[e].reshape(1, -1), wuq, sc_w[e], tab,
                _pad_rows8(sc_buf[e]), seq=seq, tm=tm)
            lat_all = lat.reshape(nb, seq, KV_RANK)
            kr_all = kr_pad.reshape(nb, seq, LANES)
            if past:
                kr_past = jnp.pad(past_kr[e].astype(F32),
                                  ((0, 0), (0, 0), (NOPE_DIM, HEAD_PAD - NOPE_DIM - ROPE_DIM)))
                lat_all = jnp.concatenate([past_lat[e].astype(F32), lat_all], 1)
                kr_all = jnp.concatenate([kr_past, kr_all], 1)
            if t_pad != t_valid:
                lat_all = jnp.pad(lat_all, ((0, 0), (0, t_pad - t_valid), (0, 0)))
                kr_all = jnp.pad(kr_all, ((0, 0), (0, t_pad - t_valid), (0, 0)))
            k, v = _kv_up(lat_all.reshape(nb * t_pad, KV_RANK), kr_all.reshape(nb * t_pad, LANES),
                          wuk, wuv, nb=nb, t_pad=t_pad, tm=tk)
            o_att = _attention(q, k, v, nb=nb, sq=seq, t_pad=t_pad, t_valid=t_valid, past=past, tq=tq, tk=tk)
            mix, mix_w = [o_att, o_sc], [w_o_e[e][:MLA_WIDTH], w_o_e[e][MLA_WIDTH:]]
            lats.append(lat.reshape(nb, seq, KV_RANK))
            krs.append(kr_pad.reshape(nb, seq, LANES)[:, :, NOPE_DIM:NOPE_DIM + ROPE_DIM])
            scs.append(sc_new[:, SUBLANES - (SC_K - 1):])
        else:
            o = i // 2
            w_qkv, w_z, w_ab, ab_const = odd_w[o]
            q, k, v, z, gb, dc_new = _odd_in(x, w_qkv, w_z, w_ab, dconv_w[o], ab_const,
                                             _pad_rows8(dconv_buf[o]), seq=seq, tm=tm)
            og, s_new = _delta_rule(q, k, v, z, gb, delta_s[o].astype(F32), g_onorm[o].reshape(1, -1), seq=seq,
                                    rows=dr, n_par=DELTA_SEQS_PER_STEP)
            mix, mix_w = [og], [w_o_o[o]]
            dcs.append(dc_new[:, SUBLANES - (DN_CONV - 1):])
            dss.append(s_new)
        w_ff_in, w_ff_out, cw = ffn_w[i]
        x, ff_new = _mix_ffn(mix, mix_w, x, row(ln_mix_g), row(ln_mix_b), _pad_rows8(ff_buf[i]),
                             w_ff_in, w_ff_out, cw, row(ln_ff_g), row(ln_ff_b), seq=seq, tm=tm)
        ffs.append(ff_new[:, SUBLANES - (FF_K - 1):])
    return (x.reshape(nb, seq, D_MODEL), jnp.stack(lats), jnp.stack(krs), jnp.stack(scs), jnp.stack(dcs),
            jnp.stack(dss), jnp.stack(ffs))


def _prepare(w_in_e, w_uq, w_ukv, w_o_e, w_in_o, a_log, dt_bias, w_o_o, w_ff_in, ffconv_w, w_ff_out):
    even_w = [_prep_even(w_in_e[e], w_uq[e], w_ukv[e]) for e in range(N_EVEN)]
    odd_w = [_prep_odd(w_in_o[o], a_log[o], dt_bias[o]) for o in range(N_ODD)]
    ffn_w = [_prep_ffn(w_ff_in[i], ffconv_w[i], w_ff_out[i]) for i in range(DEPTH)]
    return even_w, odd_w, ffn_w, w_o_e.astype(BF16), w_o_o.astype(BF16)


def kernel(x_prompt, x_sample, cache_mla_latent, cache_mla_krope, state_sconv, state_dconv, state_delta, state_ffconv, w_in_e, g_qnorm, g_kvnorm, w_uq, w_ukv, sc_w, w_o_e, w_in_o, dconv_w, a_log, dt_bias, g_onorm, w_o_o, w_ff_in, ffconv_w, w_ff_out, ln_mix_g, ln_mix_b, ln_ff_g, ln_ff_b):
    even_w, odd_w, ffn_w, w_o_e_b, w_o_o_b = _prepare(
        w_in_e, w_uq, w_ukv, w_o_e, w_in_o, a_log, dt_bias, w_o_o, w_ff_in, ffconv_w, w_ff_out)
    w = (even_w, odd_w, ffn_w, w_o_e_b, w_o_o_b, g_qnorm, g_kvnorm, sc_w, dconv_w, g_onorm,
         ln_mix_g, ln_mix_b, ln_ff_g, ln_ff_b)
    bp = x_prompt.shape[0]
    f32 = x_prompt.dtype
    y_p, p_lat, p_kr, p_sc, p_dc, p_ds, p_ff = _trunk(
        x_prompt, None, None,
        jnp.zeros((N_EVEN, bp, SC_K - 1, SC_WIDTH), f32),
        jnp.zeros((N_ODD, bp, DN_CONV - 1, DN_QKV), f32),
        jnp.zeros((N_ODD, bp, DN_HEADS, DN_DK, DN_DV), f32),
        jnp.zeros((DEPTH, bp, FF_K - 1, D_FF), f32),
        w, tm=512, tq=512, tk=512, dr=256)
    y_s, s_lat, s_kr, s_sc, s_dc, s_ds, s_ff = _trunk(
        x_sample, cache_mla_latent, cache_mla_krope, state_sconv, state_dconv, state_delta, state_ffconv,
        w, tm=512, tq=64, tk=768, dr=256)
    return (y_p, y_s, p_lat, p_kr, p_sc, p_dc, p_ds, p_ff, s_lat, s_kr, s_sc, s_dc, s_ds, s_ff)
```

```python
import functools
import math

import jax
import jax.numpy as jnp
from jax import lax
from jax.experimental import pallas as pl
from jax.experimental.pallas import tpu as pltpu

D_MODEL = 1024
DEPTH = 4
CHUNK = 64
N_EVEN = (DEPTH + 1) // 2
N_ODD = DEPTH // 2
MLA_HEADS = 8
Q_RANK = 384
KV_RANK = 256
NOPE_DIM = 64
ROPE_DIM = 32
V_DIM = 64
ROPE_THETA = 10000.0
MLA_WIDTH = MLA_HEADS * V_DIM
SC_WIDTH = 512
SC_K = 3
DN_HEADS = 8
DN_DK = 128
DN_DV = 128
DN_CONV = 4
DN_KEY = DN_HEADS * DN_DK
DN_VAL = DN_HEADS * DN_DV
DN_QKV = 2 * DN_KEY + DN_VAL
D_FF = 2816
FF_K = 3
ALPHA = (2 * DEPTH) ** 0.25
NORM_EPS = 1e-6
NEG_INF = -1e30
LOG2_E = math.log2(math.e)

F32 = jnp.float32
BF16 = jnp.bfloat16
HIGHEST = lax.Precision.HIGHEST

LANES = 128
SUBLANES = 8
BF16_SUBLANES = 16
HEAD_PAD = 128
FF_CHUNK = 256
DELTA_BASE_BLOCK = 8
DELTA_SEQS_PER_STEP = 2
ATTN_HEAD_GROUP = 8
FAST_HEAD_GROUP = 8
FAST_SOFTMAX_MAX_RISE = 60.0
VMEM_LIMIT = 56 * 1024 * 1024

NT_DIMS = (((1,), (1,)), ((), ()))
TN_DIMS = (((0,), (0,)), ((), ()))


def _params(*sem):
    return pltpu.CompilerParams(dimension_semantics=sem, vmem_limit_bytes=VMEM_LIMIT)


def _dot(a, b):
    return jnp.dot(a.astype(BF16), b.astype(BF16), preferred_element_type=F32)


def _layer_norm(r, g, b):
    mu = jnp.mean(r, axis=-1, keepdims=True)
    d = r - mu
    var = jnp.mean(d * d, axis=-1, keepdims=True)
    return d * lax.rsqrt(var + NORM_EPS) * g + b


def _rms_norm(x, g):
    return x * lax.rsqrt(jnp.mean(x * x, axis=-1, keepdims=True) + NORM_EPS) * g


def _silu(x):
    h = 0.5 * x
    return h + h * jnp.tanh(h)


def _softplus(x):
    return jnp.maximum(x, 0.0) + jnp.log1p(jnp.exp(-jnp.abs(x)))


def _causal_conv(x, prev, w):
    tm, c = x.shape
    nseq = prev.shape[0]
    k_taps = w.shape[0]
    groups = tm // nseq // SUBLANES
    x4 = x.reshape(nseq, groups, SUBLANES, c)
    prev4 = prev.reshape(nseq, 1, SUBLANES, c)
    sub = lax.broadcasted_iota(jnp.int32, x4.shape, 2)
    y = x4 * w[k_taps - 1:k_taps, :]
    for k in range(1, k_taps):
        rot = pltpu.roll(x4, k, axis=2)
        rot_prev = jnp.concatenate([pltpu.roll(prev4, k, axis=2), rot[:, :groups - 1]], axis=1)
        y = y + jnp.where(sub < k, rot_prev, rot) * w[k_taps - 1 - k:k_taps - k, :]
    return y.reshape(tm, c)


def _run_tails(x, nseq):
    tm, c = x.shape
    run = tm // nseq
    return x.reshape(nseq, run, c)[:, run - SUBLANES:, :]


def _row_tiling(m, seq, tm):
    tm = min(tm, m)
    if seq >= tm:
        return tm, seq // tm, 1
    return tm, 1, tm // seq


def _mix_ffn_kernel(*refs, n_mix, tiles_per_seq):
    ys, wos = refs[:n_mix], refs[n_mix:2 * n_mix]
    (x_ref, g1_ref, b1_ref, w_in_ref, w_out_ref, cw_ref, buf_ref, g2_ref, b2_ref,
     out_ref, nbuf_ref, act_ref, x1_ref, xb_ref, win_ref) = refs[2 * n_mix:]
    i = pl.program_id(0)
    nseq = win_ref.shape[0]

    tm = x_ref.shape[0]
    halves = [slice(0, tm // 2), slice(tm // 2, tm)]
    seq_half = [slice(0, nseq // 2), slice(nseq // 2, nseq)]

    for rows in halves:
        mix = jnp.dot(ys[0][rows, :], wos[0][...], preferred_element_type=F32)
        for j in range(1, n_mix):
            mix = mix + jnp.dot(ys[j][rows, :], wos[j][...], preferred_element_type=F32)
        x1 = _layer_norm(ALPHA * x_ref[rows, :] + mix, g1_ref[...], b1_ref[...])
        x1_ref[rows, :] = x1
        xb_ref[rows, :] = x1.astype(BF16)

    @pl.when(i % tiles_per_seq == 0)
    def _():
        win_ref[...] = buf_ref[...]

    for c in range(D_FF // FF_CHUNK):
        cols = slice(c * FF_CHUNK, (c + 1) * FF_CHUNK)
        up_cols = slice(D_FF + c * FF_CHUNK, D_FF + (c + 1) * FF_CHUNK)
        prev = win_ref[:, :, cols]
        tails = []
        for hf, rows in enumerate(halves):
            xb = xb_ref[rows, :]
            gate = jnp.dot(xb, w_in_ref[:, cols], preferred_element_type=F32)
            up = jnp.dot(xb, w_in_ref[:, up_cols], preferred_element_type=F32)
            if nseq == 1:
                y = _causal_conv(gate, prev if hf == 0 else tails[0], cw_ref[:, cols])
                tails.append(_run_tails(gate, 1))
            else:
                y = _causal_conv(gate, prev[seq_half[hf]], cw_ref[:, cols])
                tails.append(_run_tails(gate, nseq // 2))
            act_ref[rows, cols] = (_silu(y) * up).astype(BF16)
        tails = tails[1] if nseq == 1 else jnp.concatenate(tails, axis=0)
        win_ref[:, :, cols] = tails
        nbuf_ref[:, :, cols] = tails

    for rows in halves:
        acc = jnp.dot(act_ref[rows, :], w_out_ref[...], preferred_element_type=F32)
        out_ref[rows, :] = _layer_norm(ALPHA * x1_ref[rows, :] + acc, g2_ref[...], b2_ref[...])


def _mix_ffn(ys, wos, x, mix_g, mix_b, buf8, w_in, w_out, conv_w, ff_g, ff_b, *, seq, tm):
    m = x.shape[0]
    nb = m // seq
    tm, tps, nseq = _row_tiling(m, seq, tm)
    const2 = lambda i: (0, 0)
    row = lambda i: (i, 0)
    resident = lambda a: pl.BlockSpec(a.shape, const2, pipeline_mode=pl.Buffered(1))
    vec = pl.BlockSpec((1, D_MODEL), const2)
    state = pl.BlockSpec((nseq, SUBLANES, D_FF), lambda i: (i // tps, 0, 0))
    return pl.pallas_call(
        functools.partial(_mix_ffn_kernel, n_mix=len(ys), tiles_per_seq=tps),
        grid=(m // tm,),
        in_specs=([pl.BlockSpec((tm, y.shape[1]), row) for y in ys]
                  + [resident(w) for w in wos]
                  + [pl.BlockSpec((tm, D_MODEL), row), vec, vec,
                     resident(w_in), resident(w_out), pl.BlockSpec(conv_w.shape, const2), state, vec, vec]),
        out_specs=[pl.BlockSpec((tm, D_MODEL), row), state],
        out_shape=[
            jax.ShapeDtypeStruct((m, D_MODEL), F32),
            jax.ShapeDtypeStruct((nb, SUBLANES, D_FF), F32),
        ],
        scratch_shapes=[
            pltpu.VMEM((tm, D_FF), BF16),
            pltpu.VMEM((tm, D_MODEL), F32),
            pltpu.VMEM((tm, D_MODEL), BF16),
            pltpu.VMEM((nseq, SUBLANES, D_FF), F32),
        ],
        compiler_params=_params("arbitrary"),
        name="mix_ffn",
    )(*ys, *wos, x, mix_g, mix_b, w_in, w_out, conv_w, buf8, ff_g, ff_b)


def _rope(x, tab):
    return (x * tab[:, 0:LANES]
            + pltpu.roll(x, ROPE_DIM // 2, axis=1) * tab[:, LANES:2 * LANES]
            + pltpu.roll(x, LANES - ROPE_DIM // 2, axis=1) * tab[:, 2 * LANES:3 * LANES])


def _keys_values(lat, kr_pad, wuk_ref, wuvt_ref, k_ref, vt_ref):
    lb = lat.astype(BF16)
    kn = jnp.dot(lb, wuk_ref[...], preferred_element_type=F32)
    for h in range(MLA_HEADS):
        sl = slice(h * HEAD_PAD, (h + 1) * HEAD_PAD)
        k_ref[:, sl] = (kn[:, sl] + kr_pad).astype(BF16)
    vt_ref[...] = lax.dot_general(wuvt_ref[...], lb, NT_DIMS, preferred_element_type=F32).astype(BF16)


def _even_in_kernel(x_ref, w1_ref, w2_ref, gq_ref, gkv_ref, wuq_ref, scw_ref, tab_ref, buf_ref, wuk_ref, wuvt_ref,
                    q_ref, lat_ref, kr_ref, osc_ref, nbuf_ref, k_ref, vt_ref, win_ref, *, tiles_per_seq):
    i = pl.program_id(0)
    nseq = win_ref.shape[0]

    @pl.when(i % tiles_per_seq == 0)
    def _():
        win_ref[...] = buf_ref[...]

    xb = x_ref[...].astype(BF16)
    tab = tab_ref[...]
    h1 = jnp.dot(xb, w1_ref[...], preferred_element_type=F32)
    cq = h1[:, :Q_RANK]
    ckv = h1[:, Q_RANK:Q_RANK + KV_RANK]
    kr_raw = h1[:, Q_RANK + KV_RANK:]
    lat = _rms_norm(ckv, gkv_ref[...])
    kr_pad = _rope(kr_raw, tab)
    lat_ref[...] = lat
    kr_ref[...] = kr_pad
    _keys_values(lat, kr_pad, wuk_ref, wuvt_ref, k_ref, vt_ref)
    q_raw =_dot(_rms_norm(cq, gq_ref[...]), wuq_ref[...])
    scale = (NOPE_DIM + ROPE_DIM) ** -0.5 * LOG2_E
    for h in range(MLA_HEADS):
        sl = slice(h * HEAD_PAD, (h + 1) * HEAD_PAD)
        q_ref[:, sl] = (_rope(q_raw[:, sl], tab) * scale).astype(BF16)

    h2 = jnp.dot(xb, w2_ref[...], preferred_element_type=F32)
    gate_b = h2[:, :SC_WIDTH]
    u_in = h2[:, SC_WIDTH:2 * SC_WIDTH] * h2[:, 2 * SC_WIDTH:]
    u = _causal_conv(u_in, win_ref[...], scw_ref[...])
    tails = _run_tails(u_in, nseq)
    win_ref[...] = tails
    nbuf_ref[...] = tails
    osc_ref[...] = (gate_b * u).astype(BF16)


def _even_in(x, w1, w2, g_q, g_kv, w_uq, sc_w, tab, buf8, w_uk, w_uv_t, *, seq, tm):
    m = x.shape[0]
    nb = m // seq
    tm, tps, nseq = _row_tiling(m, seq, tm)
    if nseq > 1:
        tab = jnp.tile(tab, (nseq, 1))
    const2 = lambda i: (0, 0)
    return pl.pallas_call(
        functools.partial(_even_in_kernel, tiles_per_seq=tps),
        grid=(m // tm,),
        in_specs=[
            pl.BlockSpec((tm, D_MODEL), lambda i: (i, 0)),
            pl.BlockSpec(w1.shape, const2),
            pl.BlockSpec(w2.shape, const2),
            pl.BlockSpec(g_q.shape, const2),
            pl.BlockSpec(g_kv.shape, const2),
            pl.BlockSpec(w_uq.shape, const2),
            pl.BlockSpec(sc_w.shape, const2),
            pl.BlockSpec((tm, 3 * LANES), lambda i: (i % tps, 0)),
            pl.BlockSpec((nseq, SUBLANES, SC_WIDTH), lambda i: (i // tps, 0, 0)),
            pl.BlockSpec(w_uk.shape, const2),
            pl.BlockSpec(w_uv_t.shape, const2),
        ],
        out_specs=[
            pl.BlockSpec((tm, MLA_HEADS * HEAD_PAD), lambda i: (i, 0)),
            pl.BlockSpec((tm, KV_RANK), lambda i: (i, 0)),
            pl.BlockSpec((tm, LANES), lambda i: (i, 0)),
            pl.BlockSpec((tm, SC_WIDTH), lambda i: (i, 0)),
            pl.BlockSpec((nseq, SUBLANES, SC_WIDTH), lambda i: (i // tps, 0, 0)),
            pl.BlockSpec((tm, MLA_HEADS * HEAD_PAD), lambda i: (i, 0)),
            pl.BlockSpec((MLA_WIDTH, tm), lambda i: (i, 0)),
        ],
        out_shape=[
            jax.ShapeDtypeStruct((m, MLA_HEADS * HEAD_PAD), BF16),
            jax.ShapeDtypeStruct((m, KV_RANK), F32),
            jax.ShapeDtypeStruct((m, LANES), F32),
            jax.ShapeDtypeStruct((m, SC_WIDTH), BF16),
            jax.ShapeDtypeStruct((nb, SUBLANES, SC_WIDTH), F32),
            jax.ShapeDtypeStruct((m, MLA_HEADS * HEAD_PAD), BF16),
            jax.ShapeDtypeStruct((m // tm * MLA_WIDTH, tm), BF16),
        ],
        scratch_shapes=[pltpu.VMEM((nseq, SUBLANES, SC_WIDTH), F32)],
        compiler_params=_params("arbitrary"),
        name="even_in",
    )(x, w1, w2, g_q, g_kv, w_uq, sc_w, tab, buf8, w_uk, w_uv_t)


def _kv_up_kernel(lat_ref, kr_ref, wuk_ref, wuvt_ref, k_ref, vt_ref):
    _keys_values(lat_ref[...], kr_ref[...], wuk_ref, wuvt_ref, k_ref, vt_ref)


def _kv_up(lat, kr_pad, w_uk, w_uv_t, *, tm):
    m = lat.shape[0]
    const2 = lambda i: (0, 0)
    return pl.pallas_call(
        _kv_up_kernel,
        grid=(m // tm,),
        in_specs=[
            pl.BlockSpec((tm, KV_RANK), lambda i: (i, 0)),
            pl.BlockSpec((tm, LANES), lambda i: (i, 0)),
            pl.BlockSpec(w_uk.shape, const2),
            pl.BlockSpec(w_uv_t.shape, const2),
        ],
        out_specs=[
            pl.BlockSpec((tm, MLA_HEADS * HEAD_PAD), lambda i: (i, 0)),
            pl.BlockSpec((MLA_WIDTH, tm), lambda i: (i, 0)),
        ],
        out_shape=[
            jax.ShapeDtypeStruct((m, MLA_HEADS * HEAD_PAD), BF16),
            jax.ShapeDtypeStruct((m // tm * MLA_WIDTH, tm), BF16),
        ],
        compiler_params=_params("arbitrary"),
        name="kv_up",
    )(lat, kr_pad, w_uk, w_uv_t)


def _attn_kernel(q_ref, k_ref, vt_ref, o_ref, m_ref, l_ref, acc_ref, *, tq, tk, past, t_valid, nk):
    qi = pl.program_id(1)
    kj = pl.program_id(2)
    q_lo = past + qi * tq
    chunk_lo = q_lo // CHUNK
    chunk_hi = (q_lo + tq - 1) // CHUNK
    last = jnp.minimum(((chunk_hi + 1) * CHUNK - 1) // tk, nk - 1)
    k_end = (kj + 1) * tk
    full = jnp.logical_and(k_end <= (chunk_lo + 1) * CHUNK, k_end <= t_valid)
    heads = list(range(MLA_HEADS))
    hsl =[slice(h * HEAD_PAD, (h + 1) * HEAD_PAD) for h in heads]
    vsl = [slice(h * V_DIM, (h + 1) * V_DIM) for h in heads]

    first = kj == 0

    @pl.when(first)
    def _():
        m_ref[...] = jnp.zeros_like(m_ref)
        l_ref[...] = jnp.zeros_like(l_ref)
        acc_ref[...] = jnp.zeros_like(acc_ref)

    def visibility():
        kpos = kj * tk + lax.broadcasted_iota(jnp.int32, (tk, tq), 0)
        qpos = q_lo + lax.broadcasted_iota(jnp.int32, (tk, tq), 1)
        return jnp.logical_and(kpos // CHUNK <= qpos // CHUNK, kpos < t_valid)

    def scores(h, visible):
        s = lax.dot_general(k_ref[:, hsl[h]], q_ref[:, hsl[h]], NT_DIMS, preferred_element_type=F32)
        return s if visible is None else jnp.where(visible, s, NEG_INF)

    def step(masked):
        visible = visibility() if masked else None
        ones = jnp.ones((BF16_SUBLANES, tk), BF16)
        for g0 in range(0, MLA_HEADS, ATTN_HEAD_GROUP):
            grp = range(g0, g0 + ATTN_HEAD_GROUP)
            s = {h: scores(h, visible) for h in grp}
            m_prev = {h: m_ref[h:h + 1, :] for h in grp}
            m_new = {h: jnp.maximum(m_prev[h], jnp.max(s[h], axis=0, keepdims=True)) for h in grp}
            alpha = {h: jnp.exp2(m_prev[h] - m_new[h]) for h in grp}
            p = {h: jnp.exp2(s[h] - m_new[h]).astype(BF16) for h in grp}
            pv = {h: jnp.dot(jnp.concatenate([vt_ref[vsl[h], :], ones], axis=0), p[h],
                             preferred_element_type=F32) for h in grp}
            for h in grp:
                l_ref[h:h + 1, :] = alpha[h] * l_ref[h:h + 1, :] + pv[h][V_DIM:V_DIM + 1, :]
                m_ref[h:h + 1, :] = m_new[h]
                acc_ref[vsl[h], :] = alpha[h] * acc_ref[vsl[h], :] + pv[h][:V_DIM, :]

    def fast_step(masked):
        visible = visibility() if masked else None
        ones = jnp.ones((BF16_SUBLANES, tk), BF16)
        pv, t_max = {}, {}
        for g0 in range(0, MLA_HEADS, FAST_HEAD_GROUP):
            grp = range(g0, g0 + FAST_HEAD_GROUP)
            s = {h: scores(h, visible) for h in grp}
            p = {h: jnp.exp2(s[h] - m_ref[h:h + 1, :]).astype(BF16) for h in grp}
            for h in grp:
                t_max[h] = jnp.max(s[h], axis=0, keepdims=True)
            for h in grp:
                pv[h] = jnp.dot(jnp.concatenate([vt_ref[vsl[h], :], ones], axis=0), p[h],
                                preferred_element_type=F32)
        rise = t_max[0] - m_ref[0:1, :]
        fall = rise
        for h in heads[1:]:
            d = t_max[h] - m_ref[h:h + 1, :]
            rise = jnp.maximum(rise, d)
            fall = jnp.minimum(fall, d)
        in_range = jnp.logical_and(jnp.max(rise) <= FAST_SOFTMAX_MAX_RISE,
                                   jnp.logical_or(jnp.logical_not(first),
                                                  jnp.min(fall) >= -FAST_SOFTMAX_MAX_RISE))

        @pl.when(in_range)
        def _():
            for h in heads:
                m_prev = m_ref[h:h + 1, :]
                m_new = jnp.where(first, t_max[h], jnp.maximum(m_prev, t_max[h]))
                alpha = jnp.exp2(m_prev - m_new)
                l_ref[h:h + 1, :] = (l_ref[h:h + 1, :] + pv[h][V_DIM:V_DIM + 1, :]) * alpha
                acc_ref[vsl[h], :] = (acc_ref[vsl[h], :] + pv[h][:V_DIM, :]) * alpha
                m_ref[h:h + 1, :] = m_new

        @pl.when(jnp.logical_not(in_range))
        def _():
            @pl.when(first)
            def _():
                m_ref[...] = jnp.full_like(m_ref, NEG_INF)
            step(masked)

    live = kj <= last

    @pl.when(jnp.logical_and(live, full))
    def _():
        fast_step(False)

    @pl.when(jnp.logical_and(live, jnp.logical_not(full)))
    def _():
        fast_step(True)

    @pl.when(kj == last)
    def _():
        for hp in range(MLA_HEADS // 2):
            pair = jnp.concatenate(
                [acc_ref[vsl[h], :] * (1.0 / l_ref[h:h + 1, :]) for h in (2 * hp, 2 * hp + 1)], axis=0)
            o_ref[:, hp * 2 * V_DIM:(hp + 1) * 2 * V_DIM] = pair.T.astype(BF16)


def _attention(q, k, vt, *, nb, sq, t_pad, t_valid, past, tq, tk):
    nq = sq // tq
    nk = t_pad // tk

    def last_tile(qi):
        chunk_hi = (past + qi * tq + tq - 1) // CHUNK
        return jnp.minimum(((chunk_hi + 1) * CHUNK - 1) // tk, nk - 1)

    return pl.pallas_call(
        functools.partial(_attn_kernel, tq=tq, tk=tk, past=past, t_valid=t_valid, nk=nk),
        grid=(nb, nq, nk),
        in_specs=[
            pl.BlockSpec((tq, MLA_HEADS * HEAD_PAD), lambda b, qi, kj: (b * nq + qi, 0)),
            pl.BlockSpec((tk, MLA_HEADS * HEAD_PAD),
                         lambda b, qi, kj: (b * nk + jnp.minimum(kj, last_tile(qi)), 0)),
            pl.BlockSpec((MLA_WIDTH, tk), lambda b, qi, kj: (b * nk + jnp.minimum(kj, last_tile(qi)), 0)),
        ],
        out_specs=pl.BlockSpec((tq, MLA_WIDTH), lambda b, qi, kj: (b * nq + qi, 0)),
        out_shape=jax.ShapeDtypeStruct((nb * sq, MLA_WIDTH), BF16),
        scratch_shapes=[
            pltpu.VMEM((MLA_HEADS, tq), F32),
            pltpu.VMEM((MLA_HEADS, tq), F32),
            pltpu.VMEM((MLA_WIDTH, tq), F32),
        ],
        compiler_params=_params("arbitrary", "arbitrary", "arbitrary"),
        name="chunk_attention",
    )(q, k, vt)


def _odd_in_kernel(x_ref, wqkv_ref, wz_ref, wab_ref, dcw_ref, ab_ref, buf_ref,
                   q_ref, k_ref, v_ref, z_ref, gb_ref, nbuf_ref, win_ref, *, tiles_per_seq):
    i = pl.program_id(0)
    nseq = win_ref.shape[0]

    @pl.when(i % tiles_per_seq == 0)
    def _():
        win_ref[...] = buf_ref[...]

    xb = x_ref[...].astype(BF16)
    outs = (q_ref, k_ref, v_ref)
    for part in range(3):
        cols = slice(part * DN_KEY, (part + 1) * DN_KEY)
        pre = jnp.dot(xb, wqkv_ref[part], preferred_element_type=F32)
        y = _causal_conv(pre, win_ref[:, :, cols], dcw_ref[:, cols])
        tails = _run_tails(pre, nseq)
        win_ref[:, :, cols] = tails
        nbuf_ref[:, :, cols] = tails
        y = _silu(y)
        if part == 2:
            outs[part][...] = y
        else:
            post = DN_DK ** -0.5 if part == 0 else 1.0
            for h in range(DN_HEADS):
                sl = slice(h * DN_DK, (h + 1) * DN_DK)
                seg = y[:, sl]
                nrm = lax.rsqrt(jnp.sum(seg * seg, axis=-1, keepdims=True) + NORM_EPS)
                outs[part][:, sl] = seg * nrm * post if part == 0 else seg * nrm

    z_ref[...] = jnp.dot(xb, wz_ref[...], preferred_element_type=F32)
    ab = jnp.dot(xb, wab_ref[...], preferred_element_type=F32)
    lane = lax.broadcasted_iota(jnp.int32, ab.shape, 1)
    g = -jnp.exp(ab_ref[0:1, :]) * _softplus(ab + ab_ref[1:2, :])
    gb_ref[...] = jnp.where(lane < DN_HEADS, g, jax.nn.sigmoid(ab))


def _odd_in(x, w_qkv, w_z, w_ab, dconv_w, ab_const, buf8, *, seq, tm):
    m = x.shape[0]
    nb = m // seq
    tm, tps, nseq = _row_tiling(m, seq, tm)
    const2 = lambda i: (0, 0)
    row = lambda i: (i, 0)
    return pl.pallas_call(
        functools.partial(_odd_in_kernel, tiles_per_seq=tps),
        grid=(m // tm,),
        in_specs=[
            pl.BlockSpec((tm, D_MODEL), row),
            pl.BlockSpec(w_qkv.shape, lambda i: (0, 0, 0)),
            pl.BlockSpec(w_z.shape, const2),
            pl.BlockSpec(w_ab.shape, const2),
            pl.BlockSpec(dconv_w.shape, const2),
            pl.BlockSpec(ab_const.shape, const2),
            pl.BlockSpec((nseq, SUBLANES, DN_QKV), lambda i: (i // tps, 0, 0)),
        ],
        out_specs=[
            pl.BlockSpec((tm, DN_KEY), row),
            pl.BlockSpec((tm, DN_KEY), row),
            pl.BlockSpec((tm, DN_VAL), row),
            pl.BlockSpec((tm, DN_VAL), row),
            pl.BlockSpec((tm, LANES), row),
            pl.BlockSpec((nseq, SUBLANES, DN_QKV), lambda i: (i // tps, 0, 0)),
        ],
        out_shape=[
            jax.ShapeDtypeStruct((m, DN_KEY), F32),
            jax.ShapeDtypeStruct((m, DN_KEY), F32),
            jax.ShapeDtypeStruct((m, DN_VAL), F32),
            jax.ShapeDtypeStruct((m, DN_VAL), F32),
            jax.ShapeDtypeStruct((m, LANES), F32),
            jax.ShapeDtypeStruct((nb, SUBLANES, DN_QKV), F32),
        ],
        scratch_shapes=[pltpu.VMEM((nseq, SUBLANES, DN_QKV), F32)],
        compiler_params=_params("arbitrary"),
        name="odd_in",
    )(x, w_qkv, w_z, w_ab, dconv_w, ab_const, buf8)


def _delta_kernel(q_ref, k_ref, v_ref, z_ref, gb_ref, s0_ref, go_ref, o_ref, sout_ref, state_ref, *, n_steps):
    n = pl.program_id(1)
    n_par, rows = q_ref.shape[0], q_ref.shape[1]
    n_sub = rows // CHUNK

    @pl.when(n == 0)
    def _():
        state_ref[...] = s0_ref[...]

    ri = lax.broadcasted_iota(jnp.int32, (rows, rows), 0)
    ci = lax.broadcasted_iota(jnp.int32, (rows, rows), 1)

    def blk(size):
        return (ri // size) == (ci // size)

    same = blk(CHUNK)
    lower = ri > ci
    tril = jnp.logical_and(same, ri >= ci)
    strict = jnp.logical_and(same, lower)
    eye = (ri == ci).astype(F32)
    base = DELTA_BASE_BLOCK
    levels = []
    size = base
    while size < CHUNK:
        levels.append(jnp.logical_and(jnp.logical_and(blk(2 * size), jnp.logical_not(blk(size))), lower))
        size *= 2
    in_base = jnp.logical_and(blk(base), lower)
    gb = [gb_ref[b] for b in range(n_par)]
    g_col = [jnp.dot(tril.astype(F32), gb[b], preferred_element_type=F32, precision=HIGHEST)
             for b in range(n_par)]
    g_end = [jnp.dot(same.astype(F32), gb[b], preferred_element_type=F32, precision=HIGHEST)
             for b in range(n_par)]
    g_row = [g_col[b].T for b in range(n_par)]
    go = go_ref[...]
    chain = [(b, hd) for b in range(n_par) for hd in range(DN_HEADS)]
    heads = range(len(chain))
    col = [slice(hd * DN_DK, (hd + 1) * DN_DK) for _, hd in chain]
    seq_of = [b for b, _ in chain]
    head_of = [hd for _, hd in chain]

    def tile(ref, h):
        return ref[seq_of[h], :, col[h]]

    gc = [g_col[seq_of[h]][:, head_of[h]:head_of[h] + 1] for h in heads]
    ge = [g_end[seq_of[h]][:, head_of[h]:head_of[h] + 1] for h in heads]
    decay = [jnp.exp(jnp.where(tril, gc[h] - g_row[seq_of[h]][head_of[h]:head_of[h] + 1, :], -jnp.inf))
             for h in heads]
    beta = [gb[seq_of[h]][:, DN_HEADS + head_of[h]:DN_HEADS + head_of[h] + 1] for h in heads]
    kb = [tile(k_ref, h) * beta[h] for h in heads]
    kbf = [tile(k_ref, h).astype(BF16) for h in heads]
    m_full = [jnp.where(strict, lax.dot_general(kb[h].astype(BF16), kbf[h], NT_DIMS,
                                                preferred_element_type=F32) * decay[h], 0.0) for h in heads]
    p = [-jnp.where(in_base, m_full[h], 0.0) for h in heads]
    t = [eye + p[h] for h in heads]
    size = 2
    while size < base:
        p = [_dot(p[h], p[h]) for h in heads]
        t = [t[h] + _dot(t[h], p[h]) for h in heads]
        size *= 2
    size = base
    for off in levels:
        nblk = rows // (2 * size)
        t4 = [t[h].reshape(nblk, 2, size, rows) for h in heads]
        t_low = [t4[h][:, 1].reshape(rows // 2, rows) for h in heads]
        u = [_dot(t_low[h], jnp.where(off, m_full[h], 0.0)) for h in heads]
        t_low = [(t_low[h] - _dot(u[h], t[h])).reshape(nblk, 1, size, rows) for h in heads]
        t = [jnp.concatenate([t4[h][:, 0:1], t_low[h]], axis=1).reshape(rows, rows) for h in heads]
        size *= 2
    e_g = [jnp.exp(gc[h]) for h in heads]
    sol = [_dot(t[h], jnp.concatenate([tile(v_ref, h) * beta[h], kb[h] * e_g[h]], axis=1)) for h in heads]
    attn = [lax.dot_general(tile(q_ref, h).astype(BF16), kbf[h], NT_DIMS,
                            preferred_element_type=F32) * decay[h] for h in heads]
    qd = [tile(q_ref, h) * e_g[h] for h in heads]
    kt = [tile(k_ref, h) * jnp.exp(ge[h] - gc[h]) for h in heads]
    st = [state_ref[seq_of[h], head_of[h]] for h in heads]
    v_new = [[] for _ in heads]
    o_inter = [[] for _ in heads]
    for c in range(n_sub):
        rs = slice(c * CHUNK, (c + 1) * CHUNK)
        for h in heads:
            r = _dot(jnp.concatenate([sol[h][rs, DN_DV:], qd[h][rs]], axis=0), st[h])
            v_c = sol[h][rs, :DN_DV] - r[:CHUNK]
            v_new[h].append(v_c)
            o_inter[h].append(r[CHUNK:])
            g_tot = jnp.exp(ge[h][c * CHUNK:c * CHUNK + 1, :])
            st[h] = st[h] * g_tot + lax.dot_general(kt[h][rs].astype(BF16), v_c.astype(BF16), TN_DIMS,
                                                    preferred_element_type=F32)
    for h in heads:
        state_ref[seq_of[h], head_of[h]] = st[h]
        o = jnp.concatenate(o_inter[h], axis=0) + _dot(attn[h], jnp.concatenate(v_new[h], axis=0))
        zh = tile(z_ref, h)
        o_ref[seq_of[h], :, col[h]] = (_rms_norm(o, go) * _silu(zh)).astype(BF16)

    @pl.when(n == n_steps - 1)
    def _():
        sout_ref[...] = state_ref[...]


def _delta_rule(q, k, v, z, gb, s0, g_o, *, seq, rows, n_par):
    m = q.shape[0]
    nb = m // seq
    rows = min(rows, seq)
    ns = seq // rows
    q, k, v, z, gb = (a.reshape(nb, seq, a.shape[-1]) for a in (q, k, v, z, gb))
    row = lambda b, n: (b, n, 0)
    st_spec = pl.BlockSpec((n_par, DN_HEADS, DN_DK, DN_DV), lambda b, n: (b, 0, 0, 0))
    og, s_new = pl.pallas_call(
        functools.partial(_delta_kernel, n_steps=ns),
        grid=(nb // n_par, ns),
        in_specs=[
            pl.BlockSpec((n_par, rows, DN_KEY), row),
            pl.BlockSpec((n_par, rows, DN_KEY), row),
            pl.BlockSpec((n_par, rows, DN_VAL), row),
            pl.BlockSpec((n_par, rows, DN_VAL), row),
            pl.BlockSpec((n_par, rows, LANES), row),
            st_spec,
            pl.BlockSpec((1, DN_DV), lambda b, n: (0, 0)),
        ],
        out_specs=[pl.BlockSpec((n_par, rows, DN_VAL), row), st_spec],
        out_shape=[
            jax.ShapeDtypeStruct((nb, seq, DN_VAL), BF16),
            jax.ShapeDtypeStruct((nb, DN_HEADS, DN_DK, DN_DV), F32),
        ],
        scratch_shapes=[pltpu.VMEM((n_par, DN_HEADS, DN_DK, DN_DV), F32)],
        compiler_params=_params("arbitrary", "arbitrary"),
        name="gated_delta",
    )(q, k, v, z, gb, s0, g_o)
    return og.reshape(m, DN_VAL), s_new


def _rope_table(pos):
    half = ROPE_DIM // 2
    inv = ROPE_THETA ** (-jnp.arange(half, dtype=F32) / half)
    ang = pos.astype(F32)[:, None] * inv[None, :]
    cs = jnp.concatenate([jnp.cos(ang), jnp.sin(ang)], -1)
    place = jnp.zeros((2 * half, 3 * LANES), F32)
    j = jnp.arange(half)
    place = place.at[j, NOPE_DIM + j].set(1.0).at[j, NOPE_DIM + half + j].set(1.0)
    place = place.at[half + j, LANES + NOPE_DIM + half + j].set(1.0)
    place = place.at[half + j, 2 * LANES + NOPE_DIM + j].set(-1.0)
    ones = (jnp.arange(3 * LANES) < NOPE_DIM).astype(F32)
    return jnp.dot(cs, place, precision=HIGHEST) + ones[None, :]


def _pad_heads(w, per_head, keep):
    kdim = w.shape[0]
    w = w.reshape(kdim, MLA_HEADS, per_head)[:, :, :keep]
    w = jnp.pad(w, ((0, 0), (0, 0), (0, HEAD_PAD - keep)))
    return w.reshape(kdim, MLA_HEADS * HEAD_PAD)


def _prep_even(w_in, w_uq, w_ukv):
    c1 = Q_RANK + KV_RANK
    w_kr = jnp.pad(w_in[:, c1:c1 + ROPE_DIM], ((0, 0), (NOPE_DIM, HEAD_PAD - NOPE_DIM - ROPE_DIM)))
    w1 = jnp.concatenate([w_in[:, :c1], w_kr], -1).astype(BF16)
    w2 = w_in[:, c1 + ROPE_DIM:].astype(BF16)
    wuq = _pad_heads(w_uq, NOPE_DIM + ROPE_DIM, NOPE_DIM + ROPE_DIM).astype(BF16)
    wuk = _pad_heads(w_ukv, NOPE_DIM + V_DIM, NOPE_DIM).astype(BF16)
    wuv = w_ukv.reshape(KV_RANK, MLA_HEADS, NOPE_DIM + V_DIM)[:, :, NOPE_DIM:].reshape(KV_RANK, MLA_WIDTH)
    return w1, w2, wuq, wuk, wuv.T.astype(BF16)


def _prep_odd(w_in, a_log, dt_bias):
    w_qkv = w_in[:, :DN_QKV].reshape(D_MODEL, 3, DN_KEY).transpose(1, 0, 2).astype(BF16)
    w_z = w_in[:, DN_QKV:DN_QKV + DN_VAL].astype(BF16)
    w_ab = jnp.pad(w_in[:, DN_QKV + DN_VAL:], ((0, 0), (0, LANES - 2 * DN_HEADS))).astype(BF16)
    ab_const = jnp.zeros((SUBLANES, LANES), F32)
    ab_const = ab_const.at[0, :DN_HEADS].set(a_log.astype(F32)).at[1, :DN_HEADS].set(dt_bias.astype(F32))
    return w_qkv, w_z, w_ab, ab_const


def _prep_ffn(w_in, conv_w, w_out):
    return w_in.astype(BF16), w_out.astype(BF16), conv_w


def _pad_rows8(buf):
    return jnp.pad(buf.astype(F32), ((0, 0), (SUBLANES - buf.shape[1], 0), (0, 0)))


def _trunk(x, past_lat, past_kr, sc_buf, dconv_buf, delta_s, ff_buf, w, *, tm, tq, tk, dr):
    (even_w, odd_w, ffn_w, w_o_e, w_o_o, g_qnorm, g_kvnorm, sc_w, dconv_w, g_onorm,
     ln_mix_g, ln_mix_b, ln_ff_g, ln_ff_b) = w
    nb, seq, _ = x.shape
    past = 0 if past_lat is None else past_lat.shape[2]
    m = nb * seq
    x = x.reshape(m, D_MODEL)
    tab = _rope_table(past + jnp.arange(seq, dtype=jnp.int32))
    t_valid = past + seq
    tq = min(tq, seq)
    tk = min(tk, t_valid)
    t_pad = -(-t_valid // tk) * tk
    lats, krs, scs, dcs, dss, ffs = [], [], [], [], [], []
    for i in range(DEPTH):
        row = lambda a: a[i].reshape(1, -1)
        if i % 2 == 0:
            e = i // 2
            w1, w2, wuq, wuk, wuv = even_w[e]
            q, lat, kr_pad, o_sc, sc_new, k, v = _even_in(
                x, w1, w2, g_qnorm[e].reshape(1, -1), g_kvnorm[e].reshape(1, -1), wuq, sc_w[e], tab,
                _pad_rows8(sc_buf[e]), wuk, wuv, seq=seq, tm=tm)
            if past or t_pad != t_valid or tk != min(tm, seq):
                lat_all = lat.reshape(nb, seq, KV_RANK)
                kr_all = kr_pad.reshape(nb, seq, LANES)
                if past:
                    kr_past = jnp.pad(past_kr[e].astype(F32),
                                      ((0, 0), (0, 0), (NOPE_DIM, HEAD_PAD - NOPE_DIM - ROPE_DIM)))
                    lat_all = jnp.concatenate([past_lat[e].astype(F32), lat_all], 1)
                    kr_all = jnp.concatenate([kr_past, kr_all], 1)
                if t_pad != t_valid:
                    lat_all = jnp.pad(lat_all, ((0, 0), (0, t_pad - t_valid), (0, 0)))
                    kr_all = jnp.pad(kr_all, ((0, 0), (0, t_pad - t_valid), (0, 0)))
                k, v = _kv_up(lat_all.reshape(nb * t_pad, KV_RANK), kr_all.reshape(nb * t_pad, LANES),
                              wuk, wuv, tm=tk)
            o_att = _attention(q, k, v, nb=nb, sq=seq, t_pad=t_pad, t_valid=t_valid, past=past, tq=tq, tk=tk)
            mix, mix_w = [o_att, o_sc], [w_o_e[e][:MLA_WIDTH], w_o_e[e][MLA_WIDTH:]]
            lats.append(lat.reshape(nb, seq, KV_RANK))
            krs.append(kr_pad.reshape(nb, seq, LANES)[:, :, NOPE_DIM:NOPE_DIM + ROPE_DIM])
            scs.append(sc_new[:, SUBLANES - (SC_K - 1):])
        else:
            o = i // 2
            w_qkv, w_z, w_ab, ab_const = odd_w[o]
            q, k, v, z, gb, dc_new = _odd_in(x, w_qkv, w_z, w_ab, dconv_w[o], ab_const,
                                             _pad_rows8(dconv_buf[o]), seq=seq, tm=tm)
            og, s_new = _delta_rule(q, k, v, z, gb, delta_s[o].astype(F32), g_onorm[o].reshape(1, -1), seq=seq,
                                    rows=dr, n_par=DELTA_SEQS_PER_STEP)
            mix, mix_w = [og], [w_o_o[o]]
            dcs.append(dc_new[:, SUBLANES - (DN_CONV - 1):])
            dss.append(s_new)
        w_ff_in, w_ff_out, cw = ffn_w[i]
        x, ff_new = _mix_ffn(mix, mix_w, x, row(ln_mix_g), row(ln_mix_b), _pad_rows8(ff_buf[i]),
                             w_ff_in, w_ff_out, cw, row(ln_ff_g), row(ln_ff_b), seq=seq, tm=tm)
        ffs.append(ff_new[:, SUBLANES - (FF_K - 1):])
    return (x.reshape(nb, seq, D_MODEL), jnp.stack(lats), jnp.stack(krs), jnp.stack(scs), jnp.stack(dcs),
            jnp.stack(dss), jnp.stack(ffs))


def _prepare(w_in_e, w_uq, w_ukv, w_o_e, w_in_o, a_log, dt_bias, w_o_o, w_ff_in, ffconv_w, w_ff_out):
    even_w = [_prep_even(w_in_e[e], w_uq[e], w_ukv[e]) for e in range(N_EVEN)]
    odd_w = [_prep_odd(w_in_o[o], a_log[o], dt_bias[o]) for o in range(N_ODD)]
    ffn_w = [_prep_ffn(w_ff_in[i], ffconv_w[i], w_ff_out[i]) for i in range(DEPTH)]
    return even_w, odd_w, ffn_w, w_o_e.astype(BF16), w_o_o.astype(BF16)


def kernel(x_prompt, x_sample, cache_mla_latent, cache_mla_krope, state_sconv, state_dconv, state_delta, state_ffconv, w_in_e, g_qnorm, g_kvnorm, w_uq, w_ukv, sc_w, w_o_e, w_in_o, dconv_w, a_log, dt_bias, g_onorm, w_o_o, w_ff_in, ffconv_w, w_ff_out, ln_mix_g, ln_mix_b, ln_ff_g, ln_ff_b):
    even_w, odd_w, ffn_w, w_o_e_b, w_o_o_b = _prepare(
        w_in_e, w_uq, w_ukv, w_o_e, w_in_o, a_log, dt_bias, w_o_o, w_ff_in, ffconv_w, w_ff_out)
    w = (even_w, odd_w, ffn_w, w_o_e_b, w_o_o_b, g_qnorm, g_kvnorm, sc_w, dconv_w, g_onorm,
         ln_mix_g, ln_mix_b, ln_ff_g, ln_ff_b)
    bp = x_prompt.shape[0]
    f32 = x_prompt.dtype
    y_p, p_lat, p_kr, p_sc, p_dc, p_ds, p_ff = _trunk(
        x_prompt, None, None,
        jnp.zeros((N_EVEN, bp, SC_K - 1, SC_WIDTH), f32),
        jnp.zeros((N_ODD, bp, DN_CONV - 1, DN_QKV), f32),
        jnp.zeros((N_ODD, bp, DN_HEADS, DN_DK, DN_DV), f32),
        jnp.zeros((DEPTH, bp, FF_K - 1, D_FF), f32),
        w, tm=512, tq=512, tk=512, dr=256)
    y_s, s_lat, s_kr, s_sc, s_dc, s_ds, s_ff = _trunk(
        x_sample, cache_mla_latent, cache_mla_krope, state_sconv, state_dconv, state_delta, state_ffconv,
        w, tm=512, tq=64, tk=768, dr=256)
    return (y_p, y_s, p_lat, p_kr, p_sc, p_dc, p_ds, p_ff, s_lat, s_kr, s_sc, s_dc, s_ds, s_ff)
```

```python
import functools
import math

import jax
import jax.numpy as jnp
from jax import lax
from jax.experimental import pallas as pl
from jax.experimental.pallas import tpu as pltpu

D_MODEL = 1024
DEPTH = 4
CHUNK = 64
N_EVEN = (DEPTH + 1) // 2
N_ODD = DEPTH // 2
MLA_HEADS = 8
Q_RANK = 384
KV_RANK = 256
NOPE_DIM = 64
ROPE_DIM = 32
V_DIM = 64
ROPE_THETA = 10000.0
MLA_WIDTH = MLA_HEADS * V_DIM
SC_WIDTH = 512
SC_K = 3
DN_HEADS = 8
DN_DK = 128
DN_DV = 128
DN_CONV = 4
DN_KEY = DN_HEADS * DN_DK
DN_VAL = DN_HEADS * DN_DV
DN_QKV = 2 * DN_KEY + DN_VAL
D_FF = 2816
FF_K = 3
ALPHA = (2 * DEPTH) ** 0.25
NORM_EPS = 1e-6
NEG_INF = -1e30
LOG2_E = math.log2(math.e)

F32 = jnp.float32
BF16 = jnp.bfloat16
HIGHEST = lax.Precision.HIGHEST

LANES = 128
SUBLANES = 8
BF16_SUBLANES = 16
HEAD_PAD = 128
FF_CHUNK = 256
DELTA_BASE_BLOCK = 8
DELTA_SEQS_PER_STEP = 4
ATTN_HEAD_GROUP = 8
FAST_HEAD_GROUP = 8
FAST_SOFTMAX_MAX_RISE = 60.0
VMEM_LIMIT = 56 * 1024 * 1024

NT_DIMS = (((1,), (1,)), ((), ()))
TN_DIMS = (((0,), (0,)), ((), ()))


def _params(*sem):
    return pltpu.CompilerParams(dimension_semantics=sem, vmem_limit_bytes=VMEM_LIMIT)


def _dot(a, b):
    return jnp.dot(a.astype(BF16), b.astype(BF16), preferred_element_type=F32)


def _layer_norm(r, g, b):
    mu = jnp.mean(r, axis=-1, keepdims=True)
    d = r - mu
    var = jnp.mean(d * d, axis=-1, keepdims=True)
    return d * lax.rsqrt(var + NORM_EPS) * g + b


def _rms_norm(x, g):
    return x * lax.rsqrt(jnp.mean(x * x, axis=-1, keepdims=True) + NORM_EPS) * g


def _silu(x):
    h = 0.5 * x
    return h + h * jnp.tanh(h)


def _softplus(x):
    return jnp.maximum(x, 0.0) + jnp.log1p(jnp.exp(-jnp.abs(x)))


def _causal_conv(x, prev, w):
    tm, c = x.shape
    nseq = prev.shape[0]
    k_taps = w.shape[0]
    groups = tm // nseq // SUBLANES
    x4 = x.reshape(nseq, groups, SUBLANES, c)
    prev4 = prev.reshape(nseq, 1, SUBLANES, c)
    sub = lax.broadcasted_iota(jnp.int32, x4.shape, 2)
    y = x4 * w[k_taps - 1:k_taps, :]
    for k in range(1, k_taps):
        rot = pltpu.roll(x4, k, axis=2)
        rot_prev = jnp.concatenate([pltpu.roll(prev4, k, axis=2), rot[:, :groups - 1]], axis=1)
        y = y + jnp.where(sub < k, rot_prev, rot) * w[k_taps - 1 - k:k_taps - k, :]
    return y.reshape(tm, c)


def _run_tails(x, nseq):
    tm, c = x.shape
    run = tm // nseq
    return x.reshape(nseq, run, c)[:, run - SUBLANES:, :]


def _row_tiling(m, seq, tm):
    tm = min(tm, m)
    if seq >= tm:
        return tm, seq // tm, 1
    return tm, 1, tm // seq


def _mix_ffn_kernel(*refs, n_mix, tiles_per_seq):
    ys, wos = refs[:n_mix], refs[n_mix:2 * n_mix]
    (x_ref, g1_ref, b1_ref, w_in_ref, w_out_ref, cw_ref, buf_ref, g2_ref, b2_ref,
     out_ref, nbuf_ref, act_ref, x1_ref, xb_ref, win_ref) = refs[2 * n_mix:]
    i = pl.program_id(0)
    nseq = win_ref.shape[0]

    tm = x_ref.shape[0]
    halves = [slice(0, tm // 2), slice(tm // 2, tm)]
    seq_half = [slice(0, nseq // 2), slice(nseq // 2, nseq)]

    for rows in halves:
        mix = jnp.dot(ys[0][rows, :], wos[0][...], preferred_element_type=F32)
        for j in range(1, n_mix):
            mix = mix + jnp.dot(ys[j][rows, :], wos[j][...], preferred_element_type=F32)
        x1 = _layer_norm(ALPHA * x_ref[rows, :] + mix, g1_ref[...], b1_ref[...])
        x1_ref[rows, :] = x1
        xb_ref[rows, :] = x1.astype(BF16)

    @pl.when(i % tiles_per_seq == 0)
    def _():
        win_ref[...] = buf_ref[...]

    for c in range(D_FF // FF_CHUNK):
        cols = slice(c * FF_CHUNK, (c + 1) * FF_CHUNK)
        up_cols = slice(D_FF + c * FF_CHUNK, D_FF + (c + 1) * FF_CHUNK)
        prev = win_ref[:, :, cols]
        tails = []
        for hf, rows in enumerate(halves):
            xb = xb_ref[rows, :]
            gate = jnp.dot(xb, w_in_ref[:, cols], preferred_element_type=F32)
            up = jnp.dot(xb, w_in_ref[:, up_cols], preferred_element_type=F32)
            if nseq == 1:
                y = _causal_conv(gate, prev if hf == 0 else tails[0], cw_ref[:, cols])
                tails.append(_run_tails(gate, 1))
            else:
                y = _causal_conv(gate, prev[seq_half[hf]], cw_ref[:, cols])
                tails.append(_run_tails(gate, nseq // 2))
            act_ref[rows, cols] = (_silu(y) * up).astype(BF16)
        tails = tails[1] if nseq == 1 else jnp.concatenate(tails, axis=0)
        win_ref[:, :, cols] = tails
        nbuf_ref[:, :, cols] = tails

    for rows in halves:
        acc = jnp.dot(act_ref[rows, :], w_out_ref[...], preferred_element_type=F32)
        out_ref[rows, :] = _layer_norm(ALPHA * x1_ref[rows, :] + acc, g2_ref[...], b2_ref[...])


def _mix_ffn(ys, wos, x, mix_g, mix_b, buf8, w_in, w_out, conv_w, ff_g, ff_b, *, seq, tm):
    m = x.shape[0]
    nb = m // seq
    tm, tps, nseq = _row_tiling(m, seq, tm)
    const2 = lambda i: (0, 0)
    row = lambda i: (i, 0)
    resident = lambda a: pl.BlockSpec(a.shape, const2, pipeline_mode=pl.Buffered(1))
    vec = pl.BlockSpec((1, D_MODEL), const2)
    state = pl.BlockSpec((nseq, SUBLANES, D_FF), lambda i: (i // tps, 0, 0))
    return pl.pallas_call(
        functools.partial(_mix_ffn_kernel, n_mix=len(ys), tiles_per_seq=tps),
        grid=(m // tm,),
        in_specs=([pl.BlockSpec((tm, y.shape[1]), row) for y in ys]
                  + [resident(w) for w in wos]
                  + [pl.BlockSpec((tm, D_MODEL), row), vec, vec,
                     resident(w_in), resident(w_out), pl.BlockSpec(conv_w.shape, const2), state, vec, vec]),
        out_specs=[pl.BlockSpec((tm, D_MODEL), row), state],
        out_shape=[
            jax.ShapeDtypeStruct((m, D_MODEL), F32),
            jax.ShapeDtypeStruct((nb, SUBLANES, D_FF), F32),
        ],
        scratch_shapes=[
            pltpu.VMEM((tm, D_FF), BF16),
            pltpu.VMEM((tm, D_MODEL), F32),
            pltpu.VMEM((tm, D_MODEL), BF16),
            pltpu.VMEM((nseq, SUBLANES, D_FF), F32),
        ],
        compiler_params=_params("arbitrary"),
        name="mix_ffn",
    )(*ys, *wos, x, mix_g, mix_b, w_in, w_out, conv_w, buf8, ff_g, ff_b)


def _rope(x, tab):
    return (x * tab[:, 0:LANES]
            + pltpu.roll(x, ROPE_DIM // 2, axis=1) * tab[:, LANES:2 * LANES]
            + pltpu.roll(x, LANES - ROPE_DIM // 2, axis=1) * tab[:, 2 * LANES:3 * LANES])


def _keys_values(lat, kr_pad, wuk_ref, wuvt_ref, k_ref, vt_ref):
    lb = lat.astype(BF16)
    kn = jnp.dot(lb, wuk_ref[...], preferred_element_type=F32)
    for h in range(MLA_HEADS):
        sl = slice(h * HEAD_PAD, (h + 1) * HEAD_PAD)
        k_ref[:, sl] = (kn[:, sl] + kr_pad).astype(BF16)
    vt_ref[...] = lax.dot_general(wuvt_ref[...], lb, NT_DIMS, preferred_element_type=F32).astype(BF16)


def _even_in_kernel(x_ref, w1_ref, w2_ref, gq_ref, gkv_ref, wuq_ref, scw_ref, tab_ref, buf_ref, wuk_ref, wuvt_ref,
                    q_ref, lat_ref, kr_ref, osc_ref, nbuf_ref, k_ref, vt_ref, win_ref, *, tiles_per_seq):
    i = pl.program_id(0)
    nseq = win_ref.shape[0]

    @pl.when(i % tiles_per_seq == 0)
    def _():
        win_ref[...] = buf_ref[...]

    xb = x_ref[...].astype(BF16)
    tab = tab_ref[...]
    h1 = jnp.dot(xb, w1_ref[...], preferred_element_type=F32)
    cq = h1[:, :Q_RANK]
    ckv = h1[:, Q_RANK:Q_RANK + KV_RANK]
    kr_raw = h1[:, Q_RANK + KV_RANK:]
    lat = _rms_norm(ckv, gkv_ref[...])
    kr_pad = _rope(kr_raw, tab)
    lat_ref[...] = lat
    kr_ref[...] = kr_pad
    _keys_values(lat, kr_pad, wuk_ref, wuvt_ref, k_ref, vt_ref)
    q_raw =_dot(_rms_norm(cq, gq_ref[...]), wuq_ref[...])
    scale = (NOPE_DIM + ROPE_DIM) ** -0.5 * LOG2_E
    for h in range(MLA_HEADS):
        sl = slice(h * HEAD_PAD, (h + 1) * HEAD_PAD)
        q_ref[:, sl] = (_rope(q_raw[:, sl], tab) * scale).astype(BF16)

    h2 = jnp.dot(xb, w2_ref[...], preferred_element_type=F32)
    gate_b = h2[:, :SC_WIDTH]
    u_in = h2[:, SC_WIDTH:2 * SC_WIDTH] * h2[:, 2 * SC_WIDTH:]
    u = _causal_conv(u_in, win_ref[...], scw_ref[...])
    tails = _run_tails(u_in, nseq)
    win_ref[...] = tails
    nbuf_ref[...] = tails
    osc_ref[...] = (gate_b * u).astype(BF16)


def _even_in(x, w1, w2, g_q, g_kv, w_uq, sc_w, tab, buf8, w_uk, w_uv_t, *, seq, tm):
    m = x.shape[0]
    nb = m // seq
    tm, tps, nseq = _row_tiling(m, seq, tm)
    if nseq > 1:
        tab = jnp.tile(tab, (nseq, 1))
    const2 = lambda i: (0, 0)
    return pl.pallas_call(
        functools.partial(_even_in_kernel, tiles_per_seq=tps),
        grid=(m // tm,),
        in_specs=[
            pl.BlockSpec((tm, D_MODEL), lambda i: (i, 0)),
            pl.BlockSpec(w1.shape, const2),
            pl.BlockSpec(w2.shape, const2),
            pl.BlockSpec(g_q.shape, const2),
            pl.BlockSpec(g_kv.shape, const2),
            pl.BlockSpec(w_uq.shape, const2),
            pl.BlockSpec(sc_w.shape, const2),
            pl.BlockSpec((tm, 3 * LANES), lambda i: (i % tps, 0)),
            pl.BlockSpec((nseq, SUBLANES, SC_WIDTH), lambda i: (i // tps, 0, 0)),
            pl.BlockSpec(w_uk.shape, const2),
            pl.BlockSpec(w_uv_t.shape, const2),
        ],
        out_specs=[
            pl.BlockSpec((tm, MLA_HEADS * HEAD_PAD), lambda i: (i, 0)),
            pl.BlockSpec((tm, KV_RANK), lambda i: (i, 0)),
            pl.BlockSpec((tm, LANES), lambda i: (i, 0)),
            pl.BlockSpec((tm, SC_WIDTH), lambda i: (i, 0)),
            pl.BlockSpec((nseq, SUBLANES, SC_WIDTH), lambda i: (i // tps, 0, 0)),
            pl.BlockSpec((tm, MLA_HEADS * HEAD_PAD), lambda i: (i, 0)),
            pl.BlockSpec((MLA_WIDTH, tm), lambda i: (i, 0)),
        ],
        out_shape=[
            jax.ShapeDtypeStruct((m, MLA_HEADS * HEAD_PAD), BF16),
            jax.ShapeDtypeStruct((m, KV_RANK), F32),
            jax.ShapeDtypeStruct((m, LANES), F32),
            jax.ShapeDtypeStruct((m, SC_WIDTH), BF16),
            jax.ShapeDtypeStruct((nb, SUBLANES, SC_WIDTH), F32),
            jax.ShapeDtypeStruct((m, MLA_HEADS * HEAD_PAD), BF16),
            jax.ShapeDtypeStruct((m // tm * MLA_WIDTH, tm), BF16),
        ],
        scratch_shapes=[pltpu.VMEM((nseq, SUBLANES, SC_WIDTH), F32)],
        compiler_params=_params("arbitrary"),
        name="even_in",
    )(x, w1, w2, g_q, g_kv, w_uq, sc_w, tab, buf8, w_uk, w_uv_t)


def _kv_up_kernel(lat_ref, kr_ref, wuk_ref, wuvt_ref, k_ref, vt_ref):
    _keys_values(lat_ref[...], kr_ref[...], wuk_ref, wuvt_ref, k_ref, vt_ref)


def _kv_up(lat, kr_pad, w_uk, w_uv_t, *, tm):
    m = lat.shape[0]
    const2 = lambda i: (0, 0)
    return pl.pallas_call(
        _kv_up_kernel,
        grid=(m // tm,),
        in_specs=[
            pl.BlockSpec((tm, KV_RANK), lambda i: (i, 0)),
            pl.BlockSpec((tm, LANES), lambda i: (i, 0)),
            pl.BlockSpec(w_uk.shape, const2),
            pl.BlockSpec(w_uv_t.shape, const2),
        ],
        out_specs=[
            pl.BlockSpec((tm, MLA_HEADS * HEAD_PAD), lambda i: (i, 0)),
            pl.BlockSpec((MLA_WIDTH, tm), lambda i: (i, 0)),
        ],
        out_shape=[
            jax.ShapeDtypeStruct((m, MLA_HEADS * HEAD_PAD), BF16),
            jax.ShapeDtypeStruct((m // tm * MLA_WIDTH, tm), BF16),
        ],
        compiler_params=_params("arbitrary"),
        name="kv_up",
    )(lat, kr_pad, w_uk, w_uv_t)


def _attn_kernel(q_ref, k_ref, vt_ref, o_ref, m_ref, l_ref, acc_ref, *, tq, tk, past, t_valid, nk):
    qi = pl.program_id(1)
    kj = pl.program_id(2)
    q_lo = past + qi * tq
    chunk_lo = q_lo // CHUNK
    chunk_hi = (q_lo + tq - 1) // CHUNK
    last = jnp.minimum(((chunk_hi + 1) * CHUNK - 1) // tk, nk - 1)
    k_end = (kj + 1) * tk
    full = jnp.logical_and(k_end <= (chunk_lo + 1) * CHUNK, k_end <= t_valid)
    heads = list(range(MLA_HEADS))
    hsl =[slice(h * HEAD_PAD, (h + 1) * HEAD_PAD) for h in heads]
    vsl = [slice(h * V_DIM, (h + 1) * V_DIM) for h in heads]

    first = kj == 0

    @pl.when(first)
    def _():
        m_ref[...] = jnp.zeros_like(m_ref)
        l_ref[...] = jnp.zeros_like(l_ref)
        acc_ref[...] = jnp.zeros_like(acc_ref)

    def visibility():
        kpos = kj * tk + lax.broadcasted_iota(jnp.int32, (tk, tq), 0)
        qpos = q_lo + lax.broadcasted_iota(jnp.int32, (tk, tq), 1)
        return jnp.logical_and(kpos // CHUNK <= qpos // CHUNK, kpos < t_valid)

    def scores(h, visible):
        s = lax.dot_general(k_ref[:, hsl[h]], q_ref[:, hsl[h]], NT_DIMS, preferred_element_type=F32)
        return s if visible is None else jnp.where(visible, s, NEG_INF)

    def step(masked):
        visible = visibility() if masked else None
        ones = jnp.ones((BF16_SUBLANES, tk), BF16)
        for g0 in range(0, MLA_HEADS, ATTN_HEAD_GROUP):
            grp = range(g0, g0 + ATTN_HEAD_GROUP)
            s = {h: scores(h, visible) for h in grp}
            m_prev = {h: m_ref[h:h + 1, :] for h in grp}
            m_new = {h: jnp.maximum(m_prev[h], jnp.max(s[h], axis=0, keepdims=True)) for h in grp}
            alpha = {h: jnp.exp2(m_prev[h] - m_new[h]) for h in grp}
            p = {h: jnp.exp2(s[h] - m_new[h]).astype(BF16) for h in grp}
            pv = {h: jnp.dot(jnp.concatenate([vt_ref[vsl[h], :], ones], axis=0), p[h],
                             preferred_element_type=F32) for h in grp}
            for h in grp:
                l_ref[h:h + 1, :] = alpha[h] * l_ref[h:h + 1, :] + pv[h][V_DIM:V_DIM + 1, :]
                m_ref[h:h + 1, :] = m_new[h]
                acc_ref[vsl[h], :] = alpha[h] * acc_ref[vsl[h], :] + pv[h][:V_DIM, :]

    def fast_step(masked):
        visible = visibility() if masked else None
        ones = jnp.ones((BF16_SUBLANES, tk), BF16)
        pv, t_max = {}, {}
        for g0 in range(0, MLA_HEADS, FAST_HEAD_GROUP):
            grp = range(g0, g0 + FAST_HEAD_GROUP)
            s = {h: scores(h, visible) for h in grp}
            p = {h: jnp.exp2(s[h] - m_ref[h:h + 1, :]).astype(BF16) for h in grp}
            for h in grp:
                t_max[h] = jnp.max(s[h], axis=0, keepdims=True)
            for h in grp:
                pv[h] = jnp.dot(jnp.concatenate([vt_ref[vsl[h], :], ones], axis=0), p[h],
                                preferred_element_type=F32)
        rise = t_max[0] - m_ref[0:1, :]
        fall = rise
        for h in heads[1:]:
            d = t_max[h] - m_ref[h:h + 1, :]
            rise = jnp.maximum(rise, d)
            fall = jnp.minimum(fall, d)
        hi = jnp.max(rise, axis=1, keepdims=True)
        lo = jnp.min(fall, axis=1, keepdims=True)
        ok = jnp.logical_and(hi <= FAST_SOFTMAX_MAX_RISE,
                             jnp.logical_or(jnp.logical_not(first), lo >= -FAST_SOFTMAX_MAX_RISE))
        for h in heads:
            m_prev = m_ref[h:h + 1, :]
            m_new = jnp.where(first, t_max[h], jnp.maximum(m_prev, t_max[h]))
            alpha = jnp.exp2(m_prev - m_new)
            l_prev = l_ref[h:h + 1, :]
            acc_prev = acc_ref[vsl[h], :]
            l_ref[h:h + 1, :] = jnp.where(ok, (l_prev + pv[h][V_DIM:V_DIM + 1, :]) * alpha, l_prev)
            acc_ref[vsl[h], :] = jnp.where(ok, (acc_prev + pv[h][:V_DIM, :]) * alpha, acc_prev)
            m_ref[h:h + 1, :] = jnp.where(ok, m_new, m_prev)
        in_range = jnp.logical_and(jnp.max(hi) <= FAST_SOFTMAX_MAX_RISE,
                                   jnp.logical_or(jnp.logical_not(first),
                                                  jnp.min(lo) >= -FAST_SOFTMAX_MAX_RISE))

        @pl.when(jnp.logical_not(in_range))
        def _():
            @pl.when(first)
            def _():
                m_ref[...] = jnp.full_like(m_ref, NEG_INF)
            step(masked)

    live = kj <= last

    @pl.when(jnp.logical_and(live, full))
    def _():
        fast_step(False)

    @pl.when(jnp.logical_and(live, jnp.logical_not(full)))
    def _():
        fast_step(True)

    @pl.when(kj == last)
    def _():
        for hp in range(MLA_HEADS // 2):
            pair = jnp.concatenate(
                [acc_ref[vsl[h], :] * (1.0 / l_ref[h:h + 1, :]) for h in (2 * hp, 2 * hp + 1)], axis=0)
            o_ref[:, hp * 2 * V_DIM:(hp + 1) * 2 * V_DIM] = pair.T.astype(BF16)


def _attention(q, k, vt, *, nb, sq, t_pad, t_valid, past, tq, tk):
    nq = sq // tq
    nk = t_pad // tk

    def last_tile(qi):
        chunk_hi = (past + qi * tq + tq - 1) // CHUNK
        return jnp.minimum(((chunk_hi + 1) * CHUNK - 1) // tk, nk - 1)

    return pl.pallas_call(
        functools.partial(_attn_kernel, tq=tq, tk=tk, past=past, t_valid=t_valid, nk=nk),
        grid=(nb, nq, nk),
        in_specs=[
            pl.BlockSpec((tq, MLA_HEADS * HEAD_PAD), lambda b, qi, kj: (b * nq + qi, 0)),
            pl.BlockSpec((tk, MLA_HEADS * HEAD_PAD),
                         lambda b, qi, kj: (b * nk + jnp.minimum(kj, last_tile(qi)), 0)),
            pl.BlockSpec((MLA_WIDTH, tk), lambda b, qi, kj: (b * nk + jnp.minimum(kj, last_tile(qi)), 0)),
        ],
        out_specs=pl.BlockSpec((tq, MLA_WIDTH), lambda b, qi, kj: (b * nq + qi, 0)),
        out_shape=jax.ShapeDtypeStruct((nb * sq, MLA_WIDTH), BF16),
        scratch_shapes=[
            pltpu.VMEM((MLA_HEADS, tq), F32),
            pltpu.VMEM((MLA_HEADS, tq), F32),
            pltpu.VMEM((MLA_WIDTH, tq), F32),
        ],
        compiler_params=_params("arbitrary", "arbitrary", "arbitrary"),
        name="chunk_attention",
    )(q, k, vt)


def _odd_in_kernel(x_ref, wqkv_ref, wz_ref, wab_ref, dcw_ref, ab_ref, buf_ref,
                   q_ref, k_ref, v_ref, z_ref, gb_ref, nbuf_ref, win_ref, *, tiles_per_seq):
    i = pl.program_id(0)
    nseq = win_ref.shape[0]

    @pl.when(i % tiles_per_seq == 0)
    def _():
        win_ref[...] = buf_ref[...]

    xb = x_ref[...].astype(BF16)
    outs = (q_ref, k_ref, v_ref)
    for part in range(3):
        cols = slice(part * DN_KEY, (part + 1) * DN_KEY)
        pre = jnp.dot(xb, wqkv_ref[part], preferred_element_type=F32)
        y = _causal_conv(pre, win_ref[:, :, cols], dcw_ref[:, cols])
        tails = _run_tails(pre, nseq)
        win_ref[:, :, cols] = tails
        nbuf_ref[:, :, cols] = tails
        y = _silu(y)
        if part == 2:
            outs[part][...] = y
        else:
            post = DN_DK ** -0.5 if part == 0 else 1.0
            for h in range(DN_HEADS):
                sl = slice(h * DN_DK, (h + 1) * DN_DK)
                seg = y[:, sl]
                nrm = lax.rsqrt(jnp.sum(seg * seg, axis=-1, keepdims=True) + NORM_EPS)
                outs[part][:, sl] = seg * nrm * post if part == 0 else seg * nrm

    z_ref[...] = jnp.dot(xb, wz_ref[...], preferred_element_type=F32)
    ab = jnp.dot(xb, wab_ref[...], preferred_element_type=F32)
    lane = lax.broadcasted_iota(jnp.int32, ab.shape, 1)
    g = -jnp.exp(ab_ref[0:1, :]) * _softplus(ab + ab_ref[1:2, :])
    gb_ref[...] = jnp.where(lane < DN_HEADS, g, jax.nn.sigmoid(ab))


def _odd_in(x, w_qkv, w_z, w_ab, dconv_w, ab_const, buf8, *, seq, tm):
    m = x.shape[0]
    nb = m // seq
    tm, tps, nseq = _row_tiling(m, seq, tm)
    const2 = lambda i: (0, 0)
    row = lambda i: (i, 0)
    return pl.pallas_call(
        functools.partial(_odd_in_kernel, tiles_per_seq=tps),
        grid=(m // tm,),
        in_specs=[
            pl.BlockSpec((tm, D_MODEL), row),
            pl.BlockSpec(w_qkv.shape, lambda i: (0, 0, 0)),
            pl.BlockSpec(w_z.shape, const2),
            pl.BlockSpec(w_ab.shape, const2),
            pl.BlockSpec(dconv_w.shape, const2),
            pl.BlockSpec(ab_const.shape, const2),
            pl.BlockSpec((nseq, SUBLANES, DN_QKV), lambda i: (i // tps, 0, 0)),
        ],
        out_specs=[
            pl.BlockSpec((tm, DN_KEY), row),
            pl.BlockSpec((tm, DN_KEY), row),
            pl.BlockSpec((tm, DN_VAL), row),
            pl.BlockSpec((tm, DN_VAL), row),
            pl.BlockSpec((tm, LANES), row),
            pl.BlockSpec((nseq, SUBLANES, DN_QKV), lambda i: (i // tps, 0, 0)),
        ],
        out_shape=[
            jax.ShapeDtypeStruct((m, DN_KEY), F32),
            jax.ShapeDtypeStruct((m, DN_KEY), F32),
            jax.ShapeDtypeStruct((m, DN_VAL), F32),
            jax.ShapeDtypeStruct((m, DN_VAL), F32),
            jax.ShapeDtypeStruct((m, LANES), F32),
            jax.ShapeDtypeStruct((nb, SUBLANES, DN_QKV), F32),
        ],
        scratch_shapes=[pltpu.VMEM((nseq, SUBLANES, DN_QKV), F32)],
        compiler_params=_params("arbitrary"),
        name="odd_in",
    )(x, w_qkv, w_z, w_ab, dconv_w, ab_const, buf8)


def _delta_kernel(q_ref, k_ref, v_ref, z_ref, gb_ref, s0_ref, go_ref, o_ref, sout_ref, state_ref, *, n_steps):
    n = pl.program_id(1)
    n_par, rows = q_ref.shape[0], q_ref.shape[1]
    n_sub = rows // CHUNK

    @pl.when(n == 0)
    def _():
        state_ref[...] = s0_ref[...]

    ri = lax.broadcasted_iota(jnp.int32, (rows, rows), 0)
    ci = lax.broadcasted_iota(jnp.int32, (rows, rows), 1)

    def blk(size):
        return (ri // size) == (ci // size)

    same = blk(CHUNK)
    lower = ri > ci
    tril = jnp.logical_and(same, ri >= ci)
    strict = jnp.logical_and(same, lower)
    eye = (ri == ci).astype(F32)
    base = DELTA_BASE_BLOCK
    levels = []
    size = base
    while size < CHUNK:
        levels.append(jnp.logical_and(jnp.logical_and(blk(2 * size), jnp.logical_not(blk(size))), lower))
        size *= 2
    in_base = jnp.logical_and(blk(base), lower)
    gb = [gb_ref[b] for b in range(n_par)]
    g_col = [jnp.dot(tril.astype(F32), gb[b], preferred_element_type=F32, precision=HIGHEST)
             for b in range(n_par)]
    g_end = [jnp.dot(same.astype(F32), gb[b], preferred_element_type=F32, precision=HIGHEST)
             for b in range(n_par)]
    g_row = [g_col[b].T for b in range(n_par)]
    go = go_ref[...]
    chain = [(b, hd) for b in range(n_par) for hd in range(DN_HEADS)]
    heads = range(len(chain))
    col = [slice(hd * DN_DK, (hd + 1) * DN_DK) for _, hd in chain]
    seq_of = [b for b, _ in chain]
    head_of = [hd for _, hd in chain]

    def tile(ref, h):
        return ref[seq_of[h], :, col[h]]

    gc = [g_col[seq_of[h]][:, head_of[h]:head_of[h] + 1] for h in heads]
    ge = [g_end[seq_of[h]][:, head_of[h]:head_of[h] + 1] for h in heads]
    decay = [jnp.exp(jnp.where(tril, gc[h] - g_row[seq_of[h]][head_of[h]:head_of[h] + 1, :], -jnp.inf))
             for h in heads]
    beta = [gb[seq_of[h]][:, DN_HEADS + head_of[h]:DN_HEADS + head_of[h] + 1] for h in heads]
    kb = [tile(k_ref, h) * beta[h] for h in heads]
    kbf = [tile(k_ref, h).astype(BF16) for h in heads]
    m_full = [jnp.where(strict, lax.dot_general(kb[h].astype(BF16), kbf[h], NT_DIMS,
                                                preferred_element_type=F32) * decay[h], 0.0) for h in heads]
    p = [-jnp.where(in_base, m_full[h], 0.0) for h in heads]
    t = [eye + p[h] for h in heads]
    size = 2
    while size < base:
        p = [_dot(p[h], p[h]) for h in heads]
        t = [t[h] + _dot(t[h], p[h]) for h in heads]
        size *= 2
    size = base
    for off in levels:
        nblk = rows // (2 * size)
        t4 = [t[h].reshape(nblk, 2, size, rows) for h in heads]
        t_low = [t4[h][:, 1].reshape(rows // 2, rows) for h in heads]
        u = [_dot(t_low[h], jnp.where(off, m_full[h], 0.0)) for h in heads]
        t_low = [(t_low[h] - _dot(u[h], t[h])).reshape(nblk, 1, size, rows) for h in heads]
        t = [jnp.concatenate([t4[h][:, 0:1], t_low[h]], axis=1).reshape(rows, rows) for h in heads]
        size *= 2
    e_g = [jnp.exp(gc[h]) for h in heads]
    sol = [_dot(t[h], jnp.concatenate([tile(v_ref, h) * beta[h], kb[h] * e_g[h]], axis=1)) for h in heads]
    attn = [lax.dot_general(tile(q_ref, h).astype(BF16), kbf[h], NT_DIMS,
                            preferred_element_type=F32) * decay[h] for h in heads]
    qd = [tile(q_ref, h) * e_g[h] for h in heads]
    kt = [tile(k_ref, h) * jnp.exp(ge[h] - gc[h]) for h in heads]
    st = [state_ref[seq_of[h], head_of[h]] for h in heads]
    v_new = [[] for _ in heads]
    o_inter = [[] for _ in heads]
    for c in range(n_sub):
        rs = slice(c * CHUNK, (c + 1) * CHUNK)
        for h in heads:
            r = _dot(jnp.concatenate([sol[h][rs, DN_DV:], qd[h][rs]], axis=0), st[h])
            v_c = sol[h][rs, :DN_DV] - r[:CHUNK]
            v_new[h].append(v_c)
            o_inter[h].append(r[CHUNK:])
            g_tot = jnp.exp(ge[h][c * CHUNK:c * CHUNK + 1, :])
            st[h] = st[h] * g_tot + lax.dot_general(kt[h][rs].astype(BF16), v_c.astype(BF16), TN_DIMS,
                                                    preferred_element_type=F32)
    for h in heads:
        state_ref[seq_of[h], head_of[h]] = st[h]
        o = jnp.concatenate(o_inter[h], axis=0) + _dot(attn[h], jnp.concatenate(v_new[h], axis=0))
        zh = tile(z_ref, h)
        o_ref[seq_of[h], :, col[h]] = (_rms_norm(o, go) * _silu(zh)).astype(BF16)

    @pl.when(n == n_steps - 1)
    def _():
        sout_ref[...] = state_ref[...]


def _delta_rule(q, k, v, z, gb, s0, g_o, *, seq, rows, n_par):
    m = q.shape[0]
    nb = m // seq
    rows = min(rows, seq)
    ns = seq // rows
    q, k, v, z, gb = (a.reshape(nb, seq, a.shape[-1]) for a in (q, k, v, z, gb))
    row = lambda b, n: (b, n, 0)
    st_spec = pl.BlockSpec((n_par, DN_HEADS, DN_DK, DN_DV), lambda b, n: (b, 0, 0, 0))
    og, s_new = pl.pallas_call(
        functools.partial(_delta_kernel, n_steps=ns),
        grid=(nb // n_par, ns),
        in_specs=[
            pl.BlockSpec((n_par, rows, DN_KEY), row),
            pl.BlockSpec((n_par, rows, DN_KEY), row),
            pl.BlockSpec((n_par, rows, DN_VAL), row),
            pl.BlockSpec((n_par, rows, DN_VAL), row),
            pl.BlockSpec((n_par, rows, LANES), row),
            st_spec,
            pl.BlockSpec((1, DN_DV), lambda b, n: (0, 0)),
        ],
        out_specs=[pl.BlockSpec((n_par, rows, DN_VAL), row), st_spec],
        out_shape=[
            jax.ShapeDtypeStruct((nb, seq, DN_VAL), BF16),
            jax.ShapeDtypeStruct((nb, DN_HEADS, DN_DK, DN_DV), F32),
        ],
        scratch_shapes=[pltpu.VMEM((n_par, DN_HEADS, DN_DK, DN_DV), F32)],
        compiler_params=_params("arbitrary", "arbitrary"),
        name="gated_delta",
    )(q, k, v, z, gb, s0, g_o)
    return og.reshape(m, DN_VAL), s_new


def _rope_table(pos):
    half = ROPE_DIM // 2
    inv = ROPE_THETA ** (-jnp.arange(half, dtype=F32) / half)
    ang = pos.astype(F32)[:, None] * inv[None, :]
    cs = jnp.concatenate([jnp.cos(ang), jnp.sin(ang)], -1)
    place = jnp.zeros((2 * half, 3 * LANES), F32)
    j = jnp.arange(half)
    place = place.at[j, NOPE_DIM + j].set(1.0).at[j, NOPE_DIM + half + j].set(1.0)
    place = place.at[half + j, LANES + NOPE_DIM + half + j].set(1.0)
    place = place.at[half + j, 2 * LANES + NOPE_DIM + j].set(-1.0)
    ones = (jnp.arange(3 * LANES) < NOPE_DIM).astype(F32)
    return jnp.dot(cs, place, precision=HIGHEST) + ones[None, :]


def _pad_heads(w, per_head, keep):
    kdim = w.shape[0]
    w = w.reshape(kdim, MLA_HEADS, per_head)[:, :, :keep]
    w = jnp.pad(w, ((0, 0), (0, 0), (0, HEAD_PAD - keep)))
    return w.reshape(kdim, MLA_HEADS * HEAD_PAD)


def _prep_even(w_in, w_uq, w_ukv):
    c1 = Q_RANK + KV_RANK
    w_kr = jnp.pad(w_in[:, c1:c1 + ROPE_DIM], ((0, 0), (NOPE_DIM, HEAD_PAD - NOPE_DIM - ROPE_DIM)))
    w1 = jnp.concatenate([w_in[:, :c1], w_kr], -1).astype(BF16)
    w2 = w_in[:, c1 + ROPE_DIM:].astype(BF16)
    wuq = _pad_heads(w_uq, NOPE_DIM + ROPE_DIM, NOPE_DIM + ROPE_DIM).astype(BF16)
    wuk = _pad_heads(w_ukv, NOPE_DIM + V_DIM, NOPE_DIM).astype(BF16)
    wuv = w_ukv.reshape(KV_RANK, MLA_HEADS, NOPE_DIM + V_DIM)[:, :, NOPE_DIM:].reshape(KV_RANK, MLA_WIDTH)
    return w1, w2, wuq, wuk, wuv.T.astype(BF16)


def _prep_odd(w_in, a_log, dt_bias):
    w_qkv = w_in[:, :DN_QKV].reshape(D_MODEL, 3, DN_KEY).transpose(1, 0, 2).astype(BF16)
    w_z = w_in[:, DN_QKV:DN_QKV + DN_VAL].astype(BF16)
    w_ab = jnp.pad(w_in[:, DN_QKV + DN_VAL:], ((0, 0), (0, LANES - 2 * DN_HEADS))).astype(BF16)
    ab_const = jnp.zeros((SUBLANES, LANES), F32)
    ab_const = ab_const.at[0, :DN_HEADS].set(a_log.astype(F32)).at[1, :DN_HEADS].set(dt_bias.astype(F32))
    return w_qkv, w_z, w_ab, ab_const


def _prep_ffn(w_in, conv_w, w_out):
    return w_in.astype(BF16), w_out.astype(BF16), conv_w


def _pad_rows8(buf):
    return jnp.pad(buf.astype(F32), ((0, 0), (SUBLANES - buf.shape[1], 0), (0, 0)))


def _trunk(x, past_lat, past_kr, sc_buf, dconv_buf, delta_s, ff_buf, w, *, tm, tq, tk, dr):
    (even_w, odd_w, ffn_w, w_o_e, w_o_o, g_qnorm, g_kvnorm, sc_w, dconv_w, g_onorm,
     ln_mix_g, ln_mix_b, ln_ff_g, ln_ff_b) = w
    nb, seq, _ = x.shape
    past = 0 if past_lat is None else past_lat.shape[2]
    m = nb * seq
    x = x.reshape(m, D_MODEL)
    tab = _rope_table(past + jnp.arange(seq, dtype=jnp.int32))
    t_valid = past + seq
    tq = min(tq, seq)
    tk = min(tk, t_valid)
    t_pad = -(-t_valid // tk) * tk
    lats, krs, scs, dcs, dss, ffs = [], [], [], [], [], []
    for i in range(DEPTH):
        row = lambda a: a[i].reshape(1, -1)
        if i % 2 == 0:
            e = i // 2
            w1, w2, wuq, wuk, wuv = even_w[e]
            q, lat, kr_pad, o_sc, sc_new, k, v = _even_in(
                x, w1, w2, g_qnorm[e].reshape(1, -1), g_kvnorm[e].reshape(1, -1), wuq, sc_w[e], tab,
                _pad_rows8(sc_buf[e]), wuk, wuv, seq=seq, tm=tm)
            if past or t_pad != t_valid or tk != min(tm, seq):
                lat_all = lat.reshape(nb, seq, KV_RANK)
                kr_all = kr_pad.reshape(nb, seq, LANES)
                if past:
                    kr_past = jnp.pad(past_kr[e].astype(F32),
                                      ((0, 0), (0, 0), (NOPE_DIM, HEAD_PAD - NOPE_DIM - ROPE_DIM)))
                    lat_all = jnp.concatenate([past_lat[e].astype(F32), lat_all], 1)
                    kr_all = jnp.concatenate([kr_past, kr_all], 1)
                if t_pad != t_valid:
                    lat_all = jnp.pad(lat_all, ((0, 0), (0, t_pad - t_valid), (0, 0)))
                    kr_all = jnp.pad(kr_all, ((0, 0), (0, t_pad - t_valid), (0, 0)))
                k, v = _kv_up(lat_all.reshape(nb * t_pad, KV_RANK), kr_all.reshape(nb * t_pad, LANES),
                              wuk, wuv, tm=tk)
            o_att = _attention(q, k, v, nb=nb, sq=seq, t_pad=t_pad, t_valid=t_valid, past=past, tq=tq, tk=tk)
            mix, mix_w = [o_att, o_sc], [w_o_e[e][:MLA_WIDTH], w_o_e[e][MLA_WIDTH:]]
            lats.append(lat.reshape(nb, seq, KV_RANK))
            krs.append(kr_pad.reshape(nb, seq, LANES)[:, :, NOPE_DIM:NOPE_DIM + ROPE_DIM])
            scs.append(sc_new[:, SUBLANES - (SC_K - 1):])
        else:
            o = i // 2
            w_qkv, w_z, w_ab, ab_const = odd_w[o]
            q, k, v, z, gb, dc_new = _odd_in(x, w_qkv, w_z, w_ab, dconv_w[o], ab_const,
                                             _pad_rows8(dconv_buf[o]), seq=seq, tm=tm)
            og, s_new = _delta_rule(q, k, v, z, gb, delta_s[o].astype(F32), g_onorm[o].reshape(1, -1), seq=seq,
                                    rows=dr, n_par=DELTA_SEQS_PER_STEP)
            mix, mix_w = [og], [w_o_o[o]]
            dcs.append(dc_new[:, SUBLANES - (DN_CONV - 1):])
            dss.append(s_new)
        w_ff_in, w_ff_out, cw = ffn_w[i]
        x, ff_new = _mix_ffn(mix, mix_w, x, row(ln_mix_g), row(ln_mix_b), _pad_rows8(ff_buf[i]),
                             w_ff_in, w_ff_out, cw, row(ln_ff_g), row(ln_ff_b), seq=seq, tm=tm)
        ffs.append(ff_new[:, SUBLANES - (FF_K - 1):])
    return (x.reshape(nb, seq, D_MODEL), jnp.stack(lats), jnp.stack(krs), jnp.stack(scs), jnp.stack(dcs),
            jnp.stack(dss), jnp.stack(ffs))


def _prepare(w_in_e, w_uq, w_ukv, w_o_e, w_in_o, a_log, dt_bias, w_o_o, w_ff_in, ffconv_w, w_ff_out):
    even_w = [_prep_even(w_in_e[e], w_uq[e], w_ukv[e]) for e in range(N_EVEN)]
    odd_w = [_prep_odd(w_in_o[o], a_log[o], dt_bias[o]) for o in range(N_ODD)]
    ffn_w = [_prep_ffn(w_ff_in[i], ffconv_w[i], w_ff_out[i]) for i in range(DEPTH)]
    return even_w, odd_w, ffn_w, w_o_e.astype(BF16), w_o_o.astype(BF16)


def kernel(x_prompt, x_sample, cache_mla_latent, cache_mla_krope, state_sconv, state_dconv, state_delta, state_ffconv, w_in_e, g_qnorm, g_kvnorm, w_uq, w_ukv, sc_w, w_o_e, w_in_o, dconv_w, a_log, dt_bias, g_onorm, w_o_o, w_ff_in, ffconv_w, w_ff_out, ln_mix_g, ln_mix_b, ln_ff_g, ln_ff_b):
    even_w, odd_w, ffn_w, w_o_e_b, w_o_o_b = _prepare(
        w_in_e, w_uq, w_ukv, w_o_e, w_in_o, a_log, dt_bias, w_o_o, w_ff_in, ffconv_w, w_ff_out)
    w = (even_w, odd_w, ffn_w, w_o_e_b, w_o_o_b, g_qnorm, g_kvnorm, sc_w, dconv_w, g_onorm,
         ln_mix_g, ln_mix_b, ln_ff_g, ln_ff_b)
    bp = x_prompt.shape[0]
    f32 = x_prompt.dtype
    y_p, p_lat, p_kr, p_sc, p_dc, p_ds, p_ff = _trunk(
        x_prompt, None, None,
        jnp.zeros((N_EVEN, bp, SC_K - 1, SC_WIDTH), f32),
        jnp.zeros((N_ODD, bp, DN_CONV - 1, DN_QKV), f32),
        jnp.zeros((N_ODD, bp, DN_HEADS, DN_DK, DN_DV), f32),
        jnp.zeros((DEPTH, bp, FF_K - 1, D_FF), f32),
        w, tm=512, tq=512, tk=512, dr=128)
    y_s, s_lat, s_kr, s_sc, s_dc, s_ds, s_ff = _trunk(
        x_sample, cache_mla_latent, cache_mla_krope, state_sconv, state_dconv, state_delta, state_ffconv,
        w, tm=512, tq=64, tk=768, dr=256)
    return (y_p, y_s, p_lat, p_kr, p_sc, p_dc, p_ds, p_ff, s_lat, s_kr, s_sc, s_dc, s_ds, s_ff)
```

```python
import functools
import math

import jax
import jax.numpy as jnp
from jax import lax
from jax.experimental import pallas as pl
from jax.experimental.pallas import tpu as pltpu

D_MODEL = 1024
DEPTH = 4
CHUNK = 64
N_EVEN = (DEPTH + 1) // 2
N_ODD = DEPTH // 2
MLA_HEADS = 8
Q_RANK = 384
KV_RANK = 256
NOPE_DIM = 64
ROPE_DIM = 32
V_DIM = 64
ROPE_THETA = 10000.0
MLA_WIDTH = MLA_HEADS * V_DIM
SC_WIDTH = 512
SC_K = 3
DN_HEADS = 8
DN_DK = 128
DN_DV = 128
DN_CONV = 4
DN_KEY = DN_HEADS * DN_DK
DN_VAL = DN_HEADS * DN_DV
DN_QKV = 2 * DN_KEY + DN_VAL
D_FF = 2816
FF_K = 3
ALPHA = (2 * DEPTH) ** 0.25
NORM_EPS = 1e-6
NEG_INF = -1e30
LOG2_E = math.log2(math.e)

F32 = jnp.float32
BF16 = jnp.bfloat16
HIGHEST = lax.Precision.HIGHEST

LANES = 128
SUBLANES = 8
BF16_SUBLANES = 16
HEAD_PAD = 128
FF_CHUNK = 256
DELTA_BASE_BLOCK = 8
DELTA_SEQS_PER_STEP = 4
ATTN_HEAD_GROUP = 8
FAST_HEAD_GROUP = 8
FAST_SOFTMAX_MAX_RISE = 60.0
VMEM_LIMIT = 56 * 1024 * 1024

NT_DIMS = (((1,), (1,)), ((), ()))
TN_DIMS = (((0,), (0,)), ((), ()))


def _params(*sem):
    return pltpu.CompilerParams(dimension_semantics=sem, vmem_limit_bytes=VMEM_LIMIT)


def _dot(a, b):
    return jnp.dot(a.astype(BF16), b.astype(BF16), preferred_element_type=F32)


def _layer_norm(r, g, b):
    mu = jnp.mean(r, axis=-1, keepdims=True)
    d = r - mu
    var = jnp.mean(d * d, axis=-1, keepdims=True)
    return d * lax.rsqrt(var + NORM_EPS) * g + b


def _rms_norm(x, g):
    return x * lax.rsqrt(jnp.mean(x * x, axis=-1, keepdims=True) + NORM_EPS) * g


def _silu(x):
    h = 0.5 * x
    return h + h * jnp.tanh(h)


def _softplus(x):
    return jnp.maximum(x, 0.0) + jnp.log1p(jnp.exp(-jnp.abs(x)))


def _causal_conv(x, prev, w):
    tm, c = x.shape
    nseq = prev.shape[0]
    k_taps = w.shape[0]
    groups = tm // nseq // SUBLANES
    x4 = x.reshape(nseq, groups, SUBLANES, c)
    prev4 = prev.reshape(nseq, 1, SUBLANES, c)
    sub = lax.broadcasted_iota(jnp.int32, x4.shape, 2)
    y = x4 * w[k_taps - 1:k_taps, :]
    for k in range(1, k_taps):
        rot = pltpu.roll(x4, k, axis=2)
        rot_prev = jnp.concatenate([pltpu.roll(prev4, k, axis=2), rot[:, :groups - 1]], axis=1)
        y = y + jnp.where(sub < k, rot_prev, rot) * w[k_taps - 1 - k:k_taps - k, :]
    return y.reshape(tm, c)


def _run_tails(x, nseq):
    tm, c = x.shape
    run = tm // nseq
    return x.reshape(nseq, run, c)[:, run - SUBLANES:, :]


def _row_tiling(m, seq, tm):
    tm = min(tm, m)
    if seq >= tm:
        return tm, seq // tm, 1
    return tm, 1, tm // seq


def _mix_ffn_kernel(*refs, n_mix, tiles_per_seq):
    ys, wos = refs[:n_mix], refs[n_mix:2 * n_mix]
    (x_ref, g1_ref, b1_ref, w_in_ref, w_out_ref, cw_ref, buf_ref, g2_ref, b2_ref,
     out_ref, nbuf_ref, act_ref, x1_ref, xb_ref, win_ref) = refs[2 * n_mix:]
    i = pl.program_id(0)
    nseq = win_ref.shape[0]

    tm = x_ref.shape[0]
    halves = [slice(0, tm // 2), slice(tm // 2, tm)]
    seq_half = [slice(0, nseq // 2), slice(nseq // 2, nseq)]

    for rows in halves:
        mix = jnp.dot(ys[0][rows, :], wos[0][...], preferred_element_type=F32)
        for j in range(1, n_mix):
            mix = mix + jnp.dot(ys[j][rows, :], wos[j][...], preferred_element_type=F32)
        x1 = _layer_norm(ALPHA * x_ref[rows, :] + mix, g1_ref[...], b1_ref[...])
        x1_ref[rows, :] = x1
        xb_ref[rows, :] = x1.astype(BF16)

    @pl.when(i % tiles_per_seq == 0)
    def _():
        win_ref[...] = buf_ref[...]

    for c in range(D_FF // FF_CHUNK):
        cols = slice(c * FF_CHUNK, (c + 1) * FF_CHUNK)
        up_cols = slice(D_FF + c * FF_CHUNK, D_FF + (c + 1) * FF_CHUNK)
        prev = win_ref[:, :, cols]
        tails = []
        for hf, rows in enumerate(halves):
            xb = xb_ref[rows, :]
            gate = jnp.dot(xb, w_in_ref[:, cols], preferred_element_type=F32)
            up = jnp.dot(xb, w_in_ref[:, up_cols], preferred_element_type=F32)
            if nseq == 1:
                y = _causal_conv(gate, prev if hf == 0 else tails[0], cw_ref[:, cols])
                tails.append(_run_tails(gate, 1))
            else:
                y = _causal_conv(gate, prev[seq_half[hf]], cw_ref[:, cols])
                tails.append(_run_tails(gate, nseq // 2))
            act_ref[rows, cols] = (_silu(y) * up).astype(BF16)
        tails = tails[1] if nseq == 1 else jnp.concatenate(tails, axis=0)
        win_ref[:, :, cols] = tails
        nbuf_ref[:, :, cols] = tails

    for rows in halves:
        acc = jnp.dot(act_ref[rows, :], w_out_ref[...], preferred_element_type=F32)
        out_ref[rows, :] = _layer_norm(ALPHA * x1_ref[rows, :] + acc, g2_ref[...], b2_ref[...])


def _mix_ffn(ys, wos, x, mix_g, mix_b, buf8, w_in, w_out, conv_w, ff_g, ff_b, *, seq, tm):
    m = x.shape[0]
    nb = m // seq
    tm, tps, nseq = _row_tiling(m, seq, tm)
    const2 = lambda i: (0, 0)
    row = lambda i: (i, 0)
    resident = lambda a: pl.BlockSpec(a.shape, const2, pipeline_mode=pl.Buffered(1))
    vec = pl.BlockSpec((1, D_MODEL), const2)
    state = pl.BlockSpec((nseq, SUBLANES, D_FF), lambda i: (i // tps, 0, 0))
    return pl.pallas_call(
        functools.partial(_mix_ffn_kernel, n_mix=len(ys), tiles_per_seq=tps),
        grid=(m // tm,),
        in_specs=([pl.BlockSpec((tm, y.shape[1]), row) for y in ys]
                  + [resident(w) for w in wos]
                  + [pl.BlockSpec((tm, D_MODEL), row), vec, vec,
                     resident(w_in), resident(w_out), pl.BlockSpec(conv_w.shape, const2), state, vec, vec]),
        out_specs=[pl.BlockSpec((tm, D_MODEL), row), state],
        out_shape=[
            jax.ShapeDtypeStruct((m, D_MODEL), F32),
            jax.ShapeDtypeStruct((nb, SUBLANES, D_FF), F32),
        ],
        scratch_shapes=[
            pltpu.VMEM((tm, D_FF), BF16),
            pltpu.VMEM((tm, D_MODEL), F32),
            pltpu.VMEM((tm, D_MODEL), BF16),
            pltpu.VMEM((nseq, SUBLANES, D_FF), F32),
        ],
        compiler_params=_params("arbitrary"),
        name="mix_ffn",
    )(*ys, *wos, x, mix_g, mix_b, w_in, w_out, conv_w, buf8, ff_g, ff_b)


def _rope(x, tab):
    return (x * tab[:, 0:LANES]
            + pltpu.roll(x, ROPE_DIM // 2, axis=1) * tab[:, LANES:2 * LANES]
            + pltpu.roll(x, LANES - ROPE_DIM // 2, axis=1) * tab[:, 2 * LANES:3 * LANES])


def _keys_values(lat, kr_pad, wuk_ref, wuvt_ref, k_ref, vt_ref):
    lb = lat.astype(BF16)
    kn = jnp.dot(lb, wuk_ref[...], preferred_element_type=F32)
    for h in range(MLA_HEADS):
        sl = slice(h * HEAD_PAD, (h + 1) * HEAD_PAD)
        k_ref[:, sl] = (kn[:, sl] + kr_pad).astype(BF16)
    vt_ref[...] = lax.dot_general(wuvt_ref[...], lb, NT_DIMS, preferred_element_type=F32).astype(BF16)


def _even_in_kernel(x_ref, w1_ref, w2_ref, gq_ref, gkv_ref, wuq_ref, scw_ref, tab_ref, buf_ref, wuk_ref, wuvt_ref,
                    q_ref, lat_ref, kr_ref, osc_ref, nbuf_ref, k_ref, vt_ref, win_ref, *, tiles_per_seq):
    i = pl.program_id(0)
    nseq = win_ref.shape[0]

    @pl.when(i % tiles_per_seq == 0)
    def _():
        win_ref[...] = buf_ref[...]

    xb = x_ref[...].astype(BF16)
    tab = tab_ref[...]
    h1 = jnp.dot(xb, w1_ref[...], preferred_element_type=F32)
    cq = h1[:, :Q_RANK]
    ckv = h1[:, Q_RANK:Q_RANK + KV_RANK]
    kr_raw = h1[:, Q_RANK + KV_RANK:]
    lat = _rms_norm(ckv, gkv_ref[...])
    kr_pad = _rope(kr_raw, tab)
    lat_ref[...] = lat
    kr_ref[...] = kr_pad
    _keys_values(lat, kr_pad, wuk_ref, wuvt_ref, k_ref, vt_ref)
    q_raw =_dot(_rms_norm(cq, gq_ref[...]), wuq_ref[...])
    scale = (NOPE_DIM + ROPE_DIM) ** -0.5 * LOG2_E
    for h in range(MLA_HEADS):
        sl = slice(h * HEAD_PAD, (h + 1) * HEAD_PAD)
        q_ref[:, sl] = (_rope(q_raw[:, sl], tab) * scale).astype(BF16)

    h2 = jnp.dot(xb, w2_ref[...], preferred_element_type=F32)
    gate_b = h2[:, :SC_WIDTH]
    u_in = h2[:, SC_WIDTH:2 * SC_WIDTH] * h2[:, 2 * SC_WIDTH:]
    u = _causal_conv(u_in, win_ref[...], scw_ref[...])
    tails = _run_tails(u_in, nseq)
    win_ref[...] = tails
    nbuf_ref[...] = tails
    osc_ref[...] = (gate_b * u).astype(BF16)


def _even_in(x, w1, w2, g_q, g_kv, w_uq, sc_w, tab, buf8, w_uk, w_uv_t, *, seq, tm):
    m = x.shape[0]
    nb = m // seq
    tm, tps, nseq = _row_tiling(m, seq, tm)
    if nseq > 1:
        tab = jnp.tile(tab, (nseq, 1))
    const2 = lambda i: (0, 0)
    return pl.pallas_call(
        functools.partial(_even_in_kernel, tiles_per_seq=tps),
        grid=(m // tm,),
        in_specs=[
            pl.BlockSpec((tm, D_MODEL), lambda i: (i, 0)),
            pl.BlockSpec(w1.shape, const2),
            pl.BlockSpec(w2.shape, const2),
            pl.BlockSpec(g_q.shape, const2),
            pl.BlockSpec(g_kv.shape, const2),
            pl.BlockSpec(w_uq.shape, const2),
            pl.BlockSpec(sc_w.shape, const2),
            pl.BlockSpec((tm, 3 * LANES), lambda i: (i % tps, 0)),
            pl.BlockSpec((nseq, SUBLANES, SC_WIDTH), lambda i: (i // tps, 0, 0)),
            pl.BlockSpec(w_uk.shape, const2),
            pl.BlockSpec(w_uv_t.shape, const2),
        ],
        out_specs=[
            pl.BlockSpec((tm, MLA_HEADS * HEAD_PAD), lambda i: (i, 0)),
            pl.BlockSpec((tm, KV_RANK), lambda i: (i, 0)),
            pl.BlockSpec((tm, LANES), lambda i: (i, 0)),
            pl.BlockSpec((tm, SC_WIDTH), lambda i: (i, 0)),
            pl.BlockSpec((nseq, SUBLANES, SC_WIDTH), lambda i: (i // tps, 0, 0)),
            pl.BlockSpec((tm, MLA_HEADS * HEAD_PAD), lambda i: (i, 0)),
            pl.BlockSpec((MLA_WIDTH, tm), lambda i: (i, 0)),
        ],
        out_shape=[
            jax.ShapeDtypeStruct((m, MLA_HEADS * HEAD_PAD), BF16),
            jax.ShapeDtypeStruct((m, KV_RANK), F32),
            jax.ShapeDtypeStruct((m, LANES), F32),
            jax.ShapeDtypeStruct((m, SC_WIDTH), BF16),
            jax.ShapeDtypeStruct((nb, SUBLANES, SC_WIDTH), F32),
            jax.ShapeDtypeStruct((m, MLA_HEADS * HEAD_PAD), BF16),
            jax.ShapeDtypeStruct((m // tm * MLA_WIDTH, tm), BF16),
        ],
        scratch_shapes=[pltpu.VMEM((nseq, SUBLANES, SC_WIDTH), F32)],
        compiler_params=_params("arbitrary"),
        name="even_in",
    )(x, w1, w2, g_q, g_kv, w_uq, sc_w, tab, buf8, w_uk, w_uv_t)


def _kv_up_kernel(lat_ref, kr_ref, wuk_ref, wuvt_ref, k_ref, vt_ref):
    _keys_values(lat_ref[...], kr_ref[...], wuk_ref, wuvt_ref, k_ref, vt_ref)


def _kv_up(lat, kr_pad, w_uk, w_uv_t, *, tm):
    m = lat.shape[0]
    const2 = lambda i: (0, 0)
    return pl.pallas_call(
        _kv_up_kernel,
        grid=(m // tm,),
        in_specs=[
            pl.BlockSpec((tm, KV_RANK), lambda i: (i, 0)),
            pl.BlockSpec((tm, LANES), lambda i: (i, 0)),
            pl.BlockSpec(w_uk.shape, const2),
            pl.BlockSpec(w_uv_t.shape, const2),
        ],
        out_specs=[
            pl.BlockSpec((tm, MLA_HEADS * HEAD_PAD), lambda i: (i, 0)),
            pl.BlockSpec((MLA_WIDTH, tm), lambda i: (i, 0)),
        ],
        out_shape=[
            jax.ShapeDtypeStruct((m, MLA_HEADS * HEAD_PAD), BF16),
            jax.ShapeDtypeStruct((m // tm * MLA_WIDTH, tm), BF16),
        ],
        compiler_params=_params("arbitrary"),
        name="kv_up",
    )(lat, kr_pad, w_uk, w_uv_t)


def _attn_kernel(qi_tab, kj_tab, q_ref, k_ref, vt_ref, o_ref, m_ref, l_ref, acc_ref, *, tq, tk, past, t_valid, nk):
    qi = qi_tab[pl.program_id(1)]
    kj = kj_tab[pl.program_id(1)]
    q_lo = past + qi * tq
    chunk_lo = q_lo // CHUNK
    chunk_hi = (q_lo + tq - 1) // CHUNK
    last = jnp.minimum(((chunk_hi + 1) * CHUNK - 1) // tk, nk - 1)
    k_end = (kj + 1) * tk
    full = jnp.logical_and(k_end <= (chunk_lo + 1) * CHUNK, k_end <= t_valid)
    heads = list(range(MLA_HEADS))
    hsl =[slice(h * HEAD_PAD, (h + 1) * HEAD_PAD) for h in heads]
    vsl = [slice(h * V_DIM, (h + 1) * V_DIM) for h in heads]

    first = kj == 0

    @pl.when(first)
    def _():
        m_ref[...] = jnp.zeros_like(m_ref)
        l_ref[...] = jnp.zeros_like(l_ref)
        acc_ref[...] = jnp.zeros_like(acc_ref)

    def visibility():
        kpos = kj * tk + lax.broadcasted_iota(jnp.int32, (tk, tq), 0)
        qpos = q_lo + lax.broadcasted_iota(jnp.int32, (tk, tq), 1)
        return jnp.logical_and(kpos // CHUNK <= qpos // CHUNK, kpos < t_valid)

    def scores(h, visible):
        s = lax.dot_general(k_ref[:, hsl[h]], q_ref[:, hsl[h]], NT_DIMS, preferred_element_type=F32)
        return s if visible is None else jnp.where(visible, s, NEG_INF)

    def step(masked):
        visible = visibility() if masked else None
        ones = jnp.ones((BF16_SUBLANES, tk), BF16)
        for g0 in range(0, MLA_HEADS, ATTN_HEAD_GROUP):
            grp = range(g0, g0 + ATTN_HEAD_GROUP)
            s = {h: scores(h, visible) for h in grp}
            m_prev = {h: m_ref[h:h + 1, :] for h in grp}
            m_new = {h: jnp.maximum(m_prev[h], jnp.max(s[h], axis=0, keepdims=True)) for h in grp}
            alpha = {h: jnp.exp2(m_prev[h] - m_new[h]) for h in grp}
            p = {h: jnp.exp2(s[h] - m_new[h]).astype(BF16) for h in grp}
            pv = {h: jnp.dot(jnp.concatenate([vt_ref[vsl[h], :], ones], axis=0), p[h],
                             preferred_element_type=F32) for h in grp}
            for h in grp:
                l_ref[h:h + 1, :] = alpha[h] * l_ref[h:h + 1, :] + pv[h][V_DIM:V_DIM + 1, :]
                m_ref[h:h + 1, :] = m_new[h]
                acc_ref[vsl[h], :] = alpha[h] * acc_ref[vsl[h], :] + pv[h][:V_DIM, :]

    def fast_step(masked):
        visible = visibility() if masked else None
        ones = jnp.ones((BF16_SUBLANES, tk), BF16)
        pv, t_max = {}, {}
        for g0 in range(0, MLA_HEADS, FAST_HEAD_GROUP):
            grp = range(g0, g0 + FAST_HEAD_GROUP)
            s = {h: scores(h, visible) for h in grp}
            p = {h: jnp.exp2(s[h] - m_ref[h:h + 1, :]).astype(BF16) for h in grp}
            for h in grp:
                t_max[h] = jnp.max(s[h], axis=0, keepdims=True)
            for h in grp:
                pv[h] = jnp.dot(jnp.concatenate([vt_ref[vsl[h], :], ones], axis=0), p[h],
                                preferred_element_type=F32)
        rise = t_max[0] - m_ref[0:1, :]
        fall = rise
        for h in heads[1:]:
            d = t_max[h] - m_ref[h:h + 1, :]
            rise = jnp.maximum(rise, d)
            fall = jnp.minimum(fall, d)
        hi = jnp.max(rise, axis=1, keepdims=True)
        lo = jnp.min(fall, axis=1, keepdims=True)
        ok = jnp.logical_and(hi <= FAST_SOFTMAX_MAX_RISE,
                             jnp.logical_or(jnp.logical_not(first), lo >= -FAST_SOFTMAX_MAX_RISE))
        for h in heads:
            m_prev = m_ref[h:h + 1, :]
            m_new = jnp.where(first, t_max[h], jnp.maximum(m_prev, t_max[h]))
            alpha = jnp.exp2(m_prev - m_new)
            l_prev = l_ref[h:h + 1, :]
            acc_prev = acc_ref[vsl[h], :]
            l_ref[h:h + 1, :] = jnp.where(ok, (l_prev + pv[h][V_DIM:V_DIM + 1, :]) * alpha, l_prev)
            acc_ref[vsl[h], :] = jnp.where(ok, (acc_prev + pv[h][:V_DIM, :]) * alpha, acc_prev)
            m_ref[h:h + 1, :] = jnp.where(ok, m_new, m_prev)
        in_range = jnp.logical_and(jnp.max(hi) <= FAST_SOFTMAX_MAX_RISE,
                                   jnp.logical_or(jnp.logical_not(first),
                                                  jnp.min(lo) >= -FAST_SOFTMAX_MAX_RISE))

        @pl.when(jnp.logical_not(in_range))
        def _():
            @pl.when(first)
            def _():
                m_ref[...] = jnp.full_like(m_ref, NEG_INF)
            step(masked)

    live = kj <= last

    @pl.when(jnp.logical_and(live, full))
    def _():
        fast_step(False)

    @pl.when(jnp.logical_and(live, jnp.logical_not(full)))
    def _():
        fast_step(True)

    @pl.when(kj == last)
    def _():
        for hp in range(MLA_HEADS // 2):
            pair = jnp.concatenate(
                [acc_ref[vsl[h], :] * (1.0 / l_ref[h:h + 1, :]) for h in (2 * hp, 2 * hp + 1)], axis=0)
            o_ref[:, hp * 2 * V_DIM:(hp + 1) * 2 * V_DIM] = pair.T.astype(BF16)


def _attention(q, k, vt, *, nb, sq, t_pad, t_valid, past, tq, tk):
    nq = sq // tq
    nk = t_pad // tk
    pairs = []
    for qi in range(nq):
        chunk_hi = (past + qi * tq + tq - 1) // CHUNK
        last = min(((chunk_hi + 1) * CHUNK - 1) // tk, nk - 1)
        pairs += [(qi, kj) for kj in range(last + 1)]
    qi_tab = jnp.asarray([p[0] for p in pairs], jnp.int32)
    kj_tab = jnp.asarray([p[1] for p in pairs], jnp.int32)
    kv_index = lambda b, p, qt, kt: (b * nk + kt[p], 0)
    q_index = lambda b, p, qt, kt: (b * nq + qt[p], 0)
    grid_spec = pltpu.PrefetchScalarGridSpec(
        num_scalar_prefetch=2,
        grid=(nb, len(pairs)),
        in_specs=[
            pl.BlockSpec((tq, MLA_HEADS * HEAD_PAD), q_index),
            pl.BlockSpec((tk, MLA_HEADS * HEAD_PAD), kv_index),
            pl.BlockSpec((MLA_WIDTH, tk), kv_index),
        ],
        out_specs=pl.BlockSpec((tq, MLA_WIDTH), q_index),
        scratch_shapes=[
            pltpu.VMEM((MLA_HEADS, tq), F32),
            pltpu.VMEM((MLA_HEADS, tq), F32),
            pltpu.VMEM((MLA_WIDTH, tq), F32),
        ],
    )
    return pl.pallas_call(
        functools.partial(_attn_kernel, tq=tq, tk=tk, past=past, t_valid=t_valid, nk=nk),
        grid_spec=grid_spec,
        out_shape=jax.ShapeDtypeStruct((nb * sq, MLA_WIDTH), BF16),
        compiler_params=_params("arbitrary", "arbitrary"),
        name="chunk_attention",
    )(qi_tab, kj_tab, q, k, vt)


def _odd_in_kernel(x_ref, wqkv_ref, wz_ref, wab_ref, dcw_ref, ab_ref, buf_ref,
                   q_ref, k_ref, v_ref, z_ref, gb_ref, nbuf_ref, win_ref, *, tiles_per_seq):
    i = pl.program_id(0)
    nseq = win_ref.shape[0]

    @pl.when(i % tiles_per_seq == 0)
    def _():
        win_ref[...] = buf_ref[...]

    xb = x_ref[...].astype(BF16)
    outs = (q_ref, k_ref, v_ref)
    for part in range(3):
        cols = slice(part * DN_KEY, (part + 1) * DN_KEY)
        pre = jnp.dot(xb, wqkv_ref[part], preferred_element_type=F32)
        y = _causal_conv(pre, win_ref[:, :, cols], dcw_ref[:, cols])
        tails = _run_tails(pre, nseq)
        win_ref[:, :, cols] = tails
        nbuf_ref[:, :, cols] = tails
        y = _silu(y)
        if part == 2:
            outs[part][...] = y
        else:
            post = DN_DK ** -0.5 if part == 0 else 1.0
            for h in range(DN_HEADS):
                sl = slice(h * DN_DK, (h + 1) * DN_DK)
                seg = y[:, sl]
                nrm = lax.rsqrt(jnp.sum(seg * seg, axis=-1, keepdims=True) + NORM_EPS)
                outs[part][:, sl] = seg * nrm * post if part == 0 else seg * nrm

    z_ref[...] = jnp.dot(xb, wz_ref[...], preferred_element_type=F32)
    ab = jnp.dot(xb, wab_ref[...], preferred_element_type=F32)
    lane = lax.broadcasted_iota(jnp.int32, ab.shape, 1)
    g = -jnp.exp(ab_ref[0:1, :]) * _softplus(ab + ab_ref[1:2, :])
    gb_ref[...] = jnp.where(lane < DN_HEADS, g, jax.nn.sigmoid(ab))


def _odd_in(x, w_qkv, w_z, w_ab, dconv_w, ab_const, buf8, *, seq, tm):
    m = x.shape[0]
    nb = m // seq
    tm, tps, nseq = _row_tiling(m, seq, tm)
    const2 = lambda i: (0, 0)
    row = lambda i: (i, 0)
    return pl.pallas_call(
        functools.partial(_odd_in_kernel, tiles_per_seq=tps),
        grid=(m // tm,),
        in_specs=[
            pl.BlockSpec((tm, D_MODEL), row),
            pl.BlockSpec(w_qkv.shape, lambda i: (0, 0, 0)),
            pl.BlockSpec(w_z.shape, const2),
            pl.BlockSpec(w_ab.shape, const2),
            pl.BlockSpec(dconv_w.shape, const2),
            pl.BlockSpec(ab_const.shape, const2),
            pl.BlockSpec((nseq, SUBLANES, DN_QKV), lambda i: (i // tps, 0, 0)),
        ],
        out_specs=[
            pl.BlockSpec((tm, DN_KEY), row),
            pl.BlockSpec((tm, DN_KEY), row),
            pl.BlockSpec((tm, DN_VAL), row),
            pl.BlockSpec((tm, DN_VAL), row),
            pl.BlockSpec((tm, LANES), row),
            pl.BlockSpec((nseq, SUBLANES, DN_QKV), lambda i: (i // tps, 0, 0)),
        ],
        out_shape=[
            jax.ShapeDtypeStruct((m, DN_KEY), F32),
            jax.ShapeDtypeStruct((m, DN_KEY), F32),
            jax.ShapeDtypeStruct((m, DN_VAL), F32),
            jax.ShapeDtypeStruct((m, DN_VAL), F32),
            jax.ShapeDtypeStruct((m, LANES), F32),
            jax.ShapeDtypeStruct((nb, SUBLANES, DN_QKV), F32),
        ],
        scratch_shapes=[pltpu.VMEM((nseq, SUBLANES, DN_QKV), F32)],
        compiler_params=_params("arbitrary"),
        name="odd_in",
    )(x, w_qkv, w_z, w_ab, dconv_w, ab_const, buf8)


def _delta_kernel(q_ref, k_ref, v_ref, z_ref, gb_ref, s0_ref, go_ref, o_ref, sout_ref, state_ref, *, n_steps):
    n = pl.program_id(1)
    n_par, rows = q_ref.shape[0], q_ref.shape[1]
    n_sub = rows // CHUNK

    @pl.when(n == 0)
    def _():
        state_ref[...] = s0_ref[...]

    ri = lax.broadcasted_iota(jnp.int32, (rows, rows), 0)
    ci = lax.broadcasted_iota(jnp.int32, (rows, rows), 1)

    def blk(size):
        return (ri // size) == (ci // size)

    same = blk(CHUNK)
    lower = ri > ci
    tril = jnp.logical_and(same, ri >= ci)
    strict = jnp.logical_and(same, lower)
    eye = (ri == ci).astype(F32)
    base = DELTA_BASE_BLOCK
    levels = []
    size = base
    while size < CHUNK:
        levels.append(jnp.logical_and(jnp.logical_and(blk(2 * size), jnp.logical_not(blk(size))), lower))
        size *= 2
    in_base = jnp.logical_and(blk(base), lower)
    gb = [gb_ref[b] for b in range(n_par)]
    g_col = [jnp.dot(tril.astype(F32), gb[b], preferred_element_type=F32, precision=HIGHEST)
             for b in range(n_par)]
    g_end = [jnp.dot(same.astype(F32), gb[b], preferred_element_type=F32, precision=HIGHEST)
             for b in range(n_par)]
    g_row = [g_col[b].T for b in range(n_par)]
    go = go_ref[...]
    chain = [(b, hd) for b in range(n_par) for hd in range(DN_HEADS)]
    heads = range(len(chain))
    col = [slice(hd * DN_DK, (hd + 1) * DN_DK) for _, hd in chain]
    seq_of = [b for b, _ in chain]
    head_of = [hd for _, hd in chain]

    def tile(ref, h):
        return ref[seq_of[h], :, col[h]]

    gc = [g_col[seq_of[h]][:, head_of[h]:head_of[h] + 1] for h in heads]
    ge = [g_end[seq_of[h]][:, head_of[h]:head_of[h] + 1] for h in heads]
    decay = [jnp.exp(jnp.where(tril, gc[h] - g_row[seq_of[h]][head_of[h]:head_of[h] + 1, :], -jnp.inf))
             for h in heads]
    beta = [gb[seq_of[h]][:, DN_HEADS + head_of[h]:DN_HEADS + head_of[h] + 1] for h in heads]
    kb = [tile(k_ref, h) * beta[h] for h in heads]
    kbf = [tile(k_ref, h).astype(BF16) for h in heads]
    m_full = [jnp.where(strict, lax.dot_general(kb[h].astype(BF16), kbf[h], NT_DIMS,
                                                preferred_element_type=F32) * decay[h], 0.0) for h in heads]
    p = [-jnp.where(in_base, m_full[h], 0.0) for h in heads]
    t = [eye + p[h] for h in heads]
    size = 2
    while size < base:
        p = [_dot(p[h], p[h]) for h in heads]
        t = [t[h] + _dot(t[h], p[h]) for h in heads]
        size *= 2
    size = base
    for off in levels:
        nblk = rows // (2 * size)
        t4 = [t[h].reshape(nblk, 2, size, rows) for h in heads]
        t_low = [t4[h][:, 1].reshape(rows // 2, rows) for h in heads]
        u = [_dot(t_low[h], jnp.where(off, m_full[h], 0.0)) for h in heads]
        t_low = [(t_low[h] - _dot(u[h], t[h])).reshape(nblk, 1, size, rows) for h in heads]
        t = [jnp.concatenate([t4[h][:, 0:1], t_low[h]], axis=1).reshape(rows, rows) for h in heads]
        size *= 2
    e_g = [jnp.exp(gc[h]) for h in heads]
    sol = [_dot(t[h], jnp.concatenate([tile(v_ref, h) * beta[h], kb[h] * e_g[h]], axis=1)) for h in heads]
    attn = [lax.dot_general(tile(q_ref, h).astype(BF16), kbf[h], NT_DIMS,
                            preferred_element_type=F32) * decay[h] for h in heads]
    qd = [tile(q_ref, h) * e_g[h] for h in heads]
    kt = [tile(k_ref, h) * jnp.exp(ge[h] - gc[h]) for h in heads]
    st = [state_ref[seq_of[h], head_of[h]] for h in heads]
    v_new = [[] for _ in heads]
    o_inter = [[] for _ in heads]
    for c in range(n_sub):
        rs = slice(c * CHUNK, (c + 1) * CHUNK)
        for h in heads:
            r = _dot(jnp.concatenate([sol[h][rs, DN_DV:], qd[h][rs]], axis=0), st[h])
            v_c = sol[h][rs, :DN_DV] - r[:CHUNK]
            v_new[h].append(v_c)
            o_inter[h].append(r[CHUNK:])
            g_tot = jnp.exp(ge[h][c * CHUNK:c * CHUNK + 1, :])
            st[h] = st[h] * g_tot + lax.dot_general(kt[h][rs].astype(BF16), v_c.astype(BF16), TN_DIMS,
                                                    preferred_element_type=F32)
    for h in heads:
        state_ref[seq_of[h], head_of[h]] = st[h]
        o = jnp.concatenate(o_inter[h], axis=0) + _dot(attn[h], jnp.concatenate(v_new[h], axis=0))
        zh = tile(z_ref, h)
        o_ref[seq_of[h], :, col[h]] = (_rms_norm(o, go) * _silu(zh)).astype(BF16)

    @pl.when(n == n_steps - 1)
    def _():
        sout_ref[...] = state_ref[...]


def _delta_rule(q, k, v, z, gb, s0, g_o, *, seq, rows, n_par):
    m = q.shape[0]
    nb = m // seq
    rows = min(rows, seq)
    ns = seq // rows
    q, k, v, z, gb = (a.reshape(nb, seq, a.shape[-1]) for a in (q, k, v, z, gb))
    row = lambda b, n: (b, n, 0)
    st_spec = pl.BlockSpec((n_par, DN_HEADS, DN_DK, DN_DV), lambda b, n: (b, 0, 0, 0))
    og, s_new = pl.pallas_call(
        functools.partial(_delta_kernel, n_steps=ns),
        grid=(nb // n_par, ns),
        in_specs=[
            pl.BlockSpec((n_par, rows, DN_KEY), row),
            pl.BlockSpec((n_par, rows, DN_KEY), row),
            pl.BlockSpec((n_par, rows, DN_VAL), row),
            pl.BlockSpec((n_par, rows, DN_VAL), row),
            pl.BlockSpec((n_par, rows, LANES), row),
            st_spec,
            pl.BlockSpec((1, DN_DV), lambda b, n: (0, 0)),
        ],
        out_specs=[pl.BlockSpec((n_par, rows, DN_VAL), row), st_spec],
        out_shape=[
            jax.ShapeDtypeStruct((nb, seq, DN_VAL), BF16),
            jax.ShapeDtypeStruct((nb, DN_HEADS, DN_DK, DN_DV), F32),
        ],
        scratch_shapes=[pltpu.VMEM((n_par, DN_HEADS, DN_DK, DN_DV), F32)],
        compiler_params=_params("arbitrary", "arbitrary"),
        name="gated_delta",
    )(q, k, v, z, gb, s0, g_o)
    return og.reshape(m, DN_VAL), s_new


def _rope_table(pos):
    half = ROPE_DIM // 2
    inv = ROPE_THETA ** (-jnp.arange(half, dtype=F32) / half)
    ang = pos.astype(F32)[:, None] * inv[None, :]
    cs = jnp.concatenate([jnp.cos(ang), jnp.sin(ang)], -1)
    place = jnp.zeros((2 * half, 3 * LANES), F32)
    j = jnp.arange(half)
    place = place.at[j, NOPE_DIM + j].set(1.0).at[j, NOPE_DIM + half + j].set(1.0)
    place = place.at[half + j, LANES + NOPE_DIM + half + j].set(1.0)
    place = place.at[half + j, 2 * LANES + NOPE_DIM + j].set(-1.0)
    ones = (jnp.arange(3 * LANES) < NOPE_DIM).astype(F32)
    return jnp.dot(cs, place, precision=HIGHEST) + ones[None, :]


def _pad_heads(w, per_head, keep):
    kdim = w.shape[0]
    w = w.reshape(kdim, MLA_HEADS, per_head)[:, :, :keep]
    w = jnp.pad(w, ((0, 0), (0, 0), (0, HEAD_PAD - keep)))
    return w.reshape(kdim, MLA_HEADS * HEAD_PAD)


def _prep_even(w_in, w_uq, w_ukv):
    c1 = Q_RANK + KV_RANK
    w_kr = jnp.pad(w_in[:, c1:c1 + ROPE_DIM], ((0, 0), (NOPE_DIM, HEAD_PAD - NOPE_DIM - ROPE_DIM)))
    w1 = jnp.concatenate([w_in[:, :c1], w_kr], -1).astype(BF16)
    w2 = w_in[:, c1 + ROPE_DIM:].astype(BF16)
    wuq = _pad_heads(w_uq, NOPE_DIM + ROPE_DIM, NOPE_DIM + ROPE_DIM).astype(BF16)
    wuk = _pad_heads(w_ukv, NOPE_DIM + V_DIM, NOPE_DIM).astype(BF16)
    wuv = w_ukv.reshape(KV_RANK, MLA_HEADS, NOPE_DIM + V_DIM)[:, :, NOPE_DIM:].reshape(KV_RANK, MLA_WIDTH)
    return w1, w2, wuq, wuk, wuv.T.astype(BF16)


def _prep_odd(w_in, a_log, dt_bias):
    w_qkv = w_in[:, :DN_QKV].reshape(D_MODEL, 3, DN_KEY).transpose(1, 0, 2).astype(BF16)
    w_z = w_in[:, DN_QKV:DN_QKV + DN_VAL].astype(BF16)
    w_ab = jnp.pad(w_in[:, DN_QKV + DN_VAL:], ((0, 0), (0, LANES - 2 * DN_HEADS))).astype(BF16)
    ab_const = jnp.zeros((SUBLANES, LANES), F32)
    ab_const = ab_const.at[0, :DN_HEADS].set(a_log.astype(F32)).at[1, :DN_HEADS].set(dt_bias.astype(F32))
    return w_qkv, w_z, w_ab, ab_const


def _prep_ffn(w_in, conv_w, w_out):
    return w_in.astype(BF16), w_out.astype(BF16), conv_w


def _pad_rows8(buf):
    return jnp.pad(buf.astype(F32), ((0, 0), (SUBLANES - buf.shape[1], 0), (0, 0)))


def _trunk(x, past_lat, past_kr, sc_buf, dconv_buf, delta_s, ff_buf, w, *, tm, tq, tk, dr):
    (even_w, odd_w, ffn_w, w_o_e, w_o_o, g_qnorm, g_kvnorm, sc_w, dconv_w, g_onorm,
     ln_mix_g, ln_mix_b, ln_ff_g, ln_ff_b) = w
    nb, seq, _ = x.shape
    past = 0 if past_lat is None else past_lat.shape[2]
    m = nb * seq
    x = x.reshape(m, D_MODEL)
    tab = _rope_table(past + jnp.arange(seq, dtype=jnp.int32))
    t_valid = past + seq
    tq = min(tq, seq)
    tk = min(tk, t_valid)
    t_pad = -(-t_valid // tk) * tk
    lats, krs, scs, dcs, dss, ffs = [], [], [], [], [], []
    for i in range(DEPTH):
        row = lambda a: a[i].reshape(1, -1)
        if i % 2 == 0:
            e = i // 2
            w1, w2, wuq, wuk, wuv = even_w[e]
            q, lat, kr_pad, o_sc, sc_new, k, v = _even_in(
                x, w1, w2, g_qnorm[e].reshape(1, -1), g_kvnorm[e].reshape(1, -1), wuq, sc_w[e], tab,
                _pad_rows8(sc_buf[e]), wuk, wuv, seq=seq, tm=tm)
            if past or t_pad != t_valid or tk != min(tm, seq):
                lat_all = lat.reshape(nb, seq, KV_RANK)
                kr_all = kr_pad.reshape(nb, seq, LANES)
                if past:
                    kr_past = jnp.pad(past_kr[e].astype(F32),
                                      ((0, 0), (0, 0), (NOPE_DIM, HEAD_PAD - NOPE_DIM - ROPE_DIM)))
                    lat_all = jnp.concatenate([past_lat[e].astype(F32), lat_all], 1)
                    kr_all = jnp.concatenate([kr_past, kr_all], 1)
                if t_pad != t_valid:
                    lat_all = jnp.pad(lat_all, ((0, 0), (0, t_pad - t_valid), (0, 0)))
                    kr_all = jnp.pad(kr_all, ((0, 0), (0, t_pad - t_valid), (0, 0)))
                k, v = _kv_up(lat_all.reshape(nb * t_pad, KV_RANK), kr_all.reshape(nb * t_pad, LANES),
                              wuk, wuv, tm=tk)
            o_att = _attention(q, k, v, nb=nb, sq=seq, t_pad=t_pad, t_valid=t_valid, past=past, tq=tq, tk=tk)
            mix, mix_w = [o_att, o_sc], [w_o_e[e][:MLA_WIDTH], w_o_e[e][MLA_WIDTH:]]
            lats.append(lat.reshape(nb, seq, KV_RANK))
            krs.append(kr_pad.reshape(nb, seq, LANES)[:, :, NOPE_DIM:NOPE_DIM + ROPE_DIM])
            scs.append(sc_new[:, SUBLANES - (SC_K - 1):])
        else:
            o = i // 2
            w_qkv, w_z, w_ab, ab_const = odd_w[o]
            q, k, v, z, gb, dc_new = _odd_in(x, w_qkv, w_z, w_ab, dconv_w[o], ab_const,
                                             _pad_rows8(dconv_buf[o]), seq=seq, tm=tm)
            og, s_new = _delta_rule(q, k, v, z, gb, delta_s[o].astype(F32), g_onorm[o].reshape(1, -1), seq=seq,
                                    rows=dr, n_par=DELTA_SEQS_PER_STEP)
            mix, mix_w = [og], [w_o_o[o]]
            dcs.append(dc_new[:, SUBLANES - (DN_CONV - 1):])
            dss.append(s_new)
        w_ff_in, w_ff_out, cw = ffn_w[i]
        x, ff_new = _mix_ffn(mix, mix_w, x, row(ln_mix_g), row(ln_mix_b), _pad_rows8(ff_buf[i]),
                             w_ff_in, w_ff_out, cw, row(ln_ff_g), row(ln_ff_b), seq=seq, tm=tm)
        ffs.append(ff_new[:, SUBLANES - (FF_K - 1):])
    return (x.reshape(nb, seq, D_MODEL), jnp.stack(lats), jnp.stack(krs), jnp.stack(scs), jnp.stack(dcs),
            jnp.stack(dss), jnp.stack(ffs))


def _prepare(w_in_e, w_uq, w_ukv, w_o_e, w_in_o, a_log, dt_bias, w_o_o, w_ff_in, ffconv_w, w_ff_out):
    even_w = [_prep_even(w_in_e[e], w_uq[e], w_ukv[e]) for e in range(N_EVEN)]
    odd_w = [_prep_odd(w_in_o[o], a_log[o], dt_bias[o]) for o in range(N_ODD)]
    ffn_w = [_prep_ffn(w_ff_in[i], ffconv_w[i], w_ff_out[i]) for i in range(DEPTH)]
    return even_w, odd_w, ffn_w, w_o_e.astype(BF16), w_o_o.astype(BF16)


def kernel(x_prompt, x_sample, cache_mla_latent, cache_mla_krope, state_sconv, state_dconv, state_delta, state_ffconv, w_in_e, g_qnorm, g_kvnorm, w_uq, w_ukv, sc_w, w_o_e, w_in_o, dconv_w, a_log, dt_bias, g_onorm, w_o_o, w_ff_in, ffconv_w, w_ff_out, ln_mix_g, ln_mix_b, ln_ff_g, ln_ff_b):
    even_w, odd_w, ffn_w, w_o_e_b, w_o_o_b = _prepare(
        w_in_e, w_uq, w_ukv, w_o_e, w_in_o, a_log, dt_bias, w_o_o, w_ff_in, ffconv_w, w_ff_out)
    w = (even_w, odd_w, ffn_w, w_o_e_b, w_o_o_b, g_qnorm, g_kvnorm, sc_w, dconv_w, g_onorm,
         ln_mix_g, ln_mix_b, ln_ff_g, ln_ff_b)
    bp = x_prompt.shape[0]
    f32 = x_prompt.dtype
    y_p, p_lat, p_kr, p_sc, p_dc, p_ds, p_ff = _trunk(
        x_prompt, None, None,
        jnp.zeros((N_EVEN, bp, SC_K - 1, SC_WIDTH), f32),
        jnp.zeros((N_ODD, bp, DN_CONV - 1, DN_QKV), f32),
        jnp.zeros((N_ODD, bp, DN_HEADS, DN_DK, DN_DV), f32),
        jnp.zeros((DEPTH, bp, FF_K - 1, D_FF), f32),
        w, tm=512, tq=512, tk=512, dr=128)
    y_s, s_lat, s_kr, s_sc, s_dc, s_ds, s_ff = _trunk(
        x_sample, cache_mla_latent, cache_mla_krope, state_sconv, state_dconv, state_delta, state_ffconv,
        w, tm=512, tq=64, tk=768, dr=256)
    return (y_p, y_s, p_lat, p_kr, p_sc, p_dc, p_ds, p_ff, s_lat, s_kr, s_sc, s_dc, s_ds, s_ff)
```

```python
import functools
import math

import jax
import jax.numpy as jnp
from jax import lax
from jax.experimental import pallas as pl
from jax.experimental.pallas import tpu as pltpu

D_MODEL = 1024
DEPTH = 4
CHUNK = 64
N_EVEN = (DEPTH + 1) // 2
N_ODD = DEPTH // 2
MLA_HEADS = 8
Q_RANK = 384
KV_RANK = 256
NOPE_DIM = 64
ROPE_DIM = 32
V_DIM = 64
ROPE_THETA = 10000.0
MLA_WIDTH = MLA_HEADS * V_DIM
SC_WIDTH = 512
SC_K = 3
DN_HEADS = 8
DN_DK = 128
DN_DV = 128
DN_CONV = 4
DN_KEY = DN_HEADS * DN_DK
DN_VAL = DN_HEADS * DN_DV
DN_QKV = 2 * DN_KEY + DN_VAL
D_FF = 2816
FF_K = 3
ALPHA = (2 * DEPTH) ** 0.25
NORM_EPS = 1e-6
NEG_INF = -1e30
LOG2_E = math.log2(math.e)

F32 = jnp.float32
BF16 = jnp.bfloat16
HIGHEST = lax.Precision.HIGHEST

LANES = 128
SUBLANES = 8
BF16_SUBLANES = 16
HEAD_PAD = 128
FF_CHUNK = 256
DELTA_BASE_BLOCK = 8
DELTA_SEQS_PER_STEP = 4
ATTN_HEAD_GROUP = 8
FAST_HEAD_GROUP = 8
FAST_SOFTMAX_MAX_RISE = 60.0
VMEM_LIMIT = 56 * 1024 * 1024

NT_DIMS = (((1,), (1,)), ((), ()))
TN_DIMS = (((0,), (0,)), ((), ()))


def _params(*sem):
    return pltpu.CompilerParams(dimension_semantics=sem, vmem_limit_bytes=VMEM_LIMIT)


def _dot(a, b):
    return jnp.dot(a.astype(BF16), b.astype(BF16), preferred_element_type=F32)


def _layer_norm(r, g, b):
    mu = jnp.mean(r, axis=-1, keepdims=True)
    d = r - mu
    var = jnp.mean(d * d, axis=-1, keepdims=True)
    return d * lax.rsqrt(var + NORM_EPS) * g + b


def _rms_norm(x, g):
    return x * lax.rsqrt(jnp.mean(x * x, axis=-1, keepdims=True) + NORM_EPS) * g


def _silu(x):
    h = 0.5 * x
    return h + h * jnp.tanh(h)


def _softplus(x):
    return jnp.maximum(x, 0.0) + jnp.log1p(jnp.exp(-jnp.abs(x)))


def _causal_conv(x, prev, w):
    tm, c = x.shape
    nseq = prev.shape[0]
    k_taps = w.shape[0]
    groups = tm // nseq // SUBLANES
    x4 = x.reshape(nseq, groups, SUBLANES, c)
    prev4 = prev.reshape(nseq, 1, SUBLANES, c)
    sub = lax.broadcasted_iota(jnp.int32, x4.shape, 2)
    y = x4 * w[k_taps - 1:k_taps, :]
    for k in range(1, k_taps):
        rot = pltpu.roll(x4, k, axis=2)
        rot_prev = jnp.concatenate([pltpu.roll(prev4, k, axis=2), rot[:, :groups - 1]], axis=1)
        y = y + jnp.where(sub < k, rot_prev, rot) * w[k_taps - 1 - k:k_taps - k, :]
    return y.reshape(tm, c)


def _run_tails(x, nseq):
    tm, c = x.shape
    run = tm // nseq
    return x.reshape(nseq, run, c)[:, run - SUBLANES:, :]


def _row_tiling(m, seq, tm):
    tm = min(tm, m)
    if seq >= tm:
        return tm, seq // tm, 1
    return tm, 1, tm // seq


def _mix_ffn_kernel(*refs, n_mix, tiles_per_seq):
    ys, wos = refs[:n_mix], refs[n_mix:2 * n_mix]
    (x_ref, g1_ref, b1_ref, w_in_ref, w_out_ref, cw_ref, buf_ref, g2_ref, b2_ref,
     out_ref, nbuf_ref, act_ref, x1_ref, xb_ref, win_ref) = refs[2 * n_mix:]
    i = pl.program_id(0)
    nseq = win_ref.shape[0]

    tm = x_ref.shape[0]
    halves = [slice(0, tm // 2), slice(tm // 2, tm)]
    seq_half = [slice(0, nseq // 2), slice(nseq // 2, nseq)]

    for rows in halves:
        mix = jnp.dot(ys[0][rows, :], wos[0][...], preferred_element_type=F32)
        for j in range(1, n_mix):
            mix = mix + jnp.dot(ys[j][rows, :], wos[j][...], preferred_element_type=F32)
        x1 = _layer_norm(ALPHA * x_ref[rows, :] + mix, g1_ref[...], b1_ref[...])
        x1_ref[rows, :] = x1
        xb_ref[rows, :] = x1.astype(BF16)

    @pl.when(i % tiles_per_seq == 0)
    def _():
        win_ref[...] = buf_ref[...]

    for c in range(D_FF // FF_CHUNK):
        cols = slice(c * FF_CHUNK, (c + 1) * FF_CHUNK)
        up_cols = slice(D_FF + c * FF_CHUNK, D_FF + (c + 1) * FF_CHUNK)
        prev = win_ref[:, :, cols]
        tails = []
        for hf, rows in enumerate(halves):
            xb = xb_ref[rows, :]
            gate = jnp.dot(xb, w_in_ref[:, cols], preferred_element_type=F32)
            up = jnp.dot(xb, w_in_ref[:, up_cols], preferred_element_type=F32)
            if nseq == 1:
                y = _causal_conv(gate, prev if hf == 0 else tails[0], cw_ref[:, cols])
                tails.append(_run_tails(gate, 1))
            else:
                y = _causal_conv(gate, prev[seq_half[hf]], cw_ref[:, cols])
                tails.append(_run_tails(gate, nseq // 2))
            act_ref[rows, cols] = (_silu(y) * up).astype(BF16)
        tails = tails[1] if nseq == 1 else jnp.concatenate(tails, axis=0)
        win_ref[:, :, cols] = tails
        nbuf_ref[:, :, cols] = tails

    for rows in halves:
        acc = jnp.dot(act_ref[rows, :], w_out_ref[...], preferred_element_type=F32)
        out_ref[rows, :] = _layer_norm(ALPHA * x1_ref[rows, :] + acc, g2_ref[...], b2_ref[...])


def _mix_ffn(ys, wos, x, mix_g, mix_b, buf8, w_in, w_out, conv_w, ff_g, ff_b, *, seq, tm):
    m = x.shape[0]
    nb = m // seq
    tm, tps, nseq = _row_tiling(m, seq, tm)
    const2 = lambda i: (0, 0)
    row = lambda i: (i, 0)
    resident = lambda a: pl.BlockSpec(a.shape, const2, pipeline_mode=pl.Buffered(1))
    vec = pl.BlockSpec((1, D_MODEL), const2)
    state = pl.BlockSpec((nseq, SUBLANES, D_FF), lambda i: (i // tps, 0, 0))
    return pl.pallas_call(
        functools.partial(_mix_ffn_kernel, n_mix=len(ys), tiles_per_seq=tps),
        grid=(m // tm,),
        in_specs=([pl.BlockSpec((tm, y.shape[1]), row) for y in ys]
                  + [resident(w) for w in wos]
                  + [pl.BlockSpec((tm, D_MODEL), row), vec, vec,
                     resident(w_in), resident(w_out), pl.BlockSpec(conv_w.shape, const2), state, vec, vec]),
        out_specs=[pl.BlockSpec((tm, D_MODEL), row), state],
        out_shape=[
            jax.ShapeDtypeStruct((m, D_MODEL), F32),
            jax.ShapeDtypeStruct((nb, SUBLANES, D_FF), F32),
        ],
        scratch_shapes=[
            pltpu.VMEM((tm, D_FF), BF16),
            pltpu.VMEM((tm, D_MODEL), F32),
            pltpu.VMEM((tm, D_MODEL), BF16),
            pltpu.VMEM((nseq, SUBLANES, D_FF), F32),
        ],
        compiler_params=_params("arbitrary"),
        name="mix_ffn",
    )(*ys, *wos, x, mix_g, mix_b, w_in, w_out, conv_w, buf8, ff_g, ff_b)


def _rope(x, tab):
    return (x * tab[:, 0:LANES]
            + pltpu.roll(x, ROPE_DIM // 2, axis=1) * tab[:, LANES:2 * LANES]
            + pltpu.roll(x, LANES - ROPE_DIM // 2, axis=1) * tab[:, 2 * LANES:3 * LANES])


def _keys_values(lat, kr_pad, wuk_ref, wuvt_ref, k_ref, vt_ref):
    lb = lat.astype(BF16)
    kn = jnp.dot(lb, wuk_ref[...], preferred_element_type=F32)
    for h in range(MLA_HEADS):
        sl = slice(h * HEAD_PAD, (h + 1) * HEAD_PAD)
        k_ref[:, sl] = (kn[:, sl] + kr_pad).astype(BF16)
    vt_ref[...] = lax.dot_general(wuvt_ref[...], lb, NT_DIMS, preferred_element_type=F32).astype(BF16)


def _even_in_kernel(x_ref, w1_ref, w2_ref, gq_ref, gkv_ref, wuq_ref, scw_ref, tab_ref, buf_ref, wuk_ref, wuvt_ref,
                    q_ref, lat_ref, kr_ref, osc_ref, nbuf_ref, k_ref, vt_ref, win_ref, *, tiles_per_seq):
    i = pl.program_id(0)
    nseq = win_ref.shape[0]

    @pl.when(i % tiles_per_seq == 0)
    def _():
        win_ref[...] = buf_ref[...]

    xb = x_ref[...].astype(BF16)
    tab = tab_ref[...]
    h1 = jnp.dot(xb, w1_ref[...], preferred_element_type=F32)
    cq = h1[:, :Q_RANK]
    ckv = h1[:, Q_RANK:Q_RANK + KV_RANK]
    kr_raw = h1[:, Q_RANK + KV_RANK:]
    lat = _rms_norm(ckv, gkv_ref[...])
    kr_pad = _rope(kr_raw, tab)
    lat_ref[...] = lat
    kr_ref[...] = kr_pad
    _keys_values(lat, kr_pad, wuk_ref, wuvt_ref, k_ref, vt_ref)
    q_raw =_dot(_rms_norm(cq, gq_ref[...]), wuq_ref[...])
    scale = (NOPE_DIM + ROPE_DIM) ** -0.5 * LOG2_E
    for h in range(MLA_HEADS):
        sl = slice(h * HEAD_PAD, (h + 1) * HEAD_PAD)
        q_ref[:, sl] = (_rope(q_raw[:, sl], tab) * scale).astype(BF16)

    h2 = jnp.dot(xb, w2_ref[...], preferred_element_type=F32)
    gate_b = h2[:, :SC_WIDTH]
    u_in = h2[:, SC_WIDTH:2 * SC_WIDTH] * h2[:, 2 * SC_WIDTH:]
    u = _causal_conv(u_in, win_ref[...], scw_ref[...])
    tails = _run_tails(u_in, nseq)
    win_ref[...] = tails
    nbuf_ref[...] = tails
    osc_ref[...] = (gate_b * u).astype(BF16)


def _even_in(x, w1, w2, g_q, g_kv, w_uq, sc_w, tab, buf8, w_uk, w_uv_t, *, seq, tm):
    m = x.shape[0]
    nb = m // seq
    tm, tps, nseq = _row_tiling(m, seq, tm)
    if nseq > 1:
        tab = jnp.tile(tab, (nseq, 1))
    const2 = lambda i: (0, 0)
    return pl.pallas_call(
        functools.partial(_even_in_kernel, tiles_per_seq=tps),
        grid=(m // tm,),
        in_specs=[
            pl.BlockSpec((tm, D_MODEL), lambda i: (i, 0)),
            pl.BlockSpec(w1.shape, const2),
            pl.BlockSpec(w2.shape, const2),
            pl.BlockSpec(g_q.shape, const2),
            pl.BlockSpec(g_kv.shape, const2),
            pl.BlockSpec(w_uq.shape, const2),
            pl.BlockSpec(sc_w.shape, const2),
            pl.BlockSpec((tm, 3 * LANES), lambda i: (i % tps, 0)),
            pl.BlockSpec((nseq, SUBLANES, SC_WIDTH), lambda i: (i // tps, 0, 0)),
            pl.BlockSpec(w_uk.shape, const2),
            pl.BlockSpec(w_uv_t.shape, const2),
        ],
        out_specs=[
            pl.BlockSpec((tm, MLA_HEADS * HEAD_PAD), lambda i: (i, 0)),
            pl.BlockSpec((tm, KV_RANK), lambda i: (i, 0)),
            pl.BlockSpec((tm, LANES), lambda i: (i, 0)),
            pl.BlockSpec((tm, SC_WIDTH), lambda i: (i, 0)),
            pl.BlockSpec((nseq, SUBLANES, SC_WIDTH), lambda i: (i // tps, 0, 0)),
            pl.BlockSpec((tm, MLA_HEADS * HEAD_PAD), lambda i: (i, 0)),
            pl.BlockSpec((MLA_WIDTH, tm), lambda i: (i, 0)),
        ],
        out_shape=[
            jax.ShapeDtypeStruct((m, MLA_HEADS * HEAD_PAD), BF16),
            jax.ShapeDtypeStruct((m, KV_RANK), F32),
            jax.ShapeDtypeStruct((m, LANES), F32),
            jax.ShapeDtypeStruct((m, SC_WIDTH), BF16),
            jax.ShapeDtypeStruct((nb, SUBLANES, SC_WIDTH), F32),
            jax.ShapeDtypeStruct((m, MLA_HEADS * HEAD_PAD), BF16),
            jax.ShapeDtypeStruct((m // tm * MLA_WIDTH, tm), BF16),
        ],
        scratch_shapes=[pltpu.VMEM((nseq, SUBLANES, SC_WIDTH), F32)],
        compiler_params=_params("arbitrary"),
        name="even_in",
    )(x, w1, w2, g_q, g_kv, w_uq, sc_w, tab, buf8, w_uk, w_uv_t)


def _kv_up_kernel(lat_ref, kr_ref, wuk_ref, wuvt_ref, k_ref, vt_ref):
    _keys_values(lat_ref[...], kr_ref[...], wuk_ref, wuvt_ref, k_ref, vt_ref)


def _kv_up(lat, kr_pad, w_uk, w_uv_t, *, tm):
    m = lat.shape[0]
    const2 = lambda i: (0, 0)
    return pl.pallas_call(
        _kv_up_kernel,
        grid=(m // tm,),
        in_specs=[
            pl.BlockSpec((tm, KV_RANK), lambda i: (i, 0)),
            pl.BlockSpec((tm, LANES), lambda i: (i, 0)),
            pl.BlockSpec(w_uk.shape, const2),
            pl.BlockSpec(w_uv_t.shape, const2),
        ],
        out_specs=[
            pl.BlockSpec((tm, MLA_HEADS * HEAD_PAD), lambda i: (i, 0)),
            pl.BlockSpec((MLA_WIDTH, tm), lambda i: (i, 0)),
        ],
        out_shape=[
            jax.ShapeDtypeStruct((m, MLA_HEADS * HEAD_PAD), BF16),
            jax.ShapeDtypeStruct((m // tm * MLA_WIDTH, tm), BF16),
        ],
        compiler_params=_params("arbitrary"),
        name="kv_up",
    )(lat, kr_pad, w_uk, w_uv_t)


def _attn_kernel(qi_tab, ka_tab, kb_tab, kb_ok_tab, q_ref, ka_ref, vta_ref, kb_ref, vtb_ref, o_ref,
                 m_ref, l_ref, acc_ref, *, tq, tk, past, t_valid, nk):
    step_id = pl.program_id(1)
    qi = qi_tab[step_id]
    q_lo = past + qi * tq
    chunk_lo = q_lo // CHUNK
    chunk_hi = (q_lo + tq - 1) // CHUNK
    last = jnp.minimum(((chunk_hi + 1) * CHUNK - 1) // tk, nk - 1)
    heads = list(range(MLA_HEADS))
    hsl = [slice(h * HEAD_PAD, (h + 1) * HEAD_PAD) for h in heads]
    vsl = [slice(h * V_DIM, (h + 1) * V_DIM) for h in heads]

    def visibility(kj):
        kpos = kj * tk + lax.broadcasted_iota(jnp.int32, (tk, tq), 0)
        qpos = q_lo + lax.broadcasted_iota(jnp.int32, (tk, tq), 1)
        return jnp.logical_and(kpos // CHUNK <= qpos // CHUNK, kpos < t_valid)

    def scores(k_ref, h, visible):
        s = lax.dot_general(k_ref[:, hsl[h]], q_ref[:, hsl[h]], NT_DIMS, preferred_element_type=F32)
        return s if visible is None else jnp.where(visible, s, NEG_INF)

    def weighted_values(vt_ref, h, p):
        ones = jnp.ones((BF16_SUBLANES, tk), BF16)
        return jnp.dot(jnp.concatenate([vt_ref[vsl[h], :], ones], axis=0), p, preferred_element_type=F32)

    def step(k_ref, vt_ref, visible):
        for g0 in range(0, MLA_HEADS, ATTN_HEAD_GROUP):
            grp = range(g0, g0 + ATTN_HEAD_GROUP)
            s = {h: scores(k_ref, h, visible) for h in grp}
            m_prev = {h: m_ref[h:h + 1, :] for h in grp}
            m_new = {h: jnp.maximum(m_prev[h], jnp.max(s[h], axis=0, keepdims=True)) for h in grp}
            alpha = {h: jnp.exp2(m_prev[h] - m_new[h]) for h in grp}
            p = {h: jnp.exp2(s[h] - m_new[h]).astype(BF16) for h in grp}
            pv = {h: weighted_values(vt_ref, h, p[h]) for h in grp}
            for h in grp:
                l_ref[h:h + 1, :] = alpha[h] * l_ref[h:h + 1, :] + pv[h][V_DIM:V_DIM + 1, :]
                m_ref[h:h + 1, :] = m_new[h]
                acc_ref[vsl[h], :] = alpha[h] * acc_ref[vsl[h], :] + pv[h][:V_DIM, :]

    def fast_step(k_ref, vt_ref, kj, first, masked):
        visible = visibility(kj) if masked else None
        pv, t_max = {}, {}
        for g0 in range(0, MLA_HEADS, FAST_HEAD_GROUP):
            grp = range(g0, g0 + FAST_HEAD_GROUP)
            s = {h: scores(k_ref, h, visible) for h in grp}
            p = {h: jnp.exp2(s[h] - m_ref[h:h + 1, :]).astype(BF16) for h in grp}
            for h in grp:
                t_max[h] = jnp.max(s[h], axis=0, keepdims=True)
            for h in grp:
                pv[h] = weighted_values(vt_ref, h, p[h])
        rise = t_max[0] - m_ref[0:1, :]
        fall = rise
        for h in heads[1:]:
            d = t_max[h] - m_ref[h:h + 1, :]
            rise = jnp.maximum(rise, d)
            fall = jnp.minimum(fall, d)
        hi = jnp.max(rise, axis=1, keepdims=True)
        lo = jnp.min(fall, axis=1, keepdims=True)
        ok = jnp.logical_and(hi <= FAST_SOFTMAX_MAX_RISE,
                             jnp.logical_or(jnp.logical_not(first), lo >= -FAST_SOFTMAX_MAX_RISE))
        for h in heads:
            m_prev = m_ref[h:h + 1, :]
            m_new = jnp.where(first, t_max[h], jnp.maximum(m_prev, t_max[h]))
            alpha = jnp.exp2(m_prev - m_new)
            l_prev = l_ref[h:h + 1, :]
            acc_prev = acc_ref[vsl[h], :]
            l_ref[h:h + 1, :] = jnp.where(ok, (l_prev + pv[h][V_DIM:V_DIM + 1, :]) * alpha, l_prev)
            acc_ref[vsl[h], :] = jnp.where(ok, (acc_prev + pv[h][:V_DIM, :]) * alpha, acc_prev)
            m_ref[h:h + 1, :] = jnp.where(ok, m_new, m_prev)
        in_range = jnp.logical_and(jnp.max(hi) <= FAST_SOFTMAX_MAX_RISE,
                                   jnp.logical_or(jnp.logical_not(first),
                                                  jnp.min(lo) >= -FAST_SOFTMAX_MAX_RISE))

        @pl.when(jnp.logical_not(in_range))
        def _():
            @pl.when(first)
            def _():
                m_ref[...] = jnp.full_like(m_ref, NEG_INF)
            step(k_ref, vt_ref, visible)

    ka = ka_tab[step_id]
    kb = kb_tab[step_id]
    first = ka == 0

    @pl.when(first)
    def _():
        m_ref[...] = jnp.zeros_like(m_ref)
        l_ref[...] = jnp.zeros_like(l_ref)
        acc_ref[...] = jnp.zeros_like(acc_ref)

    a_end = (ka + 1) * tk
    full_a = jnp.logical_and(a_end <= (chunk_lo + 1) * CHUNK, a_end <= t_valid)

    @pl.when(full_a)
    def _():
        fast_step(ka_ref, vta_ref, ka, first, False)

    @pl.when(jnp.logical_not(full_a))
    def _():
        fast_step(ka_ref, vta_ref, ka, first, True)

    @pl.when(kb_ok_tab[step_id] != 0)
    def _():
        fast_step(kb_ref, vtb_ref, kb, kb < 0, False)

    @pl.when(ka == last)
    def _():
        for hp in range(MLA_HEADS // 2):
            pair = jnp.concatenate(
                [acc_ref[vsl[h], :] * (1.0 / l_ref[h:h + 1, :]) for h in (2 * hp, 2 * hp + 1)], axis=0)
            o_ref[:, hp * 2 * V_DIM:(hp + 1) * 2 * V_DIM] = pair.T.astype(BF16)


def _attention(q, k, vt, *, nb, sq, t_pad, t_valid, past, tq, tk):
    nq = sq // tq
    nk = t_pad // tk
    steps = []
    for qi in range(nq):
        q_lo = past + qi * tq
        last = min((((q_lo + tq - 1) // CHUNK + 1) * CHUNK - 1) // tk, nk - 1)
        fully = [kj for kj in range(last)
                 if (kj + 1) * tk <= (q_lo // CHUNK + 1) * CHUNK and (kj + 1) * tk <= t_valid]
        partly = [kj for kj in range(last + 1) if kj not in fully]
        for j in range(0, len(fully), 2):
            pair = fully[j:j + 2]
            steps.append((qi, pair[0], pair[-1], len(pair) == 2))
        steps += [(qi, kj, kj, False) for kj in partly]
    tabs = [jnp.asarray([s[c] for s in steps], jnp.int32) for c in range(4)]
    q_index = lambda b, p, qt, at, bt, okt: (b * nq + qt[p], 0)
    a_index = lambda b, p, qt, at, bt, okt: (b * nk + at[p], 0)
    b_index = lambda b, p, qt, at, bt, okt: (b * nk + bt[p], 0)
    grid_spec = pltpu.PrefetchScalarGridSpec(
        num_scalar_prefetch=4,
        grid=(nb, len(steps)),
        in_specs=[
            pl.BlockSpec((tq, MLA_HEADS * HEAD_PAD), q_index),
            pl.BlockSpec((tk, MLA_HEADS * HEAD_PAD), a_index),
            pl.BlockSpec((MLA_WIDTH, tk), a_index),
            pl.BlockSpec((tk, MLA_HEADS * HEAD_PAD), b_index),
            pl.BlockSpec((MLA_WIDTH, tk), b_index),
        ],
        out_specs=pl.BlockSpec((tq, MLA_WIDTH), q_index),
        scratch_shapes=[
            pltpu.VMEM((MLA_HEADS, tq), F32),
            pltpu.VMEM((MLA_HEADS, tq), F32),
            pltpu.VMEM((MLA_WIDTH, tq), F32),
        ],
    )
    return pl.pallas_call(
        functools.partial(_attn_kernel, tq=tq, tk=tk, past=past, t_valid=t_valid, nk=nk),
        grid_spec=grid_spec,
        out_shape=jax.ShapeDtypeStruct((nb * sq, MLA_WIDTH), BF16),
        compiler_params=_params("arbitrary", "arbitrary"),
        name="chunk_attention",
    )(*tabs, q, k, vt, k, vt)


def _odd_in_kernel(x_ref, wqkv_ref, wz_ref, wab_ref, dcw_ref, ab_ref, buf_ref,
                   q_ref, k_ref, v_ref, z_ref, gb_ref, nbuf_ref, win_ref, *, tiles_per_seq):
    i = pl.program_id(0)
    nseq = win_ref.shape[0]

    @pl.when(i % tiles_per_seq == 0)
    def _():
        win_ref[...] = buf_ref[...]

    xb = x_ref[...].astype(BF16)
    outs = (q_ref, k_ref, v_ref)
    for part in range(3):
        cols = slice(part * DN_KEY, (part + 1) * DN_KEY)
        pre = jnp.dot(xb, wqkv_ref[part], preferred_element_type=F32)
        y = _causal_conv(pre, win_ref[:, :, cols], dcw_ref[:, cols])
        tails = _run_tails(pre, nseq)
        win_ref[:, :, cols] = tails
        nbuf_ref[:, :, cols] = tails
        y = _silu(y)
        if part == 2:
            outs[part][...] = y
        else:
            post = DN_DK ** -0.5 if part == 0 else 1.0
            for h in range(DN_HEADS):
                sl = slice(h * DN_DK, (h + 1) * DN_DK)
                seg = y[:, sl]
                nrm = lax.rsqrt(jnp.sum(seg * seg, axis=-1, keepdims=True) + NORM_EPS)
                outs[part][:, sl] = seg * nrm * post if part == 0 else seg * nrm

    z_ref[...] = jnp.dot(xb, wz_ref[...], preferred_element_type=F32)
    ab = jnp.dot(xb, wab_ref[...], preferred_element_type=F32)
    lane = lax.broadcasted_iota(jnp.int32, ab.shape, 1)
    g = -jnp.exp(ab_ref[0:1, :]) * _softplus(ab + ab_ref[1:2, :])
    gb_ref[...] = jnp.where(lane < DN_HEADS, g, jax.nn.sigmoid(ab))


def _odd_in(x, w_qkv, w_z, w_ab, dconv_w, ab_const, buf8, *, seq, tm):
    m = x.shape[0]
    nb = m // seq
    tm, tps, nseq = _row_tiling(m, seq, tm)
    const2 = lambda i: (0, 0)
    row = lambda i: (i, 0)
    return pl.pallas_call(
        functools.partial(_odd_in_kernel, tiles_per_seq=tps),
        grid=(m // tm,),
        in_specs=[
            pl.BlockSpec((tm, D_MODEL), row),
            pl.BlockSpec(w_qkv.shape, lambda i: (0, 0, 0)),
            pl.BlockSpec(w_z.shape, const2),
            pl.BlockSpec(w_ab.shape, const2),
            pl.BlockSpec(dconv_w.shape, const2),
            pl.BlockSpec(ab_const.shape, const2),
            pl.BlockSpec((nseq, SUBLANES, DN_QKV), lambda i: (i // tps, 0, 0)),
        ],
        out_specs=[
            pl.BlockSpec((tm, DN_KEY), row),
            pl.BlockSpec((tm, DN_KEY), row),
            pl.BlockSpec((tm, DN_VAL), row),
            pl.BlockSpec((tm, DN_VAL), row),
            pl.BlockSpec((tm, LANES), row),
            pl.BlockSpec((nseq, SUBLANES, DN_QKV), lambda i: (i // tps, 0, 0)),
        ],
        out_shape=[
            jax.ShapeDtypeStruct((m, DN_KEY), F32),
            jax.ShapeDtypeStruct((m, DN_KEY), F32),
            jax.ShapeDtypeStruct((m, DN_VAL), F32),
            jax.ShapeDtypeStruct((m, DN_VAL), F32),
            jax.ShapeDtypeStruct((m, LANES), F32),
            jax.ShapeDtypeStruct((nb, SUBLANES, DN_QKV), F32),
        ],
        scratch_shapes=[pltpu.VMEM((nseq, SUBLANES, DN_QKV), F32)],
        compiler_params=_params("arbitrary"),
        name="odd_in",
    )(x, w_qkv, w_z, w_ab, dconv_w, ab_const, buf8)


def _delta_kernel(q_ref, k_ref, v_ref, z_ref, gb_ref, s0_ref, go_ref, o_ref, sout_ref, state_ref, *, n_steps):
    n = pl.program_id(1)
    n_par, rows = q_ref.shape[0], q_ref.shape[1]
    n_sub = rows // CHUNK

    @pl.when(n == 0)
    def _():
        state_ref[...] = s0_ref[...]

    ri = lax.broadcasted_iota(jnp.int32, (rows, rows), 0)
    ci = lax.broadcasted_iota(jnp.int32, (rows, rows), 1)

    def blk(size):
        return (ri // size) == (ci // size)

    same = blk(CHUNK)
    lower = ri > ci
    tril = jnp.logical_and(same, ri >= ci)
    strict = jnp.logical_and(same, lower)
    eye = (ri == ci).astype(F32)
    base = DELTA_BASE_BLOCK
    levels = []
    size = base
    while size < CHUNK:
        levels.append(jnp.logical_and(jnp.logical_and(blk(2 * size), jnp.logical_not(blk(size))), lower))
        size *= 2
    in_base = jnp.logical_and(blk(base), lower)
    gb = [gb_ref[b] for b in range(n_par)]
    g_col = [jnp.dot(tril.astype(F32), gb[b], preferred_element_type=F32, precision=HIGHEST)
             for b in range(n_par)]
    g_end = [jnp.dot(same.astype(F32), gb[b], preferred_element_type=F32, precision=HIGHEST)
             for b in range(n_par)]
    g_row = [g_col[b].T for b in range(n_par)]
    go = go_ref[...]
    chain = [(b, hd) for b in range(n_par) for hd in range(DN_HEADS)]
    heads = range(len(chain))
    col = [slice(hd * DN_DK, (hd + 1) * DN_DK) for _, hd in chain]
    seq_of = [b for b, _ in chain]
    head_of = [hd for _, hd in chain]

    def tile(ref, h):
        return ref[seq_of[h], :, col[h]]

    gc = [g_col[seq_of[h]][:, head_of[h]:head_of[h] + 1] for h in heads]
    ge = [g_end[seq_of[h]][:, head_of[h]:head_of[h] + 1] for h in heads]
    decay = [jnp.exp(jnp.where(tril, gc[h] - g_row[seq_of[h]][head_of[h]:head_of[h] + 1, :], -jnp.inf))
             for h in heads]
    beta = [gb[seq_of[h]][:, DN_HEADS + head_of[h]:DN_HEADS + head_of[h] + 1] for h in heads]
    kb = [tile(k_ref, h) * beta[h] for h in heads]
    kbf = [tile(k_ref, h).astype(BF16) for h in heads]
    m_full = [jnp.where(strict, lax.dot_general(kb[h].astype(BF16), kbf[h], NT_DIMS,
                                                preferred_element_type=F32) * decay[h], 0.0) for h in heads]
    p = [-jnp.where(in_base, m_full[h], 0.0) for h in heads]
    t = [eye + p[h] for h in heads]
    size = 2
    while size < base:
        p = [_dot(p[h], p[h]) for h in heads]
        t = [t[h] + _dot(t[h], p[h]) for h in heads]
        size *= 2
    size = base
    for off in levels:
        nblk = rows // (2 * size)
        t4 = [t[h].reshape(nblk, 2, size, rows) for h in heads]
        t_low = [t4[h][:, 1].reshape(rows // 2, rows) for h in heads]
        u = [_dot(t_low[h], jnp.where(off, m_full[h], 0.0)) for h in heads]
        t_low = [(t_low[h] - _dot(u[h], t[h])).reshape(nblk, 1, size, rows) for h in heads]
        t = [jnp.concatenate([t4[h][:, 0:1], t_low[h]], axis=1).reshape(rows, rows) for h in heads]
        size *= 2
    e_g = [jnp.exp(gc[h]) for h in heads]
    sol = [_dot(t[h], jnp.concatenate([tile(v_ref, h) * beta[h], kb[h] * e_g[h]], axis=1)) for h in heads]
    attn = [lax.dot_general(tile(q_ref, h).astype(BF16), kbf[h], NT_DIMS,
                            preferred_element_type=F32) * decay[h] for h in heads]
    qd = [tile(q_ref, h) * e_g[h] for h in heads]
    kt = [tile(k_ref, h) * jnp.exp(ge[h] - gc[h]) for h in heads]
    st = [state_ref[seq_of[h], head_of[h]] for h in heads]
    v_new = [[] for _ in heads]
    o_inter = [[] for _ in heads]
    for c in range(n_sub):
        rs = slice(c * CHUNK, (c + 1) * CHUNK)
        for h in heads:
            r = _dot(jnp.concatenate([sol[h][rs, DN_DV:], qd[h][rs]], axis=0), st[h])
            v_c = sol[h][rs, :DN_DV] - r[:CHUNK]
            v_new[h].append(v_c)
            o_inter[h].append(r[CHUNK:])
            g_tot = jnp.exp(ge[h][c * CHUNK:c * CHUNK + 1, :])
            st[h] = st[h] * g_tot + lax.dot_general(kt[h][rs].astype(BF16), v_c.astype(BF16), TN_DIMS,
                                                    preferred_element_type=F32)
    for h in heads:
        state_ref[seq_of[h], head_of[h]] = st[h]
        o = jnp.concatenate(o_inter[h], axis=0) + _dot(attn[h], jnp.concatenate(v_new[h], axis=0))
        zh = tile(z_ref, h)
        o_ref[seq_of[h], :, col[h]] = (_rms_norm(o, go) * _silu(zh)).astype(BF16)

    @pl.when(n == n_steps - 1)
    def _():
        sout_ref[...] = state_ref[...]


def _delta_rule(q, k, v, z, gb, s0, g_o, *, seq, rows, n_par):
    m = q.shape[0]
    nb = m // seq
    rows = min(rows, seq)
    ns = seq // rows
    q, k, v, z, gb = (a.reshape(nb, seq, a.shape[-1]) for a in (q, k, v, z, gb))
    row = lambda b, n: (b, n, 0)
    st_spec = pl.BlockSpec((n_par, DN_HEADS, DN_DK, DN_DV), lambda b, n: (b, 0, 0, 0))
    og, s_new = pl.pallas_call(
        functools.partial(_delta_kernel, n_steps=ns),
        grid=(nb // n_par, ns),
        in_specs=[
            pl.BlockSpec((n_par, rows, DN_KEY), row),
            pl.BlockSpec((n_par, rows, DN_KEY), row),
            pl.BlockSpec((n_par, rows, DN_VAL), row),
            pl.BlockSpec((n_par, rows, DN_VAL), row),
            pl.BlockSpec((n_par, rows, LANES), row),
            st_spec,
            pl.BlockSpec((1, DN_DV), lambda b, n: (0, 0)),
        ],
        out_specs=[pl.BlockSpec((n_par, rows, DN_VAL), row), st_spec],
        out_shape=[
            jax.ShapeDtypeStruct((nb, seq, DN_VAL), BF16),
            jax.ShapeDtypeStruct((nb, DN_HEADS, DN_DK, DN_DV), F32),
        ],
        scratch_shapes=[pltpu.VMEM((n_par, DN_HEADS, DN_DK, DN_DV), F32)],
        compiler_params=_params("arbitrary", "arbitrary"),
        name="gated_delta",
    )(q, k, v, z, gb, s0, g_o)
    return og.reshape(m, DN_VAL), s_new


def _rope_table(pos):
    half = ROPE_DIM // 2
    inv = ROPE_THETA ** (-jnp.arange(half, dtype=F32) / half)
    ang = pos.astype(F32)[:, None] * inv[None, :]
    cs = jnp.concatenate([jnp.cos(ang), jnp.sin(ang)], -1)
    place = jnp.zeros((2 * half, 3 * LANES), F32)
    j = jnp.arange(half)
    place = place.at[j, NOPE_DIM + j].set(1.0).at[j, NOPE_DIM + half + j].set(1.0)
    place = place.at[half + j, LANES + NOPE_DIM + half + j].set(1.0)
    place = place.at[half + j, 2 * LANES + NOPE_DIM + j].set(-1.0)
    ones = (jnp.arange(3 * LANES) < NOPE_DIM).astype(F32)
    return jnp.dot(cs, place, precision=HIGHEST) + ones[None, :]


def _pad_heads(w, per_head, keep):
    kdim = w.shape[0]
    w = w.reshape(kdim, MLA_HEADS, per_head)[:, :, :keep]
    w = jnp.pad(w, ((0, 0), (0, 0), (0, HEAD_PAD - keep)))
    return w.reshape(kdim, MLA_HEADS * HEAD_PAD)


def _prep_even(w_in, w_uq, w_ukv):
    c1 = Q_RANK + KV_RANK
    w_kr = jnp.pad(w_in[:, c1:c1 + ROPE_DIM], ((0, 0), (NOPE_DIM, HEAD_PAD - NOPE_DIM - ROPE_DIM)))
    w1 = jnp.concatenate([w_in[:, :c1], w_kr], -1).astype(BF16)
    w2 = w_in[:, c1 + ROPE_DIM:].astype(BF16)
    wuq = _pad_heads(w_uq, NOPE_DIM + ROPE_DIM, NOPE_DIM + ROPE_DIM).astype(BF16)
    wuk = _pad_heads(w_ukv, NOPE_DIM + V_DIM, NOPE_DIM).astype(BF16)
    wuv = w_ukv.reshape(KV_RANK, MLA_HEADS, NOPE_DIM + V_DIM)[:, :, NOPE_DIM:].reshape(KV_RANK, MLA_WIDTH)
    return w1, w2, wuq, wuk, wuv.T.astype(BF16)


def _prep_odd(w_in, a_log, dt_bias):
    w_qkv = w_in[:, :DN_QKV].reshape(D_MODEL, 3, DN_KEY).transpose(1, 0, 2).astype(BF16)
    w_z = w_in[:, DN_QKV:DN_QKV + DN_VAL].astype(BF16)
    w_ab = jnp.pad(w_in[:, DN_QKV + DN_VAL:], ((0, 0), (0, LANES - 2 * DN_HEADS))).astype(BF16)
    ab_const = jnp.zeros((SUBLANES, LANES), F32)
    ab_const = ab_const.at[0, :DN_HEADS].set(a_log.astype(F32)).at[1, :DN_HEADS].set(dt_bias.astype(F32))
    return w_qkv, w_z, w_ab, ab_const


def _prep_ffn(w_in, conv_w, w_out):
    return w_in.astype(BF16), w_out.astype(BF16), conv_w


def _pad_rows8(buf):
    return jnp.pad(buf.astype(F32), ((0, 0), (SUBLANES - buf.shape[1], 0), (0, 0)))


def _trunk(x, past_lat, past_kr, sc_buf, dconv_buf, delta_s, ff_buf, w, *, tm, tq, tk, dr):
    (even_w, odd_w, ffn_w, w_o_e, w_o_o, g_qnorm, g_kvnorm, sc_w, dconv_w, g_onorm,
     ln_mix_g, ln_mix_b, ln_ff_g, ln_ff_b) = w
    nb, seq, _ = x.shape
    past = 0 if past_lat is None else past_lat.shape[2]
    m = nb * seq
    x = x.reshape(m, D_MODEL)
    tab = _rope_table(past + jnp.arange(seq, dtype=jnp.int32))
    t_valid = past + seq
    tq = min(tq, seq)
    tk = min(tk, t_valid)
    t_pad = -(-t_valid // tk) * tk
    lats, krs, scs, dcs, dss, ffs = [], [], [], [], [], []
    for i in range(DEPTH):
        row = lambda a: a[i].reshape(1, -1)
        if i % 2 == 0:
            e = i // 2
            w1, w2, wuq, wuk, wuv = even_w[e]
            q, lat, kr_pad, o_sc, sc_new, k, v = _even_in(
                x, w1, w2, g_qnorm[e].reshape(1, -1), g_kvnorm[e].reshape(1, -1), wuq, sc_w[e], tab,
                _pad_rows8(sc_buf[e]), wuk, wuv, seq=seq, tm=tm)
            if past or t_pad != t_valid or tk != min(tm, seq):
                lat_all = lat.reshape(nb, seq, KV_RANK)
                kr_all = kr_pad.reshape(nb, seq, LANES)
                if past:
                    kr_past = jnp.pad(past_kr[e].astype(F32),
                                      ((0, 0), (0, 0), (NOPE_DIM, HEAD_PAD - NOPE_DIM - ROPE_DIM)))
                    lat_all = jnp.concatenate([past_lat[e].astype(F32), lat_all], 1)
                    kr_all = jnp.concatenate([kr_past, kr_all], 1)
                if t_pad != t_valid:
                    lat_all = jnp.pad(lat_all, ((0, 0), (0, t_pad - t_valid), (0, 0)))
                    kr_all = jnp.pad(kr_all, ((0, 0), (0, t_pad - t_valid), (0, 0)))
                k, v = _kv_up(lat_all.reshape(nb * t_pad, KV_RANK), kr_all.reshape(nb * t_pad, LANES),
                              wuk, wuv, tm=tk)
            o_att = _attention(q, k, v, nb=nb, sq=seq, t_pad=t_pad, t_valid=t_valid, past=past, tq=tq, tk=tk)
            mix, mix_w = [o_att, o_sc], [w_o_e[e][:MLA_WIDTH], w_o_e[e][MLA_WIDTH:]]
            lats.append(lat.reshape(nb, seq, KV_RANK))
            krs.append(kr_pad.reshape(nb, seq, LANES)[:, :, NOPE_DIM:NOPE_DIM + ROPE_DIM])
            scs.append(sc_new[:, SUBLANES - (SC_K - 1):])
        else:
            o = i // 2
            w_qkv, w_z, w_ab, ab_const = odd_w[o]
            q, k, v, z, gb, dc_new = _odd_in(x, w_qkv, w_z, w_ab, dconv_w[o], ab_const,
                                             _pad_rows8(dconv_buf[o]), seq=seq, tm=tm)
            og, s_new = _delta_rule(q, k, v, z, gb, delta_s[o].astype(F32), g_onorm[o].reshape(1, -1), seq=seq,
                                    rows=dr, n_par=DELTA_SEQS_PER_STEP)
            mix, mix_w = [og], [w_o_o[o]]
            dcs.append(dc_new[:, SUBLANES - (DN_CONV - 1):])
            dss.append(s_new)
        w_ff_in, w_ff_out, cw = ffn_w[i]
        x, ff_new = _mix_ffn(mix, mix_w, x, row(ln_mix_g), row(ln_mix_b), _pad_rows8(ff_buf[i]),
                             w_ff_in, w_ff_out, cw, row(ln_ff_g), row(ln_ff_b), seq=seq, tm=tm)
        ffs.append(ff_new[:, SUBLANES - (FF_K - 1):])
    return (x.reshape(nb, seq, D_MODEL), jnp.stack(lats), jnp.stack(krs), jnp.stack(scs), jnp.stack(dcs),
            jnp.stack(dss), jnp.stack(ffs))


def _prepare(w_in_e, w_uq, w_ukv, w_o_e, w_in_o, a_log, dt_bias, w_o_o, w_ff_in, ffconv_w, w_ff_out):
    even_w = [_prep_even(w_in_e[e], w_uq[e], w_ukv[e]) for e in range(N_EVEN)]
    odd_w = [_prep_odd(w_in_o[o], a_log[o], dt_bias[o]) for o in range(N_ODD)]
    ffn_w = [_prep_ffn(w_ff_in[i], ffconv_w[i], w_ff_out[i]) for i in range(DEPTH)]
    return even_w, odd_w, ffn_w, w_o_e.astype(BF16), w_o_o.astype(BF16)


def kernel(x_prompt, x_sample, cache_mla_latent, cache_mla_krope, state_sconv, state_dconv, state_delta, state_ffconv, w_in_e, g_qnorm, g_kvnorm, w_uq, w_ukv, sc_w, w_o_e, w_in_o, dconv_w, a_log, dt_bias, g_onorm, w_o_o, w_ff_in, ffconv_w, w_ff_out, ln_mix_g, ln_mix_b, ln_ff_g, ln_ff_b):
    even_w, odd_w, ffn_w, w_o_e_b, w_o_o_b = _prepare(
        w_in_e, w_uq, w_ukv, w_o_e, w_in_o, a_log, dt_bias, w_o_o, w_ff_in, ffconv_w, w_ff_out)
    w = (even_w, odd_w, ffn_w, w_o_e_b, w_o_o_b, g_qnorm, g_kvnorm, sc_w, dconv_w, g_onorm,
         ln_mix_g, ln_mix_b, ln_ff_g, ln_ff_b)
    bp = x_prompt.shape[0]
    f32 = x_prompt.dtype
    y_p, p_lat, p_kr, p_sc, p_dc, p_ds, p_ff = _trunk(
        x_prompt, None, None,
        jnp.zeros((N_EVEN, bp, SC_K - 1, SC_WIDTH), f32),
        jnp.zeros((N_ODD, bp, DN_CONV - 1, DN_QKV), f32),
        jnp.zeros((N_ODD, bp, DN_HEADS, DN_DK, DN_DV), f32),
        jnp.zeros((DEPTH, bp, FF_K - 1, D_FF), f32),
        w, tm=512, tq=512, tk=512, dr=128)
    y_s, s_lat, s_kr, s_sc, s_dc, s_ds, s_ff = _trunk(
        x_sample, cache_mla_latent, cache_mla_krope, state_sconv, state_dconv, state_delta, state_ffconv,
        w, tm=512, tq=64, tk=2304, dr=256)
    return (y_p, y_s, p_lat, p_kr, p_sc, p_dc, p_ds, p_ff, s_lat, s_kr, s_sc, s_dc, s_ds, s_ff)
```

```python
import functools
import math

import jax
import jax.numpy as jnp
from jax import lax
from jax.experimental import pallas as pl
from jax.experimental.pallas import tpu as pltpu

D_MODEL = 1024
DEPTH = 4
CHUNK = 64
N_EVEN = (DEPTH + 1) // 2
N_ODD = DEPTH // 2
MLA_HEADS = 8
Q_RANK = 384
KV_RANK = 256
NOPE_DIM = 64
ROPE_DIM = 32
V_DIM = 64
ROPE_THETA = 10000.0
MLA_WIDTH = MLA_HEADS * V_DIM
SC_WIDTH = 512
SC_K = 3
DN_HEADS = 8
DN_DK = 128
DN_DV = 128
DN_CONV = 4
DN_KEY = DN_HEADS * DN_DK
DN_VAL = DN_HEADS * DN_DV
DN_QKV = 2 * DN_KEY + DN_VAL
D_FF = 2816
FF_K = 3
ALPHA = (2 * DEPTH) ** 0.25
NORM_EPS = 1e-6
NEG_INF = -1e30
LOG2_E = math.log2(math.e)

F32 = jnp.float32
BF16 = jnp.bfloat16
HIGHEST = lax.Precision.HIGHEST

LANES = 128
SUBLANES = 8
BF16_SUBLANES = 16
HEAD_PAD = 128
FF_CHUNK = 256
DELTA_BASE_BLOCK = 8
DELTA_SEQS_PER_STEP = 4
ATTN_HEAD_GROUP = 8
FAST_HEAD_GROUP = 8
FAST_SOFTMAX_MAX_RISE = 60.0
VMEM_LIMIT = 56 * 1024 * 1024

NT_DIMS = (((1,), (1,)), ((), ()))
TN_DIMS = (((0,), (0,)), ((), ()))


def _params(*sem):
    return pltpu.CompilerParams(dimension_semantics=sem, vmem_limit_bytes=VMEM_LIMIT)


def _dot(a, b):
    return jnp.dot(a.astype(BF16), b.astype(BF16), preferred_element_type=F32)


def _layer_norm(r, g, b):
    mu = jnp.mean(r, axis=-1, keepdims=True)
    d = r - mu
    var = jnp.mean(d * d, axis=-1, keepdims=True)
    return d * lax.rsqrt(var + NORM_EPS) * g + b


def _rms_norm(x, g):
    return x * lax.rsqrt(jnp.mean(x * x, axis=-1, keepdims=True) + NORM_EPS) * g


def _silu(x):
    h = 0.5 * x
    return h + h * jnp.tanh(h)


def _softplus(x):
    return jnp.maximum(x, 0.0) + jnp.log1p(jnp.exp(-jnp.abs(x)))


def _causal_conv(x, prev, w):
    tm, c = x.shape
    nseq = prev.shape[0]
    k_taps = w.shape[0]
    groups = tm // nseq // SUBLANES
    x4 = x.reshape(nseq, groups, SUBLANES, c)
    prev4 = prev.reshape(nseq, 1, SUBLANES, c)
    sub = lax.broadcasted_iota(jnp.int32, x4.shape, 2)
    y = x4 * w[k_taps - 1:k_taps, :]
    for k in range(1, k_taps):
        rot = pltpu.roll(x4, k, axis=2)
        rot_prev = jnp.concatenate([pltpu.roll(prev4, k, axis=2), rot[:, :groups - 1]], axis=1)
        y = y + jnp.where(sub < k, rot_prev, rot) * w[k_taps - 1 - k:k_taps - k, :]
    return y.reshape(tm, c)


def _run_tails(x, nseq):
    tm, c = x.shape
    run = tm // nseq
    return x.reshape(nseq, run, c)[:, run - SUBLANES:, :]


def _row_tiling(m, seq, tm):
    tm = min(tm, m)
    if seq >= tm:
        return tm, seq // tm, 1
    return tm, 1, tm // seq


def _mix_ffn_kernel(*refs, n_mix, tiles_per_seq):
    ys, wos = refs[:n_mix], refs[n_mix:2 * n_mix]
    (x_ref, g1_ref, b1_ref, w_in_ref, w_out_ref, cw_ref, buf_ref, g2_ref, b2_ref,
     out_ref, nbuf_ref, act_ref, x1_ref, xb_ref, win_ref) = refs[2 * n_mix:]
    i = pl.program_id(0)
    nseq = win_ref.shape[0]

    tm = x_ref.shape[0]
    halves = [slice(0, tm // 2), slice(tm // 2, tm)]
    seq_half = [slice(0, nseq // 2), slice(nseq // 2, nseq)]

    for rows in halves:
        mix = jnp.dot(ys[0][rows, :], wos[0][...], preferred_element_type=F32)
        for j in range(1, n_mix):
            mix = mix + jnp.dot(ys[j][rows, :], wos[j][...], preferred_element_type=F32)
        x1 = _layer_norm(ALPHA * x_ref[rows, :] + mix, g1_ref[...], b1_ref[...])
        x1_ref[rows, :] = x1
        xb_ref[rows, :] = x1.astype(BF16)

    @pl.when(i % tiles_per_seq == 0)
    def _():
        win_ref[...] = buf_ref[...]

    for c in range(D_FF // FF_CHUNK):
        cols = slice(c * FF_CHUNK, (c + 1) * FF_CHUNK)
        up_cols = slice(D_FF + c * FF_CHUNK, D_FF + (c + 1) * FF_CHUNK)
        prev = win_ref[:, :, cols]
        tails = []
        for hf, rows in enumerate(halves):
            xb = xb_ref[rows, :]
            gate = jnp.dot(xb, w_in_ref[:, cols], preferred_element_type=F32)
            up = jnp.dot(xb, w_in_ref[:, up_cols], preferred_element_type=F32)
            if nseq == 1:
                y = _causal_conv(gate, prev if hf == 0 else tails[0], cw_ref[:, cols])
                tails.append(_run_tails(gate, 1))
            else:
                y = _causal_conv(gate, prev[seq_half[hf]], cw_ref[:, cols])
                tails.append(_run_tails(gate, nseq // 2))
            act_ref[rows, cols] = (_silu(y) * up).astype(BF16)
        tails = tails[1] if nseq == 1 else jnp.concatenate(tails, axis=0)
        win_ref[:, :, cols] = tails
        nbuf_ref[:, :, cols] = tails

    for rows in halves:
        acc = jnp.dot(act_ref[rows, :], w_out_ref[...], preferred_element_type=F32)
        out_ref[rows, :] = _layer_norm(ALPHA * x1_ref[rows, :] + acc, g2_ref[...], b2_ref[...])


def _mix_ffn(ys, wos, x, mix_g, mix_b, buf8, w_in, w_out, conv_w, ff_g, ff_b, *, seq, tm):
    m = x.shape[0]
    nb = m // seq
    tm, tps, nseq = _row_tiling(m, seq, tm)
    const2 = lambda i: (0, 0)
    row = lambda i: (i, 0)
    resident = lambda a: pl.BlockSpec(a.shape, const2, pipeline_mode=pl.Buffered(1))
    vec = pl.BlockSpec((1, D_MODEL), const2)
    state = pl.BlockSpec((nseq, SUBLANES, D_FF), lambda i: (i // tps, 0, 0))
    return pl.pallas_call(
        functools.partial(_mix_ffn_kernel, n_mix=len(ys), tiles_per_seq=tps),
        grid=(m // tm,),
        in_specs=([pl.BlockSpec((tm, y.shape[1]), row) for y in ys]
                  + [resident(w) for w in wos]
                  + [pl.BlockSpec((tm, D_MODEL), row), vec, vec,
                     resident(w_in), resident(w_out), pl.BlockSpec(conv_w.shape, const2), state, vec, vec]),
        out_specs=[pl.BlockSpec((tm, D_MODEL), row), state],
        out_shape=[
            jax.ShapeDtypeStruct((m, D_MODEL), F32),
            jax.ShapeDtypeStruct((nb, SUBLANES, D_FF), F32),
        ],
        scratch_shapes=[
            pltpu.VMEM((tm, D_FF), BF16),
            pltpu.VMEM((tm, D_MODEL), F32),
            pltpu.VMEM((tm, D_MODEL), BF16),
            pltpu.VMEM((nseq, SUBLANES, D_FF), F32),
        ],
        compiler_params=_params("arbitrary"),
        name="mix_ffn",
    )(*ys, *wos, x, mix_g, mix_b, w_in, w_out, conv_w, buf8, ff_g, ff_b)


def _rope(x, tab):
    return (x * tab[:, 0:LANES]
            + pltpu.roll(x, ROPE_DIM // 2, axis=1) * tab[:, LANES:2 * LANES]
            + pltpu.roll(x, LANES - ROPE_DIM // 2, axis=1) * tab[:, 2 * LANES:3 * LANES])


def _keys_values(lat, kr_pad, wuk_ref, wuvt_ref, k_ref, vt_ref):
    lb = lat.astype(BF16)
    kn = jnp.dot(lb, wuk_ref[...], preferred_element_type=F32)
    for h in range(MLA_HEADS):
        sl = slice(h * HEAD_PAD, (h + 1) * HEAD_PAD)
        k_ref[:, sl] = (kn[:, sl] + kr_pad).astype(BF16)
    vt_ref[...] = lax.dot_general(wuvt_ref[...], lb, NT_DIMS, preferred_element_type=F32).astype(BF16)


def _even_in_kernel(x_ref, w1_ref, w2_ref, gq_ref, gkv_ref, wuq_ref, scw_ref, tab_ref, buf_ref, wuk_ref, wuvt_ref,
                    q_ref, lat_ref, kr_ref, osc_ref, nbuf_ref, k_ref, vt_ref, win_ref, *, tiles_per_seq):
    i = pl.program_id(0)
    nseq = win_ref.shape[0]

    @pl.when(i % tiles_per_seq == 0)
    def _():
        win_ref[...] = buf_ref[...]

    xb = x_ref[...].astype(BF16)
    tab = tab_ref[...]
    h1 = jnp.dot(xb, w1_ref[...], preferred_element_type=F32)
    cq = h1[:, :Q_RANK]
    ckv = h1[:, Q_RANK:Q_RANK + KV_RANK]
    kr_raw = h1[:, Q_RANK + KV_RANK:]
    lat = _rms_norm(ckv, gkv_ref[...])
    kr_pad = _rope(kr_raw, tab)
    lat_ref[...] = lat
    kr_ref[...] = kr_pad
    _keys_values(lat, kr_pad, wuk_ref, wuvt_ref, k_ref, vt_ref)
    q_raw =_dot(_rms_norm(cq, gq_ref[...]), wuq_ref[...])
    scale = (NOPE_DIM + ROPE_DIM) ** -0.5 * LOG2_E
    for h in range(MLA_HEADS):
        sl = slice(h * HEAD_PAD, (h + 1) * HEAD_PAD)
        q_ref[:, sl] = (_rope(q_raw[:, sl], tab) * scale).astype(BF16)

    h2 = jnp.dot(xb, w2_ref[...], preferred_element_type=F32)
    gate_b = h2[:, :SC_WIDTH]
    u_in = h2[:, SC_WIDTH:2 * SC_WIDTH] * h2[:, 2 * SC_WIDTH:]
    u = _causal_conv(u_in, win_ref[...], scw_ref[...])
    tails = _run_tails(u_in, nseq)
    win_ref[...] = tails
    nbuf_ref[...] = tails
    osc_ref[...] = (gate_b * u).astype(BF16)


def _even_in(x, w1, w2, g_q, g_kv, w_uq, sc_w, tab, buf8, w_uk, w_uv_t, *, seq, tm):
    m = x.shape[0]
    nb = m // seq
    tm, tps, nseq = _row_tiling(m, seq, tm)
    if nseq > 1:
        tab = jnp.tile(tab, (nseq, 1))
    const2 = lambda i: (0, 0)
    return pl.pallas_call(
        functools.partial(_even_in_kernel, tiles_per_seq=tps),
        grid=(m // tm,),
        in_specs=[
            pl.BlockSpec((tm, D_MODEL), lambda i: (i, 0)),
            pl.BlockSpec(w1.shape, const2),
            pl.BlockSpec(w2.shape, const2),
            pl.BlockSpec(g_q.shape, const2),
            pl.BlockSpec(g_kv.shape, const2),
            pl.BlockSpec(w_uq.shape, const2),
            pl.BlockSpec(sc_w.shape, const2),
            pl.BlockSpec((tm, 3 * LANES), lambda i: (i % tps, 0)),
            pl.BlockSpec((nseq, SUBLANES, SC_WIDTH), lambda i: (i // tps, 0, 0)),
            pl.BlockSpec(w_uk.shape, const2),
            pl.BlockSpec(w_uv_t.shape, const2),
        ],
        out_specs=[
            pl.BlockSpec((tm, MLA_HEADS * HEAD_PAD), lambda i: (i, 0)),
            pl.BlockSpec((tm, KV_RANK), lambda i: (i, 0)),
            pl.BlockSpec((tm, LANES), lambda i: (i, 0)),
            pl.BlockSpec((tm, SC_WIDTH), lambda i: (i, 0)),
            pl.BlockSpec((nseq, SUBLANES, SC_WIDTH), lambda i: (i // tps, 0, 0)),
            pl.BlockSpec((tm, MLA_HEADS * HEAD_PAD), lambda i: (i, 0)),
            pl.BlockSpec((MLA_WIDTH, tm), lambda i: (i, 0)),
        ],
        out_shape=[
            jax.ShapeDtypeStruct((m, MLA_HEADS * HEAD_PAD), BF16),
            jax.ShapeDtypeStruct((m, KV_RANK), F32),
            jax.ShapeDtypeStruct((m, LANES), F32),
            jax.ShapeDtypeStruct((m, SC_WIDTH), BF16),
            jax.ShapeDtypeStruct((nb, SUBLANES, SC_WIDTH), F32),
            jax.ShapeDtypeStruct((m, MLA_HEADS * HEAD_PAD), BF16),
            jax.ShapeDtypeStruct((m // tm * MLA_WIDTH, tm), BF16),
        ],
        scratch_shapes=[pltpu.VMEM((nseq, SUBLANES, SC_WIDTH), F32)],
        compiler_params=_params("arbitrary"),
        name="even_in",
    )(x, w1, w2, g_q, g_kv, w_uq, sc_w, tab, buf8, w_uk, w_uv_t)


def _kv_up_kernel(lat_ref, kr_ref, wuk_ref, wuvt_ref, k_ref, vt_ref):
    _keys_values(lat_ref[...], kr_ref[...], wuk_ref, wuvt_ref, k_ref, vt_ref)


def _kv_up(lat, kr_pad, w_uk, w_uv_t, *, tm):
    m = lat.shape[0]
    const2 = lambda i: (0, 0)
    return pl.pallas_call(
        _kv_up_kernel,
        grid=(m // tm,),
        in_specs=[
            pl.BlockSpec((tm, KV_RANK), lambda i: (i, 0)),
            pl.BlockSpec((tm, LANES), lambda i: (i, 0)),
            pl.BlockSpec(w_uk.shape, const2),
            pl.BlockSpec(w_uv_t.shape, const2),
        ],
        out_specs=[
            pl.BlockSpec((tm, MLA_HEADS * HEAD_PAD), lambda i: (i, 0)),
            pl.BlockSpec((MLA_WIDTH, tm), lambda i: (i, 0)),
        ],
        out_shape=[
            jax.ShapeDtypeStruct((m, MLA_HEADS * HEAD_PAD), BF16),
            jax.ShapeDtypeStruct((m // tm * MLA_WIDTH, tm), BF16),
        ],
        compiler_params=_params("arbitrary"),
        name="kv_up",
    )(lat, kr_pad, w_uk, w_uv_t)


def _attn_kernel(qi_tab, ka_tab, kb_tab, kb_ok_tab, q_ref, ka_ref, vta_ref, kb_ref, vtb_ref, o_ref,
                 m_ref, l_ref, acc_ref, *, tq, tk, past, t_valid, nk):
    step_id = pl.program_id(1)
    qi = qi_tab[step_id]
    q_lo = past + qi * tq
    chunk_lo = q_lo // CHUNK
    chunk_hi = (q_lo + tq - 1) // CHUNK
    last = jnp.minimum(((chunk_hi + 1) * CHUNK - 1) // tk, nk - 1)
    heads = list(range(MLA_HEADS))
    hsl = [slice(h * HEAD_PAD, (h + 1) * HEAD_PAD) for h in heads]
    vsl = [slice(h * V_DIM, (h + 1) * V_DIM) for h in heads]

    def visibility(kj):
        kpos = kj * tk + lax.broadcasted_iota(jnp.int32, (tk, tq), 0)
        qpos = q_lo + lax.broadcasted_iota(jnp.int32, (tk, tq), 1)
        return jnp.logical_and(kpos // CHUNK <= qpos // CHUNK, kpos < t_valid)

    def scores(k_ref, h, visible):
        s = lax.dot_general(k_ref[:, hsl[h]], q_ref[:, hsl[h]], NT_DIMS, preferred_element_type=F32)
        return s if visible is None else jnp.where(visible, s, NEG_INF)

    def weighted_values(vt_ref, h, p):
        ones = jnp.ones((BF16_SUBLANES, tk), BF16)
        return jnp.dot(jnp.concatenate([vt_ref[vsl[h], :], ones], axis=0), p, preferred_element_type=F32)

    def step(k_ref, vt_ref, visible):
        for g0 in range(0, MLA_HEADS, ATTN_HEAD_GROUP):
            grp = range(g0, g0 + ATTN_HEAD_GROUP)
            s = {h: scores(k_ref, h, visible) for h in grp}
            m_prev = {h: m_ref[h:h + 1, :] for h in grp}
            m_new = {h: jnp.maximum(m_prev[h], jnp.max(s[h], axis=0, keepdims=True)) for h in grp}
            alpha = {h: jnp.exp2(m_prev[h] - m_new[h]) for h in grp}
            p = {h: jnp.exp2(s[h] - m_new[h]).astype(BF16) for h in grp}
            pv = {h: weighted_values(vt_ref, h, p[h]) for h in grp}
            for h in grp:
                l_ref[h:h + 1, :] = alpha[h] * l_ref[h:h + 1, :] + pv[h][V_DIM:V_DIM + 1, :]
                m_ref[h:h + 1, :] = m_new[h]
                acc_ref[vsl[h], :] = alpha[h] * acc_ref[vsl[h], :] + pv[h][:V_DIM, :]

    def fast_step(tiles, first, masked_kj=None):
        visible = None if masked_kj is None else visibility(masked_kj)
        pv, t_max = {}, {}
        for g0 in range(0, MLA_HEADS, FAST_HEAD_GROUP):
            grp = range(g0, g0 + FAST_HEAD_GROUP)
            for k_ref, vt_ref in tiles:
                s = {h: scores(k_ref, h, visible) for h in grp}
                p = {h: jnp.exp2(s[h] - m_ref[h:h + 1, :]).astype(BF16) for h in grp}
                for h in grp:
                    tile_max = jnp.max(s[h], axis=0, keepdims=True)
                    t_max[h] = tile_max if h not in t_max else jnp.maximum(t_max[h], tile_max)
                for h in grp:
                    part = weighted_values(vt_ref, h, p[h])
                    pv[h] = part if h not in pv else pv[h] + part
        rise = t_max[0] - m_ref[0:1, :]
        fall = rise
        for h in heads[1:]:
            d = t_max[h] - m_ref[h:h + 1, :]
            rise = jnp.maximum(rise, d)
            fall = jnp.minimum(fall, d)
        hi = jnp.max(rise, axis=1, keepdims=True)
        lo = jnp.min(fall, axis=1, keepdims=True)
        ok = jnp.logical_and(hi <= FAST_SOFTMAX_MAX_RISE,
                             jnp.logical_or(jnp.logical_not(first), lo >= -FAST_SOFTMAX_MAX_RISE))
        for h in heads:
            m_prev = m_ref[h:h + 1, :]
            m_new = jnp.where(first, t_max[h], jnp.maximum(m_prev, t_max[h]))
            alpha = jnp.exp2(m_prev - m_new)
            l_prev = l_ref[h:h + 1, :]
            acc_prev = acc_ref[vsl[h], :]
            l_ref[h:h + 1, :] = jnp.where(ok, (l_prev + pv[h][V_DIM:V_DIM + 1, :]) * alpha, l_prev)
            acc_ref[vsl[h], :] = jnp.where(ok, (acc_prev + pv[h][:V_DIM, :]) * alpha, acc_prev)
            m_ref[h:h + 1, :] = jnp.where(ok, m_new, m_prev)
        in_range = jnp.logical_and(jnp.max(hi) <= FAST_SOFTMAX_MAX_RISE,
                                   jnp.logical_or(jnp.logical_not(first),
                                                  jnp.min(lo) >= -FAST_SOFTMAX_MAX_RISE))

        @pl.when(jnp.logical_not(in_range))
        def _():
            @pl.when(first)
            def _():
                m_ref[...] = jnp.full_like(m_ref, NEG_INF)
            for k_ref, vt_ref in tiles:
                step(k_ref, vt_ref, visible)

    ka = ka_tab[step_id]
    first = ka == 0

    @pl.when(first)
    def _():
        m_ref[...] = jnp.zeros_like(m_ref)
        l_ref[...] = jnp.zeros_like(l_ref)
        acc_ref[...] = jnp.zeros_like(acc_ref)

    a_end = (ka + 1) * tk
    full_a = jnp.logical_and(a_end <= (chunk_lo + 1) * CHUNK, a_end <= t_valid)

    two = kb_ok_tab[step_id] != 0

    @pl.when(two)
    def _():
        fast_step([(ka_ref, vta_ref), (kb_ref, vtb_ref)], first)

    @pl.when(jnp.logical_and(jnp.logical_not(two), full_a))
    def _():
        fast_step([(ka_ref, vta_ref)], first)

    @pl.when(jnp.logical_and(jnp.logical_not(two), jnp.logical_not(full_a)))
    def _():
        fast_step([(ka_ref, vta_ref)], first, masked_kj=ka)

    @pl.when(ka == last)
    def _():
        for hp in range(MLA_HEADS // 2):
            pair = jnp.concatenate(
                [acc_ref[vsl[h], :] * (1.0 / l_ref[h:h + 1, :]) for h in (2 * hp, 2 * hp + 1)], axis=0)
            o_ref[:, hp * 2 * V_DIM:(hp + 1) * 2 * V_DIM] = pair.T.astype(BF16)


def _attention(q, k, vt, *, nb, sq, t_pad, t_valid, past, tq, tk):
    nq = sq // tq
    nk = t_pad // tk
    steps = []
    for qi in range(nq):
        q_lo = past + qi * tq
        last = min((((q_lo + tq - 1) // CHUNK + 1) * CHUNK - 1) // tk, nk - 1)
        fully = [kj for kj in range(last)
                 if (kj + 1) * tk <= (q_lo // CHUNK + 1) * CHUNK and (kj + 1) * tk <= t_valid]
        partly = [kj for kj in range(last + 1) if kj not in fully]
        for j in range(0, len(fully), 2):
            pair = fully[j:j + 2]
            steps.append((qi, pair[0], pair[-1], len(pair) == 2))
        steps += [(qi, kj, kj, False) for kj in partly]
    tabs = [jnp.asarray([s[c] for s in steps], jnp.int32) for c in range(4)]
    q_index = lambda b, p, qt, at, bt, okt: (b * nq + qt[p], 0)
    a_index = lambda b, p, qt, at, bt, okt: (b * nk + at[p], 0)
    b_index = lambda b, p, qt, at, bt, okt: (b * nk + bt[p], 0)
    grid_spec = pltpu.PrefetchScalarGridSpec(
        num_scalar_prefetch=4,
        grid=(nb, len(steps)),
        in_specs=[
            pl.BlockSpec((tq, MLA_HEADS * HEAD_PAD), q_index),
            pl.BlockSpec((tk, MLA_HEADS * HEAD_PAD), a_index),
            pl.BlockSpec((MLA_WIDTH, tk), a_index),
            pl.BlockSpec((tk, MLA_HEADS * HEAD_PAD), b_index),
            pl.BlockSpec((MLA_WIDTH, tk), b_index),
        ],
        out_specs=pl.BlockSpec((tq, MLA_WIDTH), q_index),
        scratch_shapes=[
            pltpu.VMEM((MLA_HEADS, tq), F32),
            pltpu.VMEM((MLA_HEADS, tq), F32),
            pltpu.VMEM((MLA_WIDTH, tq), F32),
        ],
    )
    return pl.pallas_call(
        functools.partial(_attn_kernel, tq=tq, tk=tk, past=past, t_valid=t_valid, nk=nk),
        grid_spec=grid_spec,
        out_shape=jax.ShapeDtypeStruct((nb * sq, MLA_WIDTH), BF16),
        compiler_params=_params("arbitrary", "arbitrary"),
        name="chunk_attention",
    )(*tabs, q, k, vt, k, vt)


def _odd_in_kernel(x_ref, wqkv_ref, wz_ref, wab_ref, dcw_ref, ab_ref, buf_ref,
                   q_ref, k_ref, v_ref, z_ref, gb_ref, nbuf_ref, win_ref, *, tiles_per_seq):
    i = pl.program_id(0)
    nseq = win_ref.shape[0]

    @pl.when(i % tiles_per_seq == 0)
    def _():
        win_ref[...] = buf_ref[...]

    xb = x_ref[...].astype(BF16)
    outs = (q_ref, k_ref, v_ref)
    for part in range(3):
        cols = slice(part * DN_KEY, (part + 1) * DN_KEY)
        pre = jnp.dot(xb, wqkv_ref[part], preferred_element_type=F32)
        y = _causal_conv(pre, win_ref[:, :, cols], dcw_ref[:, cols])
        tails = _run_tails(pre, nseq)
        win_ref[:, :, cols] = tails
        nbuf_ref[:, :, cols] = tails
        y = _silu(y)
        if part == 2:
            outs[part][...] = y
        else:
            post = DN_DK ** -0.5 if part == 0 else 1.0
            for h in range(DN_HEADS):
                sl = slice(h * DN_DK, (h + 1) * DN_DK)
                seg = y[:, sl]
                nrm = lax.rsqrt(jnp.sum(seg * seg, axis=-1, keepdims=True) + NORM_EPS)
                outs[part][:, sl] = seg * nrm * post if part == 0 else seg * nrm

    z_ref[...] = jnp.dot(xb, wz_ref[...], preferred_element_type=F32)
    ab = jnp.dot(xb, wab_ref[...], preferred_element_type=F32)
    lane = lax.broadcasted_iota(jnp.int32, ab.shape, 1)
    g = -jnp.exp(ab_ref[0:1, :]) * _softplus(ab + ab_ref[1:2, :])
    gb_ref[...] = jnp.where(lane < DN_HEADS, g, jax.nn.sigmoid(ab))


def _odd_in(x, w_qkv, w_z, w_ab, dconv_w, ab_const, buf8, *, seq, tm):
    m = x.shape[0]
    nb = m // seq
    tm, tps, nseq = _row_tiling(m, seq, tm)
    const2 = lambda i: (0, 0)
    row = lambda i: (i, 0)
    return pl.pallas_call(
        functools.partial(_odd_in_kernel, tiles_per_seq=tps),
        grid=(m // tm,),
        in_specs=[
            pl.BlockSpec((tm, D_MODEL), row),
            pl.BlockSpec(w_qkv.shape, lambda i: (0, 0, 0)),
            pl.BlockSpec(w_z.shape, const2),
            pl.BlockSpec(w_ab.shape, const2),
            pl.BlockSpec(dconv_w.shape, const2),
            pl.BlockSpec(ab_const.shape, const2),
            pl.BlockSpec((nseq, SUBLANES, DN_QKV), lambda i: (i // tps, 0, 0)),
        ],
        out_specs=[
            pl.BlockSpec((tm, DN_KEY), row),
            pl.BlockSpec((tm, DN_KEY), row),
            pl.BlockSpec((tm, DN_VAL), row),
            pl.BlockSpec((tm, DN_VAL), row),
            pl.BlockSpec((tm, LANES), row),
            pl.BlockSpec((nseq, SUBLANES, DN_QKV), lambda i: (i // tps, 0, 0)),
        ],
        out_shape=[
            jax.ShapeDtypeStruct((m, DN_KEY), F32),
            jax.ShapeDtypeStruct((m, DN_KEY), F32),
            jax.ShapeDtypeStruct((m, DN_VAL), F32),
            jax.ShapeDtypeStruct((m, DN_VAL), F32),
            jax.ShapeDtypeStruct((m, LANES), F32),
            jax.ShapeDtypeStruct((nb, SUBLANES, DN_QKV), F32),
        ],
        scratch_shapes=[pltpu.VMEM((nseq, SUBLANES, DN_QKV), F32)],
        compiler_params=_params("arbitrary"),
        name="odd_in",
    )(x, w_qkv, w_z, w_ab, dconv_w, ab_const, buf8)


def _delta_kernel(q_ref, k_ref, v_ref, z_ref, gb_ref, s0_ref, go_ref, o_ref, sout_ref, state_ref, *, n_steps):
    n = pl.program_id(1)
    n_par, rows = q_ref.shape[0], q_ref.shape[1]
    n_sub = rows // CHUNK

    @pl.when(n == 0)
    def _():
        state_ref[...] = s0_ref[...]

    ri = lax.broadcasted_iota(jnp.int32, (rows, rows), 0)
    ci = lax.broadcasted_iota(jnp.int32, (rows, rows), 1)

    def blk(size):
        return (ri // size) == (ci // size)

    same = blk(CHUNK)
    lower = ri > ci
    tril = jnp.logical_and(same, ri >= ci)
    strict = jnp.logical_and(same, lower)
    eye = (ri == ci).astype(F32)
    base = DELTA_BASE_BLOCK
    levels = []
    size = base
    while size < CHUNK:
        levels.append(jnp.logical_and(jnp.logical_and(blk(2 * size), jnp.logical_not(blk(size))), lower))
        size *= 2
    in_base = jnp.logical_and(blk(base), lower)
    gb = [gb_ref[b] for b in range(n_par)]
    g_col = [jnp.dot(tril.astype(F32), gb[b], preferred_element_type=F32, precision=HIGHEST)
             for b in range(n_par)]
    g_end = [jnp.dot(same.astype(F32), gb[b], preferred_element_type=F32, precision=HIGHEST)
             for b in range(n_par)]
    g_row = [g_col[b].T for b in range(n_par)]
    go = go_ref[...]
    chain = [(b, hd) for b in range(n_par) for hd in range(DN_HEADS)]
    heads = range(len(chain))
    col = [slice(hd * DN_DK, (hd + 1) * DN_DK) for _, hd in chain]
    seq_of = [b for b, _ in chain]
    head_of = [hd for _, hd in chain]

    def tile(ref, h):
        return ref[seq_of[h], :, col[h]]

    gc = [g_col[seq_of[h]][:, head_of[h]:head_of[h] + 1] for h in heads]
    ge = [g_end[seq_of[h]][:, head_of[h]:head_of[h] + 1] for h in heads]
    decay = [jnp.exp(jnp.where(tril, gc[h] - g_row[seq_of[h]][head_of[h]:head_of[h] + 1, :], -jnp.inf))
             for h in heads]
    beta = [gb[seq_of[h]][:, DN_HEADS + head_of[h]:DN_HEADS + head_of[h] + 1] for h in heads]
    kb = [tile(k_ref, h) * beta[h] for h in heads]
    kbf = [tile(k_ref, h).astype(BF16) for h in heads]
    m_full = [jnp.where(strict, lax.dot_general(kb[h].astype(BF16), kbf[h], NT_DIMS,
                                                preferred_element_type=F32) * decay[h], 0.0) for h in heads]
    p = [-jnp.where(in_base, m_full[h], 0.0) for h in heads]
    t = [eye + p[h] for h in heads]
    size = 2
    while size < base:
        p = [_dot(p[h], p[h]) for h in heads]
        t = [t[h] + _dot(t[h], p[h]) for h in heads]
        size *= 2
    size = base
    for off in levels:
        nblk = rows // (2 * size)
        t4 = [t[h].reshape(nblk, 2, size, rows) for h in heads]
        t_low = [t4[h][:, 1].reshape(rows // 2, rows) for h in heads]
        u = [_dot(t_low[h], jnp.where(off, m_full[h], 0.0)) for h in heads]
        t_low = [(t_low[h] - _dot(u[h], t[h])).reshape(nblk, 1, size, rows) for h in heads]
        t = [jnp.concatenate([t4[h][:, 0:1], t_low[h]], axis=1).reshape(rows, rows) for h in heads]
        size *= 2
    e_g = [jnp.exp(gc[h]) for h in heads]
    sol = [_dot(t[h], jnp.concatenate([tile(v_ref, h) * beta[h], kb[h] * e_g[h]], axis=1)) for h in heads]
    attn = [lax.dot_general(tile(q_ref, h).astype(BF16), kbf[h], NT_DIMS,
                            preferred_element_type=F32) * decay[h] for h in heads]
    qd = [tile(q_ref, h) * e_g[h] for h in heads]
    kt = [tile(k_ref, h) * jnp.exp(ge[h] - gc[h]) for h in heads]
    st = [state_ref[seq_of[h], head_of[h]] for h in heads]
    v_new = [[] for _ in heads]
    o_inter = [[] for _ in heads]
    for c in range(n_sub):
        rs = slice(c * CHUNK, (c + 1) * CHUNK)
        for h in heads:
            r = _dot(jnp.concatenate([sol[h][rs, DN_DV:], qd[h][rs]], axis=0), st[h])
            v_c = sol[h][rs, :DN_DV] - r[:CHUNK]
            v_new[h].append(v_c)
            o_inter[h].append(r[CHUNK:])
            g_tot = jnp.exp(ge[h][c * CHUNK:c * CHUNK + 1, :])
            st[h] = st[h] * g_tot + lax.dot_general(kt[h][rs].astype(BF16), v_c.astype(BF16), TN_DIMS,
                                                    preferred_element_type=F32)
    for h in heads:
        state_ref[seq_of[h], head_of[h]] = st[h]
        o = jnp.concatenate(o_inter[h], axis=0) + _dot(attn[h], jnp.concatenate(v_new[h], axis=0))
        zh = tile(z_ref, h)
        o_ref[seq_of[h], :, col[h]] = (_rms_norm(o, go) * _silu(zh)).astype(BF16)

    @pl.when(n == n_steps - 1)
    def _():
        sout_ref[...] = state_ref[...]


def _delta_rule(q, k, v, z, gb, s0, g_o, *, seq, rows, n_par):
    m = q.shape[0]
    nb = m // seq
    rows = min(rows, seq)
    ns = seq // rows
    q, k, v, z, gb = (a.reshape(nb, seq, a.shape[-1]) for a in (q, k, v, z, gb))
    row = lambda b, n: (b, n, 0)
    st_spec = pl.BlockSpec((n_par, DN_HEADS, DN_DK, DN_DV), lambda b, n: (b, 0, 0, 0))
    og, s_new = pl.pallas_call(
        functools.partial(_delta_kernel, n_steps=ns),
        grid=(nb // n_par, ns),
        in_specs=[
            pl.BlockSpec((n_par, rows, DN_KEY), row),
            pl.BlockSpec((n_par, rows, DN_KEY), row),
            pl.BlockSpec((n_par, rows, DN_VAL), row),
            pl.BlockSpec((n_par, rows, DN_VAL), row),
            pl.BlockSpec((n_par, rows, LANES), row),
            st_spec,
            pl.BlockSpec((1, DN_DV), lambda b, n: (0, 0)),
        ],
        out_specs=[pl.BlockSpec((n_par, rows, DN_VAL), row), st_spec],
        out_shape=[
            jax.ShapeDtypeStruct((nb, seq, DN_VAL), BF16),
            jax.ShapeDtypeStruct((nb, DN_HEADS, DN_DK, DN_DV), F32),
        ],
        scratch_shapes=[pltpu.VMEM((n_par, DN_HEADS, DN_DK, DN_DV), F32)],
        compiler_params=_params("arbitrary", "arbitrary"),
        name="gated_delta",
    )(q, k, v, z, gb, s0, g_o)
    return og.reshape(m, DN_VAL), s_new


def _rope_table(pos):
    half = ROPE_DIM // 2
    inv = ROPE_THETA ** (-jnp.arange(half, dtype=F32) / half)
    ang = pos.astype(F32)[:, None] * inv[None, :]
    cs = jnp.concatenate([jnp.cos(ang), jnp.sin(ang)], -1)
    place = jnp.zeros((2 * half, 3 * LANES), F32)
    j = jnp.arange(half)
    place = place.at[j, NOPE_DIM + j].set(1.0).at[j, NOPE_DIM + half + j].set(1.0)
    place = place.at[half + j, LANES + NOPE_DIM + half + j].set(1.0)
    place = place.at[half + j, 2 * LANES + NOPE_DIM + j].set(-1.0)
    ones = (jnp.arange(3 * LANES) < NOPE_DIM).astype(F32)
    return jnp.dot(cs, place, precision=HIGHEST) + ones[None, :]


def _pad_heads(w, per_head, keep):
    kdim = w.shape[0]
    w = w.reshape(kdim, MLA_HEADS, per_head)[:, :, :keep]
    w = jnp.pad(w, ((0, 0), (0, 0), (0, HEAD_PAD - keep)))
    return w.reshape(kdim, MLA_HEADS * HEAD_PAD)


def _prep_even(w_in, w_uq, w_ukv):
    c1 = Q_RANK + KV_RANK
    w_kr = jnp.pad(w_in[:, c1:c1 + ROPE_DIM], ((0, 0), (NOPE_DIM, HEAD_PAD - NOPE_DIM - ROPE_DIM)))
    w1 = jnp.concatenate([w_in[:, :c1], w_kr], -1).astype(BF16)
    w2 = w_in[:, c1 + ROPE_DIM:].astype(BF16)
    wuq = _pad_heads(w_uq, NOPE_DIM + ROPE_DIM, NOPE_DIM + ROPE_DIM).astype(BF16)
    wuk = _pad_heads(w_ukv, NOPE_DIM + V_DIM, NOPE_DIM).astype(BF16)
    wuv = w_ukv.reshape(KV_RANK, MLA_HEADS, NOPE_DIM + V_DIM)[:, :, NOPE_DIM:].reshape(KV_RANK, MLA_WIDTH)
    return w1, w2, wuq, wuk, wuv.T.astype(BF16)


def _prep_odd(w_in, a_log, dt_bias):
    w_qkv = w_in[:, :DN_QKV].reshape(D_MODEL, 3, DN_KEY).transpose(1, 0, 2).astype(BF16)
    w_z = w_in[:, DN_QKV:DN_QKV + DN_VAL].astype(BF16)
    w_ab = jnp.pad(w_in[:, DN_QKV + DN_VAL:], ((0, 0), (0, LANES - 2 * DN_HEADS))).astype(BF16)
    ab_const = jnp.zeros((SUBLANES, LANES), F32)
    ab_const = ab_const.at[0, :DN_HEADS].set(a_log.astype(F32)).at[1, :DN_HEADS].set(dt_bias.astype(F32))
    return w_qkv, w_z, w_ab, ab_const


def _prep_ffn(w_in, conv_w, w_out):
    return w_in.astype(BF16), w_out.astype(BF16), conv_w


def _pad_rows8(buf):
    return jnp.pad(buf.astype(F32), ((0, 0), (SUBLANES - buf.shape[1], 0), (0, 0)))


def _trunk(x, past_lat, past_kr, sc_buf, dconv_buf, delta_s, ff_buf, w, *, tm, tq, tk, dr):
    (even_w, odd_w, ffn_w, w_o_e, w_o_o, g_qnorm, g_kvnorm, sc_w, dconv_w, g_onorm,
     ln_mix_g, ln_mix_b, ln_ff_g, ln_ff_b) = w
    nb, seq, _ = x.shape
    past = 0 if past_lat is None else past_lat.shape[2]
    m = nb * seq
    x = x.reshape(m, D_MODEL)
    tab = _rope_table(past + jnp.arange(seq, dtype=jnp.int32))
    t_valid = past + seq
    tq = min(tq, seq)
    tk = min(tk, t_valid)
    t_pad = -(-t_valid // tk) * tk
    lats, krs, scs, dcs, dss, ffs = [], [], [], [], [], []
    for i in range(DEPTH):
        row = lambda a: a[i].reshape(1, -1)
        if i % 2 == 0:
            e = i // 2
            w1, w2, wuq, wuk, wuv = even_w[e]
            q, lat, kr_pad, o_sc, sc_new, k, v = _even_in(
                x, w1, w2, g_qnorm[e].reshape(1, -1), g_kvnorm[e].reshape(1, -1), wuq, sc_w[e], tab,
                _pad_rows8(sc_buf[e]), wuk, wuv, seq=seq, tm=tm)
            if past or t_pad != t_valid or tk != min(tm, seq):
                lat_all = lat.reshape(nb, seq, KV_RANK)
                kr_all = kr_pad.reshape(nb, seq, LANES)
                if past:
                    kr_past = jnp.pad(past_kr[e].astype(F32),
                                      ((0, 0), (0, 0), (NOPE_DIM, HEAD_PAD - NOPE_DIM - ROPE_DIM)))
                    lat_all = jnp.concatenate([past_lat[e].astype(F32), lat_all], 1)
                    kr_all = jnp.concatenate([kr_past, kr_all], 1)
                if t_pad != t_valid:
                    lat_all = jnp.pad(lat_all, ((0, 0), (0, t_pad - t_valid), (0, 0)))
                    kr_all = jnp.pad(kr_all, ((0, 0), (0, t_pad - t_valid), (0, 0)))
                k, v = _kv_up(lat_all.reshape(nb * t_pad, KV_RANK), kr_all.reshape(nb * t_pad, LANES),
                              wuk, wuv, tm=tk)
            o_att = _attention(q, k, v, nb=nb, sq=seq, t_pad=t_pad, t_valid=t_valid, past=past, tq=tq, tk=tk)
            mix, mix_w = [o_att, o_sc], [w_o_e[e][:MLA_WIDTH], w_o_e[e][MLA_WIDTH:]]
            lats.append(lat.reshape(nb, seq, KV_RANK))
            krs.append(kr_pad.reshape(nb, seq, LANES)[:, :, NOPE_DIM:NOPE_DIM + ROPE_DIM])
            scs.append(sc_new[:, SUBLANES - (SC_K - 1):])
        else:
            o = i // 2
            w_qkv, w_z, w_ab, ab_const = odd_w[o]
            q, k, v, z, gb, dc_new = _odd_in(x, w_qkv, w_z, w_ab, dconv_w[o], ab_const,
                                             _pad_rows8(dconv_buf[o]), seq=seq, tm=tm)
            og, s_new = _delta_rule(q, k, v, z, gb, delta_s[o].astype(F32), g_onorm[o].reshape(1, -1), seq=seq,
                                    rows=dr, n_par=DELTA_SEQS_PER_STEP)
            mix, mix_w = [og], [w_o_o[o]]
            dcs.append(dc_new[:, SUBLANES - (DN_CONV - 1):])
            dss.append(s_new)
        w_ff_in, w_ff_out, cw = ffn_w[i]
        x, ff_new = _mix_ffn(mix, mix_w, x, row(ln_mix_g), row(ln_mix_b), _pad_rows8(ff_buf[i]),
                             w_ff_in, w_ff_out, cw, row(ln_ff_g), row(ln_ff_b), seq=seq, tm=tm)
        ffs.append(ff_new[:, SUBLANES - (FF_K - 1):])
    return (x.reshape(nb, seq, D_MODEL), jnp.stack(lats), jnp.stack(krs), jnp.stack(scs), jnp.stack(dcs),
            jnp.stack(dss), jnp.stack(ffs))


def _prepare(w_in_e, w_uq, w_ukv, w_o_e, w_in_o, a_log, dt_bias, w_o_o, w_ff_in, ffconv_w, w_ff_out):
    even_w = [_prep_even(w_in_e[e], w_uq[e], w_ukv[e]) for e in range(N_EVEN)]
    odd_w = [_prep_odd(w_in_o[o], a_log[o], dt_bias[o]) for o in range(N_ODD)]
    ffn_w = [_prep_ffn(w_ff_in[i], ffconv_w[i], w_ff_out[i]) for i in range(DEPTH)]
    return even_w, odd_w, ffn_w, w_o_e.astype(BF16), w_o_o.astype(BF16)


def kernel(x_prompt, x_sample, cache_mla_latent, cache_mla_krope, state_sconv, state_dconv, state_delta, state_ffconv, w_in_e, g_qnorm, g_kvnorm, w_uq, w_ukv, sc_w, w_o_e, w_in_o, dconv_w, a_log, dt_bias, g_onorm, w_o_o, w_ff_in, ffconv_w, w_ff_out, ln_mix_g, ln_mix_b, ln_ff_g, ln_ff_b):
    even_w, odd_w, ffn_w, w_o_e_b, w_o_o_b = _prepare(
        w_in_e, w_uq, w_ukv, w_o_e, w_in_o, a_log, dt_bias, w_o_o, w_ff_in, ffconv_w, w_ff_out)
    w = (even_w, odd_w, ffn_w, w_o_e_b, w_o_o_b, g_qnorm, g_kvnorm, sc_w, dconv_w, g_onorm,
         ln_mix_g, ln_mix_b, ln_ff_g, ln_ff_b)
    bp = x_prompt.shape[0]
    f32 = x_prompt.dtype
    y_p, p_lat, p_kr, p_sc, p_dc, p_ds, p_ff = _trunk(
        x_prompt, None, None,
        jnp.zeros((N_EVEN, bp, SC_K - 1, SC_WIDTH), f32),
        jnp.zeros((N_ODD, bp, DN_CONV - 1, DN_QKV), f32),
        jnp.zeros((N_ODD, bp, DN_HEADS, DN_DK, DN_DV), f32),
        jnp.zeros((DEPTH, bp, FF_K - 1, D_FF), f32),
        w, tm=512, tq=512, tk=512, dr=128)
    y_s, s_lat, s_kr, s_sc, s_dc, s_ds, s_ff = _trunk(
        x_sample, cache_mla_latent, cache_mla_krope, state_sconv, state_dconv, state_delta, state_ffconv,
        w, tm=512, tq=64, tk=768, dr=256)
    return (y_p, y_s, p_lat, p_kr, p_sc, p_dc, p_ds, p_ff, s_lat, s_kr, s_sc, s_dc, s_ds, s_ff)
```

```python
import functools
import math

import jax
import jax.numpy as jnp
from jax import lax
from jax.experimental import pallas as pl
from jax.experimental.pallas import tpu as pltpu

D_MODEL = 1024
DEPTH = 4
CHUNK = 64
N_EVEN = (DEPTH + 1) // 2
N_ODD = DEPTH // 2
MLA_HEADS = 8
Q_RANK = 384
KV_RANK = 256
NOPE_DIM = 64
ROPE_DIM = 32
V_DIM = 64
ROPE_THETA = 10000.0
MLA_WIDTH = MLA_HEADS * V_DIM
SC_WIDTH = 512
SC_K = 3
DN_HEADS = 8
DN_DK = 128
DN_DV = 128
DN_CONV = 4
DN_KEY = DN_HEADS * DN_DK
DN_VAL = DN_HEADS * DN_DV
DN_QKV = 2 * DN_KEY + DN_VAL
D_FF = 2816
FF_K = 3
ALPHA = (2 * DEPTH) ** 0.25
NORM_EPS = 1e-6
NEG_INF = -1e30
LOG2_E = math.log2(math.e)

F32 = jnp.float32
BF16 = jnp.bfloat16
HIGHEST = lax.Precision.HIGHEST

LANES = 128
SUBLANES = 8
BF16_SUBLANES = 16
HEAD_PAD = 128
FF_CHUNK = 256
DELTA_BASE_BLOCK = 8
DELTA_SEQS_PER_STEP = 4
ATTN_TILES_PER_STEP = 2
ATTN_HEAD_GROUP = 8
FAST_HEAD_GROUP = 8
FAST_SOFTMAX_MAX_RISE = 60.0
VMEM_LIMIT = 56 * 1024 * 1024

NT_DIMS = (((1,), (1,)), ((), ()))
TN_DIMS = (((0,), (0,)), ((), ()))


def _params(*sem):
    return pltpu.CompilerParams(dimension_semantics=sem, vmem_limit_bytes=VMEM_LIMIT)


def _dot(a, b):
    return jnp.dot(a.astype(BF16), b.astype(BF16), preferred_element_type=F32)


def _layer_norm(r, g, b):
    mu = jnp.mean(r, axis=-1, keepdims=True)
    d = r - mu
    var = jnp.mean(d * d, axis=-1, keepdims=True)
    return d * lax.rsqrt(var + NORM_EPS) * g + b


def _rms_norm(x, g):
    return x * lax.rsqrt(jnp.mean(x * x, axis=-1, keepdims=True) + NORM_EPS) * g


def _silu(x):
    h = 0.5 * x
    return h + h * jnp.tanh(h)


def _softplus(x):
    return jnp.maximum(x, 0.0) + jnp.log1p(jnp.exp(-jnp.abs(x)))


def _causal_conv(x, prev, w):
    tm, c = x.shape
    nseq = prev.shape[0]
    k_taps = w.shape[0]
    groups = tm // nseq // SUBLANES
    x4 = x.reshape(nseq, groups, SUBLANES, c)
    prev4 = prev.reshape(nseq, 1, SUBLANES, c)
    sub = lax.broadcasted_iota(jnp.int32, x4.shape, 2)
    y = x4 * w[k_taps - 1:k_taps, :]
    for k in range(1, k_taps):
        rot = pltpu.roll(x4, k, axis=2)
        rot_prev = jnp.concatenate([pltpu.roll(prev4, k, axis=2), rot[:, :groups - 1]], axis=1)
        y = y + jnp.where(sub < k, rot_prev, rot) * w[k_taps - 1 - k:k_taps - k, :]
    return y.reshape(tm, c)


def _run_tails(x, nseq):
    tm, c = x.shape
    run = tm // nseq
    return x.reshape(nseq, run, c)[:, run - SUBLANES:, :]


def _row_tiling(m, seq, tm):
    tm = min(tm, m)
    if seq >= tm:
        return tm, seq // tm, 1
    return tm, 1, tm // seq


def _mix_ffn_kernel(*refs, n_mix, tiles_per_seq):
    ys, wos = refs[:n_mix], refs[n_mix:2 * n_mix]
    (x_ref, g1_ref, b1_ref, w_in_ref, w_out_ref, cw_ref, buf_ref, g2_ref, b2_ref,
     out_ref, nbuf_ref, act_ref, x1_ref, xb_ref, win_ref) = refs[2 * n_mix:]
    i = pl.program_id(0)
    nseq = win_ref.shape[0]

    tm = x_ref.shape[0]
    halves = [slice(0, tm // 2), slice(tm // 2, tm)]
    seq_half = [slice(0, nseq // 2), slice(nseq // 2, nseq)]

    for rows in halves:
        mix = jnp.dot(ys[0][rows, :], wos[0][...], preferred_element_type=F32)
        for j in range(1, n_mix):
            mix = mix + jnp.dot(ys[j][rows, :], wos[j][...], preferred_element_type=F32)
        x1 = _layer_norm(ALPHA * x_ref[rows, :] + mix, g1_ref[...], b1_ref[...])
        x1_ref[rows, :] = x1
        xb_ref[rows, :] = x1.astype(BF16)

    @pl.when(i % tiles_per_seq == 0)
    def _():
        win_ref[...] = buf_ref[...]

    for c in range(D_FF // FF_CHUNK):
        cols = slice(c * FF_CHUNK, (c + 1) * FF_CHUNK)
        up_cols = slice(D_FF + c * FF_CHUNK, D_FF + (c + 1) * FF_CHUNK)
        prev = win_ref[:, :, cols]
        tails = []
        for hf, rows in enumerate(halves):
            xb = xb_ref[rows, :]
            gate = jnp.dot(xb, w_in_ref[:, cols], preferred_element_type=F32)
            up = jnp.dot(xb, w_in_ref[:, up_cols], preferred_element_type=F32)
            if nseq == 1:
                y = _causal_conv(gate, prev if hf == 0 else tails[0], cw_ref[:, cols])
                tails.append(_run_tails(gate, 1))
            else:
                y = _causal_conv(gate, prev[seq_half[hf]], cw_ref[:, cols])
                tails.append(_run_tails(gate, nseq // 2))
            act_ref[rows, cols] = (_silu(y) * up).astype(BF16)
        tails = tails[1] if nseq == 1 else jnp.concatenate(tails, axis=0)
        win_ref[:, :, cols] = tails
        nbuf_ref[:, :, cols] = tails

    for rows in halves:
        acc = jnp.dot(act_ref[rows, :], w_out_ref[...], preferred_element_type=F32)
        out_ref[rows, :] = _layer_norm(ALPHA * x1_ref[rows, :] + acc, g2_ref[...], b2_ref[...])


def _mix_ffn(ys, wos, x, mix_g, mix_b, buf8, w_in, w_out, conv_w, ff_g, ff_b, *, seq, tm):
    m = x.shape[0]
    nb = m // seq
    tm, tps, nseq = _row_tiling(m, seq, tm)
    const2 = lambda i: (0, 0)
    row = lambda i: (i, 0)
    resident = lambda a: pl.BlockSpec(a.shape, const2, pipeline_mode=pl.Buffered(1))
    vec = pl.BlockSpec((1, D_MODEL), const2)
    state = pl.BlockSpec((nseq, SUBLANES, D_FF), lambda i: (i // tps, 0, 0))
    return pl.pallas_call(
        functools.partial(_mix_ffn_kernel, n_mix=len(ys), tiles_per_seq=tps),
        grid=(m // tm,),
        in_specs=([pl.BlockSpec((tm, y.shape[1]), row) for y in ys]
                  + [resident(w) for w in wos]
                  + [pl.BlockSpec((tm, D_MODEL), row), vec, vec,
                     resident(w_in), resident(w_out), pl.BlockSpec(conv_w.shape, const2), state, vec, vec]),
        out_specs=[pl.BlockSpec((tm, D_MODEL), row), state],
        out_shape=[
            jax.ShapeDtypeStruct((m, D_MODEL), F32),
            jax.ShapeDtypeStruct((nb, SUBLANES, D_FF), F32),
        ],
        scratch_shapes=[
            pltpu.VMEM((tm, D_FF), BF16),
            pltpu.VMEM((tm, D_MODEL), F32),
            pltpu.VMEM((tm, D_MODEL), BF16),
            pltpu.VMEM((nseq, SUBLANES, D_FF), F32),
        ],
        compiler_params=_params("arbitrary"),
        name="mix_ffn",
    )(*ys, *wos, x, mix_g, mix_b, w_in, w_out, conv_w, buf8, ff_g, ff_b)


def _rope(x, tab):
    return (x * tab[:, 0:LANES]
            + pltpu.roll(x, ROPE_DIM // 2, axis=1) * tab[:, LANES:2 * LANES]
            + pltpu.roll(x, LANES - ROPE_DIM // 2, axis=1) * tab[:, 2 * LANES:3 * LANES])


def _keys_values(lat, kr_pad, wuk_ref, wuvt_ref, k_ref, vt_ref):
    lb = lat.astype(BF16)
    kn = jnp.dot(lb, wuk_ref[...], preferred_element_type=F32)
    for h in range(MLA_HEADS):
        sl = slice(h * HEAD_PAD, (h + 1) * HEAD_PAD)
        k_ref[:, sl] = (kn[:, sl] + kr_pad).astype(BF16)
    vt_ref[...] = lax.dot_general(wuvt_ref[...], lb, NT_DIMS, preferred_element_type=F32).astype(BF16)


def _even_in_kernel(x_ref, w1_ref, w2_ref, gq_ref, gkv_ref, wuq_ref, scw_ref, tab_ref, buf_ref, wuk_ref, wuvt_ref,
                    q_ref, lat_ref, kr_ref, osc_ref, nbuf_ref, k_ref, vt_ref, win_ref, *, tiles_per_seq):
    i = pl.program_id(0)
    nseq = win_ref.shape[0]

    @pl.when(i % tiles_per_seq == 0)
    def _():
        win_ref[...] = buf_ref[...]

    xb = x_ref[...].astype(BF16)
    tab = tab_ref[...]
    h1 = jnp.dot(xb, w1_ref[...], preferred_element_type=F32)
    cq = h1[:, :Q_RANK]
    ckv = h1[:, Q_RANK:Q_RANK + KV_RANK]
    kr_raw = h1[:, Q_RANK + KV_RANK:]
    lat = _rms_norm(ckv, gkv_ref[...])
    kr_pad = _rope(kr_raw, tab)
    lat_ref[...] = lat
    kr_ref[...] = kr_pad
    _keys_values(lat, kr_pad, wuk_ref, wuvt_ref, k_ref, vt_ref)
    q_raw =_dot(_rms_norm(cq, gq_ref[...]), wuq_ref[...])
    scale = (NOPE_DIM + ROPE_DIM) ** -0.5 * LOG2_E
    for h in range(MLA_HEADS):
        sl = slice(h * HEAD_PAD, (h + 1) * HEAD_PAD)
        q_ref[:, sl] = (_rope(q_raw[:, sl], tab) * scale).astype(BF16)

    h2 = jnp.dot(xb, w2_ref[...], preferred_element_type=F32)
    gate_b = h2[:, :SC_WIDTH]
    u_in = h2[:, SC_WIDTH:2 * SC_WIDTH] * h2[:, 2 * SC_WIDTH:]
    u = _causal_conv(u_in, win_ref[...], scw_ref[...])
    tails = _run_tails(u_in, nseq)
    win_ref[...] = tails
    nbuf_ref[...] = tails
    osc_ref[...] = (gate_b * u).astype(BF16)


def _even_in(x, w1, w2, g_q, g_kv, w_uq, sc_w, tab, buf8, w_uk, w_uv_t, *, seq, tm):
    m = x.shape[0]
    nb = m // seq
    tm, tps, nseq = _row_tiling(m, seq, tm)
    if nseq > 1:
        tab = jnp.tile(tab, (nseq, 1))
    const2 = lambda i: (0, 0)
    return pl.pallas_call(
        functools.partial(_even_in_kernel, tiles_per_seq=tps),
        grid=(m // tm,),
        in_specs=[
            pl.BlockSpec((tm, D_MODEL), lambda i: (i, 0)),
            pl.BlockSpec(w1.shape, const2),
            pl.BlockSpec(w2.shape, const2),
            pl.BlockSpec(g_q.shape, const2),
            pl.BlockSpec(g_kv.shape, const2),
            pl.BlockSpec(w_uq.shape, const2),
            pl.BlockSpec(sc_w.shape, const2),
            pl.BlockSpec((tm, 3 * LANES), lambda i: (i % tps, 0)),
            pl.BlockSpec((nseq, SUBLANES, SC_WIDTH), lambda i: (i // tps, 0, 0)),
            pl.BlockSpec(w_uk.shape, const2),
            pl.BlockSpec(w_uv_t.shape, const2),
        ],
        out_specs=[
            pl.BlockSpec((tm, MLA_HEADS * HEAD_PAD), lambda i: (i, 0)),
            pl.BlockSpec((tm, KV_RANK), lambda i: (i, 0)),
            pl.BlockSpec((tm, LANES), lambda i: (i, 0)),
            pl.BlockSpec((tm, SC_WIDTH), lambda i: (i, 0)),
            pl.BlockSpec((nseq, SUBLANES, SC_WIDTH), lambda i: (i // tps, 0, 0)),
            pl.BlockSpec((tm, MLA_HEADS * HEAD_PAD), lambda i: (i, 0)),
            pl.BlockSpec((MLA_WIDTH, tm), lambda i: (i, 0)),
        ],
        out_shape=[
            jax.ShapeDtypeStruct((m, MLA_HEADS * HEAD_PAD), BF16),
            jax.ShapeDtypeStruct((m, KV_RANK), F32),
            jax.ShapeDtypeStruct((m, LANES), F32),
            jax.ShapeDtypeStruct((m, SC_WIDTH), BF16),
            jax.ShapeDtypeStruct((nb, SUBLANES, SC_WIDTH), F32),
            jax.ShapeDtypeStruct((m, MLA_HEADS * HEAD_PAD), BF16),
            jax.ShapeDtypeStruct((m // tm * MLA_WIDTH, tm), BF16),
        ],
        scratch_shapes=[pltpu.VMEM((nseq, SUBLANES, SC_WIDTH), F32)],
        compiler_params=_params("arbitrary"),
        name="even_in",
    )(x, w1, w2, g_q, g_kv, w_uq, sc_w, tab, buf8, w_uk, w_uv_t)


def _kv_up_kernel(lat_ref, kr_ref, wuk_ref, wuvt_ref, k_ref, vt_ref):
    _keys_values(lat_ref[...], kr_ref[...], wuk_ref, wuvt_ref, k_ref, vt_ref)


def _kv_up(lat, kr_pad, w_uk, w_uv_t, *, tm):
    m = lat.shape[0]
    const2 = lambda i: (0, 0)
    return pl.pallas_call(
        _kv_up_kernel,
        grid=(m // tm,),
        in_specs=[
            pl.BlockSpec((tm, KV_RANK), lambda i: (i, 0)),
            pl.BlockSpec((tm, LANES), lambda i: (i, 0)),
            pl.BlockSpec(w_uk.shape, const2),
            pl.BlockSpec(w_uv_t.shape, const2),
        ],
        out_specs=[
            pl.BlockSpec((tm, MLA_HEADS * HEAD_PAD), lambda i: (i, 0)),
            pl.BlockSpec((MLA_WIDTH, tm), lambda i: (i, 0)),
        ],
        out_shape=[
            jax.ShapeDtypeStruct((m, MLA_HEADS * HEAD_PAD), BF16),
            jax.ShapeDtypeStruct((m // tm * MLA_WIDTH, tm), BF16),
        ],
        compiler_params=_params("arbitrary"),
        name="kv_up",
    )(lat, kr_pad, w_uk, w_uv_t)


def _attn_kernel(*refs, n_tiles, tq, tk, past, t_valid, nk):
    qi_tab, all_tab = refs[0], refs[1]
    ka_tab = refs[2]
    q_ref = refs[2 + n_tiles]
    tile_refs = [(refs[3 + n_tiles + 2 * t], refs[4 + n_tiles + 2 * t]) for t in range(n_tiles)]
    o_ref, m_ref, l_ref, acc_ref = refs[3 + 3 * n_tiles:]
    ka_ref, vta_ref = tile_refs[0]
    step_id = pl.program_id(1)
    qi = qi_tab[step_id]
    q_lo = past + qi * tq
    chunk_lo = q_lo // CHUNK
    chunk_hi = (q_lo + tq - 1) // CHUNK
    last = jnp.minimum(((chunk_hi + 1) * CHUNK - 1) // tk, nk - 1)
    heads = list(range(MLA_HEADS))
    hsl = [slice(h * HEAD_PAD, (h + 1) * HEAD_PAD) for h in heads]
    vsl = [slice(h * V_DIM, (h + 1) * V_DIM) for h in heads]

    def visibility(kj):
        kpos = kj * tk + lax.broadcasted_iota(jnp.int32, (tk, tq), 0)
        qpos = q_lo + lax.broadcasted_iota(jnp.int32, (tk, tq), 1)
        return jnp.logical_and(kpos // CHUNK <= qpos // CHUNK, kpos < t_valid)

    def scores(k_ref, h, visible):
        s = lax.dot_general(k_ref[:, hsl[h]], q_ref[:, hsl[h]], NT_DIMS, preferred_element_type=F32)
        return s if visible is None else jnp.where(visible, s, NEG_INF)

    def weighted_values(vt_ref, h, p):
        ones = jnp.ones((BF16_SUBLANES, tk), BF16)
        return jnp.dot(jnp.concatenate([vt_ref[vsl[h], :], ones], axis=0), p, preferred_element_type=F32)

    def step(k_ref, vt_ref, visible):
        for g0 in range(0, MLA_HEADS, ATTN_HEAD_GROUP):
            grp = range(g0, g0 + ATTN_HEAD_GROUP)
            s = {h: scores(k_ref, h, visible) for h in grp}
            m_prev = {h: m_ref[h:h + 1, :] for h in grp}
            m_new = {h: jnp.maximum(m_prev[h], jnp.max(s[h], axis=0, keepdims=True)) for h in grp}
            alpha = {h: jnp.exp2(m_prev[h] - m_new[h]) for h in grp}
            p = {h: jnp.exp2(s[h] - m_new[h]).astype(BF16) for h in grp}
            pv = {h: weighted_values(vt_ref, h, p[h]) for h in grp}
            for h in grp:
                l_ref[h:h + 1, :] = alpha[h] * l_ref[h:h + 1, :] + pv[h][V_DIM:V_DIM + 1, :]
                m_ref[h:h + 1, :] = m_new[h]
                acc_ref[vsl[h], :] = alpha[h] * acc_ref[vsl[h], :] + pv[h][:V_DIM, :]

    def fast_step(tiles, first, masked_kj=None):
        visible = None if masked_kj is None else visibility(masked_kj)
        pv, t_max = {}, {}
        for g0 in range(0, MLA_HEADS, FAST_HEAD_GROUP):
            grp = range(g0, g0 + FAST_HEAD_GROUP)
            for k_ref, vt_ref in tiles:
                s = {h: scores(k_ref, h, visible) for h in grp}
                p = {h: jnp.exp2(s[h] - m_ref[h:h + 1, :]).astype(BF16) for h in grp}
                for h in grp:
                    tile_max = jnp.max(s[h], axis=0, keepdims=True)
                    t_max[h] = tile_max if h not in t_max else jnp.maximum(t_max[h], tile_max)
                for h in grp:
                    part = weighted_values(vt_ref, h, p[h])
                    pv[h] = part if h not in pv else pv[h] + part
        rise = t_max[0] - m_ref[0:1, :]
        fall = rise
        for h in heads[1:]:
            d = t_max[h] - m_ref[h:h + 1, :]
            rise = jnp.maximum(rise, d)
            fall = jnp.minimum(fall, d)
        hi = jnp.max(rise, axis=1, keepdims=True)
        lo = jnp.min(fall, axis=1, keepdims=True)
        ok = jnp.logical_and(hi <= FAST_SOFTMAX_MAX_RISE,
                             jnp.logical_or(jnp.logical_not(first), lo >= -FAST_SOFTMAX_MAX_RISE))
        for h in heads:
            m_prev = m_ref[h:h + 1, :]
            m_new = jnp.where(first, t_max[h], jnp.maximum(m_prev, t_max[h]))
            alpha = jnp.exp2(m_prev - m_new)
            l_prev = l_ref[h:h + 1, :]
            acc_prev = acc_ref[vsl[h], :]
            l_ref[h:h + 1, :] = jnp.where(ok, (l_prev + pv[h][V_DIM:V_DIM + 1, :]) * alpha, l_prev)
            acc_ref[vsl[h], :] = jnp.where(ok, (acc_prev + pv[h][:V_DIM, :]) * alpha, acc_prev)
            m_ref[h:h + 1, :] = jnp.where(ok, m_new, m_prev)
        in_range = jnp.logical_and(jnp.max(hi) <= FAST_SOFTMAX_MAX_RISE,
                                   jnp.logical_or(jnp.logical_not(first),
                                                  jnp.min(lo) >= -FAST_SOFTMAX_MAX_RISE))

        @pl.when(jnp.logical_not(in_range))
        def _():
            @pl.when(first)
            def _():
                m_ref[...] = jnp.full_like(m_ref, NEG_INF)
            for k_ref, vt_ref in tiles:
                step(k_ref, vt_ref, visible)

    ka = ka_tab[step_id]
    first = ka == 0

    @pl.when(first)
    def _():
        m_ref[...] = jnp.zeros_like(m_ref)
        l_ref[...] = jnp.zeros_like(l_ref)
        acc_ref[...] = jnp.zeros_like(acc_ref)

    a_end = (ka + 1) * tk
    full_a = jnp.logical_and(a_end <= (chunk_lo + 1) * CHUNK, a_end <= t_valid)

    joint = all_tab[step_id] != 0

    @pl.when(joint)
    def _():
        fast_step(tile_refs, first)

    @pl.when(jnp.logical_and(jnp.logical_not(joint), full_a))
    def _():
        fast_step([(ka_ref, vta_ref)], first)

    @pl.when(jnp.logical_and(jnp.logical_not(joint), jnp.logical_not(full_a)))
    def _():
        fast_step([(ka_ref, vta_ref)], first, masked_kj=ka)

    @pl.when(ka == last)
    def _():
        for hp in range(MLA_HEADS // 2):
            pair = jnp.concatenate(
                [acc_ref[vsl[h], :] * (1.0 / l_ref[h:h + 1, :]) for h in (2 * hp, 2 * hp + 1)], axis=0)
            o_ref[:, hp * 2 * V_DIM:(hp + 1) * 2 * V_DIM] = pair.T.astype(BF16)


def _attention(q, k, vt, *, nb, sq, t_pad, t_valid, past, tq, tk):
    nq = sq // tq
    nk = t_pad // tk
    n = ATTN_TILES_PER_STEP
    steps = []
    for qi in range(nq):
        q_lo = past + qi * tq
        last = min((((q_lo + tq - 1) // CHUNK + 1) * CHUNK - 1) // tk, nk - 1)
        fully = [kj for kj in range(last)
                 if (kj + 1) * tk <= (q_lo // CHUNK + 1) * CHUNK and (kj + 1) * tk <= t_valid]
        n_joint = len(fully) // n * n
        for j in range(0, n_joint, n):
            steps.append((qi, 1, fully[j:j + n]))
        steps += [(qi, 0, [kj] * n) for kj in range(last + 1) if kj not in fully[:n_joint]]
    tabs = ([jnp.asarray([s[0] for s in steps], jnp.int32), jnp.asarray([s[1] for s in steps], jnp.int32)]
            + [jnp.asarray([s[2][t] for s in steps], jnp.int32) for t in range(n)])
    q_index = lambda b, p, qt, *tabs_: (b * nq + qt[p], 0)

    def tile_index(t):
        return lambda b, p, qt, at, *k_tabs: (b * nk + k_tabs[t][p], 0)

    tile_specs = []
    for t in range(n):
        tile_specs += [pl.BlockSpec((tk, MLA_HEADS * HEAD_PAD), tile_index(t)),
                       pl.BlockSpec((MLA_WIDTH, tk), tile_index(t))]
    grid_spec = pltpu.PrefetchScalarGridSpec(
        num_scalar_prefetch=2 + n,
        grid=(nb, len(steps)),
        in_specs=[pl.BlockSpec((tq, MLA_HEADS * HEAD_PAD), q_index)] + tile_specs,
        out_specs=pl.BlockSpec((tq, MLA_WIDTH), q_index),
        scratch_shapes=[
            pltpu.VMEM((MLA_HEADS, tq), F32),
            pltpu.VMEM((MLA_HEADS, tq), F32),
            pltpu.VMEM((MLA_WIDTH, tq), F32),
        ],
    )
    return pl.pallas_call(
        functools.partial(_attn_kernel, n_tiles=n, tq=tq, tk=tk, past=past, t_valid=t_valid, nk=nk),
        grid_spec=grid_spec,
        out_shape=jax.ShapeDtypeStruct((nb * sq, MLA_WIDTH), BF16),
        compiler_params=_params("arbitrary", "arbitrary"),
        name="chunk_attention",
    )(*tabs, q, *([k, vt] * n))


def _odd_in_kernel(x_ref, wqkv_ref, wz_ref, wab_ref, dcw_ref, ab_ref, buf_ref,
                   q_ref, k_ref, v_ref, z_ref, gb_ref, nbuf_ref, win_ref, *, tiles_per_seq):
    i = pl.program_id(0)
    nseq = win_ref.shape[0]

    @pl.when(i % tiles_per_seq == 0)
    def _():
        win_ref[...] = buf_ref[...]

    xb = x_ref[...].astype(BF16)
    outs = (q_ref, k_ref, v_ref)
    for part in range(3):
        cols = slice(part * DN_KEY, (part + 1) * DN_KEY)
        pre = jnp.dot(xb, wqkv_ref[part], preferred_element_type=F32)
        y = _causal_conv(pre, win_ref[:, :, cols], dcw_ref[:, cols])
        tails = _run_tails(pre, nseq)
        win_ref[:, :, cols] = tails
        nbuf_ref[:, :, cols] = tails
        y = _silu(y)
        if part == 2:
            outs[part][...] = y
        else:
            post = DN_DK ** -0.5 if part == 0 else 1.0
            for h in range(DN_HEADS):
                sl = slice(h * DN_DK, (h + 1) * DN_DK)
                seg = y[:, sl]
                nrm = lax.rsqrt(jnp.sum(seg * seg, axis=-1, keepdims=True) + NORM_EPS)
                outs[part][:, sl] = seg * nrm * post if part == 0 else seg * nrm

    z_ref[...] = jnp.dot(xb, wz_ref[...], preferred_element_type=F32)
    ab = jnp.dot(xb, wab_ref[...], preferred_element_type=F32)
    lane = lax.broadcasted_iota(jnp.int32, ab.shape, 1)
    g = -jnp.exp(ab_ref[0:1, :]) * _softplus(ab + ab_ref[1:2, :])
    gb_ref[...] = jnp.where(lane < DN_HEADS, g, jax.nn.sigmoid(ab))


def _odd_in(x, w_qkv, w_z, w_ab, dconv_w, ab_const, buf8, *, seq, tm):
    m = x.shape[0]
    nb = m // seq
    tm, tps, nseq = _row_tiling(m, seq, tm)
    const2 = lambda i: (0, 0)
    row = lambda i: (i, 0)
    return pl.pallas_call(
        functools.partial(_odd_in_kernel, tiles_per_seq=tps),
        grid=(m // tm,),
        in_specs=[
            pl.BlockSpec((tm, D_MODEL), row),
            pl.BlockSpec(w_qkv.shape, lambda i: (0, 0, 0)),
            pl.BlockSpec(w_z.shape, const2),
            pl.BlockSpec(w_ab.shape, const2),
            pl.BlockSpec(dconv_w.shape, const2),
            pl.BlockSpec(ab_const.shape, const2),
            pl.BlockSpec((nseq, SUBLANES, DN_QKV), lambda i: (i // tps, 0, 0)),
        ],
        out_specs=[
            pl.BlockSpec((tm, DN_KEY), row),
            pl.BlockSpec((tm, DN_KEY), row),
            pl.BlockSpec((tm, DN_VAL), row),
            pl.BlockSpec((tm, DN_VAL), row),
            pl.BlockSpec((tm, LANES), row),
            pl.BlockSpec((nseq, SUBLANES, DN_QKV), lambda i: (i // tps, 0, 0)),
        ],
        out_shape=[
            jax.ShapeDtypeStruct((m, DN_KEY), F32),
            jax.ShapeDtypeStruct((m, DN_KEY), F32),
            jax.ShapeDtypeStruct((m, DN_VAL), F32),
            jax.ShapeDtypeStruct((m, DN_VAL), F32),
            jax.ShapeDtypeStruct((m, LANES), F32),
            jax.ShapeDtypeStruct((nb, SUBLANES, DN_QKV), F32),
        ],
        scratch_shapes=[pltpu.VMEM((nseq, SUBLANES, DN_QKV), F32)],
        compiler_params=_params("arbitrary"),
        name="odd_in",
    )(x, w_qkv, w_z, w_ab, dconv_w, ab_const, buf8)


def _delta_kernel(q_ref, k_ref, v_ref, z_ref, gb_ref, s0_ref, go_ref, o_ref, sout_ref, state_ref, *, n_steps):
    n = pl.program_id(1)
    n_par, rows = q_ref.shape[0], q_ref.shape[1]
    n_sub = rows // CHUNK

    @pl.when(n == 0)
    def _():
        state_ref[...] = s0_ref[...]

    ri = lax.broadcasted_iota(jnp.int32, (rows, rows), 0)
    ci = lax.broadcasted_iota(jnp.int32, (rows, rows), 1)

    def blk(size):
        return (ri // size) == (ci // size)

    same = blk(CHUNK)
    lower = ri > ci
    tril = jnp.logical_and(same, ri >= ci)
    strict = jnp.logical_and(same, lower)
    eye = (ri == ci).astype(F32)
    base = DELTA_BASE_BLOCK
    levels = []
    size = base
    while size < CHUNK:
        levels.append(jnp.logical_and(jnp.logical_and(blk(2 * size), jnp.logical_not(blk(size))), lower))
        size *= 2
    in_base = jnp.logical_and(blk(base), lower)
    gb = [gb_ref[b] for b in range(n_par)]
    g_col = [jnp.dot(tril.astype(F32), gb[b], preferred_element_type=F32, precision=HIGHEST)
             for b in range(n_par)]
    g_end = [jnp.dot(same.astype(F32), gb[b], preferred_element_type=F32, precision=HIGHEST)
             for b in range(n_par)]
    g_row = [g_col[b].T for b in range(n_par)]
    go = go_ref[...]
    chain = [(b, hd) for b in range(n_par) for hd in range(DN_HEADS)]
    heads = range(len(chain))
    col = [slice(hd * DN_DK, (hd + 1) * DN_DK) for _, hd in chain]
    seq_of = [b for b, _ in chain]
    head_of = [hd for _, hd in chain]

    def tile(ref, h):
        return ref[seq_of[h], :, col[h]]

    gc = [g_col[seq_of[h]][:, head_of[h]:head_of[h] + 1] for h in heads]
    ge = [g_end[seq_of[h]][:, head_of[h]:head_of[h] + 1] for h in heads]
    decay = [jnp.exp(jnp.where(tril, gc[h] - g_row[seq_of[h]][head_of[h]:head_of[h] + 1, :], -jnp.inf))
             for h in heads]
    beta = [gb[seq_of[h]][:, DN_HEADS + head_of[h]:DN_HEADS + head_of[h] + 1] for h in heads]
    kb = [tile(k_ref, h) * beta[h] for h in heads]
    kbf = [tile(k_ref, h).astype(BF16) for h in heads]
    m_full = [jnp.where(strict, lax.dot_general(kb[h].astype(BF16), kbf[h], NT_DIMS,
                                                preferred_element_type=F32) * decay[h], 0.0) for h in heads]
    p = [-jnp.where(in_base, m_full[h], 0.0) for h in heads]
    t = [eye + p[h] for h in heads]
    size = 2
    while size < base:
        p = [_dot(p[h], p[h]) for h in heads]
        t = [t[h] + _dot(t[h], p[h]) for h in heads]
        size *= 2
    size = base
    for off in levels:
        nblk = rows // (2 * size)
        t4 = [t[h].reshape(nblk, 2, size, rows) for h in heads]
        t_low = [t4[h][:, 1].reshape(rows // 2, rows) for h in heads]
        u = [_dot(t_low[h], jnp.where(off, m_full[h], 0.0)) for h in heads]
        t_low = [(t_low[h] - _dot(u[h], t[h])).reshape(nblk, 1, size, rows) for h in heads]
        t = [jnp.concatenate([t4[h][:, 0:1], t_low[h]], axis=1).reshape(rows, rows) for h in heads]
        size *= 2
    e_g = [jnp.exp(gc[h]) for h in heads]
    sol = [_dot(t[h], jnp.concatenate([tile(v_ref, h) * beta[h], kb[h] * e_g[h]], axis=1)) for h in heads]
    attn = [lax.dot_general(tile(q_ref, h).astype(BF16), kbf[h], NT_DIMS,
                            preferred_element_type=F32) * decay[h] for h in heads]
    qd = [tile(q_ref, h) * e_g[h] for h in heads]
    kt = [tile(k_ref, h) * jnp.exp(ge[h] - gc[h]) for h in heads]
    st = [state_ref[seq_of[h], head_of[h]] for h in heads]
    v_new = [[] for _ in heads]
    o_inter = [[] for _ in heads]
    for c in range(n_sub):
        rs = slice(c * CHUNK, (c + 1) * CHUNK)
        for h in heads:
            r = _dot(jnp.concatenate([sol[h][rs, DN_DV:], qd[h][rs]], axis=0), st[h])
            v_c = sol[h][rs, :DN_DV] - r[:CHUNK]
            v_new[h].append(v_c)
            o_inter[h].append(r[CHUNK:])
            g_tot = jnp.exp(ge[h][c * CHUNK:c * CHUNK + 1, :])
            st[h] = st[h] * g_tot + lax.dot_general(kt[h][rs].astype(BF16), v_c.astype(BF16), TN_DIMS,
                                                    preferred_element_type=F32)
    for h in heads:
        state_ref[seq_of[h], head_of[h]] = st[h]
        o = jnp.concatenate(o_inter[h], axis=0) + _dot(attn[h], jnp.concatenate(v_new[h], axis=0))
        zh = tile(z_ref, h)
        o_ref[seq_of[h], :, col[h]] = (_rms_norm(o, go) * _silu(zh)).astype(BF16)

    @pl.when(n == n_steps - 1)
    def _():
        sout_ref[...] = state_ref[...]


def _delta_rule(q, k, v, z, gb, s0, g_o, *, seq, rows, n_par):
    m = q.shape[0]
    nb = m // seq
    rows = min(rows, seq)
    ns = seq // rows
    q, k, v, z, gb = (a.reshape(nb, seq, a.shape[-1]) for a in (q, k, v, z, gb))
    row = lambda b, n: (b, n, 0)
    st_spec = pl.BlockSpec((n_par, DN_HEADS, DN_DK, DN_DV), lambda b, n: (b, 0, 0, 0))
    og, s_new = pl.pallas_call(
        functools.partial(_delta_kernel, n_steps=ns),
        grid=(nb // n_par, ns),
        in_specs=[
            pl.BlockSpec((n_par, rows, DN_KEY), row),
            pl.BlockSpec((n_par, rows, DN_KEY), row),
            pl.BlockSpec((n_par, rows, DN_VAL), row),
            pl.BlockSpec((n_par, rows, DN_VAL), row),
            pl.BlockSpec((n_par, rows, LANES), row),
            st_spec,
            pl.BlockSpec((1, DN_DV), lambda b, n: (0, 0)),
        ],
        out_specs=[pl.BlockSpec((n_par, rows, DN_VAL), row), st_spec],
        out_shape=[
            jax.ShapeDtypeStruct((nb, seq, DN_VAL), BF16),
            jax.ShapeDtypeStruct((nb, DN_HEADS, DN_DK, DN_DV), F32),
        ],
        scratch_shapes=[pltpu.VMEM((n_par, DN_HEADS, DN_DK, DN_DV), F32)],
        compiler_params=_params("arbitrary", "arbitrary"),
        name="gated_delta",
    )(q, k, v, z, gb, s0, g_o)
    return og.reshape(m, DN_VAL), s_new


def _rope_table(pos):
    half = ROPE_DIM // 2
    inv = ROPE_THETA ** (-jnp.arange(half, dtype=F32) / half)
    ang = pos.astype(F32)[:, None] * inv[None, :]
    cs = jnp.concatenate([jnp.cos(ang), jnp.sin(ang)], -1)
    place = jnp.zeros((2 * half, 3 * LANES), F32)
    j = jnp.arange(half)
    place = place.at[j, NOPE_DIM + j].set(1.0).at[j, NOPE_DIM + half + j].set(1.0)
    place = place.at[half + j, LANES + NOPE_DIM + half + j].set(1.0)
    place = place.at[half + j, 2 * LANES + NOPE_DIM + j].set(-1.0)
    ones = (jnp.arange(3 * LANES) < NOPE_DIM).astype(F32)
    return jnp.dot(cs, place, precision=HIGHEST) + ones[None, :]


def _pad_heads(w, per_head, keep):
    kdim = w.shape[0]
    w = w.reshape(kdim, MLA_HEADS, per_head)[:, :, :keep]
    w = jnp.pad(w, ((0, 0), (0, 0), (0, HEAD_PAD - keep)))
    return w.reshape(kdim, MLA_HEADS * HEAD_PAD)


def _prep_even(w_in, w_uq, w_ukv):
    c1 = Q_RANK + KV_RANK
    w_kr = jnp.pad(w_in[:, c1:c1 + ROPE_DIM], ((0, 0), (NOPE_DIM, HEAD_PAD - NOPE_DIM - ROPE_DIM)))
    w1 = jnp.concatenate([w_in[:, :c1], w_kr], -1).astype(BF16)
    w2 = w_in[:, c1 + ROPE_DIM:].astype(BF16)
    wuq = _pad_heads(w_uq, NOPE_DIM + ROPE_DIM, NOPE_DIM + ROPE_DIM).astype(BF16)
    wuk = _pad_heads(w_ukv, NOPE_DIM + V_DIM, NOPE_DIM).astype(BF16)
    wuv = w_ukv.reshape(KV_RANK, MLA_HEADS, NOPE_DIM + V_DIM)[:, :, NOPE_DIM:].reshape(KV_RANK, MLA_WIDTH)
    return w1, w2, wuq, wuk, wuv.T.astype(BF16)


def _prep_odd(w_in, a_log, dt_bias):
    w_qkv = w_in[:, :DN_QKV].reshape(D_MODEL, 3, DN_KEY).transpose(1, 0, 2).astype(BF16)
    w_z = w_in[:, DN_QKV:DN_QKV + DN_VAL].astype(BF16)
    w_ab = jnp.pad(w_in[:, DN_QKV + DN_VAL:], ((0, 0), (0, LANES - 2 * DN_HEADS))).astype(BF16)
    ab_const = jnp.zeros((SUBLANES, LANES), F32)
    ab_const = ab_const.at[0, :DN_HEADS].set(a_log.astype(F32)).at[1, :DN_HEADS].set(dt_bias.astype(F32))
    return w_qkv, w_z, w_ab, ab_const


def _prep_ffn(w_in, conv_w, w_out):
    return w_in.astype(BF16), w_out.astype(BF16), conv_w


def _pad_rows8(buf):
    return jnp.pad(buf.astype(F32), ((0, 0), (SUBLANES - buf.shape[1], 0), (0, 0)))


def _trunk(x, past_lat, past_kr, sc_buf, dconv_buf, delta_s, ff_buf, w, *, tm, tq, tk, dr):
    (even_w, odd_w, ffn_w, w_o_e, w_o_o, g_qnorm, g_kvnorm, sc_w, dconv_w, g_onorm,
     ln_mix_g, ln_mix_b, ln_ff_g, ln_ff_b) = w
    nb, seq, _ = x.shape
    past = 0 if past_lat is None else past_lat.shape[2]
    m = nb * seq
    x = x.reshape(m, D_MODEL)
    tab = _rope_table(past + jnp.arange(seq, dtype=jnp.int32))
    t_valid = past + seq
    tq = min(tq, seq)
    tk = min(tk, t_valid)
    t_pad = -(-t_valid // tk) * tk
    lats, krs, scs, dcs, dss, ffs = [], [], [], [], [], []
    for i in range(DEPTH):
        row = lambda a: a[i].reshape(1, -1)
        if i % 2 == 0:
            e = i // 2
            w1, w2, wuq, wuk, wuv = even_w[e]
            q, lat, kr_pad, o_sc, sc_new, k, v = _even_in(
                x, w1, w2, g_qnorm[e].reshape(1, -1), g_kvnorm[e].reshape(1, -1), wuq, sc_w[e], tab,
                _pad_rows8(sc_buf[e]), wuk, wuv, seq=seq, tm=tm)
            if past or t_pad != t_valid or tk != min(tm, seq):
                lat_all = lat.reshape(nb, seq, KV_RANK)
                kr_all = kr_pad.reshape(nb, seq, LANES)
                if past:
                    kr_past = jnp.pad(past_kr[e].astype(F32),
                                      ((0, 0), (0, 0), (NOPE_DIM, HEAD_PAD - NOPE_DIM - ROPE_DIM)))
                    lat_all = jnp.concatenate([past_lat[e].astype(F32), lat_all], 1)
                    kr_all = jnp.concatenate([kr_past, kr_all], 1)
                if t_pad != t_valid:
                    lat_all = jnp.pad(lat_all, ((0, 0), (0, t_pad - t_valid), (0, 0)))
                    kr_all = jnp.pad(kr_all, ((0, 0), (0, t_pad - t_valid), (0, 0)))
                k, v = _kv_up(lat_all.reshape(nb * t_pad, KV_RANK), kr_all.reshape(nb * t_pad, LANES),
                              wuk, wuv, tm=tk)
            o_att = _attention(q, k, v, nb=nb, sq=seq, t_pad=t_pad, t_valid=t_valid, past=past, tq=tq, tk=tk)
            mix, mix_w = [o_att, o_sc], [w_o_e[e][:MLA_WIDTH], w_o_e[e][MLA_WIDTH:]]
            lats.append(lat.reshape(nb, seq, KV_RANK))
            krs.append(kr_pad.reshape(nb, seq, LANES)[:, :, NOPE_DIM:NOPE_DIM + ROPE_DIM])
            scs.append(sc_new[:, SUBLANES - (SC_K - 1):])
        else:
            o = i // 2
            w_qkv, w_z, w_ab, ab_const = odd_w[o]
            q, k, v, z, gb, dc_new = _odd_in(x, w_qkv, w_z, w_ab, dconv_w[o], ab_const,
                                             _pad_rows8(dconv_buf[o]), seq=seq, tm=tm)
            og, s_new = _delta_rule(q, k, v, z, gb, delta_s[o].astype(F32), g_onorm[o].reshape(1, -1), seq=seq,
                                    rows=dr, n_par=DELTA_SEQS_PER_STEP)
            mix, mix_w = [og], [w_o_o[o]]
            dcs.append(dc_new[:, SUBLANES - (DN_CONV - 1):])
            dss.append(s_new)
        w_ff_in, w_ff_out, cw = ffn_w[i]
        x, ff_new = _mix_ffn(mix, mix_w, x, row(ln_mix_g), row(ln_mix_b), _pad_rows8(ff_buf[i]),
                             w_ff_in, w_ff_out, cw, row(ln_ff_g), row(ln_ff_b), seq=seq, tm=tm)
        ffs.append(ff_new[:, SUBLANES - (FF_K - 1):])
    return (x.reshape(nb, seq, D_MODEL), jnp.stack(lats), jnp.stack(krs), jnp.stack(scs), jnp.stack(dcs),
            jnp.stack(dss), jnp.stack(ffs))


def _prepare(w_in_e, w_uq, w_ukv, w_o_e, w_in_o, a_log, dt_bias, w_o_o, w_ff_in, ffconv_w, w_ff_out):
    even_w = [_prep_even(w_in_e[e], w_uq[e], w_ukv[e]) for e in range(N_EVEN)]
    odd_w = [_prep_odd(w_in_o[o], a_log[o], dt_bias[o]) for o in range(N_ODD)]
    ffn_w = [_prep_ffn(w_ff_in[i], ffconv_w[i], w_ff_out[i]) for i in range(DEPTH)]
    return even_w, odd_w, ffn_w, w_o_e.astype(BF16), w_o_o.astype(BF16)


def kernel(x_prompt, x_sample, cache_mla_latent, cache_mla_krope, state_sconv, state_dconv, state_delta, state_ffconv, w_in_e, g_qnorm, g_kvnorm, w_uq, w_ukv, sc_w, w_o_e, w_in_o, dconv_w, a_log, dt_bias, g_onorm, w_o_o, w_ff_in, ffconv_w, w_ff_out, ln_mix_g, ln_mix_b, ln_ff_g, ln_ff_b):
    even_w, odd_w, ffn_w, w_o_e_b, w_o_o_b = _prepare(
        w_in_e, w_uq, w_ukv, w_o_e, w_in_o, a_log, dt_bias, w_o_o, w_ff_in, ffconv_w, w_ff_out)
    w = (even_w, odd_w, ffn_w, w_o_e_b, w_o_o_b, g_qnorm, g_kvnorm, sc_w, dconv_w, g_onorm,
         ln_mix_g, ln_mix_b, ln_ff_g, ln_ff_b)
    bp = x_prompt.shape[0]
    f32 = x_prompt.dtype
    y_p, p_lat, p_kr, p_sc, p_dc, p_ds, p_ff = _trunk(
        x_prompt, None, None,
        jnp.zeros((N_EVEN, bp, SC_K - 1, SC_WIDTH), f32),
        jnp.zeros((N_ODD, bp, DN_CONV - 1, DN_QKV), f32),
        jnp.zeros((N_ODD, bp, DN_HEADS, DN_DK, DN_DV), f32),
        jnp.zeros((DEPTH, bp, FF_K - 1, D_FF), f32),
        w, tm=512, tq=512, tk=512, dr=128)
    y_s, s_lat, s_kr, s_sc, s_dc, s_ds, s_ff = _trunk(
        x_sample, cache_mla_latent, cache_mla_krope, state_sconv, state_dconv, state_delta, state_ffconv,
        w, tm=512, tq=64, tk=768, dr=256)
    return (y_p, y_s, p_lat, p_kr, p_sc, p_dc, p_ds, p_ff, s_lat, s_kr, s_sc, s_dc, s_ds, s_ff)
```

```python
import functools
import math

import jax
import jax.numpy as jnp
from jax import lax
from jax.experimental import pallas as pl
from jax.experimental.pallas import tpu as pltpu

D_MODEL = 1024
DEPTH = 4
CHUNK = 64
N_EVEN = (DEPTH + 1) // 2
N_ODD = DEPTH // 2
MLA_HEADS = 8
Q_RANK = 384
KV_RANK = 256
NOPE_DIM = 64
ROPE_DIM = 32
V_DIM = 64
ROPE_THETA = 10000.0
MLA_WIDTH = MLA_HEADS * V_DIM
SC_WIDTH = 512
SC_K = 3
DN_HEADS = 8
DN_DK = 128
DN_DV = 128
DN_CONV = 4
DN_KEY = DN_HEADS * DN_DK
DN_VAL = DN_HEADS * DN_DV
DN_QKV = 2 * DN_KEY + DN_VAL
D_FF = 2816
FF_K = 3
ALPHA = (2 * DEPTH) ** 0.25
NORM_EPS = 1e-6
NEG_INF = -1e30
LOG2_E = math.log2(math.e)

F32 = jnp.float32
BF16 = jnp.bfloat16
HIGHEST = lax.Precision.HIGHEST

LANES = 128
SUBLANES = 8
BF16_SUBLANES = 16
HEAD_PAD = 128
FF_CHUNK = 256
DELTA_BASE_BLOCK = 8
DELTA_SEQS_PER_STEP = 4
ATTN_TILES_PER_STEP = 2
ATTN_HEAD_GROUP = 8
FAST_HEAD_GROUP = 8
FAST_SOFTMAX_MAX_RISE = 60.0
VMEM_LIMIT = 56 * 1024 * 1024

NT_DIMS = (((1,), (1,)), ((), ()))
TN_DIMS = (((0,), (0,)), ((), ()))


def _params(*sem):
    return pltpu.CompilerParams(dimension_semantics=sem, vmem_limit_bytes=VMEM_LIMIT)


def _dot(a, b):
    return jnp.dot(a.astype(BF16), b.astype(BF16), preferred_element_type=F32)


def _layer_norm(r, g, b):
    mu = jnp.mean(r, axis=-1, keepdims=True)
    d = r - mu
    var = jnp.mean(d * d, axis=-1, keepdims=True)
    return d * lax.rsqrt(var + NORM_EPS) * g + b


def _rms_norm(x, g):
    return x * lax.rsqrt(jnp.mean(x * x, axis=-1, keepdims=True) + NORM_EPS) * g


def _silu(x):
    h = 0.5 * x
    return h + h * jnp.tanh(h)


def _softplus(x):
    return jnp.maximum(x, 0.0) + jnp.log1p(jnp.exp(-jnp.abs(x)))


def _causal_conv(x, prev, w):
    tm, c = x.shape
    nseq = prev.shape[0]
    k_taps = w.shape[0]
    groups = tm // nseq // SUBLANES
    x4 = x.reshape(nseq, groups, SUBLANES, c)
    prev4 = prev.reshape(nseq, 1, SUBLANES, c)
    sub = lax.broadcasted_iota(jnp.int32, x4.shape, 2)
    y = x4 * w[k_taps - 1:k_taps, :]
    for k in range(1, k_taps):
        rot = pltpu.roll(x4, k, axis=2)
        rot_prev = jnp.concatenate([pltpu.roll(prev4, k, axis=2), rot[:, :groups - 1]], axis=1)
        y = y + jnp.where(sub < k, rot_prev, rot) * w[k_taps - 1 - k:k_taps - k, :]
    return y.reshape(tm, c)


def _run_tails(x, nseq):
    tm, c = x.shape
    run = tm // nseq
    return x.reshape(nseq, run, c)[:, run - SUBLANES:, :]


def _row_tiling(m, seq, tm):
    tm = min(tm, m)
    if seq >= tm:
        return tm, seq // tm, 1
    return tm, 1, tm // seq


def _mix_ffn_kernel(*refs, n_mix, tiles_per_seq):
    ys, wos = refs[:n_mix], refs[n_mix:2 * n_mix]
    (x_ref, g1_ref, b1_ref, w_in_ref, w_out_ref, cw_ref, buf_ref, g2_ref, b2_ref,
     out_ref, nbuf_ref, act_ref, x1_ref, xb_ref, win_ref) = refs[2 * n_mix:]
    i = pl.program_id(0)
    nseq = win_ref.shape[0]

    tm = x_ref.shape[0]
    halves = [slice(0, tm // 2), slice(tm // 2, tm)]
    seq_half = [slice(0, nseq // 2), slice(nseq // 2, nseq)]

    for rows in halves:
        mix = jnp.dot(ys[0][rows, :], wos[0][...], preferred_element_type=F32)
        for j in range(1, n_mix):
            mix = mix + jnp.dot(ys[j][rows, :], wos[j][...], preferred_element_type=F32)
        x1 = _layer_norm(ALPHA * x_ref[rows, :] + mix, g1_ref[...], b1_ref[...])
        x1_ref[rows, :] = x1
        xb_ref[rows, :] = x1.astype(BF16)

    @pl.when(i % tiles_per_seq == 0)
    def _():
        win_ref[...] = buf_ref[...]

    for c in range(D_FF // FF_CHUNK):
        cols = slice(c * FF_CHUNK, (c + 1) * FF_CHUNK)
        up_cols = slice(D_FF + c * FF_CHUNK, D_FF + (c + 1) * FF_CHUNK)
        prev = win_ref[:, :, cols]
        tails = []
        for hf, rows in enumerate(halves):
            xb = xb_ref[rows, :]
            gate = jnp.dot(xb, w_in_ref[:, cols], preferred_element_type=F32)
            up = jnp.dot(xb, w_in_ref[:, up_cols], preferred_element_type=F32)
            if nseq == 1:
                y = _causal_conv(gate, prev if hf == 0 else tails[0], cw_ref[:, cols])
                tails.append(_run_tails(gate, 1))
            else:
                y = _causal_conv(gate, prev[seq_half[hf]], cw_ref[:, cols])
                tails.append(_run_tails(gate, nseq // 2))
            act_ref[rows, cols] = (_silu(y) * up).astype(BF16)
        tails = tails[1] if nseq == 1 else jnp.concatenate(tails, axis=0)
        win_ref[:, :, cols] = tails
        nbuf_ref[:, :, cols] = tails

    for rows in halves:
        acc = jnp.dot(act_ref[rows, :], w_out_ref[...], preferred_element_type=F32)
        out_ref[rows, :] = _layer_norm(ALPHA * x1_ref[rows, :] + acc, g2_ref[...], b2_ref[...])


def _mix_ffn(ys, wos, x, mix_g, mix_b, buf8, w_in, w_out, conv_w, ff_g, ff_b, *, layer, seq, tm):
    m = x.shape[0]
    nb = m // seq
    tm, tps, nseq = _row_tiling(m, seq, tm)
    const2 = lambda i: (0, 0)
    row = lambda i: (i, 0)
    resident = lambda a: pl.BlockSpec(a.shape, const2, pipeline_mode=pl.Buffered(1))
    of_layer = lambda a: pl.BlockSpec((None,) + a.shape[1:], lambda i: (layer, 0, 0),
                                      pipeline_mode=pl.Buffered(1))
    vec = pl.BlockSpec((1, D_MODEL), const2)
    state = pl.BlockSpec((nseq, SUBLANES, D_FF), lambda i: (i // tps, 0, 0))
    return pl.pallas_call(
        functools.partial(_mix_ffn_kernel, n_mix=len(ys), tiles_per_seq=tps),
        grid=(m // tm,),
        in_specs=([pl.BlockSpec((tm, y.shape[1]), row) for y in ys]
                  + [resident(w) for w in wos]
                  + [pl.BlockSpec((tm, D_MODEL), row), vec, vec,
                     of_layer(w_in), of_layer(w_out), pl.BlockSpec(conv_w.shape, const2), state, vec, vec]),
        out_specs=[pl.BlockSpec((tm, D_MODEL), row), state],
        out_shape=[
            jax.ShapeDtypeStruct((m, D_MODEL), F32),
            jax.ShapeDtypeStruct((nb, SUBLANES, D_FF), F32),
        ],
        scratch_shapes=[
            pltpu.VMEM((tm, D_FF), BF16),
            pltpu.VMEM((tm, D_MODEL), F32),
            pltpu.VMEM((tm, D_MODEL), BF16),
            pltpu.VMEM((nseq, SUBLANES, D_FF), F32),
        ],
        compiler_params=_params("arbitrary"),
        name="mix_ffn",
    )(*ys, *wos, x, mix_g, mix_b, w_in, w_out, conv_w, buf8, ff_g, ff_b)


def _rope(x, tab):
    return (x * tab[:, 0:LANES]
            + pltpu.roll(x, ROPE_DIM // 2, axis=1) * tab[:, LANES:2 * LANES]
            + pltpu.roll(x, LANES - ROPE_DIM // 2, axis=1) * tab[:, 2 * LANES:3 * LANES])


def _keys_values(lat, kr_pad, wuk_ref, wuvt_ref, k_ref, vt_ref):
    lb = lat.astype(BF16)
    kn = jnp.dot(lb, wuk_ref[...], preferred_element_type=F32)
    for h in range(MLA_HEADS):
        sl = slice(h * HEAD_PAD, (h + 1) * HEAD_PAD)
        k_ref[:, sl] = (kn[:, sl] + kr_pad).astype(BF16)
    vt_ref[...] = lax.dot_general(wuvt_ref[...], lb, NT_DIMS, preferred_element_type=F32).astype(BF16)


def _even_in_kernel(x_ref, w1_ref, w2_ref, gq_ref, gkv_ref, wuq_ref, scw_ref, tab_ref, buf_ref, wuk_ref, wuvt_ref,
                    q_ref, lat_ref, kr_ref, osc_ref, nbuf_ref, k_ref, vt_ref, win_ref, *, tiles_per_seq):
    i = pl.program_id(0)
    nseq = win_ref.shape[0]

    @pl.when(i % tiles_per_seq == 0)
    def _():
        win_ref[...] = buf_ref[...]

    xb = x_ref[...].astype(BF16)
    tab = tab_ref[...]
    h1 = jnp.dot(xb, w1_ref[...], preferred_element_type=F32)
    cq = h1[:, :Q_RANK]
    ckv = h1[:, Q_RANK:Q_RANK + KV_RANK]
    kr_raw = h1[:, Q_RANK + KV_RANK:]
    lat = _rms_norm(ckv, gkv_ref[...])
    kr_pad = _rope(kr_raw, tab)
    lat_ref[...] = lat
    kr_ref[...] = kr_pad
    _keys_values(lat, kr_pad, wuk_ref, wuvt_ref, k_ref, vt_ref)
    q_raw =_dot(_rms_norm(cq, gq_ref[...]), wuq_ref[...])
    scale = (NOPE_DIM + ROPE_DIM) ** -0.5 * LOG2_E
    for h in range(MLA_HEADS):
        sl = slice(h * HEAD_PAD, (h + 1) * HEAD_PAD)
        q_ref[:, sl] = (_rope(q_raw[:, sl], tab) * scale).astype(BF16)

    h2 = jnp.dot(xb, w2_ref[...], preferred_element_type=F32)
    gate_b = h2[:, :SC_WIDTH]
    u_in = h2[:, SC_WIDTH:2 * SC_WIDTH] * h2[:, 2 * SC_WIDTH:]
    u = _causal_conv(u_in, win_ref[...], scw_ref[...])
    tails = _run_tails(u_in, nseq)
    win_ref[...] = tails
    nbuf_ref[...] = tails
    osc_ref[...] = (gate_b * u).astype(BF16)


def _even_in(x, w1, w2, g_q, g_kv, w_uq, sc_w, tab, buf8, w_uk, w_uv_t, *, seq, tm):
    m = x.shape[0]
    nb = m // seq
    tm, tps, nseq = _row_tiling(m, seq, tm)
    if nseq > 1:
        tab = jnp.tile(tab, (nseq, 1))
    const2 = lambda i: (0, 0)
    return pl.pallas_call(
        functools.partial(_even_in_kernel, tiles_per_seq=tps),
        grid=(m // tm,),
        in_specs=[
            pl.BlockSpec((tm, D_MODEL), lambda i: (i, 0)),
            pl.BlockSpec(w1.shape, const2),
            pl.BlockSpec(w2.shape, const2),
            pl.BlockSpec(g_q.shape, const2),
            pl.BlockSpec(g_kv.shape, const2),
            pl.BlockSpec(w_uq.shape, const2),
            pl.BlockSpec(sc_w.shape, const2),
            pl.BlockSpec((tm, 3 * LANES), lambda i: (i % tps, 0)),
            pl.BlockSpec((nseq, SUBLANES, SC_WIDTH), lambda i: (i // tps, 0, 0)),
            pl.BlockSpec(w_uk.shape, const2),
            pl.BlockSpec(w_uv_t.shape, const2),
        ],
        out_specs=[
            pl.BlockSpec((tm, MLA_HEADS * HEAD_PAD), lambda i: (i, 0)),
            pl.BlockSpec((tm, KV_RANK), lambda i: (i, 0)),
            pl.BlockSpec((tm, LANES), lambda i: (i, 0)),
            pl.BlockSpec((tm, SC_WIDTH), lambda i: (i, 0)),
            pl.BlockSpec((nseq, SUBLANES, SC_WIDTH), lambda i: (i // tps, 0, 0)),
            pl.BlockSpec((tm, MLA_HEADS * HEAD_PAD), lambda i: (i, 0)),
            pl.BlockSpec((MLA_WIDTH, tm), lambda i: (i, 0)),
        ],
        out_shape=[
            jax.ShapeDtypeStruct((m, MLA_HEADS * HEAD_PAD), BF16),
            jax.ShapeDtypeStruct((m, KV_RANK), F32),
            jax.ShapeDtypeStruct((m, LANES), F32),
            jax.ShapeDtypeStruct((m, SC_WIDTH), BF16),
            jax.ShapeDtypeStruct((nb, SUBLANES, SC_WIDTH), F32),
            jax.ShapeDtypeStruct((m, MLA_HEADS * HEAD_PAD), BF16),
            jax.ShapeDtypeStruct((m // tm * MLA_WIDTH, tm), BF16),
        ],
        scratch_shapes=[pltpu.VMEM((nseq, SUBLANES, SC_WIDTH), F32)],
        compiler_params=_params("arbitrary"),
        name="even_in",
    )(x, w1, w2, g_q, g_kv, w_uq, sc_w, tab, buf8, w_uk, w_uv_t)


def _kv_up_kernel(lat_ref, kr_ref, wuk_ref, wuvt_ref, k_ref, vt_ref):
    _keys_values(lat_ref[...], kr_ref[...], wuk_ref, wuvt_ref, k_ref, vt_ref)


def _kv_up(lat, kr_pad, w_uk, w_uv_t, *, tm):
    m = lat.shape[0]
    const2 = lambda i: (0, 0)
    return pl.pallas_call(
        _kv_up_kernel,
        grid=(m // tm,),
        in_specs=[
            pl.BlockSpec((tm, KV_RANK), lambda i: (i, 0)),
            pl.BlockSpec((tm, LANES), lambda i: (i, 0)),
            pl.BlockSpec(w_uk.shape, const2),
            pl.BlockSpec(w_uv_t.shape, const2),
        ],
        out_specs=[
            pl.BlockSpec((tm, MLA_HEADS * HEAD_PAD), lambda i: (i, 0)),
            pl.BlockSpec((MLA_WIDTH, tm), lambda i: (i, 0)),
        ],
        out_shape=[
            jax.ShapeDtypeStruct((m, MLA_HEADS * HEAD_PAD), BF16),
            jax.ShapeDtypeStruct((m // tm * MLA_WIDTH, tm), BF16),
        ],
        compiler_params=_params("arbitrary"),
        name="kv_up",
    )(lat, kr_pad, w_uk, w_uv_t)


def _attn_kernel(*refs, n_tiles, tq, tk, past, t_valid, nk):
    qi_tab, all_tab = refs[0], refs[1]
    ka_tab = refs[2]
    q_ref = refs[2 + n_tiles]
    tile_refs = [(refs[3 + n_tiles + 2 * t], refs[4 + n_tiles + 2 * t]) for t in range(n_tiles)]
    o_ref, m_ref, l_ref, acc_ref = refs[3 + 3 * n_tiles:]
    ka_ref, vta_ref = tile_refs[0]
    step_id = pl.program_id(1)
    qi = qi_tab[step_id]
    q_lo = past + qi * tq
    chunk_lo = q_lo // CHUNK
    chunk_hi = (q_lo + tq - 1) // CHUNK
    last = jnp.minimum(((chunk_hi + 1) * CHUNK - 1) // tk, nk - 1)
    heads = list(range(MLA_HEADS))
    hsl = [slice(h * HEAD_PAD, (h + 1) * HEAD_PAD) for h in heads]
    vsl = [slice(h * V_DIM, (h + 1) * V_DIM) for h in heads]

    def visibility(kj):
        kpos = kj * tk + lax.broadcasted_iota(jnp.int32, (tk, tq), 0)
        qpos = q_lo + lax.broadcasted_iota(jnp.int32, (tk, tq), 1)
        return jnp.logical_and(kpos // CHUNK <= qpos // CHUNK, kpos < t_valid)

    def scores(k_ref, h, visible):
        s = lax.dot_general(k_ref[:, hsl[h]], q_ref[:, hsl[h]], NT_DIMS, preferred_element_type=F32)
        return s if visible is None else jnp.where(visible, s, NEG_INF)

    def weighted_values(vt_ref, h, p):
        ones = jnp.ones((BF16_SUBLANES, tk), BF16)
        return jnp.dot(jnp.concatenate([vt_ref[vsl[h], :], ones], axis=0), p, preferred_element_type=F32)

    def step(k_ref, vt_ref, visible):
        for g0 in range(0, MLA_HEADS, ATTN_HEAD_GROUP):
            grp = range(g0, g0 + ATTN_HEAD_GROUP)
            s = {h: scores(k_ref, h, visible) for h in grp}
            m_prev = {h: m_ref[h:h + 1, :] for h in grp}
            m_new = {h: jnp.maximum(m_prev[h], jnp.max(s[h], axis=0, keepdims=True)) for h in grp}
            alpha = {h: jnp.exp2(m_prev[h] - m_new[h]) for h in grp}
            p = {h: jnp.exp2(s[h] - m_new[h]).astype(BF16) for h in grp}
            pv = {h: weighted_values(vt_ref, h, p[h]) for h in grp}
            for h in grp:
                l_ref[h:h + 1, :] = alpha[h] * l_ref[h:h + 1, :] + pv[h][V_DIM:V_DIM + 1, :]
                m_ref[h:h + 1, :] = m_new[h]
                acc_ref[vsl[h], :] = alpha[h] * acc_ref[vsl[h], :] + pv[h][:V_DIM, :]

    def fast_step(tiles, first, masked_kj=None):
        visible = None if masked_kj is None else visibility(masked_kj)
        pv, t_max = {}, {}
        for g0 in range(0, MLA_HEADS, FAST_HEAD_GROUP):
            grp = range(g0, g0 + FAST_HEAD_GROUP)
            for k_ref, vt_ref in tiles:
                s = {h: scores(k_ref, h, visible) for h in grp}
                p = {h: jnp.exp2(s[h] - m_ref[h:h + 1, :]).astype(BF16) for h in grp}
                for h in grp:
                    tile_max = jnp.max(s[h], axis=0, keepdims=True)
                    t_max[h] = tile_max if h not in t_max else jnp.maximum(t_max[h], tile_max)
                for h in grp:
                    part = weighted_values(vt_ref, h, p[h])
                    pv[h] = part if h not in pv else pv[h] + part
        rise = t_max[0] - m_ref[0:1, :]
        fall = rise
        for h in heads[1:]:
            d = t_max[h] - m_ref[h:h + 1, :]
            rise = jnp.maximum(rise, d)
            fall = jnp.minimum(fall, d)
        hi = jnp.max(rise, axis=1, keepdims=True)
        lo = jnp.min(fall, axis=1, keepdims=True)
        ok = jnp.logical_and(hi <= FAST_SOFTMAX_MAX_RISE,
                             jnp.logical_or(jnp.logical_not(first), lo >= -FAST_SOFTMAX_MAX_RISE))
        for h in heads:
            m_prev = m_ref[h:h + 1, :]
            m_new = jnp.where(first, t_max[h], jnp.maximum(m_prev, t_max[h]))
            alpha = jnp.exp2(m_prev - m_new)
            l_prev = l_ref[h:h + 1, :]
            acc_prev = acc_ref[vsl[h], :]
            l_ref[h:h + 1, :] = jnp.where(ok, (l_prev + pv[h][V_DIM:V_DIM + 1, :]) * alpha, l_prev)
            acc_ref[vsl[h], :] = jnp.where(ok, (acc_prev + pv[h][:V_DIM, :]) * alpha, acc_prev)
            m_ref[h:h + 1, :] = jnp.where(ok, m_new, m_prev)
        in_range = jnp.logical_and(jnp.max(hi) <= FAST_SOFTMAX_MAX_RISE,
                                   jnp.logical_or(jnp.logical_not(first),
                                                  jnp.min(lo) >= -FAST_SOFTMAX_MAX_RISE))

        @pl.when(jnp.logical_not(in_range))
        def _():
            @pl.when(first)
            def _():
                m_ref[...] = jnp.full_like(m_ref, NEG_INF)
            for k_ref, vt_ref in tiles:
                step(k_ref, vt_ref, visible)

    ka = ka_tab[step_id]
    first = ka == 0

    @pl.when(first)
    def _():
        m_ref[...] = jnp.zeros_like(m_ref)
        l_ref[...] = jnp.zeros_like(l_ref)
        acc_ref[...] = jnp.zeros_like(acc_ref)

    a_end = (ka + 1) * tk
    full_a = jnp.logical_and(a_end <= (chunk_lo + 1) * CHUNK, a_end <= t_valid)

    joint = all_tab[step_id] != 0

    @pl.when(joint)
    def _():
        fast_step(tile_refs, first)

    @pl.when(jnp.logical_and(jnp.logical_not(joint), full_a))
    def _():
        fast_step([(ka_ref, vta_ref)], first)

    @pl.when(jnp.logical_and(jnp.logical_not(joint), jnp.logical_not(full_a)))
    def _():
        fast_step([(ka_ref, vta_ref)], first, masked_kj=ka)

    @pl.when(ka == last)
    def _():
        for hp in range(MLA_HEADS // 2):
            pair = jnp.concatenate(
                [acc_ref[vsl[h], :] * (1.0 / l_ref[h:h + 1, :]) for h in (2 * hp, 2 * hp + 1)], axis=0)
            o_ref[:, hp * 2 * V_DIM:(hp + 1) * 2 * V_DIM] = pair.T.astype(BF16)


def _attention(q, k, vt, *, nb, sq, t_pad, t_valid, past, tq, tk):
    nq = sq // tq
    nk = t_pad // tk
    n = ATTN_TILES_PER_STEP
    steps = []
    for qi in range(nq):
        q_lo = past + qi * tq
        last = min((((q_lo + tq - 1) // CHUNK + 1) * CHUNK - 1) // tk, nk - 1)
        fully = [kj for kj in range(last)
                 if (kj + 1) * tk <= (q_lo // CHUNK + 1) * CHUNK and (kj + 1) * tk <= t_valid]
        n_joint = len(fully) // n * n
        for j in range(0, n_joint, n):
            steps.append((qi, 1, fully[j:j + n]))
        steps += [(qi, 0, [kj] * n) for kj in range(last + 1) if kj not in fully[:n_joint]]
    tabs = ([jnp.asarray([s[0] for s in steps], jnp.int32), jnp.asarray([s[1] for s in steps], jnp.int32)]
            + [jnp.asarray([s[2][t] for s in steps], jnp.int32) for t in range(n)])
    q_index = lambda b, p, qt, *tabs_: (b * nq + qt[p], 0)

    def tile_index(t):
        return lambda b, p, qt, at, *k_tabs: (b * nk + k_tabs[t][p], 0)

    tile_specs = []
    for t in range(n):
        tile_specs += [pl.BlockSpec((tk, MLA_HEADS * HEAD_PAD), tile_index(t)),
                       pl.BlockSpec((MLA_WIDTH, tk), tile_index(t))]
    grid_spec = pltpu.PrefetchScalarGridSpec(
        num_scalar_prefetch=2 + n,
        grid=(nb, len(steps)),
        in_specs=[pl.BlockSpec((tq, MLA_HEADS * HEAD_PAD), q_index)] + tile_specs,
        out_specs=pl.BlockSpec((tq, MLA_WIDTH), q_index),
        scratch_shapes=[
            pltpu.VMEM((MLA_HEADS, tq), F32),
            pltpu.VMEM((MLA_HEADS, tq), F32),
            pltpu.VMEM((MLA_WIDTH, tq), F32),
        ],
    )
    return pl.pallas_call(
        functools.partial(_attn_kernel, n_tiles=n, tq=tq, tk=tk, past=past, t_valid=t_valid, nk=nk),
        grid_spec=grid_spec,
        out_shape=jax.ShapeDtypeStruct((nb * sq, MLA_WIDTH), BF16),
        compiler_params=_params("arbitrary", "arbitrary"),
        name="chunk_attention",
    )(*tabs, q, *([k, vt] * n))


def _odd_in_kernel(x_ref, wqkv_ref, wz_ref, wab_ref, dcw_ref, ab_ref, buf_ref,
                   q_ref, k_ref, v_ref, z_ref, gb_ref, nbuf_ref, win_ref, *, tiles_per_seq):
    i = pl.program_id(0)
    nseq = win_ref.shape[0]

    @pl.when(i % tiles_per_seq == 0)
    def _():
        win_ref[...] = buf_ref[...]

    xb = x_ref[...].astype(BF16)
    outs = (q_ref, k_ref, v_ref)
    for part in range(3):
        cols = slice(part * DN_KEY, (part + 1) * DN_KEY)
        pre = jnp.dot(xb, wqkv_ref[part], preferred_element_type=F32)
        y = _causal_conv(pre, win_ref[:, :, cols], dcw_ref[:, cols])
        tails = _run_tails(pre, nseq)
        win_ref[:, :, cols] = tails
        nbuf_ref[:, :, cols] = tails
        y = _silu(y)
        if part == 2:
            outs[part][...] = y
        else:
            post = DN_DK ** -0.5 if part == 0 else 1.0
            for h in range(DN_HEADS):
                sl = slice(h * DN_DK, (h + 1) * DN_DK)
                seg = y[:, sl]
                nrm = lax.rsqrt(jnp.sum(seg * seg, axis=-1, keepdims=True) + NORM_EPS)
                outs[part][:, sl] = seg * nrm * post if part == 0 else seg * nrm

    z_ref[...] = jnp.dot(xb, wz_ref[...], preferred_element_type=F32)
    ab = jnp.dot(xb, wab_ref[...], preferred_element_type=F32)
    lane = lax.broadcasted_iota(jnp.int32, ab.shape, 1)
    g = -jnp.exp(ab_ref[0:1, :]) * _softplus(ab + ab_ref[1:2, :])
    gb_ref[...] = jnp.where(lane < DN_HEADS, g, jax.nn.sigmoid(ab))


def _odd_in(x, w_qkv, w_z, w_ab, dconv_w, ab_const, buf8, *, seq, tm):
    m = x.shape[0]
    nb = m // seq
    tm, tps, nseq = _row_tiling(m, seq, tm)
    const2 = lambda i: (0, 0)
    row = lambda i: (i, 0)
    return pl.pallas_call(
        functools.partial(_odd_in_kernel, tiles_per_seq=tps),
        grid=(m // tm,),
        in_specs=[
            pl.BlockSpec((tm, D_MODEL), row),
            pl.BlockSpec(w_qkv.shape, lambda i: (0, 0, 0)),
            pl.BlockSpec(w_z.shape, const2),
            pl.BlockSpec(w_ab.shape, const2),
            pl.BlockSpec(dconv_w.shape, const2),
            pl.BlockSpec(ab_const.shape, const2),
            pl.BlockSpec((nseq, SUBLANES, DN_QKV), lambda i: (i // tps, 0, 0)),
        ],
        out_specs=[
            pl.BlockSpec((tm, DN_KEY), row),
            pl.BlockSpec((tm, DN_KEY), row),
            pl.BlockSpec((tm, DN_VAL), row),
            pl.BlockSpec((tm, DN_VAL), row),
            pl.BlockSpec((tm, LANES), row),
            pl.BlockSpec((nseq, SUBLANES, DN_QKV), lambda i: (i // tps, 0, 0)),
        ],
        out_shape=[
            jax.ShapeDtypeStruct((m, DN_KEY), F32),
            jax.ShapeDtypeStruct((m, DN_KEY), F32),
            jax.ShapeDtypeStruct((m, DN_VAL), F32),
            jax.ShapeDtypeStruct((m, DN_VAL), F32),
            jax.ShapeDtypeStruct((m, LANES), F32),
            jax.ShapeDtypeStruct((nb, SUBLANES, DN_QKV), F32),
        ],
        scratch_shapes=[pltpu.VMEM((nseq, SUBLANES, DN_QKV), F32)],
        compiler_params=_params("arbitrary"),
        name="odd_in",
    )(x, w_qkv, w_z, w_ab, dconv_w, ab_const, buf8)


def _delta_kernel(q_ref, k_ref, v_ref, z_ref, gb_ref, s0_ref, go_ref, o_ref, sout_ref, state_ref, *, n_steps):
    n = pl.program_id(1)
    n_par, rows = q_ref.shape[0], q_ref.shape[1]
    n_sub = rows // CHUNK

    @pl.when(n == 0)
    def _():
        state_ref[...] = s0_ref[...]

    ri = lax.broadcasted_iota(jnp.int32, (rows, rows), 0)
    ci = lax.broadcasted_iota(jnp.int32, (rows, rows), 1)

    def blk(size):
        return (ri // size) == (ci // size)

    same = blk(CHUNK)
    lower = ri > ci
    tril = jnp.logical_and(same, ri >= ci)
    strict = jnp.logical_and(same, lower)
    eye = (ri == ci).astype(F32)
    base = DELTA_BASE_BLOCK
    levels = []
    size = base
    while size < CHUNK:
        levels.append(jnp.logical_and(jnp.logical_and(blk(2 * size), jnp.logical_not(blk(size))), lower))
        size *= 2
    in_base = jnp.logical_and(blk(base), lower)
    gb = [gb_ref[b] for b in range(n_par)]
    g_col = [jnp.dot(tril.astype(F32), gb[b], preferred_element_type=F32, precision=HIGHEST)
             for b in range(n_par)]
    g_end = [jnp.dot(same.astype(F32), gb[b], preferred_element_type=F32, precision=HIGHEST)
             for b in range(n_par)]
    g_row = [g_col[b].T for b in range(n_par)]
    go = go_ref[...]
    chain = [(b, hd) for b in range(n_par) for hd in range(DN_HEADS)]
    heads = range(len(chain))
    col = [slice(hd * DN_DK, (hd + 1) * DN_DK) for _, hd in chain]
    seq_of = [b for b, _ in chain]
    head_of = [hd for _, hd in chain]

    def tile(ref, h):
        return ref[seq_of[h], :, col[h]]

    gc = [g_col[seq_of[h]][:, head_of[h]:head_of[h] + 1] for h in heads]
    ge = [g_end[seq_of[h]][:, head_of[h]:head_of[h] + 1] for h in heads]
    decay = [jnp.exp(jnp.where(tril, gc[h] - g_row[seq_of[h]][head_of[h]:head_of[h] + 1, :], -jnp.inf))
             for h in heads]
    beta = [gb[seq_of[h]][:, DN_HEADS + head_of[h]:DN_HEADS + head_of[h] + 1] for h in heads]
    kb = [tile(k_ref, h) * beta[h] for h in heads]
    kbf = [tile(k_ref, h).astype(BF16) for h in heads]
    m_full = [jnp.where(strict, lax.dot_general(kb[h].astype(BF16), kbf[h], NT_DIMS,
                                                preferred_element_type=F32) * decay[h], 0.0) for h in heads]
    p = [-jnp.where(in_base, m_full[h], 0.0) for h in heads]
    t = [eye + p[h] for h in heads]
    size = 2
    while size < base:
        p = [_dot(p[h], p[h]) for h in heads]
        t = [t[h] + _dot(t[h], p[h]) for h in heads]
        size *= 2
    size = base
    for off in levels:
        nblk = rows // (2 * size)
        t4 = [t[h].reshape(nblk, 2, size, rows) for h in heads]
        t_low = [t4[h][:, 1].reshape(rows // 2, rows) for h in heads]
        u = [_dot(t_low[h], jnp.where(off, m_full[h], 0.0)) for h in heads]
        t_low = [(t_low[h] - _dot(u[h], t[h])).reshape(nblk, 1, size, rows) for h in heads]
        t = [jnp.concatenate([t4[h][:, 0:1], t_low[h]], axis=1).reshape(rows, rows) for h in heads]
        size *= 2
    e_g = [jnp.exp(gc[h]) for h in heads]
    sol = [_dot(t[h], jnp.concatenate([tile(v_ref, h) * beta[h], kb[h] * e_g[h]], axis=1)) for h in heads]
    attn = [lax.dot_general(tile(q_ref, h).astype(BF16), kbf[h], NT_DIMS,
                            preferred_element_type=F32) * decay[h] for h in heads]
    qd = [tile(q_ref, h) * e_g[h] for h in heads]
    kt = [tile(k_ref, h) * jnp.exp(ge[h] - gc[h]) for h in heads]
    st = [state_ref[seq_of[h], head_of[h]] for h in heads]
    v_new = [[] for _ in heads]
    o_inter = [[] for _ in heads]
    for c in range(n_sub):
        rs = slice(c * CHUNK, (c + 1) * CHUNK)
        for h in heads:
            r = _dot(jnp.concatenate([sol[h][rs, DN_DV:], qd[h][rs]], axis=0), st[h])
            v_c = sol[h][rs, :DN_DV] - r[:CHUNK]
            v_new[h].append(v_c)
            o_inter[h].append(r[CHUNK:])
            g_tot = jnp.exp(ge[h][c * CHUNK:c * CHUNK + 1, :])
            st[h] = st[h] * g_tot + lax.dot_general(kt[h][rs].astype(BF16), v_c.astype(BF16), TN_DIMS,
                                                    preferred_element_type=F32)
    for h in heads:
        state_ref[seq_of[h], head_of[h]] = st[h]
        o = jnp.concatenate(o_inter[h], axis=0) + _dot(attn[h], jnp.concatenate(v_new[h], axis=0))
        zh = tile(z_ref, h)
        o_ref[seq_of[h], :, col[h]] = (_rms_norm(o, go) * _silu(zh)).astype(BF16)

    @pl.when(n == n_steps - 1)
    def _():
        sout_ref[...] = state_ref[...]


def _delta_rule(q, k, v, z, gb, s0, g_o, *, seq, rows, n_par):
    m = q.shape[0]
    nb = m // seq
    rows = min(rows, seq)
    ns = seq // rows
    q, k, v, z, gb = (a.reshape(nb, seq, a.shape[-1]) for a in (q, k, v, z, gb))
    row = lambda b, n: (b, n, 0)
    st_spec = pl.BlockSpec((n_par, DN_HEADS, DN_DK, DN_DV), lambda b, n: (b, 0, 0, 0))
    og, s_new = pl.pallas_call(
        functools.partial(_delta_kernel, n_steps=ns),
        grid=(nb // n_par, ns),
        in_specs=[
            pl.BlockSpec((n_par, rows, DN_KEY), row),
            pl.BlockSpec((n_par, rows, DN_KEY), row),
            pl.BlockSpec((n_par, rows, DN_VAL), row),
            pl.BlockSpec((n_par, rows, DN_VAL), row),
            pl.BlockSpec((n_par, rows, LANES), row),
            st_spec,
            pl.BlockSpec((1, DN_DV), lambda b, n: (0, 0)),
        ],
        out_specs=[pl.BlockSpec((n_par, rows, DN_VAL), row), st_spec],
        out_shape=[
            jax.ShapeDtypeStruct((nb, seq, DN_VAL), BF16),
            jax.ShapeDtypeStruct((nb, DN_HEADS, DN_DK, DN_DV), F32),
        ],
        scratch_shapes=[pltpu.VMEM((n_par, DN_HEADS, DN_DK, DN_DV), F32)],
        compiler_params=_params("arbitrary", "arbitrary"),
        name="gated_delta",
    )(q, k, v, z, gb, s0, g_o)
    return og.reshape(m, DN_VAL), s_new


def _rope_table(pos):
    half = ROPE_DIM // 2
    inv = ROPE_THETA ** (-jnp.arange(half, dtype=F32) / half)
    ang = pos.astype(F32)[:, None] * inv[None, :]
    cs = jnp.concatenate([jnp.cos(ang), jnp.sin(ang)], -1)
    place = jnp.zeros((2 * half, 3 * LANES), F32)
    j = jnp.arange(half)
    place = place.at[j, NOPE_DIM + j].set(1.0).at[j, NOPE_DIM + half + j].set(1.0)
    place = place.at[half + j, LANES + NOPE_DIM + half + j].set(1.0)
    place = place.at[half + j, 2 * LANES + NOPE_DIM + j].set(-1.0)
    ones = (jnp.arange(3 * LANES) < NOPE_DIM).astype(F32)
    return jnp.dot(cs, place, precision=HIGHEST) + ones[None, :]


def _pad_heads(w, per_head, keep):
    kdim = w.shape[0]
    w = w.reshape(kdim, MLA_HEADS, per_head)[:, :, :keep]
    w = jnp.pad(w, ((0, 0), (0, 0), (0, HEAD_PAD - keep)))
    return w.reshape(kdim, MLA_HEADS * HEAD_PAD)


def _prep_even(w_in, w_uq, w_ukv):
    c1 = Q_RANK + KV_RANK
    w_kr = jnp.pad(w_in[:, c1:c1 + ROPE_DIM], ((0, 0), (NOPE_DIM, HEAD_PAD - NOPE_DIM - ROPE_DIM)))
    w1 = jnp.concatenate([w_in[:, :c1], w_kr], -1).astype(BF16)
    w2 = w_in[:, c1 + ROPE_DIM:].astype(BF16)
    wuq = _pad_heads(w_uq, NOPE_DIM + ROPE_DIM, NOPE_DIM + ROPE_DIM).astype(BF16)
    wuk = _pad_heads(w_ukv, NOPE_DIM + V_DIM, NOPE_DIM).astype(BF16)
    wuv = w_ukv.reshape(KV_RANK, MLA_HEADS, NOPE_DIM + V_DIM)[:, :, NOPE_DIM:].reshape(KV_RANK, MLA_WIDTH)
    return w1, w2, wuq, wuk, wuv.T.astype(BF16)


def _prep_odd(w_in, a_log, dt_bias):
    w_qkv = w_in[:, :DN_QKV].reshape(D_MODEL, 3, DN_KEY).transpose(1, 0, 2).astype(BF16)
    w_z = w_in[:, DN_QKV:DN_QKV + DN_VAL].astype(BF16)
    w_ab = jnp.pad(w_in[:, DN_QKV + DN_VAL:], ((0, 0), (0, LANES - 2 * DN_HEADS))).astype(BF16)
    ab_const = jnp.zeros((SUBLANES, LANES), F32)
    ab_const = ab_const.at[0, :DN_HEADS].set(a_log.astype(F32)).at[1, :DN_HEADS].set(dt_bias.astype(F32))
    return w_qkv, w_z, w_ab, ab_const


def _pad_rows8(buf):
    return jnp.pad(buf.astype(F32), ((0, 0), (SUBLANES - buf.shape[1], 0), (0, 0)))


def _trunk(x, past_lat, past_kr, sc_buf, dconv_buf, delta_s, ff_buf, w, *, tm, tq, tk, dr):
    (even_w, odd_w, ffn_w, w_o_e, w_o_o, g_qnorm, g_kvnorm, sc_w, dconv_w, g_onorm,
     ln_mix_g, ln_mix_b, ln_ff_g, ln_ff_b) = w
    nb, seq, _ = x.shape
    past = 0 if past_lat is None else past_lat.shape[2]
    m = nb * seq
    x = x.reshape(m, D_MODEL)
    tab = _rope_table(past + jnp.arange(seq, dtype=jnp.int32))
    t_valid = past + seq
    tq = min(tq, seq)
    tk = min(tk, t_valid)
    t_pad = -(-t_valid // tk) * tk
    lats, krs, scs, dcs, dss, ffs = [], [], [], [], [], []
    for i in range(DEPTH):
        row = lambda a: a[i].reshape(1, -1)
        if i % 2 == 0:
            e = i // 2
            w1, w2, wuq, wuk, wuv = even_w[e]
            q, lat, kr_pad, o_sc, sc_new, k, v = _even_in(
                x, w1, w2, g_qnorm[e].reshape(1, -1), g_kvnorm[e].reshape(1, -1), wuq, sc_w[e], tab,
                _pad_rows8(sc_buf[e]), wuk, wuv, seq=seq, tm=tm)
            if past or t_pad != t_valid or tk != min(tm, seq):
                lat_all = lat.reshape(nb, seq, KV_RANK)
                kr_all = kr_pad.reshape(nb, seq, LANES)
                if past:
                    kr_past = jnp.pad(past_kr[e].astype(F32),
                                      ((0, 0), (0, 0), (NOPE_DIM, HEAD_PAD - NOPE_DIM - ROPE_DIM)))
                    lat_all = jnp.concatenate([past_lat[e].astype(F32), lat_all], 1)
                    kr_all = jnp.concatenate([kr_past, kr_all], 1)
                if t_pad != t_valid:
                    lat_all = jnp.pad(lat_all, ((0, 0), (0, t_pad - t_valid), (0, 0)))
                    kr_all = jnp.pad(kr_all, ((0, 0), (0, t_pad - t_valid), (0, 0)))
                k, v = _kv_up(lat_all.reshape(nb * t_pad, KV_RANK), kr_all.reshape(nb * t_pad, LANES),
                              wuk, wuv, tm=tk)
            o_att = _attention(q, k, v, nb=nb, sq=seq, t_pad=t_pad, t_valid=t_valid, past=past, tq=tq, tk=tk)
            mix, mix_w = [o_att, o_sc], [w_o_e[e][:MLA_WIDTH], w_o_e[e][MLA_WIDTH:]]
            lats.append(lat.reshape(nb, seq, KV_RANK))
            krs.append(kr_pad.reshape(nb, seq, LANES)[:, :, NOPE_DIM:NOPE_DIM + ROPE_DIM])
            scs.append(sc_new[:, SUBLANES - (SC_K - 1):])
        else:
            o = i // 2
            w_qkv, w_z, w_ab, ab_const = odd_w[o]
            q, k, v, z, gb, dc_new = _odd_in(x, w_qkv, w_z, w_ab, dconv_w[o], ab_const,
                                             _pad_rows8(dconv_buf[o]), seq=seq, tm=tm)
            og, s_new = _delta_rule(q, k, v, z, gb, delta_s[o].astype(F32), g_onorm[o].reshape(1, -1), seq=seq,
                                    rows=dr, n_par=DELTA_SEQS_PER_STEP)
            mix, mix_w = [og], [w_o_o[o]]
            dcs.append(dc_new[:, SUBLANES - (DN_CONV - 1):])
            dss.append(s_new)
        w_ff_in, w_ff_out, ffconv_w = ffn_w
        x, ff_new = _mix_ffn(mix, mix_w, x, row(ln_mix_g), row(ln_mix_b), _pad_rows8(ff_buf[i]),
                             w_ff_in, w_ff_out, ffconv_w[i], row(ln_ff_g), row(ln_ff_b),
                             layer=i, seq=seq, tm=tm)
        ffs.append(ff_new[:, SUBLANES - (FF_K - 1):])
    return (x.reshape(nb, seq, D_MODEL), jnp.stack(lats), jnp.stack(krs), jnp.stack(scs), jnp.stack(dcs),
            jnp.stack(dss), jnp.stack(ffs))


def _prepare(w_in_e, w_uq, w_ukv, w_o_e, w_in_o, a_log, dt_bias, w_o_o, w_ff_in, ffconv_w, w_ff_out):
    even_w = [_prep_even(w_in_e[e], w_uq[e], w_ukv[e]) for e in range(N_EVEN)]
    odd_w = [_prep_odd(w_in_o[o], a_log[o], dt_bias[o]) for o in range(N_ODD)]
    ffn_w = (w_ff_in.astype(BF16), w_ff_out.astype(BF16), ffconv_w)
    return even_w, odd_w, ffn_w, w_o_e.astype(BF16), w_o_o.astype(BF16)


def kernel(x_prompt, x_sample, cache_mla_latent, cache_mla_krope, state_sconv, state_dconv, state_delta, state_ffconv, w_in_e, g_qnorm, g_kvnorm, w_uq, w_ukv, sc_w, w_o_e, w_in_o, dconv_w, a_log, dt_bias, g_onorm, w_o_o, w_ff_in, ffconv_w, w_ff_out, ln_mix_g, ln_mix_b, ln_ff_g, ln_ff_b):
    even_w, odd_w, ffn_w, w_o_e_b, w_o_o_b = _prepare(
        w_in_e, w_uq, w_ukv, w_o_e, w_in_o, a_log, dt_bias, w_o_o, w_ff_in, ffconv_w, w_ff_out)
    w = (even_w, odd_w, ffn_w, w_o_e_b, w_o_o_b, g_qnorm, g_kvnorm, sc_w, dconv_w, g_onorm,
         ln_mix_g, ln_mix_b, ln_ff_g, ln_ff_b)
    bp = x_prompt.shape[0]
    f32 = x_prompt.dtype
    y_p, p_lat, p_kr, p_sc, p_dc, p_ds, p_ff = _trunk(
        x_prompt, None, None,
        jnp.zeros((N_EVEN, bp, SC_K - 1, SC_WIDTH), f32),
        jnp.zeros((N_ODD, bp, DN_CONV - 1, DN_QKV), f32),
        jnp.zeros((N_ODD, bp, DN_HEADS, DN_DK, DN_DV), f32),
        jnp.zeros((DEPTH, bp, FF_K - 1, D_FF), f32),
        w, tm=512, tq=512, tk=512, dr=128)
    y_s, s_lat, s_kr, s_sc, s_dc, s_ds, s_ff = _trunk(
        x_sample, cache_mla_latent, cache_mla_krope, state_sconv, state_dconv, state_delta, state_ffconv,
        w, tm=512, tq=64, tk=768, dr=256)
    return (y_p, y_s, p_lat, p_kr, p_sc, p_dc, p_ds, p_ff, s_lat, s_kr, s_sc, s_dc, s_ds, s_ff)
```

```python
import functools
import math

import jax
import jax.numpy as jnp
from jax import lax
from jax.experimental import pallas as pl
from jax.experimental.pallas import tpu as pltpu

D_MODEL = 1024
DEPTH = 4
CHUNK = 64
N_EVEN = (DEPTH + 1) // 2
N_ODD = DEPTH // 2
MLA_HEADS = 8
Q_RANK = 384
KV_RANK = 256
NOPE_DIM = 64
ROPE_DIM = 32
V_DIM = 64
ROPE_THETA = 10000.0
MLA_WIDTH = MLA_HEADS * V_DIM
SC_WIDTH = 512
SC_K = 3
DN_HEADS = 8
DN_DK = 128
DN_DV = 128
DN_CONV = 4
DN_KEY = DN_HEADS * DN_DK
DN_VAL = DN_HEADS * DN_DV
DN_QKV = 2 * DN_KEY + DN_VAL
D_FF = 2816
FF_K = 3
ALPHA = (2 * DEPTH) ** 0.25
NORM_EPS = 1e-6
NEG_INF = -1e30
LOG2_E = math.log2(math.e)

F32 = jnp.float32
BF16 = jnp.bfloat16
HIGHEST = lax.Precision.HIGHEST

LANES = 128
SUBLANES = 8
BF16_SUBLANES = 16
HEAD_PAD = 128
FF_CHUNK = 256
DELTA_BASE_BLOCK = 8
DELTA_SEQS_PER_STEP = 4
ATTN_TILES_PER_STEP = 2
ATTN_HEAD_GROUP = 8
FAST_HEAD_GROUP = 8
FAST_SOFTMAX_MAX_RISE = 60.0
VMEM_LIMIT = 56 * 1024 * 1024

NT_DIMS = (((1,), (1,)), ((), ()))
TN_DIMS = (((0,), (0,)), ((), ()))


def _params(*sem):
    return pltpu.CompilerParams(dimension_semantics=sem, vmem_limit_bytes=VMEM_LIMIT)


def _dot(a, b):
    return jnp.dot(a.astype(BF16), b.astype(BF16), preferred_element_type=F32)


def _layer_norm(r, g, b):
    mu = jnp.mean(r, axis=-1, keepdims=True)
    d = r - mu
    var = jnp.mean(d * d, axis=-1, keepdims=True)
    return d * lax.rsqrt(var + NORM_EPS) * g + b


def _rms_norm(x, g):
    return x * lax.rsqrt(jnp.mean(x * x, axis=-1, keepdims=True) + NORM_EPS) * g


def _silu(x):
    h = 0.5 * x
    return h + h * jnp.tanh(h)


def _softplus(x):
    return jnp.maximum(x, 0.0) + jnp.log1p(jnp.exp(-jnp.abs(x)))


def _causal_conv(x, prev, w):
    tm, c = x.shape
    nseq = prev.shape[0]
    k_taps = w.shape[0]
    groups = tm // nseq // SUBLANES
    x4 = x.reshape(nseq, groups, SUBLANES, c)
    prev4 = prev.reshape(nseq, 1, SUBLANES, c)
    sub = lax.broadcasted_iota(jnp.int32, x4.shape, 2)
    y = x4 * w[k_taps - 1:k_taps, :]
    for k in range(1, k_taps):
        rot = pltpu.roll(x4, k, axis=2)
        rot_prev = jnp.concatenate([pltpu.roll(prev4, k, axis=2), rot[:, :groups - 1]], axis=1)
        y = y + jnp.where(sub < k, rot_prev, rot) * w[k_taps - 1 - k:k_taps - k, :]
    return y.reshape(tm, c)


def _run_tails(x, nseq):
    tm, c = x.shape
    run = tm // nseq
    return x.reshape(nseq, run, c)[:, run - SUBLANES:, :]


def _row_tiling(m, seq, tm):
    tm = min(tm, m)
    if seq >= tm:
        return tm, seq // tm, 1
    return tm, 1, tm // seq


def _mix_ffn_kernel(*refs, n_mix, tiles_per_seq):
    ys, wos = refs[:n_mix], refs[n_mix:2 * n_mix]
    (x_ref, g1_ref, b1_ref, w_in_ref, w_out_ref, cw_ref, buf_ref, g2_ref, b2_ref,
     out_ref, nbuf_ref, act_ref, x1_ref, xb_ref, win_ref) = refs[2 * n_mix:]
    i = pl.program_id(0)
    nseq = win_ref.shape[0]

    tm = x_ref.shape[0]
    halves = [slice(0, tm // 2), slice(tm // 2, tm)]
    seq_half = [slice(0, nseq // 2), slice(nseq // 2, nseq)]

    for rows in halves:
        mix = jnp.dot(ys[0][rows, :], wos[0][...], preferred_element_type=F32)
        for j in range(1, n_mix):
            mix = mix + jnp.dot(ys[j][rows, :], wos[j][...], preferred_element_type=F32)
        x1 = _layer_norm(ALPHA * x_ref[rows, :] + mix, g1_ref[...], b1_ref[...])
        x1_ref[rows, :] = x1
        xb_ref[rows, :] = x1.astype(BF16)

    @pl.when(i % tiles_per_seq == 0)
    def _():
        win_ref[...] = buf_ref[...]

    for c in range(D_FF // FF_CHUNK):
        cols = slice(c * FF_CHUNK, (c + 1) * FF_CHUNK)
        up_cols = slice(D_FF + c * FF_CHUNK, D_FF + (c + 1) * FF_CHUNK)
        prev = win_ref[:, :, cols]
        tails = []
        for hf, rows in enumerate(halves):
            xb = xb_ref[rows, :]
            gate = jnp.dot(xb, w_in_ref[:, cols], preferred_element_type=F32)
            up = jnp.dot(xb, w_in_ref[:, up_cols], preferred_element_type=F32)
            if nseq == 1:
                y = _causal_conv(gate, prev if hf == 0 else tails[0], cw_ref[:, cols])
                tails.append(_run_tails(gate, 1))
            else:
                y = _causal_conv(gate, prev[seq_half[hf]], cw_ref[:, cols])
                tails.append(_run_tails(gate, nseq // 2))
            act_ref[rows, cols] = (_silu(y) * up).astype(BF16)
        tails = tails[1] if nseq == 1 else jnp.concatenate(tails, axis=0)
        win_ref[:, :, cols] = tails
        nbuf_ref[:, :, cols] = tails

    for rows in halves:
        acc = jnp.dot(act_ref[rows, :], w_out_ref[...], preferred_element_type=F32)
        out_ref[rows, :] = _layer_norm(ALPHA * x1_ref[rows, :] + acc, g2_ref[...], b2_ref[...])


def _mix_ffn(ys, wos, x, mix_g, mix_b, buf8, w_in, w_out, conv_w, ff_g, ff_b, *, layer, seq, tm):
    m = x.shape[0]
    nb = m // seq
    tm, tps, nseq = _row_tiling(m, seq, tm)
    const2 = lambda i: (0, 0)
    row = lambda i: (i, 0)
    resident = lambda a: pl.BlockSpec(a.shape, const2, pipeline_mode=pl.Buffered(1))
    of_layer = lambda a: pl.BlockSpec((None,) + a.shape[1:], lambda i: (layer, 0, 0),
                                      pipeline_mode=pl.Buffered(1))
    vec = pl.BlockSpec((1, D_MODEL), const2)
    state = pl.BlockSpec((nseq, SUBLANES, D_FF), lambda i: (i // tps, 0, 0))
    return pl.pallas_call(
        functools.partial(_mix_ffn_kernel, n_mix=len(ys), tiles_per_seq=tps),
        grid=(m // tm,),
        in_specs=([pl.BlockSpec((tm, y.shape[1]), row) for y in ys]
                  + [resident(w) for w in wos]
                  + [pl.BlockSpec((tm, D_MODEL), row), vec, vec,
                     of_layer(w_in), of_layer(w_out), pl.BlockSpec(conv_w.shape, const2), state, vec, vec]),
        out_specs=[pl.BlockSpec((tm, D_MODEL), row), state],
        out_shape=[
            jax.ShapeDtypeStruct((m, D_MODEL), F32),
            jax.ShapeDtypeStruct((nb, SUBLANES, D_FF), F32),
        ],
        scratch_shapes=[
            pltpu.VMEM((tm, D_FF), BF16),
            pltpu.VMEM((tm, D_MODEL), F32),
            pltpu.VMEM((tm, D_MODEL), BF16),
            pltpu.VMEM((nseq, SUBLANES, D_FF), F32),
        ],
        compiler_params=_params("arbitrary"),
        name="mix_ffn",
    )(*ys, *wos, x, mix_g, mix_b, w_in, w_out, conv_w, buf8, ff_g, ff_b)


def _rope(x, tab):
    return (x * tab[:, 0:LANES]
            + pltpu.roll(x, ROPE_DIM // 2, axis=1) * tab[:, LANES:2 * LANES]
            + pltpu.roll(x, LANES - ROPE_DIM // 2, axis=1) * tab[:, 2 * LANES:3 * LANES])


def _keys_values(lat, kr_pad, wuk_ref, wuvt_ref, k_ref, vt_ref):
    lb = lat.astype(BF16)
    kn = jnp.dot(lb, wuk_ref[...], preferred_element_type=F32)
    for h in range(MLA_HEADS):
        sl = slice(h * HEAD_PAD, (h + 1) * HEAD_PAD)
        k_ref[:, sl] = (kn[:, sl] + kr_pad).astype(BF16)
    vt_ref[...] = lax.dot_general(wuvt_ref[...], lb, NT_DIMS, preferred_element_type=F32).astype(BF16)


def _even_in_kernel(x_ref, w1_ref, w2_ref, gq_ref, gkv_ref, wuq_ref, scw_ref, tab_ref, buf_ref, wuk_ref, wuvt_ref,
                    q_ref, lat_ref, kr_ref, osc_ref, nbuf_ref, k_ref, vt_ref, win_ref, *, tiles_per_seq):
    i = pl.program_id(0)
    nseq = win_ref.shape[0]

    @pl.when(i % tiles_per_seq == 0)
    def _():
        win_ref[...] = buf_ref[...]

    xb = x_ref[...].astype(BF16)
    tab = tab_ref[...]
    h1 = jnp.dot(xb, w1_ref[...], preferred_element_type=F32)
    cq = h1[:, :Q_RANK]
    ckv = h1[:, Q_RANK:Q_RANK + KV_RANK]
    kr_raw = h1[:, Q_RANK + KV_RANK:]
    lat = _rms_norm(ckv, gkv_ref[...])
    kr_pad = _rope(kr_raw, tab)
    lat_ref[...] = lat
    kr_ref[...] = pltpu.roll(kr_pad, LANES - NOPE_DIM, axis=1)[:, :ROPE_DIM]
    _keys_values(lat, kr_pad, wuk_ref, wuvt_ref, k_ref, vt_ref)
    q_raw =_dot(_rms_norm(cq, gq_ref[...]), wuq_ref[...])
    scale = (NOPE_DIM + ROPE_DIM) ** -0.5 * LOG2_E
    for h in range(MLA_HEADS):
        sl = slice(h * HEAD_PAD, (h + 1) * HEAD_PAD)
        q_ref[:, sl] = (_rope(q_raw[:, sl], tab) * scale).astype(BF16)

    h2 = jnp.dot(xb, w2_ref[...], preferred_element_type=F32)
    gate_b = h2[:, :SC_WIDTH]
    u_in = h2[:, SC_WIDTH:2 * SC_WIDTH] * h2[:, 2 * SC_WIDTH:]
    u = _causal_conv(u_in, win_ref[...], scw_ref[...])
    tails = _run_tails(u_in, nseq)
    win_ref[...] = tails
    nbuf_ref[...] = tails
    osc_ref[...] = (gate_b * u).astype(BF16)


def _even_in(x, w1, w2, g_q, g_kv, w_uq, sc_w, tab, buf8, w_uk, w_uv_t, *, seq, tm):
    m = x.shape[0]
    nb = m // seq
    tm, tps, nseq = _row_tiling(m, seq, tm)
    if nseq > 1:
        tab = jnp.tile(tab, (nseq, 1))
    const2 = lambda i: (0, 0)
    return pl.pallas_call(
        functools.partial(_even_in_kernel, tiles_per_seq=tps),
        grid=(m // tm,),
        in_specs=[
            pl.BlockSpec((tm, D_MODEL), lambda i: (i, 0)),
            pl.BlockSpec(w1.shape, const2),
            pl.BlockSpec(w2.shape, const2),
            pl.BlockSpec(g_q.shape, const2),
            pl.BlockSpec(g_kv.shape, const2),
            pl.BlockSpec(w_uq.shape, const2),
            pl.BlockSpec(sc_w.shape, const2),
            pl.BlockSpec((tm, 3 * LANES), lambda i: (i % tps, 0)),
            pl.BlockSpec((nseq, SUBLANES, SC_WIDTH), lambda i: (i // tps, 0, 0)),
            pl.BlockSpec(w_uk.shape, const2),
            pl.BlockSpec(w_uv_t.shape, const2),
        ],
        out_specs=[
            pl.BlockSpec((tm, MLA_HEADS * HEAD_PAD), lambda i: (i, 0)),
            pl.BlockSpec((tm, KV_RANK), lambda i: (i, 0)),
            pl.BlockSpec((tm, ROPE_DIM), lambda i: (i, 0)),
            pl.BlockSpec((tm, SC_WIDTH), lambda i: (i, 0)),
            pl.BlockSpec((nseq, SUBLANES, SC_WIDTH), lambda i: (i // tps, 0, 0)),
            pl.BlockSpec((tm, MLA_HEADS * HEAD_PAD), lambda i: (i, 0)),
            pl.BlockSpec((MLA_WIDTH, tm), lambda i: (i, 0)),
        ],
        out_shape=[
            jax.ShapeDtypeStruct((m, MLA_HEADS * HEAD_PAD), BF16),
            jax.ShapeDtypeStruct((m, KV_RANK), F32),
            jax.ShapeDtypeStruct((m, ROPE_DIM), F32),
            jax.ShapeDtypeStruct((m, SC_WIDTH), BF16),
            jax.ShapeDtypeStruct((nb, SUBLANES, SC_WIDTH), F32),
            jax.ShapeDtypeStruct((m, MLA_HEADS * HEAD_PAD), BF16),
            jax.ShapeDtypeStruct((m // tm * MLA_WIDTH, tm), BF16),
        ],
        scratch_shapes=[pltpu.VMEM((nseq, SUBLANES, SC_WIDTH), F32)],
        compiler_params=_params("arbitrary"),
        name="even_in",
    )(x, w1, w2, g_q, g_kv, w_uq, sc_w, tab, buf8, w_uk, w_uv_t)


def _kv_up_kernel(lat_ref, kr_ref, wuk_ref, wuvt_ref, k_ref, vt_ref):
    _keys_values(lat_ref[...], kr_ref[...], wuk_ref, wuvt_ref, k_ref, vt_ref)


def _kv_up(lat, kr_pad, w_uk, w_uv_t, *, tm):
    m = lat.shape[0]
    const2 = lambda i: (0, 0)
    return pl.pallas_call(
        _kv_up_kernel,
        grid=(m // tm,),
        in_specs=[
            pl.BlockSpec((tm, KV_RANK), lambda i: (i, 0)),
            pl.BlockSpec((tm, LANES), lambda i: (i, 0)),
            pl.BlockSpec(w_uk.shape, const2),
            pl.BlockSpec(w_uv_t.shape, const2),
        ],
        out_specs=[
            pl.BlockSpec((tm, MLA_HEADS * HEAD_PAD), lambda i: (i, 0)),
            pl.BlockSpec((MLA_WIDTH, tm), lambda i: (i, 0)),
        ],
        out_shape=[
            jax.ShapeDtypeStruct((m, MLA_HEADS * HEAD_PAD), BF16),
            jax.ShapeDtypeStruct((m // tm * MLA_WIDTH, tm), BF16),
        ],
        compiler_params=_params("arbitrary"),
        name="kv_up",
    )(lat, kr_pad, w_uk, w_uv_t)


def _attn_kernel(*refs, n_tiles, tq, tk, past, t_valid, nk):
    qi_tab, all_tab = refs[0], refs[1]
    ka_tab = refs[2]
    q_ref = refs[2 + n_tiles]
    tile_refs = [(refs[3 + n_tiles + 2 * t], refs[4 + n_tiles + 2 * t]) for t in range(n_tiles)]
    o_ref, m_ref, l_ref, acc_ref = refs[3 + 3 * n_tiles:]
    ka_ref, vta_ref = tile_refs[0]
    step_id = pl.program_id(1)
    qi = qi_tab[step_id]
    q_lo = past + qi * tq
    chunk_lo = q_lo // CHUNK
    chunk_hi = (q_lo + tq - 1) // CHUNK
    last = jnp.minimum(((chunk_hi + 1) * CHUNK - 1) // tk, nk - 1)
    heads = list(range(MLA_HEADS))
    hsl = [slice(h * HEAD_PAD, (h + 1) * HEAD_PAD) for h in heads]
    vsl = [slice(h * V_DIM, (h + 1) * V_DIM) for h in heads]

    def visibility(kj):
        kpos = kj * tk + lax.broadcasted_iota(jnp.int32, (tk, tq), 0)
        qpos = q_lo + lax.broadcasted_iota(jnp.int32, (tk, tq), 1)
        return jnp.logical_and(kpos // CHUNK <= qpos // CHUNK, kpos < t_valid)

    def scores(k_ref, h, visible):
        s = lax.dot_general(k_ref[:, hsl[h]], q_ref[:, hsl[h]], NT_DIMS, preferred_element_type=F32)
        return s if visible is None else jnp.where(visible, s, NEG_INF)

    def weighted_values(vt_ref, h, p):
        ones = jnp.ones((BF16_SUBLANES, tk), BF16)
        return jnp.dot(jnp.concatenate([vt_ref[vsl[h], :], ones], axis=0), p, preferred_element_type=F32)

    def step(k_ref, vt_ref, visible):
        for g0 in range(0, MLA_HEADS, ATTN_HEAD_GROUP):
            grp = range(g0, g0 + ATTN_HEAD_GROUP)
            s = {h: scores(k_ref, h, visible) for h in grp}
            m_prev = {h: m_ref[h:h + 1, :] for h in grp}
            m_new = {h: jnp.maximum(m_prev[h], jnp.max(s[h], axis=0, keepdims=True)) for h in grp}
            alpha = {h: jnp.exp2(m_prev[h] - m_new[h]) for h in grp}
            p = {h: jnp.exp2(s[h] - m_new[h]).astype(BF16) for h in grp}
            pv = {h: weighted_values(vt_ref, h, p[h]) for h in grp}
            for h in grp:
                l_ref[h:h + 1, :] = alpha[h] * l_ref[h:h + 1, :] + pv[h][V_DIM:V_DIM + 1, :]
                m_ref[h:h + 1, :] = m_new[h]
                acc_ref[vsl[h], :] = alpha[h] * acc_ref[vsl[h], :] + pv[h][:V_DIM, :]

    def fast_step(tiles, first, masked_kj=None):
        visible = None if masked_kj is None else visibility(masked_kj)
        pv, t_max = {}, {}
        for g0 in range(0, MLA_HEADS, FAST_HEAD_GROUP):
            grp = range(g0, g0 + FAST_HEAD_GROUP)
            for k_ref, vt_ref in tiles:
                s = {h: scores(k_ref, h, visible) for h in grp}
                p = {h: jnp.exp2(s[h] - m_ref[h:h + 1, :]).astype(BF16) for h in grp}
                for h in grp:
                    tile_max = jnp.max(s[h], axis=0, keepdims=True)
                    t_max[h] = tile_max if h not in t_max else jnp.maximum(t_max[h], tile_max)
                for h in grp:
                    part = weighted_values(vt_ref, h, p[h])
                    pv[h] = part if h not in pv else pv[h] + part
        rise = t_max[0] - m_ref[0:1, :]
        fall = rise
        for h in heads[1:]:
            d = t_max[h] - m_ref[h:h + 1, :]
            rise = jnp.maximum(rise, d)
            fall = jnp.minimum(fall, d)
        hi = jnp.max(rise, axis=1, keepdims=True)
        lo = jnp.min(fall, axis=1, keepdims=True)
        ok = jnp.logical_and(hi <= FAST_SOFTMAX_MAX_RISE,
                             jnp.logical_or(jnp.logical_not(first), lo >= -FAST_SOFTMAX_MAX_RISE))
        for h in heads:
            m_prev = m_ref[h:h + 1, :]
            m_new = jnp.where(first, t_max[h], jnp.maximum(m_prev, t_max[h]))
            alpha = jnp.exp2(m_prev - m_new)
            l_prev = l_ref[h:h + 1, :]
            acc_prev = acc_ref[vsl[h], :]
            l_ref[h:h + 1, :] = jnp.where(ok, (l_prev + pv[h][V_DIM:V_DIM + 1, :]) * alpha, l_prev)
            acc_ref[vsl[h], :] = jnp.where(ok, (acc_prev + pv[h][:V_DIM, :]) * alpha, acc_prev)
            m_ref[h:h + 1, :] = jnp.where(ok, m_new, m_prev)
        in_range = jnp.logical_and(jnp.max(hi) <= FAST_SOFTMAX_MAX_RISE,
                                   jnp.logical_or(jnp.logical_not(first),
                                                  jnp.min(lo) >= -FAST_SOFTMAX_MAX_RISE))

        @pl.when(jnp.logical_not(in_range))
        def _():
            @pl.when(first)
            def _():
                m_ref[...] = jnp.full_like(m_ref, NEG_INF)
            for k_ref, vt_ref in tiles:
                step(k_ref, vt_ref, visible)

    ka = ka_tab[step_id]
    first = ka == 0

    @pl.when(first)
    def _():
        m_ref[...] = jnp.zeros_like(m_ref)
        l_ref[...] = jnp.zeros_like(l_ref)
        acc_ref[...] = jnp.zeros_like(acc_ref)

    a_end = (ka + 1) * tk
    full_a = jnp.logical_and(a_end <= (chunk_lo + 1) * CHUNK, a_end <= t_valid)

    joint = all_tab[step_id] != 0

    @pl.when(joint)
    def _():
        fast_step(tile_refs, first)

    @pl.when(jnp.logical_and(jnp.logical_not(joint), full_a))
    def _():
        fast_step([(ka_ref, vta_ref)], first)

    @pl.when(jnp.logical_and(jnp.logical_not(joint), jnp.logical_not(full_a)))
    def _():
        fast_step([(ka_ref, vta_ref)], first, masked_kj=ka)

    @pl.when(ka == last)
    def _():
        for hp in range(MLA_HEADS // 2):
            pair = jnp.concatenate(
                [acc_ref[vsl[h], :] * (1.0 / l_ref[h:h + 1, :]) for h in (2 * hp, 2 * hp + 1)], axis=0)
            o_ref[:, hp * 2 * V_DIM:(hp + 1) * 2 * V_DIM] = pair.T.astype(BF16)


def _attention(q, k, vt, *, nb, sq, t_pad, t_valid, past, tq, tk):
    nq = sq // tq
    nk = t_pad // tk
    n = ATTN_TILES_PER_STEP
    steps = []
    for qi in range(nq):
        q_lo = past + qi * tq
        last = min((((q_lo + tq - 1) // CHUNK + 1) * CHUNK - 1) // tk, nk - 1)
        fully = [kj for kj in range(last)
                 if (kj + 1) * tk <= (q_lo // CHUNK + 1) * CHUNK and (kj + 1) * tk <= t_valid]
        n_joint = len(fully) // n * n
        for j in range(0, n_joint, n):
            steps.append((qi, 1, fully[j:j + n]))
        steps += [(qi, 0, [kj] * n) for kj in range(last + 1) if kj not in fully[:n_joint]]
    tabs = ([jnp.asarray([s[0] for s in steps], jnp.int32), jnp.asarray([s[1] for s in steps], jnp.int32)]
            + [jnp.asarray([s[2][t] for s in steps], jnp.int32) for t in range(n)])
    q_index = lambda b, p, qt, *tabs_: (b * nq + qt[p], 0)

    def tile_index(t):
        return lambda b, p, qt, at, *k_tabs: (b * nk + k_tabs[t][p], 0)

    tile_specs = []
    for t in range(n):
        tile_specs += [pl.BlockSpec((tk, MLA_HEADS * HEAD_PAD), tile_index(t)),
                       pl.BlockSpec((MLA_WIDTH, tk), tile_index(t))]
    grid_spec = pltpu.PrefetchScalarGridSpec(
        num_scalar_prefetch=2 + n,
        grid=(nb, len(steps)),
        in_specs=[pl.BlockSpec((tq, MLA_HEADS * HEAD_PAD), q_index)] + tile_specs,
        out_specs=pl.BlockSpec((tq, MLA_WIDTH), q_index),
        scratch_shapes=[
            pltpu.VMEM((MLA_HEADS, tq), F32),
            pltpu.VMEM((MLA_HEADS, tq), F32),
            pltpu.VMEM((MLA_WIDTH, tq), F32),
        ],
    )
    return pl.pallas_call(
        functools.partial(_attn_kernel, n_tiles=n, tq=tq, tk=tk, past=past, t_valid=t_valid, nk=nk),
        grid_spec=grid_spec,
        out_shape=jax.ShapeDtypeStruct((nb * sq, MLA_WIDTH), BF16),
        compiler_params=_params("arbitrary", "arbitrary"),
        name="chunk_attention",
    )(*tabs, q, *([k, vt] * n))


def _odd_in_kernel(x_ref, wqkv_ref, wz_ref, wab_ref, dcw_ref, ab_ref, buf_ref,
                   q_ref, k_ref, v_ref, z_ref, gb_ref, nbuf_ref, win_ref, *, tiles_per_seq):
    i = pl.program_id(0)
    nseq = win_ref.shape[0]

    @pl.when(i % tiles_per_seq == 0)
    def _():
        win_ref[...] = buf_ref[...]

    xb = x_ref[...].astype(BF16)
    outs = (q_ref, k_ref, v_ref)
    for part in range(3):
        cols = slice(part * DN_KEY, (part + 1) * DN_KEY)
        pre = jnp.dot(xb, wqkv_ref[part], preferred_element_type=F32)
        y = _causal_conv(pre, win_ref[:, :, cols], dcw_ref[:, cols])
        tails = _run_tails(pre, nseq)
        win_ref[:, :, cols] = tails
        nbuf_ref[:, :, cols] = tails
        y = _silu(y)
        if part == 2:
            outs[part][...] = y
        else:
            post = DN_DK ** -0.5 if part == 0 else 1.0
            for h in range(DN_HEADS):
                sl = slice(h * DN_DK, (h + 1) * DN_DK)
                seg = y[:, sl]
                nrm = lax.rsqrt(jnp.sum(seg * seg, axis=-1, keepdims=True) + NORM_EPS)
                outs[part][:, sl] = seg * nrm * post if part == 0 else seg * nrm

    z_ref[...] = jnp.dot(xb, wz_ref[...], preferred_element_type=F32)
    ab = jnp.dot(xb, wab_ref[...], preferred_element_type=F32)
    lane = lax.broadcasted_iota(jnp.int32, ab.shape, 1)
    g = -jnp.exp(ab_ref[0:1, :]) * _softplus(ab + ab_ref[1:2, :])
    gb_ref[...] = jnp.where(lane < DN_HEADS, g, jax.nn.sigmoid(ab))


def _odd_in(x, w_qkv, w_z, w_ab, dconv_w, ab_const, buf8, *, seq, tm):
    m = x.shape[0]
    nb = m // seq
    tm, tps, nseq = _row_tiling(m, seq, tm)
    const2 = lambda i: (0, 0)
    row = lambda i: (i, 0)
    return pl.pallas_call(
        functools.partial(_odd_in_kernel, tiles_per_seq=tps),
        grid=(m // tm,),
        in_specs=[
            pl.BlockSpec((tm, D_MODEL), row),
            pl.BlockSpec(w_qkv.shape, lambda i: (0, 0, 0)),
            pl.BlockSpec(w_z.shape, const2),
            pl.BlockSpec(w_ab.shape, const2),
            pl.BlockSpec(dconv_w.shape, const2),
            pl.BlockSpec(ab_const.shape, const2),
            pl.BlockSpec((nseq, SUBLANES, DN_QKV), lambda i: (i // tps, 0, 0)),
        ],
        out_specs=[
            pl.BlockSpec((tm, DN_KEY), row),
            pl.BlockSpec((tm, DN_KEY), row),
            pl.BlockSpec((tm, DN_VAL), row),
            pl.BlockSpec((tm, DN_VAL), row),
            pl.BlockSpec((tm, LANES), row),
            pl.BlockSpec((nseq, SUBLANES, DN_QKV), lambda i: (i // tps, 0, 0)),
        ],
        out_shape=[
            jax.ShapeDtypeStruct((m, DN_KEY), F32),
            jax.ShapeDtypeStruct((m, DN_KEY), F32),
            jax.ShapeDtypeStruct((m, DN_VAL), F32),
            jax.ShapeDtypeStruct((m, DN_VAL), F32),
            jax.ShapeDtypeStruct((m, LANES), F32),
            jax.ShapeDtypeStruct((nb, SUBLANES, DN_QKV), F32),
        ],
        scratch_shapes=[pltpu.VMEM((nseq, SUBLANES, DN_QKV), F32)],
        compiler_params=_params("arbitrary"),
        name="odd_in",
    )(x, w_qkv, w_z, w_ab, dconv_w, ab_const, buf8)


def _delta_kernel(q_ref, k_ref, v_ref, z_ref, gb_ref, s0_ref, go_ref, o_ref, sout_ref, state_ref, *, n_steps):
    n = pl.program_id(1)
    n_par, rows = q_ref.shape[0], q_ref.shape[1]
    n_sub = rows // CHUNK

    @pl.when(n == 0)
    def _():
        state_ref[...] = s0_ref[...]

    ri = lax.broadcasted_iota(jnp.int32, (rows, rows), 0)
    ci = lax.broadcasted_iota(jnp.int32, (rows, rows), 1)

    def blk(size):
        return (ri // size) == (ci // size)

    same = blk(CHUNK)
    lower = ri > ci
    tril = jnp.logical_and(same, ri >= ci)
    strict = jnp.logical_and(same, lower)
    eye = (ri == ci).astype(F32)
    base = DELTA_BASE_BLOCK
    levels = []
    size = base
    while size < CHUNK:
        levels.append(jnp.logical_and(jnp.logical_and(blk(2 * size), jnp.logical_not(blk(size))), lower))
        size *= 2
    in_base = jnp.logical_and(blk(base), lower)
    gb = [gb_ref[b] for b in range(n_par)]
    g_col = [jnp.dot(tril.astype(F32), gb[b], preferred_element_type=F32, precision=HIGHEST)
             for b in range(n_par)]
    g_end = [jnp.dot(same.astype(F32), gb[b], preferred_element_type=F32, precision=HIGHEST)
             for b in range(n_par)]
    g_row = [g_col[b].T for b in range(n_par)]
    go = go_ref[...]
    chain = [(b, hd) for b in range(n_par) for hd in range(DN_HEADS)]
    heads = range(len(chain))
    col = [slice(hd * DN_DK, (hd + 1) * DN_DK) for _, hd in chain]
    seq_of = [b for b, _ in chain]
    head_of = [hd for _, hd in chain]

    def tile(ref, h):
        return ref[seq_of[h], :, col[h]]

    gc = [g_col[seq_of[h]][:, head_of[h]:head_of[h] + 1] for h in heads]
    ge = [g_end[seq_of[h]][:, head_of[h]:head_of[h] + 1] for h in heads]
    decay = [jnp.exp(jnp.where(tril, gc[h] - g_row[seq_of[h]][head_of[h]:head_of[h] + 1, :], -jnp.inf))
             for h in heads]
    beta = [gb[seq_of[h]][:, DN_HEADS + head_of[h]:DN_HEADS + head_of[h] + 1] for h in heads]
    kb = [tile(k_ref, h) * beta[h] for h in heads]
    kbf = [tile(k_ref, h).astype(BF16) for h in heads]
    m_full = [jnp.where(strict, lax.dot_general(kb[h].astype(BF16), kbf[h], NT_DIMS,
                                                preferred_element_type=F32) * decay[h], 0.0) for h in heads]
    p = [-jnp.where(in_base, m_full[h], 0.0) for h in heads]
    t = [eye + p[h] for h in heads]
    size = 2
    while size < base:
        p = [_dot(p[h], p[h]) for h in heads]
        t = [t[h] + _dot(t[h], p[h]) for h in heads]
        size *= 2
    size = base
    for off in levels:
        nblk = rows // (2 * size)
        t4 = [t[h].reshape(nblk, 2, size, rows) for h in heads]
        t_low = [t4[h][:, 1].reshape(rows // 2, rows) for h in heads]
        u = [_dot(t_low[h], jnp.where(off, m_full[h], 0.0)) for h in heads]
        t_low = [(t_low[h] - _dot(u[h], t[h])).reshape(nblk, 1, size, rows) for h in heads]
        t = [jnp.concatenate([t4[h][:, 0:1], t_low[h]], axis=1).reshape(rows, rows) for h in heads]
        size *= 2
    e_g = [jnp.exp(gc[h]) for h in heads]
    sol = [_dot(t[h], jnp.concatenate([tile(v_ref, h) * beta[h], kb[h] * e_g[h]], axis=1)) for h in heads]
    attn = [lax.dot_general(tile(q_ref, h).astype(BF16), kbf[h], NT_DIMS,
                            preferred_element_type=F32) * decay[h] for h in heads]
    qd = [tile(q_ref, h) * e_g[h] for h in heads]
    kt = [tile(k_ref, h) * jnp.exp(ge[h] - gc[h]) for h in heads]
    st = [state_ref[seq_of[h], head_of[h]] for h in heads]
    v_new = [[] for _ in heads]
    o_inter = [[] for _ in heads]
    for c in range(n_sub):
        rs = slice(c * CHUNK, (c + 1) * CHUNK)
        for h in heads:
            r = _dot(jnp.concatenate([sol[h][rs, DN_DV:], qd[h][rs]], axis=0), st[h])
            v_c = sol[h][rs, :DN_DV] - r[:CHUNK]
            v_new[h].append(v_c)
            o_inter[h].append(r[CHUNK:])
            g_tot = jnp.exp(ge[h][c * CHUNK:c * CHUNK + 1, :])
            st[h] = st[h] * g_tot + lax.dot_general(kt[h][rs].astype(BF16), v_c.astype(BF16), TN_DIMS,
                                                    preferred_element_type=F32)
    for h in heads:
        state_ref[seq_of[h], head_of[h]] = st[h]
        o = jnp.concatenate(o_inter[h], axis=0) + _dot(attn[h], jnp.concatenate(v_new[h], axis=0))
        zh = tile(z_ref, h)
        o_ref[seq_of[h], :, col[h]] = (_rms_norm(o, go) * _silu(zh)).astype(BF16)

    @pl.when(n == n_steps - 1)
    def _():
        sout_ref[...] = state_ref[...]


def _delta_rule(q, k, v, z, gb, s0, g_o, *, seq, rows, n_par):
    m = q.shape[0]
    nb = m // seq
    rows = min(rows, seq)
    ns = seq // rows
    q, k, v, z, gb = (a.reshape(nb, seq, a.shape[-1]) for a in (q, k, v, z, gb))
    row = lambda b, n: (b, n, 0)
    st_spec = pl.BlockSpec((n_par, DN_HEADS, DN_DK, DN_DV), lambda b, n: (b, 0, 0, 0))
    og, s_new = pl.pallas_call(
        functools.partial(_delta_kernel, n_steps=ns),
        grid=(nb // n_par, ns),
        in_specs=[
            pl.BlockSpec((n_par, rows, DN_KEY), row),
            pl.BlockSpec((n_par, rows, DN_KEY), row),
            pl.BlockSpec((n_par, rows, DN_VAL), row),
            pl.BlockSpec((n_par, rows, DN_VAL), row),
            pl.BlockSpec((n_par, rows, LANES), row),
            st_spec,
            pl.BlockSpec((1, DN_DV), lambda b, n: (0, 0)),
        ],
        out_specs=[pl.BlockSpec((n_par, rows, DN_VAL), row), st_spec],
        out_shape=[
            jax.ShapeDtypeStruct((nb, seq, DN_VAL), BF16),
            jax.ShapeDtypeStruct((nb, DN_HEADS, DN_DK, DN_DV), F32),
        ],
        scratch_shapes=[pltpu.VMEM((n_par, DN_HEADS, DN_DK, DN_DV), F32)],
        compiler_params=_params("arbitrary", "arbitrary"),
        name="gated_delta",
    )(q, k, v, z, gb, s0, g_o)
    return og.reshape(m, DN_VAL), s_new


def _rope_table(pos):
    half = ROPE_DIM // 2
    inv = ROPE_THETA ** (-jnp.arange(half, dtype=F32) / half)
    ang = pos.astype(F32)[:, None] * inv[None, :]
    cs = jnp.concatenate([jnp.cos(ang), jnp.sin(ang)], -1)
    place = jnp.zeros((2 * half, 3 * LANES), F32)
    j = jnp.arange(half)
    place = place.at[j, NOPE_DIM + j].set(1.0).at[j, NOPE_DIM + half + j].set(1.0)
    place = place.at[half + j, LANES + NOPE_DIM + half + j].set(1.0)
    place = place.at[half + j, 2 * LANES + NOPE_DIM + j].set(-1.0)
    ones = (jnp.arange(3 * LANES) < NOPE_DIM).astype(F32)
    return jnp.dot(cs, place, precision=HIGHEST) + ones[None, :]


def _pad_heads(w, per_head, keep):
    kdim = w.shape[0]
    w = w.reshape(kdim, MLA_HEADS, per_head)[:, :, :keep]
    w = jnp.pad(w, ((0, 0), (0, 0), (0, HEAD_PAD - keep)))
    return w.reshape(kdim, MLA_HEADS * HEAD_PAD)


def _prep_even(w_in, w_uq, w_ukv):
    c1 = Q_RANK + KV_RANK
    w_kr = jnp.pad(w_in[:, c1:c1 + ROPE_DIM], ((0, 0), (NOPE_DIM, HEAD_PAD - NOPE_DIM - ROPE_DIM)))
    w1 = jnp.concatenate([w_in[:, :c1], w_kr], -1).astype(BF16)
    w2 = w_in[:, c1 + ROPE_DIM:].astype(BF16)
    wuq = _pad_heads(w_uq, NOPE_DIM + ROPE_DIM, NOPE_DIM + ROPE_DIM).astype(BF16)
    wuk = _pad_heads(w_ukv, NOPE_DIM + V_DIM, NOPE_DIM).astype(BF16)
    wuv = w_ukv.reshape(KV_RANK, MLA_HEADS, NOPE_DIM + V_DIM)[:, :, NOPE_DIM:].reshape(KV_RANK, MLA_WIDTH)
    return w1, w2, wuq, wuk, wuv.T.astype(BF16)


def _prep_odd(w_in, a_log, dt_bias):
    w_qkv = w_in[:, :DN_QKV].reshape(D_MODEL, 3, DN_KEY).transpose(1, 0, 2).astype(BF16)
    w_z = w_in[:, DN_QKV:DN_QKV + DN_VAL].astype(BF16)
    w_ab = jnp.pad(w_in[:, DN_QKV + DN_VAL:], ((0, 0), (0, LANES - 2 * DN_HEADS))).astype(BF16)
    ab_const = jnp.zeros((SUBLANES, LANES), F32)
    ab_const = ab_const.at[0, :DN_HEADS].set(a_log.astype(F32)).at[1, :DN_HEADS].set(dt_bias.astype(F32))
    return w_qkv, w_z, w_ab, ab_const


def _pad_rows8(buf):
    return jnp.pad(buf.astype(F32), ((0, 0), (SUBLANES - buf.shape[1], 0), (0, 0)))


def _trunk(x, past_lat, past_kr, sc_buf, dconv_buf, delta_s, ff_buf, w, *, tm, tq, tk, dr):
    (even_w, odd_w, ffn_w, w_o_e, w_o_o, g_qnorm, g_kvnorm, sc_w, dconv_w, g_onorm,
     ln_mix_g, ln_mix_b, ln_ff_g, ln_ff_b) = w
    nb, seq, _ = x.shape
    past = 0 if past_lat is None else past_lat.shape[2]
    m = nb * seq
    x = x.reshape(m, D_MODEL)
    tab = _rope_table(past + jnp.arange(seq, dtype=jnp.int32))
    t_valid = past + seq
    tq = min(tq, seq)
    tk = min(tk, t_valid)
    t_pad = -(-t_valid // tk) * tk
    lats, krs, scs, dcs, dss, ffs = [], [], [], [], [], []
    for i in range(DEPTH):
        row = lambda a: a[i].reshape(1, -1)
        if i % 2 == 0:
            e = i // 2
            w1, w2, wuq, wuk, wuv = even_w[e]
            q, lat, kr, o_sc, sc_new, k, v = _even_in(
                x, w1, w2, g_qnorm[e].reshape(1, -1), g_kvnorm[e].reshape(1, -1), wuq, sc_w[e], tab,
                _pad_rows8(sc_buf[e]), wuk, wuv, seq=seq, tm=tm)
            if past or t_pad != t_valid or tk != min(tm, seq):
                lat_all = lat.reshape(nb, seq, KV_RANK)
                kr_all = kr.reshape(nb, seq, ROPE_DIM)
                if past:
                    lat_all = jnp.concatenate([past_lat[e].astype(F32), lat_all], 1)
                    kr_all = jnp.concatenate([past_kr[e].astype(F32), kr_all], 1)
                lat_all = jnp.pad(lat_all, ((0, 0), (0, t_pad - t_valid), (0, 0)))
                kr_all = jnp.pad(kr_all, ((0, 0), (0, t_pad - t_valid),
                                          (NOPE_DIM, HEAD_PAD - NOPE_DIM - ROPE_DIM)))
                k, v = _kv_up(lat_all.reshape(nb * t_pad, KV_RANK), kr_all.reshape(nb * t_pad, LANES),
                              wuk, wuv, tm=tk)
            o_att = _attention(q, k, v, nb=nb, sq=seq, t_pad=t_pad, t_valid=t_valid, past=past, tq=tq, tk=tk)
            mix, mix_w = [o_att, o_sc], [w_o_e[e][:MLA_WIDTH], w_o_e[e][MLA_WIDTH:]]
            lats.append(lat.reshape(nb, seq, KV_RANK))
            krs.append(kr.reshape(nb, seq, ROPE_DIM))
            scs.append(sc_new[:, SUBLANES - (SC_K - 1):])
        else:
            o = i // 2
            w_qkv, w_z, w_ab, ab_const = odd_w[o]
            q, k, v, z, gb, dc_new = _odd_in(x, w_qkv, w_z, w_ab, dconv_w[o], ab_const,
                                             _pad_rows8(dconv_buf[o]), seq=seq, tm=tm)
            og, s_new = _delta_rule(q, k, v, z, gb, delta_s[o].astype(F32), g_onorm[o].reshape(1, -1), seq=seq,
                                    rows=dr, n_par=DELTA_SEQS_PER_STEP)
            mix, mix_w = [og], [w_o_o[o]]
            dcs.append(dc_new[:, SUBLANES - (DN_CONV - 1):])
            dss.append(s_new)
        w_ff_in, w_ff_out, ffconv_w = ffn_w
        x, ff_new = _mix_ffn(mix, mix_w, x, row(ln_mix_g), row(ln_mix_b), _pad_rows8(ff_buf[i]),
                             w_ff_in, w_ff_out, ffconv_w[i], row(ln_ff_g), row(ln_ff_b),
                             layer=i, seq=seq, tm=tm)
        ffs.append(ff_new[:, SUBLANES - (FF_K - 1):])
    return (x.reshape(nb, seq, D_MODEL), jnp.stack(lats), jnp.stack(krs), jnp.stack(scs), jnp.stack(dcs),
            jnp.stack(dss), jnp.stack(ffs))


def _prepare(w_in_e, w_uq, w_ukv, w_o_e, w_in_o, a_log, dt_bias, w_o_o, w_ff_in, ffconv_w, w_ff_out):
    even_w = [_prep_even(w_in_e[e], w_uq[e], w_ukv[e]) for e in range(N_EVEN)]
    odd_w = [_prep_odd(w_in_o[o], a_log[o], dt_bias[o]) for o in range(N_ODD)]
    ffn_w = (w_ff_in.astype(BF16), w_ff_out.astype(BF16), ffconv_w)
    return even_w, odd_w, ffn_w, w_o_e.astype(BF16), w_o_o.astype(BF16)


def kernel(x_prompt, x_sample, cache_mla_latent, cache_mla_krope, state_sconv, state_dconv, state_delta, state_ffconv, w_in_e, g_qnorm, g_kvnorm, w_uq, w_ukv, sc_w, w_o_e, w_in_o, dconv_w, a_log, dt_bias, g_onorm, w_o_o, w_ff_in, ffconv_w, w_ff_out, ln_mix_g, ln_mix_b, ln_ff_g, ln_ff_b):
    even_w, odd_w, ffn_w, w_o_e_b, w_o_o_b = _prepare(
        w_in_e, w_uq, w_ukv, w_o_e, w_in_o, a_log, dt_bias, w_o_o, w_ff_in, ffconv_w, w_ff_out)
    w = (even_w, odd_w, ffn_w, w_o_e_b, w_o_o_b, g_qnorm, g_kvnorm, sc_w, dconv_w, g_onorm,
         ln_mix_g, ln_mix_b, ln_ff_g, ln_ff_b)
    bp = x_prompt.shape[0]
    f32 = x_prompt.dtype
    y_p, p_lat, p_kr, p_sc, p_dc, p_ds, p_ff = _trunk(
        x_prompt, None, None,
        jnp.zeros((N_EVEN, bp, SC_K - 1, SC_WIDTH), f32),
        jnp.zeros((N_ODD, bp, DN_CONV - 1, DN_QKV), f32),
        jnp.zeros((N_ODD, bp, DN_HEADS, DN_DK, DN_DV), f32),
        jnp.zeros((DEPTH, bp, FF_K - 1, D_FF), f32),
        w, tm=512, tq=512, tk=512, dr=128)
    y_s, s_lat, s_kr, s_sc, s_dc, s_ds, s_ff = _trunk(
        x_sample, cache_mla_latent, cache_mla_krope, state_sconv, state_dconv, state_delta, state_ffconv,
        w, tm=512, tq=64, tk=768, dr=256)
    return (y_p, y_s, p_lat, p_kr, p_sc, p_dc, p_ds, p_ff, s_lat, s_kr, s_sc, s_dc, s_ds, s_ff)
```

```python
import functools
import math

import jax
import jax.numpy as jnp
from jax import lax
from jax.experimental import pallas as pl
from jax.experimental.pallas import tpu as pltpu

D_MODEL = 1024
DEPTH = 4
CHUNK = 64
N_EVEN = (DEPTH + 1) // 2
N_ODD = DEPTH // 2
MLA_HEADS = 8
Q_RANK = 384
KV_RANK = 256
NOPE_DIM = 64
ROPE_DIM = 32
V_DIM = 64
ROPE_THETA = 10000.0
MLA_WIDTH = MLA_HEADS * V_DIM
SC_WIDTH = 512
SC_K = 3
DN_HEADS = 8
DN_DK = 128
DN_DV = 128
DN_CONV = 4
DN_KEY = DN_HEADS * DN_DK
DN_VAL = DN_HEADS * DN_DV
DN_QKV = 2 * DN_KEY + DN_VAL
D_FF = 2816
FF_K = 3
ALPHA = (2 * DEPTH) ** 0.25
NORM_EPS = 1e-6
NEG_INF = -1e30
LOG2_E = math.log2(math.e)

F32 = jnp.float32
BF16 = jnp.bfloat16
HIGHEST = lax.Precision.HIGHEST

LANES = 128
SUBLANES = 8
BF16_SUBLANES = 16
HEAD_PAD = 128
FF_CHUNK = 256
DELTA_BASE_BLOCK = 8
DELTA_SEQS_PER_STEP = 4
ATTN_TILES_PER_STEP = 2
ATTN_HEAD_GROUP = 8
FAST_HEAD_GROUP = 8
FAST_SOFTMAX_MAX_RISE = 60.0
VMEM_LIMIT = 56 * 1024 * 1024

NT_DIMS = (((1,), (1,)), ((), ()))
TN_DIMS = (((0,), (0,)), ((), ()))


def _params(*sem):
    return pltpu.CompilerParams(dimension_semantics=sem, vmem_limit_bytes=VMEM_LIMIT)


def _dot(a, b):
    return jnp.dot(a.astype(BF16), b.astype(BF16), preferred_element_type=F32)


def _layer_norm(r, g, b):
    mu = jnp.mean(r, axis=-1, keepdims=True)
    d = r - mu
    var = jnp.mean(d * d, axis=-1, keepdims=True)
    return d * lax.rsqrt(var + NORM_EPS) * g + b


def _rms_norm(x, g):
    return x * lax.rsqrt(jnp.mean(x * x, axis=-1, keepdims=True) + NORM_EPS) * g


def _silu(x):
    h = 0.5 * x
    return h + h * jnp.tanh(h)


def _softplus(x):
    return jnp.maximum(x, 0.0) + jnp.log1p(jnp.exp(-jnp.abs(x)))


def _causal_conv(x, prev, w):
    tm, c = x.shape
    nseq = prev.shape[0]
    k_taps = w.shape[0]
    groups = tm // nseq // SUBLANES
    x4 = x.reshape(nseq, groups, SUBLANES, c)
    prev4 = prev.reshape(nseq, 1, SUBLANES, c)
    sub = lax.broadcasted_iota(jnp.int32, x4.shape, 2)
    y = x4 * w[k_taps - 1:k_taps, :]
    for k in range(1, k_taps):
        rot = pltpu.roll(x4, k, axis=2)
        rot_prev = jnp.concatenate([pltpu.roll(prev4, k, axis=2), rot[:, :groups - 1]], axis=1)
        y = y + jnp.where(sub < k, rot_prev, rot) * w[k_taps - 1 - k:k_taps - k, :]
    return y.reshape(tm, c)


def _run_tails(x, nseq):
    tm, c = x.shape
    run = tm // nseq
    return x.reshape(nseq, run, c)[:, run - SUBLANES:, :]


def _row_tiling(m, seq, tm):
    tm = min(tm, m)
    if seq >= tm:
        return tm, seq // tm, 1
    return tm, 1, tm // seq


def _mix_ffn_kernel(*refs, n_mix, tiles_per_seq):
    ys, wos = refs[:n_mix], refs[n_mix:2 * n_mix]
    (x_ref, g1_ref, b1_ref, w_in_ref, w_out_ref, cw_ref, buf_ref, g2_ref, b2_ref,
     out_ref, nbuf_ref, act_ref, x1_ref, xb_ref, win_ref) = refs[2 * n_mix:]
    i = pl.program_id(0)
    nseq = win_ref.shape[0]

    tm = x_ref.shape[0]
    halves = [slice(0, tm // 2), slice(tm // 2, tm)]
    seq_half = [slice(0, nseq // 2), slice(nseq // 2, nseq)]

    for rows in halves:
        mix = jnp.dot(ys[0][rows, :], wos[0][...], preferred_element_type=F32)
        for j in range(1, n_mix):
            mix = mix + jnp.dot(ys[j][rows, :], wos[j][...], preferred_element_type=F32)
        x1 = _layer_norm(ALPHA * x_ref[rows, :] + mix, g1_ref[...], b1_ref[...])
        x1_ref[rows, :] = x1
        xb_ref[rows, :] = x1.astype(BF16)

    @pl.when(i % tiles_per_seq == 0)
    def _():
        win_ref[...] = buf_ref[...]

    for c in range(D_FF // FF_CHUNK):
        cols = slice(c * FF_CHUNK, (c + 1) * FF_CHUNK)
        up_cols = slice(D_FF + c * FF_CHUNK, D_FF + (c + 1) * FF_CHUNK)
        prev = win_ref[:, :, cols]
        tails = []
        for hf, rows in enumerate(halves):
            xb = xb_ref[rows, :]
            gate = jnp.dot(xb, w_in_ref[:, cols], preferred_element_type=F32)
            up = jnp.dot(xb, w_in_ref[:, up_cols], preferred_element_type=F32)
            if nseq == 1:
                y = _causal_conv(gate, prev if hf == 0 else tails[0], cw_ref[:, cols])
                tails.append(_run_tails(gate, 1))
            else:
                y = _causal_conv(gate, prev[seq_half[hf]], cw_ref[:, cols])
                tails.append(_run_tails(gate, nseq // 2))
            act_ref[rows, cols] = (_silu(y) * up).astype(BF16)
        tails = tails[1] if nseq == 1 else jnp.concatenate(tails, axis=0)
        win_ref[:, :, cols] = tails
        nbuf_ref[:, :, cols] = tails

    for rows in halves:
        acc = jnp.dot(act_ref[rows, :], w_out_ref[...], preferred_element_type=F32)
        out_ref[rows, :] = _layer_norm(ALPHA * x1_ref[rows, :] + acc, g2_ref[...], b2_ref[...])


def _mix_ffn(ys, wos, x, mix_g, mix_b, buf8, w_in, w_out, conv_w, ff_g, ff_b, *, layer, seq, tm):
    m = x.shape[0]
    nb = m // seq
    tm, tps, nseq = _row_tiling(m, seq, tm)
    const2 = lambda i: (0, 0)
    row = lambda i: (i, 0)
    resident = lambda a: pl.BlockSpec(a.shape, const2, pipeline_mode=pl.Buffered(1))
    of_layer = lambda a: pl.BlockSpec((None,) + a.shape[1:], lambda i: (layer, 0, 0),
                                      pipeline_mode=pl.Buffered(1))
    vec = pl.BlockSpec((1, D_MODEL), const2)
    state = pl.BlockSpec((nseq, SUBLANES, D_FF), lambda i: (i // tps, 0, 0))
    return pl.pallas_call(
        functools.partial(_mix_ffn_kernel, n_mix=len(ys), tiles_per_seq=tps),
        grid=(m // tm,),
        in_specs=([pl.BlockSpec((tm, y.shape[1]), row) for y in ys]
                  + [resident(w) for w in wos]
                  + [pl.BlockSpec((tm, D_MODEL), row), vec, vec,
                     of_layer(w_in), of_layer(w_out), pl.BlockSpec(conv_w.shape, const2), state, vec, vec]),
        out_specs=[pl.BlockSpec((tm, D_MODEL), row), state],
        out_shape=[
            jax.ShapeDtypeStruct((m, D_MODEL), F32),
            jax.ShapeDtypeStruct((nb, SUBLANES, D_FF), F32),
        ],
        scratch_shapes=[
            pltpu.VMEM((tm, D_FF), BF16),
            pltpu.VMEM((tm, D_MODEL), F32),
            pltpu.VMEM((tm, D_MODEL), BF16),
            pltpu.VMEM((nseq, SUBLANES, D_FF), F32),
        ],
        compiler_params=_params("arbitrary"),
        name="mix_ffn",
    )(*ys, *wos, x, mix_g, mix_b, w_in, w_out, conv_w, buf8, ff_g, ff_b)


def _rope(x, tab):
    return (x * tab[:, 0:LANES]
            + pltpu.roll(x, ROPE_DIM // 2, axis=1) * tab[:, LANES:2 * LANES]
            + pltpu.roll(x, LANES - ROPE_DIM // 2, axis=1) * tab[:, 2 * LANES:3 * LANES])


def _keys_values(lat, kr_pad, wuk_ref, wuvt_ref, k_ref, vt_ref):
    lb = lat.astype(BF16)
    kn = jnp.dot(lb, wuk_ref[...], preferred_element_type=F32)
    for h in range(MLA_HEADS):
        sl = slice(h * HEAD_PAD, (h + 1) * HEAD_PAD)
        k_ref[:, sl] = (kn[:, sl] + kr_pad).astype(BF16)
    vt_ref[...] = lax.dot_general(wuvt_ref[...], lb, NT_DIMS, preferred_element_type=F32).astype(BF16)


def _even_in_kernel(x_ref, w1_ref, w2_ref, gq_ref, gkv_ref, wuq_ref, scw_ref, tab_ref, buf_ref, wuk_ref, wuvt_ref,
                    q_ref, lat_ref, kr_ref, osc_ref, nbuf_ref, k_ref, vt_ref, win_ref, *, tiles_per_seq):
    i = pl.program_id(0)
    nseq = win_ref.shape[0]

    @pl.when(i % tiles_per_seq == 0)
    def _():
        win_ref[...] = buf_ref[...]

    xb = x_ref[...].astype(BF16)
    tab = tab_ref[...]
    h1 = jnp.dot(xb, w1_ref[...], preferred_element_type=F32)
    cq = h1[:, :Q_RANK]
    ckv = h1[:, Q_RANK:Q_RANK + KV_RANK]
    kr_raw = h1[:, Q_RANK + KV_RANK:]
    lat = _rms_norm(ckv, gkv_ref[...])
    kr_pad = _rope(kr_raw, tab)
    lat_ref[...] = lat
    kr_ref[...] = pltpu.roll(kr_pad, LANES - NOPE_DIM, axis=1)[:, :ROPE_DIM]
    _keys_values(lat, kr_pad, wuk_ref, wuvt_ref, k_ref, vt_ref)
    q_raw =_dot(_rms_norm(cq, gq_ref[...]), wuq_ref[...])
    scale = (NOPE_DIM + ROPE_DIM) ** -0.5 * LOG2_E
    for h in range(MLA_HEADS):
        sl = slice(h * HEAD_PAD, (h + 1) * HEAD_PAD)
        q_ref[:, sl] = (_rope(q_raw[:, sl], tab) * scale).astype(BF16)

    h2 = jnp.dot(xb, w2_ref[...], preferred_element_type=F32)
    gate_b = h2[:, :SC_WIDTH]
    u_in = h2[:, SC_WIDTH:2 * SC_WIDTH] * h2[:, 2 * SC_WIDTH:]
    u = _causal_conv(u_in, win_ref[...], scw_ref[...])
    tails = _run_tails(u_in, nseq)
    win_ref[...] = tails
    nbuf_ref[...] = tails
    osc_ref[...] = (gate_b * u).astype(BF16)


def _even_in(x, w1, w2, g_q, g_kv, w_uq, sc_w, tab, buf8, w_uk, w_uv_t, *, seq, tm):
    m = x.shape[0]
    nb = m // seq
    tm, tps, nseq = _row_tiling(m, seq, tm)
    if nseq > 1:
        tab = jnp.tile(tab, (nseq, 1))
    const2 = lambda i: (0, 0)
    return pl.pallas_call(
        functools.partial(_even_in_kernel, tiles_per_seq=tps),
        grid=(m // tm,),
        in_specs=[
            pl.BlockSpec((tm, D_MODEL), lambda i: (i, 0)),
            pl.BlockSpec(w1.shape, const2),
            pl.BlockSpec(w2.shape, const2),
            pl.BlockSpec(g_q.shape, const2),
            pl.BlockSpec(g_kv.shape, const2),
            pl.BlockSpec(w_uq.shape, const2),
            pl.BlockSpec(sc_w.shape, const2),
            pl.BlockSpec((tm, 3 * LANES), lambda i: (i % tps, 0)),
            pl.BlockSpec((nseq, SUBLANES, SC_WIDTH), lambda i: (i // tps, 0, 0)),
            pl.BlockSpec(w_uk.shape, const2),
            pl.BlockSpec(w_uv_t.shape, const2),
        ],
        out_specs=[
            pl.BlockSpec((tm, MLA_HEADS * HEAD_PAD), lambda i: (i, 0)),
            pl.BlockSpec((tm, KV_RANK), lambda i: (i, 0)),
            pl.BlockSpec((tm, ROPE_DIM), lambda i: (i, 0)),
            pl.BlockSpec((tm, SC_WIDTH), lambda i: (i, 0)),
            pl.BlockSpec((nseq, SUBLANES, SC_WIDTH), lambda i: (i // tps, 0, 0)),
            pl.BlockSpec((tm, MLA_HEADS * HEAD_PAD), lambda i: (i, 0)),
            pl.BlockSpec((MLA_WIDTH, tm), lambda i: (i, 0)),
        ],
        out_shape=[
            jax.ShapeDtypeStruct((m, MLA_HEADS * HEAD_PAD), BF16),
            jax.ShapeDtypeStruct((m, KV_RANK), F32),
            jax.ShapeDtypeStruct((m, ROPE_DIM), F32),
            jax.ShapeDtypeStruct((m, SC_WIDTH), BF16),
            jax.ShapeDtypeStruct((nb, SUBLANES, SC_WIDTH), F32),
            jax.ShapeDtypeStruct((m, MLA_HEADS * HEAD_PAD), BF16),
            jax.ShapeDtypeStruct((m // tm * MLA_WIDTH, tm), BF16),
        ],
        scratch_shapes=[pltpu.VMEM((nseq, SUBLANES, SC_WIDTH), F32)],
        compiler_params=_params("arbitrary"),
        name="even_in",
    )(x, w1, w2, g_q, g_kv, w_uq, sc_w, tab, buf8, w_uk, w_uv_t)


def _kv_up_kernel(lat_ref, kr_ref, wuk_ref, wuvt_ref, k_ref, vt_ref):
    _keys_values(lat_ref[...], kr_ref[...], wuk_ref, wuvt_ref, k_ref, vt_ref)


def _kv_up(lat, kr_pad, w_uk, w_uv_t, *, tm):
    m = lat.shape[0]
    const2 = lambda i: (0, 0)
    return pl.pallas_call(
        _kv_up_kernel,
        grid=(m // tm,),
        in_specs=[
            pl.BlockSpec((tm, KV_RANK), lambda i: (i, 0)),
            pl.BlockSpec((tm, LANES), lambda i: (i, 0)),
            pl.BlockSpec(w_uk.shape, const2),
            pl.BlockSpec(w_uv_t.shape, const2),
        ],
        out_specs=[
            pl.BlockSpec((tm, MLA_HEADS * HEAD_PAD), lambda i: (i, 0)),
            pl.BlockSpec((MLA_WIDTH, tm), lambda i: (i, 0)),
        ],
        out_shape=[
            jax.ShapeDtypeStruct((m, MLA_HEADS * HEAD_PAD), BF16),
            jax.ShapeDtypeStruct((m // tm * MLA_WIDTH, tm), BF16),
        ],
        compiler_params=_params("arbitrary"),
        name="kv_up",
    )(lat, kr_pad, w_uk, w_uv_t)


def _attn_kernel(*refs, n_tiles, tq, tk, past, t_valid, nk):
    qi_tab, all_tab = refs[0], refs[1]
    ka_tab = refs[2]
    q_ref = refs[2 + n_tiles]
    tile_refs = [(refs[3 + n_tiles + 2 * t], refs[4 + n_tiles + 2 * t]) for t in range(n_tiles)]
    o_ref, m_ref, l_ref, acc_ref = refs[3 + 3 * n_tiles:]
    ka_ref, vta_ref = tile_refs[0]
    step_id = pl.program_id(1)
    qi = qi_tab[step_id]
    q_lo = past + qi * tq
    chunk_lo = q_lo // CHUNK
    chunk_hi = (q_lo + tq - 1) // CHUNK
    last = jnp.minimum(((chunk_hi + 1) * CHUNK - 1) // tk, nk - 1)
    heads = list(range(MLA_HEADS))
    hsl = [slice(h * HEAD_PAD, (h + 1) * HEAD_PAD) for h in heads]
    vsl = [slice(h * V_DIM, (h + 1) * V_DIM) for h in heads]

    def visibility(kj):
        kpos = kj * tk + lax.broadcasted_iota(jnp.int32, (tk, tq), 0)
        qpos = q_lo + lax.broadcasted_iota(jnp.int32, (tk, tq), 1)
        return jnp.logical_and(kpos // CHUNK <= qpos // CHUNK, kpos < t_valid)

    def scores(k_ref, h, visible):
        s = lax.dot_general(k_ref[:, hsl[h]], q_ref[:, hsl[h]], NT_DIMS, preferred_element_type=F32)
        return s if visible is None else jnp.where(visible, s, NEG_INF)

    def weighted_values(vt_ref, h, p):
        ones = jnp.ones((BF16_SUBLANES, tk), BF16)
        return jnp.dot(jnp.concatenate([vt_ref[vsl[h], :], ones], axis=0), p, preferred_element_type=F32)

    def step(k_ref, vt_ref, visible):
        for g0 in range(0, MLA_HEADS, ATTN_HEAD_GROUP):
            grp = range(g0, g0 + ATTN_HEAD_GROUP)
            s = {h: scores(k_ref, h, visible) for h in grp}
            m_prev = {h: m_ref[h:h + 1, :] for h in grp}
            m_new = {h: jnp.maximum(m_prev[h], jnp.max(s[h], axis=0, keepdims=True)) for h in grp}
            alpha = {h: jnp.exp2(m_prev[h] - m_new[h]) for h in grp}
            p = {h: jnp.exp2(s[h] - m_new[h]).astype(BF16) for h in grp}
            pv = {h: weighted_values(vt_ref, h, p[h]) for h in grp}
            for h in grp:
                l_ref[h:h + 1, :] = alpha[h] * l_ref[h:h + 1, :] + pv[h][V_DIM:V_DIM + 1, :]
                m_ref[h:h + 1, :] = m_new[h]
                acc_ref[vsl[h], :] = alpha[h] * acc_ref[vsl[h], :] + pv[h][:V_DIM, :]

    def fast_step(tiles, first, masked_kj=None):
        visible = None if masked_kj is None else visibility(masked_kj)
        pv, t_max = {}, {}
        for g0 in range(0, MLA_HEADS, FAST_HEAD_GROUP):
            grp = range(g0, g0 + FAST_HEAD_GROUP)
            for k_ref, vt_ref in tiles:
                s = {h: scores(k_ref, h, visible) for h in grp}
                p = {h: jnp.exp2(s[h] - m_ref[h:h + 1, :]).astype(BF16) for h in grp}
                for h in grp:
                    tile_max = jnp.max(s[h], axis=0, keepdims=True)
                    t_max[h] = tile_max if h not in t_max else jnp.maximum(t_max[h], tile_max)
                for h in grp:
                    part = weighted_values(vt_ref, h, p[h])
                    pv[h] = part if h not in pv else pv[h] + part
        rise = t_max[0] - m_ref[0:1, :]
        fall = rise
        for h in heads[1:]:
            d = t_max[h] - m_ref[h:h + 1, :]
            rise = jnp.maximum(rise, d)
            fall = jnp.minimum(fall, d)
        hi = jnp.max(rise, axis=1, keepdims=True)
        lo = jnp.min(fall, axis=1, keepdims=True)
        ok = jnp.logical_and(hi <= FAST_SOFTMAX_MAX_RISE,
                             jnp.logical_or(jnp.logical_not(first), lo >= -FAST_SOFTMAX_MAX_RISE))
        for h in heads:
            m_prev = m_ref[h:h + 1, :]
            m_new = jnp.where(first, t_max[h], jnp.maximum(m_prev, t_max[h]))
            alpha = jnp.exp2(m_prev - m_new)
            l_prev = l_ref[h:h + 1, :]
            acc_prev = acc_ref[vsl[h], :]
            l_ref[h:h + 1, :] = jnp.where(ok, (l_prev + pv[h][V_DIM:V_DIM + 1, :]) * alpha, l_prev)
            acc_ref[vsl[h], :] = jnp.where(ok, (acc_prev + pv[h][:V_DIM, :]) * alpha, acc_prev)
            m_ref[h:h + 1, :] = jnp.where(ok, m_new, m_prev)
        in_range = jnp.logical_and(jnp.max(hi) <= FAST_SOFTMAX_MAX_RISE,
                                   jnp.logical_or(jnp.logical_not(first),
                                                  jnp.min(lo) >= -FAST_SOFTMAX_MAX_RISE))

        @pl.when(jnp.logical_not(in_range))
        def _():
            @pl.when(first)
            def _():
                m_ref[...] = jnp.full_like(m_ref, NEG_INF)
            for k_ref, vt_ref in tiles:
                step(k_ref, vt_ref, visible)

    ka = ka_tab[step_id]
    first = ka == 0

    @pl.when(first)
    def _():
        m_ref[...] = jnp.zeros_like(m_ref)
        l_ref[...] = jnp.zeros_like(l_ref)
        acc_ref[...] = jnp.zeros_like(acc_ref)

    a_end = (ka + 1) * tk
    full_a = jnp.logical_and(a_end <= (chunk_lo + 1) * CHUNK, a_end <= t_valid)

    joint = all_tab[step_id] != 0

    @pl.when(joint)
    def _():
        fast_step(tile_refs, first)

    @pl.when(jnp.logical_and(jnp.logical_not(joint), full_a))
    def _():
        fast_step([(ka_ref, vta_ref)], first)

    @pl.when(jnp.logical_and(jnp.logical_not(joint), jnp.logical_not(full_a)))
    def _():
        fast_step([(ka_ref, vta_ref)], first, masked_kj=ka)

    @pl.when(ka == last)
    def _():
        for hp in range(MLA_HEADS // 2):
            pair = jnp.concatenate(
                [acc_ref[vsl[h], :] * (1.0 / l_ref[h:h + 1, :]) for h in (2 * hp, 2 * hp + 1)], axis=0)
            o_ref[:, hp * 2 * V_DIM:(hp + 1) * 2 * V_DIM] = pair.T.astype(BF16)


def _attention(q, k, vt, *, nb, sq, t_pad, t_valid, past, tq, tk):
    nq = sq // tq
    nk = t_pad // tk
    n = ATTN_TILES_PER_STEP
    steps = []
    for qi in range(nq):
        q_lo = past + qi * tq
        last = min((((q_lo + tq - 1) // CHUNK + 1) * CHUNK - 1) // tk, nk - 1)
        fully = [kj for kj in range(last)
                 if (kj + 1) * tk <= (q_lo // CHUNK + 1) * CHUNK and (kj + 1) * tk <= t_valid]
        n_joint = len(fully) // n * n
        for j in range(0, n_joint, n):
            steps.append((qi, 1, fully[j:j + n]))
        steps += [(qi, 0, [kj] * n) for kj in range(last + 1) if kj not in fully[:n_joint]]
    tabs = ([jnp.asarray([s[0] for s in steps], jnp.int32), jnp.asarray([s[1] for s in steps], jnp.int32)]
            + [jnp.asarray([s[2][t] for s in steps], jnp.int32) for t in range(n)])
    q_index = lambda b, p, qt, *tabs_: (b * nq + qt[p], 0)

    def tile_index(t):
        return lambda b, p, qt, at, *k_tabs: (b * nk + k_tabs[t][p], 0)

    tile_specs = []
    for t in range(n):
        tile_specs += [pl.BlockSpec((tk, MLA_HEADS * HEAD_PAD), tile_index(t)),
                       pl.BlockSpec((MLA_WIDTH, tk), tile_index(t))]
    grid_spec = pltpu.PrefetchScalarGridSpec(
        num_scalar_prefetch=2 + n,
        grid=(nb, len(steps)),
        in_specs=[pl.BlockSpec((tq, MLA_HEADS * HEAD_PAD), q_index)] + tile_specs,
        out_specs=pl.BlockSpec((tq, MLA_WIDTH), q_index),
        scratch_shapes=[
            pltpu.VMEM((MLA_HEADS, tq), F32),
            pltpu.VMEM((MLA_HEADS, tq), F32),
            pltpu.VMEM((MLA_WIDTH, tq), F32),
        ],
    )
    return pl.pallas_call(
        functools.partial(_attn_kernel, n_tiles=n, tq=tq, tk=tk, past=past, t_valid=t_valid, nk=nk),
        grid_spec=grid_spec,
        out_shape=jax.ShapeDtypeStruct((nb * sq, MLA_WIDTH), BF16),
        compiler_params=_params("arbitrary", "arbitrary"),
        name="chunk_attention",
    )(*tabs, q, *([k, vt] * n))


def _odd_in_kernel(x_ref, wqkv_ref, wz_ref, wab_ref, dcw_ref, ab_ref, buf_ref,
                   q_ref, k_ref, v_ref, z_ref, gb_ref, nbuf_ref, win_ref, *, tiles_per_seq):
    i = pl.program_id(0)
    nseq = win_ref.shape[0]

    @pl.when(i % tiles_per_seq == 0)
    def _():
        win_ref[...] = buf_ref[...]

    xb = x_ref[...].astype(BF16)
    outs = (q_ref, k_ref, v_ref)
    for part in range(3):
        cols = slice(part * DN_KEY, (part + 1) * DN_KEY)
        pre = jnp.dot(xb, wqkv_ref[part], preferred_element_type=F32)
        y = _causal_conv(pre, win_ref[:, :, cols], dcw_ref[:, cols])
        tails = _run_tails(pre, nseq)
        win_ref[:, :, cols] = tails
        nbuf_ref[:, :, cols] = tails
        y = _silu(y)
        if part == 2:
            outs[part][...] = y.astype(BF16)
        else:
            post = DN_DK ** -0.5 if part == 0 else 1.0
            for h in range(DN_HEADS):
                sl = slice(h * DN_DK, (h + 1) * DN_DK)
                seg = y[:, sl]
                nrm = lax.rsqrt(jnp.sum(seg * seg, axis=-1, keepdims=True) + NORM_EPS)
                outs[part][:, sl] = (seg * nrm * post if part == 0 else seg * nrm).astype(BF16)

    z_ref[...] = jnp.dot(xb, wz_ref[...], preferred_element_type=F32).astype(BF16)
    ab = jnp.dot(xb, wab_ref[...], preferred_element_type=F32)
    lane = lax.broadcasted_iota(jnp.int32, ab.shape, 1)
    g = -jnp.exp(ab_ref[0:1, :]) * _softplus(ab + ab_ref[1:2, :])
    gb_ref[...] = jnp.where(lane < DN_HEADS, g, jax.nn.sigmoid(ab))


def _odd_in(x, w_qkv, w_z, w_ab, dconv_w, ab_const, buf8, *, seq, tm):
    m = x.shape[0]
    nb = m // seq
    tm, tps, nseq = _row_tiling(m, seq, tm)
    const2 = lambda i: (0, 0)
    row = lambda i: (i, 0)
    return pl.pallas_call(
        functools.partial(_odd_in_kernel, tiles_per_seq=tps),
        grid=(m // tm,),
        in_specs=[
            pl.BlockSpec((tm, D_MODEL), row),
            pl.BlockSpec(w_qkv.shape, lambda i: (0, 0, 0)),
            pl.BlockSpec(w_z.shape, const2),
            pl.BlockSpec(w_ab.shape, const2),
            pl.BlockSpec(dconv_w.shape, const2),
            pl.BlockSpec(ab_const.shape, const2),
            pl.BlockSpec((nseq, SUBLANES, DN_QKV), lambda i: (i // tps, 0, 0)),
        ],
        out_specs=[
            pl.BlockSpec((tm, DN_KEY), row),
            pl.BlockSpec((tm, DN_KEY), row),
            pl.BlockSpec((tm, DN_VAL), row),
            pl.BlockSpec((tm, DN_VAL), row),
            pl.BlockSpec((tm, LANES), row),
            pl.BlockSpec((nseq, SUBLANES, DN_QKV), lambda i: (i // tps, 0, 0)),
        ],
        out_shape=[
            jax.ShapeDtypeStruct((m, DN_KEY), BF16),
            jax.ShapeDtypeStruct((m, DN_KEY), BF16),
            jax.ShapeDtypeStruct((m, DN_VAL), BF16),
            jax.ShapeDtypeStruct((m, DN_VAL), BF16),
            jax.ShapeDtypeStruct((m, LANES), F32),
            jax.ShapeDtypeStruct((nb, SUBLANES, DN_QKV), F32),
        ],
        scratch_shapes=[pltpu.VMEM((nseq, SUBLANES, DN_QKV), F32)],
        compiler_params=_params("arbitrary"),
        name="odd_in",
    )(x, w_qkv, w_z, w_ab, dconv_w, ab_const, buf8)


def _delta_kernel(q_ref, k_ref, v_ref, z_ref, gb_ref, s0_ref, go_ref, o_ref, sout_ref, state_ref, *, n_steps):
    n = pl.program_id(1)
    n_par, rows = q_ref.shape[0], q_ref.shape[1]
    n_sub = rows // CHUNK

    @pl.when(n == 0)
    def _():
        state_ref[...] = s0_ref[...]

    ri = lax.broadcasted_iota(jnp.int32, (rows, rows), 0)
    ci = lax.broadcasted_iota(jnp.int32, (rows, rows), 1)

    def blk(size):
        return (ri // size) == (ci // size)

    same = blk(CHUNK)
    lower = ri > ci
    tril = jnp.logical_and(same, ri >= ci)
    strict = jnp.logical_and(same, lower)
    eye = (ri == ci).astype(F32)
    base = DELTA_BASE_BLOCK
    levels = []
    size = base
    while size < CHUNK:
        levels.append(jnp.logical_and(jnp.logical_and(blk(2 * size), jnp.logical_not(blk(size))), lower))
        size *= 2
    in_base = jnp.logical_and(blk(base), lower)
    gb = [gb_ref[b] for b in range(n_par)]
    g_col = [jnp.dot(tril.astype(F32), gb[b], preferred_element_type=F32, precision=HIGHEST)
             for b in range(n_par)]
    g_end = [jnp.dot(same.astype(F32), gb[b], preferred_element_type=F32, precision=HIGHEST)
             for b in range(n_par)]
    g_row = [g_col[b].T for b in range(n_par)]
    go = go_ref[...]
    chain = [(b, hd) for b in range(n_par) for hd in range(DN_HEADS)]
    heads = range(len(chain))
    col = [slice(hd * DN_DK, (hd + 1) * DN_DK) for _, hd in chain]
    seq_of = [b for b, _ in chain]
    head_of = [hd for _, hd in chain]

    def tile(ref, h):
        return ref[seq_of[h], :, col[h]].astype(F32)

    gc = [g_col[seq_of[h]][:, head_of[h]:head_of[h] + 1] for h in heads]
    ge = [g_end[seq_of[h]][:, head_of[h]:head_of[h] + 1] for h in heads]
    decay = [jnp.exp(jnp.where(tril, gc[h] - g_row[seq_of[h]][head_of[h]:head_of[h] + 1, :], -jnp.inf))
             for h in heads]
    beta = [gb[seq_of[h]][:, DN_HEADS + head_of[h]:DN_HEADS + head_of[h] + 1] for h in heads]
    kb = [tile(k_ref, h) * beta[h] for h in heads]
    kbf = [tile(k_ref, h).astype(BF16) for h in heads]
    m_full = [jnp.where(strict, lax.dot_general(kb[h].astype(BF16), kbf[h], NT_DIMS,
                                                preferred_element_type=F32) * decay[h], 0.0) for h in heads]
    p = [-jnp.where(in_base, m_full[h], 0.0) for h in heads]
    t = [eye + p[h] for h in heads]
    size = 2
    while size < base:
        p = [_dot(p[h], p[h]) for h in heads]
        t = [t[h] + _dot(t[h], p[h]) for h in heads]
        size *= 2
    size = base
    for off in levels:
        nblk = rows // (2 * size)
        t4 = [t[h].reshape(nblk, 2, size, rows) for h in heads]
        t_low = [t4[h][:, 1].reshape(rows // 2, rows) for h in heads]
        u = [_dot(t_low[h], jnp.where(off, m_full[h], 0.0)) for h in heads]
        t_low = [(t_low[h] - _dot(u[h], t[h])).reshape(nblk, 1, size, rows) for h in heads]
        t = [jnp.concatenate([t4[h][:, 0:1], t_low[h]], axis=1).reshape(rows, rows) for h in heads]
        size *= 2
    e_g = [jnp.exp(gc[h]) for h in heads]
    sol = [_dot(t[h], jnp.concatenate([tile(v_ref, h) * beta[h], kb[h] * e_g[h]], axis=1)) for h in heads]
    attn = [lax.dot_general(tile(q_ref, h).astype(BF16), kbf[h], NT_DIMS,
                            preferred_element_type=F32) * decay[h] for h in heads]
    qd = [tile(q_ref, h) * e_g[h] for h in heads]
    kt = [tile(k_ref, h) * jnp.exp(ge[h] - gc[h]) for h in heads]
    st = [state_ref[seq_of[h], head_of[h]] for h in heads]
    v_new = [[] for _ in heads]
    o_inter = [[] for _ in heads]
    for c in range(n_sub):
        rs = slice(c * CHUNK, (c + 1) * CHUNK)
        for h in heads:
            r = _dot(jnp.concatenate([sol[h][rs, DN_DV:], qd[h][rs]], axis=0), st[h])
            v_c = sol[h][rs, :DN_DV] - r[:CHUNK]
            v_new[h].append(v_c)
            o_inter[h].append(r[CHUNK:])
            g_tot = jnp.exp(ge[h][c * CHUNK:c * CHUNK + 1, :])
            st[h] = st[h] * g_tot + lax.dot_general(kt[h][rs].astype(BF16), v_c.astype(BF16), TN_DIMS,
                                                    preferred_element_type=F32)
    for h in heads:
        state_ref[seq_of[h], head_of[h]] = st[h]
        o = jnp.concatenate(o_inter[h], axis=0) + _dot(attn[h], jnp.concatenate(v_new[h], axis=0))
        zh = tile(z_ref, h)
        o_ref[seq_of[h], :, col[h]] = (_rms_norm(o, go) * _silu(zh)).astype(BF16)

    @pl.when(n == n_steps - 1)
    def _():
        sout_ref[...] = state_ref[...]


def _delta_rule(q, k, v, z, gb, s0, g_o, *, seq, rows, n_par):
    m = q.shape[0]
    nb = m // seq
    rows = min(rows, seq)
    ns = seq // rows
    q, k, v, z, gb = (a.reshape(nb, seq, a.shape[-1]) for a in (q, k, v, z, gb))
    row = lambda b, n: (b, n, 0)
    st_spec = pl.BlockSpec((n_par, DN_HEADS, DN_DK, DN_DV), lambda b, n: (b, 0, 0, 0))
    og, s_new = pl.pallas_call(
        functools.partial(_delta_kernel, n_steps=ns),
        grid=(nb // n_par, ns),
        in_specs=[
            pl.BlockSpec((n_par, rows, DN_KEY), row),
            pl.BlockSpec((n_par, rows, DN_KEY), row),
            pl.BlockSpec((n_par, rows, DN_VAL), row),
            pl.BlockSpec((n_par, rows, DN_VAL), row),
            pl.BlockSpec((n_par, rows, LANES), row),
            st_spec,
            pl.BlockSpec((1, DN_DV), lambda b, n: (0, 0)),
        ],
        out_specs=[pl.BlockSpec((n_par, rows, DN_VAL), row), st_spec],
        out_shape=[
            jax.ShapeDtypeStruct((nb, seq, DN_VAL), BF16),
            jax.ShapeDtypeStruct((nb, DN_HEADS, DN_DK, DN_DV), F32),
        ],
        scratch_shapes=[pltpu.VMEM((n_par, DN_HEADS, DN_DK, DN_DV), F32)],
        compiler_params=_params("arbitrary", "arbitrary"),
        name="gated_delta",
    )(q, k, v, z, gb, s0, g_o)
    return og.reshape(m, DN_VAL), s_new


def _rope_table(pos):
    half = ROPE_DIM // 2
    inv = ROPE_THETA ** (-jnp.arange(half, dtype=F32) / half)
    ang = pos.astype(F32)[:, None] * inv[None, :]
    cs = jnp.concatenate([jnp.cos(ang), jnp.sin(ang)], -1)
    place = jnp.zeros((2 * half, 3 * LANES), F32)
    j = jnp.arange(half)
    place = place.at[j, NOPE_DIM + j].set(1.0).at[j, NOPE_DIM + half + j].set(1.0)
    place = place.at[half + j, LANES + NOPE_DIM + half + j].set(1.0)
    place = place.at[half + j, 2 * LANES + NOPE_DIM + j].set(-1.0)
    ones = (jnp.arange(3 * LANES) < NOPE_DIM).astype(F32)
    return jnp.dot(cs, place, precision=HIGHEST) + ones[None, :]


def _pad_heads(w, per_head, keep):
    kdim = w.shape[0]
    w = w.reshape(kdim, MLA_HEADS, per_head)[:, :, :keep]
    w = jnp.pad(w, ((0, 0), (0, 0), (0, HEAD_PAD - keep)))
    return w.reshape(kdim, MLA_HEADS * HEAD_PAD)


def _prep_even(w_in, w_uq, w_ukv):
    c1 = Q_RANK + KV_RANK
    w_kr = jnp.pad(w_in[:, c1:c1 + ROPE_DIM], ((0, 0), (NOPE_DIM, HEAD_PAD - NOPE_DIM - ROPE_DIM)))
    w1 = jnp.concatenate([w_in[:, :c1], w_kr], -1).astype(BF16)
    w2 = w_in[:, c1 + ROPE_DIM:].astype(BF16)
    wuq = _pad_heads(w_uq, NOPE_DIM + ROPE_DIM, NOPE_DIM + ROPE_DIM).astype(BF16)
    wuk = _pad_heads(w_ukv, NOPE_DIM + V_DIM, NOPE_DIM).astype(BF16)
    wuv = w_ukv.reshape(KV_RANK, MLA_HEADS, NOPE_DIM + V_DIM)[:, :, NOPE_DIM:].reshape(KV_RANK, MLA_WIDTH)
    return w1, w2, wuq, wuk, wuv.T.astype(BF16)


def _prep_odd(w_in, a_log, dt_bias):
    w_qkv = w_in[:, :DN_QKV].reshape(D_MODEL, 3, DN_KEY).transpose(1, 0, 2).astype(BF16)
    w_z = w_in[:, DN_QKV:DN_QKV + DN_VAL].astype(BF16)
    w_ab = jnp.pad(w_in[:, DN_QKV + DN_VAL:], ((0, 0), (0, LANES - 2 * DN_HEADS))).astype(BF16)
    ab_const = jnp.zeros((SUBLANES, LANES), F32)
    ab_const = ab_const.at[0, :DN_HEADS].set(a_log.astype(F32)).at[1, :DN_HEADS].set(dt_bias.astype(F32))
    return w_qkv, w_z, w_ab, ab_const


def _pad_rows8(buf):
    return jnp.pad(buf.astype(F32), ((0, 0), (SUBLANES - buf.shape[1], 0), (0, 0)))


def _trunk(x, past_lat, past_kr, sc_buf, dconv_buf, delta_s, ff_buf, w, *, tm, tq, tk, dr):
    (even_w, odd_w, ffn_w, w_o_e, w_o_o, g_qnorm, g_kvnorm, sc_w, dconv_w, g_onorm,
     ln_mix_g, ln_mix_b, ln_ff_g, ln_ff_b) = w
    nb, seq, _ = x.shape
    past = 0 if past_lat is None else past_lat.shape[2]
    m = nb * seq
    x = x.reshape(m, D_MODEL)
    tab = _rope_table(past + jnp.arange(seq, dtype=jnp.int32))
    t_valid = past + seq
    tq = min(tq, seq)
    tk = min(tk, t_valid)
    t_pad = -(-t_valid // tk) * tk
    lats, krs, scs, dcs, dss, ffs = [], [], [], [], [], []
    for i in range(DEPTH):
        row = lambda a: a[i].reshape(1, -1)
        if i % 2 == 0:
            e = i // 2
            w1, w2, wuq, wuk, wuv = even_w[e]
            q, lat, kr, o_sc, sc_new, k, v = _even_in(
                x, w1, w2, g_qnorm[e].reshape(1, -1), g_kvnorm[e].reshape(1, -1), wuq, sc_w[e], tab,
                _pad_rows8(sc_buf[e]), wuk, wuv, seq=seq, tm=tm)
            if past or t_pad != t_valid or tk != min(tm, seq):
                lat_all = lat.reshape(nb, seq, KV_RANK)
                kr_all = kr.reshape(nb, seq, ROPE_DIM)
                if past:
                    lat_all = jnp.concatenate([past_lat[e].astype(F32), lat_all], 1)
                    kr_all = jnp.concatenate([past_kr[e].astype(F32), kr_all], 1)
                lat_all = jnp.pad(lat_all, ((0, 0), (0, t_pad - t_valid), (0, 0)))
                kr_all = jnp.pad(kr_all, ((0, 0), (0, t_pad - t_valid),
                                          (NOPE_DIM, HEAD_PAD - NOPE_DIM - ROPE_DIM)))
                k, v = _kv_up(lat_all.reshape(nb * t_pad, KV_RANK), kr_all.reshape(nb * t_pad, LANES),
                              wuk, wuv, tm=tk)
            o_att = _attention(q, k, v, nb=nb, sq=seq, t_pad=t_pad, t_valid=t_valid, past=past, tq=tq, tk=tk)
            mix, mix_w = [o_att, o_sc], [w_o_e[e][:MLA_WIDTH], w_o_e[e][MLA_WIDTH:]]
            lats.append(lat.reshape(nb, seq, KV_RANK))
            krs.append(kr.reshape(nb, seq, ROPE_DIM))
            scs.append(sc_new[:, SUBLANES - (SC_K - 1):])
        else:
            o = i // 2
            w_qkv, w_z, w_ab, ab_const = odd_w[o]
            q, k, v, z, gb, dc_new = _odd_in(x, w_qkv, w_z, w_ab, dconv_w[o], ab_const,
                                             _pad_rows8(dconv_buf[o]), seq=seq, tm=tm)
            og, s_new = _delta_rule(q, k, v, z, gb, delta_s[o].astype(F32), g_onorm[o].reshape(1, -1), seq=seq,
                                    rows=dr, n_par=DELTA_SEQS_PER_STEP)
            mix, mix_w = [og], [w_o_o[o]]
            dcs.append(dc_new[:, SUBLANES - (DN_CONV - 1):])
            dss.append(s_new)
        w_ff_in, w_ff_out, ffconv_w = ffn_w
        x, ff_new = _mix_ffn(mix, mix_w, x, row(ln_mix_g), row(ln_mix_b), _pad_rows8(ff_buf[i]),
                             w_ff_in, w_ff_out, ffconv_w[i], row(ln_ff_g), row(ln_ff_b),
                             layer=i, seq=seq, tm=tm)
        ffs.append(ff_new[:, SUBLANES - (FF_K - 1):])
    return (x.reshape(nb, seq, D_MODEL), jnp.stack(lats), jnp.stack(krs), jnp.stack(scs), jnp.stack(dcs),
            jnp.stack(dss), jnp.stack(ffs))


def _prepare(w_in_e, w_uq, w_ukv, w_o_e, w_in_o, a_log, dt_bias, w_o_o, w_ff_in, ffconv_w, w_ff_out):
    even_w = [_prep_even(w_in_e[e], w_uq[e], w_ukv[e]) for e in range(N_EVEN)]
    odd_w = [_prep_odd(w_in_o[o], a_log[o], dt_bias[o]) for o in range(N_ODD)]
    ffn_w = (w_ff_in.astype(BF16), w_ff_out.astype(BF16), ffconv_w)
    return even_w, odd_w, ffn_w, w_o_e.astype(BF16), w_o_o.astype(BF16)


def kernel(x_prompt, x_sample, cache_mla_latent, cache_mla_krope, state_sconv, state_dconv, state_delta, state_ffconv, w_in_e, g_qnorm, g_kvnorm, w_uq, w_ukv, sc_w, w_o_e, w_in_o, dconv_w, a_log, dt_bias, g_onorm, w_o_o, w_ff_in, ffconv_w, w_ff_out, ln_mix_g, ln_mix_b, ln_ff_g, ln_ff_b):
    even_w, odd_w, ffn_w, w_o_e_b, w_o_o_b = _prepare(
        w_in_e, w_uq, w_ukv, w_o_e, w_in_o, a_log, dt_bias, w_o_o, w_ff_in, ffconv_w, w_ff_out)
    w = (even_w, odd_w, ffn_w, w_o_e_b, w_o_o_b, g_qnorm, g_kvnorm, sc_w, dconv_w, g_onorm,
         ln_mix_g, ln_mix_b, ln_ff_g, ln_ff_b)
    bp = x_prompt.shape[0]
    f32 = x_prompt.dtype
    y_p, p_lat, p_kr, p_sc, p_dc, p_ds, p_ff = _trunk(
        x_prompt, None, None,
        jnp.zeros((N_EVEN, bp, SC_K - 1, SC_WIDTH), f32),
        jnp.zeros((N_ODD, bp, DN_CONV - 1, DN_QKV), f32),
        jnp.zeros((N_ODD, bp, DN_HEADS, DN_DK, DN_DV), f32),
        jnp.zeros((DEPTH, bp, FF_K - 1, D_FF), f32),
        w, tm=512, tq=512, tk=512, dr=128)
    y_s, s_lat, s_kr, s_sc, s_dc, s_ds, s_ff = _trunk(
        x_sample, cache_mla_latent, cache_mla_krope, state_sconv, state_dconv, state_delta, state_ffconv,
        w, tm=512, tq=64, tk=768, dr=256)
    return (y_p, y_s, p_lat, p_kr, p_sc, p_dc, p_ds, p_ff, s_lat, s_kr, s_sc, s_dc, s_ds, s_ff)
```
